```python
import math
import jax, jax.numpy as jnp
from jax import lax
import numpy as np

D_MODEL = 1024
BATCH = 32
SEQ = 256
DEPTH = 2
DEC_BATCH = 8
DEC_SEQ = 1024
PAST_LEN = 256

GRID_W = 64
MIX_W = D_MODEL
ATTN_W = D_MODEL // 2
HEAD_DIM = 64
N_HEADS = ATTN_W // HEAD_DIM
N_KV_HEADS = 2
KV_GROUP = N_HEADS // N_KV_HEADS
KV_W = N_KV_HEADS * HEAD_DIM
Q_BLOCK = 128
ROPE_THETA = 10000.0
SSM_W = D_MODEL // 4
SSM_CH = 16
SSM_GROUPS = SSM_W // SSM_CH
SSM_STATE = 64
POOL_W = D_MODEL // 4
POOL_WINDOWS = (2, 4, 8, 16)
POOL_GROUPS = len(POOL_WINDOWS)
POOL_CH = POOL_W // POOL_GROUPS
IN_COLS = ATTN_W + 2 * KV_W + SSM_W + POOL_W
FF_DENSE = 11 * D_MODEL // 4
N_EXPERTS = 8
TOP_K = 2
FF_EXPERT = FF_DENSE // 2
N_DENSE = (DEPTH + 1) // 2
N_MOE = DEPTH // 2
EPS = 1e-6

kernel_name = 'hybrid_prefix_diffusion_step'


def _rmsnorm(x, g):
    x32 = x.astype(jnp.float32)
    y = x32 * lax.rsqrt(jnp.mean(x32 * x32, axis=-1, keepdims=True) + EPS)
    return (y * g.astype(jnp.float32)).astype(x.dtype)


def _modulation(cond, w, b):
    m = jax.nn.silu(cond.astype(jnp.float32)) @ w.astype(jnp.float32) + b.astype(jnp.float32)
    return jnp.split(m[:, None, :], 6, axis=-1)


def _ada(x, g, shift, scale):
    y = _rmsnorm(x, g).astype(jnp.float32) * (1.0 + scale) + shift
    return y.astype(x.dtype)


def _rope_axis(x, pos):
    nf = x.shape[-1] // 2
    inv = ROPE_THETA ** (-jnp.arange(nf, dtype=jnp.float32) / nf)
    ang = pos.astype(jnp.float32)[:, None] * inv[None, :]
    cos = jnp.cos(ang)[None, :, None, :]
    sin = jnp.sin(ang)[None, :, None, :]
    x1, x2 = x[..., :nf], x[..., nf:]
    return jnp.concatenate([x1 * cos - x2 * sin, x2 * cos + x1 * sin], axis=-1)


def _rope_2d(x):
    L = x.shape[1]
    rows = L // GRID_W
    row = jnp.repeat(jnp.arange(rows), GRID_W)
    col = jnp.tile(jnp.arange(GRID_W), rows)
    x32 = x.astype(jnp.float32)
    half = HEAD_DIM // 2
    out = jnp.concatenate([_rope_axis(x32[..., :half], row), _rope_axis(x32[..., half:], col)], axis=-1)
    return out.astype(x.dtype)


def _attend(q, k, v):
    B_, Lq = q.shape[:2]
    nb = Lq // Q_BLOCK
    qb = q.astype(jnp.float32).reshape(B_, nb, Q_BLOCK, N_KV_HEADS, KV_GROUP, HEAD_DIM)
    qb = qb.transpose(1, 0, 2, 3, 4, 5)
    k32 = k.astype(jnp.float32)
    v32 = v.astype(jnp.float32)
    scale = HEAD_DIM ** -0.5

    def block(qi):
        s = jnp.einsum('bqkgd,bskd->bkgqs', qi, k32) * scale
        p = jax.nn.softmax(s, axis=-1)
        return jnp.einsum('bkgqs,bskd->bqkgd', p, v32)

    o = lax.map(block, qb)
    return o.transpose(1, 0, 2, 3, 4, 5).reshape(B_, Lq, ATTN_W).astype(v.dtype)


def _diag_scan(u32, lam_bar, b_bar, h0, reverse):
    bu = jnp.einsum('blgc,gpc->blgp', u32.astype(jnp.complex64), b_bar)
    if reverse:
        bu = bu[:, ::-1]
    bu = bu.at[:, 0].add(lam_bar[None] * h0)
    a = jnp.broadcast_to(lam_bar, bu.shape)

    def comb(e1, e2):
        a1, b1 = e1
        a2, b2 = e2
        return a1 * a2, a2 * b1 + b2

    _, h = lax.associative_scan(comb, (a, bu), axis=1)
    hf = h[:, -1]
    if reverse:
        h = h[:, ::-1]
    return h, hf


def _s5_mix(u, P, l, h0):
    f32 = jnp.float32
    B_, L, _ = u.shape
    u32 = u.astype(f32).reshape(B_, L, SSM_GROUPS, SSM_CH)
    y = P['ssm_d'][l].astype(f32).reshape(SSM_GROUPS, SSM_CH) * u32
    finals = []
    for d in range(2):
        lam = lax.complex(P['ssm_a_re'][l, d].astype(f32), P['ssm_a_im'][l, d].astype(f32))
        dt = jnp.exp(P['ssm_log_dt'][l, d].astype(f32))[:, None]
        lam_bar = jnp.exp(lam * dt)
        bm = lax.complex(P['ssm_b_re'][l, d].astype(f32), P['ssm_b_im'][l, d].astype(f32))
        b_bar = ((lam_bar - 1.0) / lam)[..., None] * bm
        cm = lax.complex(P['ssm_c_re'][l, d].astype(f32), P['ssm_c_im'][l, d].astype(f32))
        h, hf = _diag_scan(u32, lam_bar, b_bar, h0[:, d], reverse=(d == 1))
        y = y + jnp.real(jnp.einsum('blgp,gcp->blgc', h, cm))
        finals.append(hf)
    z = jax.nn.gelu(y.reshape(B_, L, SSM_W))
    g = z @ P['ssm_glu_w'][l].astype(f32)
    out = g[..., :SSM_W] * jax.nn.sigmoid(g[..., SSM_W:])
    return out.astype(u.dtype), jnp.stack(finals, axis=1)


def _pool_mix(u, w, scale):
    f32 = jnp.float32
    B_, L, _ = u.shape
    u32 = u.astype(f32).reshape(B_, L, POOL_GROUPS, POOL_CH)
    cs = jnp.concatenate([jnp.zeros((B_, 1, POOL_GROUPS, POOL_CH), f32), jnp.cumsum(u32, axis=1)], axis=1)
    t = np.arange(L)
    outs = []
    for gi, win in enumerate(POOL_WINDOWS):
        lo = np.clip(t - win // 2, 0, L)
        hi = np.clip(t + win // 2, 0, L)
        cnt = jnp.asarray((hi - lo).astype(np.float32))[None, :, None]
        csg = cs[:, :, gi]
        outs.append((csg[:, hi] - csg[:, lo]) / cnt - u32[:, :, gi])
    pooled = jnp.stack(outs, axis=2)
    y = jnp.einsum('blgc,gcd->blgd', pooled, w.astype(f32))
    y = y * scale.astype(f32).reshape(POOL_GROUPS, POOL_CH)
    return y.reshape(B_, L, POOL_W).astype(u.dtype)


def _token_mix(h, P, l, k_ctx, v_ctx, h0, latent):
    B_, L, _ = h.shape
    p = h @ P['w_in'][l]
    o1 = ATTN_W
    o2 = o1 + KV_W
    o3 = o2 + KV_W
    o4 = o3 + SSM_W
    q = p[..., :o1].reshape(B_, L, N_HEADS, HEAD_DIM)
    k = p[..., o1:o2].reshape(B_, L, N_KV_HEADS, HEAD_DIM)
    v = p[..., o2:o3].reshape(B_, L, N_KV_HEADS, HEAD_DIM)
    u_ssm = p[..., o3:o4]
    u_pool = p[..., o4:]
    q = _rmsnorm(q, P['q_norm_g'][l])
    k = _rmsnorm(k, P['k_norm_g'][l])
    if latent:
        q_att = _rope_2d(q)
        k_att = jnp.concatenate([_rope_2d(k), k_ctx.astype(k.dtype)], axis=1)
        v_att = jnp.concatenate([v, v_ctx.astype(v.dtype)], axis=1)
    else:
        q_att, k_att, v_att = q, k, v
    attn = _attend(q_att.reshape(B_, L, N_KV_HEADS, KV_GROUP, HEAD_DIM), k_att, v_att)
    ssm, hf = _s5_mix(u_ssm, P, l, h0)
    pool = _pool_mix(u_pool, P['pool_w'][l], P['pool_scale'][l])
    o = jnp.concatenate([attn, ssm, pool], axis=-1) @ P['w_out'][l]
    return o, k, v, hf


def _swiglu(h, wg, wu, wd):
    return (jax.nn.silu(h @ wg) * (h @ wu)) @ wd


def _moe(h, wr, br, wg, wu, wd):
    f32 = jnp.float32
    logits = h.astype(f32) @ wr.astype(f32) + br.astype(f32)
    vals, idx = lax.top_k(logits, TOP_K)
    wts = jax.nn.softmax(vals, axis=-1)
    gates = jnp.sum(jax.nn.one_hot(idx, N_EXPERTS, dtype=f32) * wts[..., None], axis=-2)
    a = jax.nn.silu(jnp.einsum('bld,edf->blef', h, wg)) * jnp.einsum('bld,edf->blef', h, wu)
    a = a * gates[..., None].astype(a.dtype)
    return jnp.einsum('blef,efd->bld', a, wd)


def _layer(x, mods, P, l, k_ctx, v_ctx, h0, latent):
    f32 = jnp.float32
    sh1, sc1, g1, sh2, sc2, g2 = mods
    h = _ada(x, P['norm1_g'][l], sh1, sc1)
    o, k, v, hf = _token_mix(h, P, l, k_ctx, v_ctx, h0, latent)
    x = x + (g1 * o.astype(f32)).astype(x.dtype)
    h2 = _ada(x, P['norm2_g'][l], sh2, sc2)
    i = l // 2
    if l % 2 == 0:
        f = _swiglu(h2, P['ffn_w_gate'][i], P['ffn_w_up'][i], P['ffn_w_down'][i])
    else:
        f = _moe(h2, P['moe_router_w'][i], P['moe_router_b'][i], P['moe_w_gate'][i],
                 P['moe_w_up'][i], P['moe_w_down'][i])
    x = x + (g2 * f.astype(f32)).astype(x.dtype)
    return x, k, v, hf


def setup_inputs(seed: int = 0) -> dict:
    key = jax.random.key(seed)
    ks = iter(jax.random.split(key, 48))
    f32 = jnp.float32

    def nrm(shape, s):
        return jax.random.normal(next(ks), shape, f32) * s

    G, Pn, CH = SSM_GROUPS, SSM_STATE, SSM_CH
    x_prompt = nrm((BATCH, SEQ, D_MODEL), 1.0)
    x_sample = nrm((DEC_BATCH, DEC_SEQ, D_MODEL), 1.0)
    c = nrm((DEC_BATCH, D_MODEL), 1.0)
    cache_k = nrm((DEC_BATCH, DEPTH, PAST_LEN, N_KV_HEADS, HEAD_DIM), 1.0)
    cache_v = nrm((DEC_BATCH, DEPTH, PAST_LEN, N_KV_HEADS, HEAD_DIM), 1.0)
    state_ssm_re = nrm((DEC_BATCH, DEPTH, 2, G, Pn), 0.1)
    state_ssm_im = nrm((DEC_BATCH, DEPTH, 2, G, Pn), 0.1)
    c_ctx = nrm((D_MODEL,), 1.0)
    mod_w = nrm((DEPTH, D_MODEL, 6 * D_MODEL), 0.5 * D_MODEL ** -0.5)
    mod_b = nrm((DEPTH, 6 * D_MODEL), 0.01)
    norm1_g = 1.0 + nrm((DEPTH, D_MODEL), 0.1)
    norm2_g = 1.0 + nrm((DEPTH, D_MODEL), 0.1)
    w_in = nrm((DEPTH, D_MODEL, IN_COLS), D_MODEL ** -0.5)
    w_out = nrm((DEPTH, MIX_W, D_MODEL), MIX_W ** -0.5)
    q_norm_g = 1.0 + nrm((DEPTH, HEAD_DIM), 0.1)
    k_norm_g = 1.0 + nrm((DEPTH, HEAD_DIM), 0.1)
    ssm_a_re = -0.5 + nrm((DEPTH, 2, G, Pn), 0.02)
    ssm_a_im = jnp.pi * jnp.arange(Pn, dtype=f32) + nrm((DEPTH, 2, G, Pn), 0.02)
    ssm_log_dt = jax.random.uniform(next(ks), (DEPTH, 2, G), f32, math.log(1e-3), math.log(1e-1))
    ssm_b_re = nrm((DEPTH, 2, G, Pn, CH), (2 * CH) ** -0.5)
    ssm_b_im = nrm((DEPTH, 2, G, Pn, CH), (2 * CH) ** -0.5)
    ssm_c_re = nrm((DEPTH, 2, G, CH, Pn), Pn ** -0.5)
    ssm_c_im = nrm((DEPTH, 2, G, CH, Pn), Pn ** -0.5)
    ssm_d = nrm((DEPTH, SSM_W), 1.0)
    ssm_glu_w = nrm((DEPTH, SSM_W, 2 * SSM_W), SSM_W ** -0.5)
    pool_w = nrm((DEPTH, POOL_GROUPS, POOL_CH, POOL_CH), POOL_CH ** -0.5)
    pool_scale = 1.0 + nrm((DEPTH, POOL_W), 0.1)
    ffn_w_gate = nrm((N_DENSE, D_MODEL, FF_DENSE), D_MODEL ** -0.5)
    ffn_w_up = nrm((N_DENSE, D_MODEL, FF_DENSE), D_MODEL ** -0.5)
    ffn_w_down = nrm((N_DENSE, FF_DENSE, D_MODEL), FF_DENSE ** -0.5)
    moe_router_w = nrm((N_MOE, D_MODEL, N_EXPERTS), D_MODEL ** -0.5)
    moe_router_b = nrm((N_MOE, N_EXPERTS), 0.01)
    moe_w_gate = nrm((N_MOE, N_EXPERTS, D_MODEL, FF_EXPERT), D_MODEL ** -0.5)
    moe_w_up = nrm((N_MOE, N_EXPERTS, D_MODEL, FF_EXPERT), D_MODEL ** -0.5)
    moe_w_down = nrm((N_MOE, N_EXPERTS, FF_EXPERT, D_MODEL), FF_EXPERT ** -0.5)
    final_g = 1.0 + nrm((D_MODEL,), 0.1)
    return {'x_prompt': x_prompt, 'x_sample': x_sample, 'c': c, 'cache_k': cache_k, 'cache_v': cache_v,
            'state_ssm_re': state_ssm_re, 'state_ssm_im': state_ssm_im, 'c_ctx': c_ctx,
            'mod_w': mod_w, 'mod_b': mod_b, 'norm1_g': norm1_g, 'norm2_g': norm2_g,
            'w_in': w_in, 'w_out': w_out, 'q_norm_g': q_norm_g, 'k_norm_g': k_norm_g,
            'ssm_a_re': ssm_a_re, 'ssm_a_im': ssm_a_im, 'ssm_log_dt': ssm_log_dt,
            'ssm_b_re': ssm_b_re, 'ssm_b_im': ssm_b_im, 'ssm_c_re': ssm_c_re, 'ssm_c_im': ssm_c_im,
            'ssm_d': ssm_d, 'ssm_glu_w': ssm_glu_w, 'pool_w': pool_w, 'pool_scale': pool_scale,
            'ffn_w_gate': ffn_w_gate, 'ffn_w_up': ffn_w_up, 'ffn_w_down': ffn_w_down,
            'moe_router_w': moe_router_w, 'moe_router_b': moe_router_b, 'moe_w_gate': moe_w_gate,
            'moe_w_up': moe_w_up, 'moe_w_down': moe_w_down, 'final_g': final_g}


def reference(x_prompt, x_sample, c, cache_k, cache_v, state_ssm_re, state_ssm_im, c_ctx,
              mod_w, mod_b, norm1_g, norm2_g, w_in, w_out, q_norm_g, k_norm_g,
              ssm_a_re, ssm_a_im, ssm_log_dt, ssm_b_re, ssm_b_im, ssm_c_re, ssm_c_im,
              ssm_d, ssm_glu_w, pool_w, pool_scale, ffn_w_gate, ffn_w_up, ffn_w_down,
              moe_router_w, moe_router_b, moe_w_gate, moe_w_up, moe_w_down, final_g):
    f32 = jnp.float32
    P = {'w_in': w_in, 'w_out': w_out, 'norm1_g': norm1_g, 'norm2_g': norm2_g,
         'q_norm_g': q_norm_g, 'k_norm_g': k_norm_g,
         'ssm_a_re': ssm_a_re, 'ssm_a_im': ssm_a_im, 'ssm_log_dt': ssm_log_dt,
         'ssm_b_re': ssm_b_re, 'ssm_b_im': ssm_b_im, 'ssm_c_re': ssm_c_re, 'ssm_c_im': ssm_c_im,
         'ssm_d': ssm_d, 'ssm_glu_w': ssm_glu_w, 'pool_w': pool_w, 'pool_scale': pool_scale,
         'ffn_w_gate': ffn_w_gate, 'ffn_w_up': ffn_w_up, 'ffn_w_down': ffn_w_down,
         'moe_router_w': moe_router_w, 'moe_router_b': moe_router_b,
         'moe_w_gate': moe_w_gate, 'moe_w_up': moe_w_up, 'moe_w_down': moe_w_down}
    xp = x_prompt
    xs = x_sample
    h0_ctx = jnp.zeros((xp.shape[0], 2, SSM_GROUPS, SSM_STATE), jnp.complex64)
    ks, vs, s_re, s_im = [], [], [], []
    for l in range(DEPTH):
        mods_p = _modulation(c_ctx[None, :], mod_w[l], mod_b[l])
        mods_s = _modulation(c, mod_w[l], mod_b[l])
        xp, k_l, v_l, hf = _layer(xp, mods_p, P, l, None, None, h0_ctx, False)
        ks.append(k_l)
        vs.append(v_l)
        s_re.append(jnp.real(hf))
        s_im.append(jnp.imag(hf))
        h0_lat = lax.complex(state_ssm_re[:, l].astype(f32), state_ssm_im[:, l].astype(f32))
        xs, _, _, _ = _layer(xs, mods_s, P, l, cache_k[:, l], cache_v[:, l], h0_lat, True)
    y_prompt = _rmsnorm(xp, final_g)
    y_sample = _rmsnorm(xs, final_g)
    new_cache_k = jnp.stack(ks, axis=1)
    new_cache_v = jnp.stack(vs, axis=1)
    new_state_ssm_re = jnp.stack(s_re, axis=1)
    new_state_ssm_im = jnp.stack(s_im, axis=1)
    return (y_prompt, y_sample, new_cache_k, new_cache_v, new_state_ssm_re, new_state_ssm_im)
```

```python
import functools

import numpy as np
import jax
import jax.numpy as jnp
from jax import lax
from jax.experimental import pallas as pl
from jax.experimental.pallas import tpu as pltpu

D_MODEL = 1024
DEPTH = 2
GRID_W = 64
ATTN_W = 512
HEAD_DIM = 64
N_HEADS = 8
N_KV_HEADS = 2
KV_W = 128
ROPE_THETA = 10000.0
SSM_W = 256
SSM_CH = 16
SSM_GROUPS = 16
SSM_STATE = 64
SSM_GP = SSM_GROUPS * SSM_STATE
POOL_W = 256
POOL_WINDOWS = (2, 4, 8, 16)
POOL_CH = 64
POOL_PAD = max(POOL_WINDOWS) // 2
IN_COLS = ATTN_W + 2 * KV_W + SSM_W + POOL_W
FF_DENSE = 2816
N_EXPERTS = 8
FF_EXPERT = 1408
EPS = 1e-6

SEQ_GROUP = 8
SSM_CHUNK = 32
SSM_COLBLK = 512
MOD_ROWS = 16
CTX_ROW = 8
VMEM_LIMIT = 56 * 1024 * 1024

F32 = jnp.float32
BF16 = jnp.bfloat16


def _silu(x):
    return x * jax.nn.sigmoid(x)


def _rms(x32, g):
    return x32 * lax.rsqrt(jnp.mean(x32 * x32, axis=-1, keepdims=True) + EPS) * g


def _mod_kernel(cond_ref, w_ref, b_ref, o_ref):
    s = _silu(cond_ref[...])
    o_ref[...] = jnp.dot(s, w_ref[...], preferred_element_type=F32,
                         precision=lax.Precision.HIGHEST) + b_ref[...]


def _modulation(cond, mod_w, mod_b):
    tn = 1536
    n = 6 * D_MODEL
    return pl.pallas_call(
        _mod_kernel,
        out_shape=jax.ShapeDtypeStruct((DEPTH, MOD_ROWS, n), F32),
        grid=(DEPTH, n // tn),
        in_specs=[
            pl.BlockSpec((MOD_ROWS, D_MODEL), lambda l, j: (0, 0)),
            pl.BlockSpec((None, D_MODEL, tn), lambda l, j: (l, 0, j)),
            pl.BlockSpec((None, 1, tn), lambda l, j: (l, 0, j)),
        ],
        out_specs=pl.BlockSpec((None, MOD_ROWS, tn), lambda l, j: (l, 0, j)),
        compiler_params=pltpu.CompilerParams(
            dimension_semantics=("arbitrary", "arbitrary"), vmem_limit_bytes=VMEM_LIMIT),
        name="modulation",
    )(cond, mod_w, mod_b.reshape(DEPTH, 1, n))


def _ssm_prep_kernel(are_ref, aim_ref, dt_ref, bre_ref, bim_ref,
                     lre_ref, lim_ref, bbre_ref, bbim_ref):
    a_re = are_ref[...]
    a_im = aim_ref[...]
    dt = jnp.exp(dt_ref[...])
    mag = jnp.exp(a_re * dt)
    l_re = mag * jnp.cos(a_im * dt)
    l_im = mag * jnp.sin(a_im * dt)
    lre_ref[...] = l_re
    lim_ref[...] = l_im
    x = l_re - 1.0
    y = l_im
    den = a_re * a_re + a_im * a_im
    c_re = (x * a_re + y * a_im) / den
    c_im = (y * a_re - x * a_im) / den
    b_re = bre_ref[...]
    b_im = bim_ref[...]
    bbre_ref[...] = c_re[:, None, :] * b_re - c_im[:, None, :] * b_im
    bbim_ref[...] = c_re[:, None, :] * b_im + c_im[:, None, :] * b_re


def _ssm_prep(a_re, a_im, log_dt, b_re, b_im):
    ld = DEPTH * 2
    are = a_re.reshape(ld, SSM_GP)
    aim = a_im.reshape(ld, SSM_GP)
    dt = jnp.repeat(log_dt.reshape(ld, SSM_GROUPS), SSM_STATE, axis=1)
    bre = b_re.reshape(ld, SSM_GROUPS, SSM_STATE, SSM_CH).transpose(0, 3, 1, 2).reshape(ld, SSM_CH, SSM_GP)
    bim = b_im.reshape(ld, SSM_GROUPS, SSM_STATE, SSM_CH).transpose(0, 3, 1, 2).reshape(ld, SSM_CH, SSM_GP)
    return pl.pallas_call(
        _ssm_prep_kernel,
        out_shape=(jax.ShapeDtypeStruct((ld, SSM_GP), F32), jax.ShapeDtypeStruct((ld, SSM_GP), F32),
                   jax.ShapeDtypeStruct((ld, SSM_CH, SSM_GP), F32),
                   jax.ShapeDtypeStruct((ld, SSM_CH, SSM_GP), F32)),
        name="ssm_prep",
    )(are, aim, dt, bre, bim)


def _ssm_matrices(lam_re, lam_im, bb_re, bb_im, c_re, c_im):
    eye = jnp.eye(SSM_GROUPS, dtype=F32)

    def b_block(bb):
        bb = bb.reshape(2, SSM_CH, SSM_GROUPS, SSM_STATE)
        m = eye[None, :, None, :, None] * bb[:, None, :, :, :]
        return m.reshape(2, SSM_W, SSM_GP)

    def c_block(cc):
        m = eye[None, :, None, :, None] * cc.transpose(0, 1, 3, 2)[:, :, :, None, :]
        return m.reshape(2, SSM_GP, SSM_W)

    out = []
    for l in range(DEPTH):
        sl = slice(2 * l, 2 * l + 2)
        bw = jnp.concatenate([b_block(bb_re[sl]), b_block(bb_im[sl])], axis=2)
        bw = jnp.concatenate([bw[0], bw[1]], axis=1).astype(BF16)
        cw = jnp.concatenate([c_block(c_re[l]), -c_block(c_im[l])], axis=1)
        cw = jnp.concatenate([cw[0], cw[1]], axis=0).astype(BF16)
        lam = jnp.stack([lam_re[2 * l], lam_im[2 * l], lam_re[2 * l + 1], lam_im[2 * l + 1]], axis=0)
        lam = jnp.broadcast_to(lam[:, None, :], (4, SEQ_GROUP, SSM_GP))
        out.append((bw, cw, lam))
    return out


def _inproj_kernel(*refs, latent):
    if latent:
        (x_ref, mod_ref, g_ref, w_ref, qg_ref, kg_ref, sq_ref, sk_ref, cos_ref, sa_ref, sb_ref,
         q_ref, kd_ref, vd_ref, us_ref, up_ref) = refs
    else:
        (x_ref, mod_ref, g_ref, w_ref, qg_ref, kg_ref, sq_ref, sk_ref,
         q_ref, kd_ref, vd_ref, us_ref, up_ref, ko_ref, vo_ref) = refs
    x = x_ref[...]
    mod = mod_ref[...]
    h = _rms(x, g_ref[...]) * (1.0 + mod[1:2]) + mod[0:1]
    p = jnp.dot(h.astype(BF16), w_ref[...], preferred_element_type=F32)
    q = p[:, :ATTN_W]
    k = p[:, ATTN_W:ATTN_W + KV_W]
    v = p[:, ATTN_W + KV_W:ATTN_W + 2 * KV_W]
    q_ms = jnp.dot((q * q).astype(BF16), sq_ref[...], preferred_element_type=F32)
    k_ms = jnp.dot((k * k).astype(BF16), sk_ref[...], preferred_element_type=F32)
    q = q * lax.rsqrt(q_ms + EPS) * qg_ref[...]
    k = k * lax.rsqrt(k_ms + EPS) * kg_ref[...]
    if latent:
        cos = cos_ref[...]
        sa = sa_ref[...]
        sb = sb_ref[...]
        cos4 = jnp.concatenate([cos] * 4, axis=1)
        sa4 = jnp.concatenate([sa] * 4, axis=1)
        sb4 = jnp.concatenate([sb] * 4, axis=1)
        q = (q * cos4 + pltpu.roll(q, ATTN_W - 16, axis=1) * sa4 + pltpu.roll(q, 16, axis=1) * sb4)
        k = (k * cos + pltpu.roll(k, KV_W - 16, axis=1) * sa + pltpu.roll(k, 16, axis=1) * sb)
    else:
        ko_ref[...] = k
        vo_ref[...] = v
    q_ref[...] = (q * (HEAD_DIM ** -0.5)).astype(BF16)
    lane = lax.broadcasted_iota(jnp.int32, k.shape, 1)
    first = lane < HEAD_DIM
    k_sw = pltpu.roll(k, HEAD_DIM, axis=1)
    v_sw = pltpu.roll(v, HEAD_DIM, axis=1)
    kd_ref[:, :KV_W] = jnp.where(first, k, k_sw).astype(BF16)
    kd_ref[:, KV_W:] = jnp.where(first, k_sw, k).astype(BF16)
    vd_ref[:, :KV_W] = jnp.where(first, v, v_sw).astype(BF16)
    vd_ref[:, KV_W:] = jnp.where(first, v_sw, v).astype(BF16)
    o3 = ATTN_W + 2 * KV_W
    us_ref[...] = p[:, o3:o3 + SSM_W]
    up_ref[...] = p[:, o3 + SSM_W:]


def _head_mean_matrix(width):
    i = np.arange(width) // HEAD_DIM
    return jnp.asarray((i[:, None] == i[None, :]).astype(np.float32) / HEAD_DIM, dtype=BF16)


def _rope_tables(seq):
    t = np.arange(seq)
    row = (t // GRID_W).astype(np.float64)
    col = (t % GRID_W).astype(np.float64)
    nf = HEAD_DIM // 4
    inv = ROPE_THETA ** (-np.arange(nf, dtype=np.float64) / nf)
    ang_r = row[:, None] * inv[None, :]
    ang_c = col[:, None] * inv[None, :]
    cos = np.concatenate([np.cos(ang_r), np.cos(ang_r), np.cos(ang_c), np.cos(ang_c)], axis=1)
    sin = np.concatenate([np.sin(ang_r), np.sin(ang_r), np.sin(ang_c), np.sin(ang_c)], axis=1)
    lower = np.tile(np.concatenate([np.ones(nf), np.zeros(nf)]), 2)[None, :]
    sa = -sin * lower
    sb = sin * (1.0 - lower)
    tile = lambda a: jnp.asarray(np.tile(a, (1, 2)), dtype=F32)
    return tile(cos), tile(sa), tile(sb)


def _inproj(x, mods_l, g1, w_in_bf, qg, kg, latent):
    b, seq, _ = x.shape
    tr = 256
    nbg = b // SEQ_GROUP
    grid = (b, seq // tr)
    mod_row = (lambda i: i) if latent else (lambda i: CTX_ROW)
    const = lambda shape: pl.BlockSpec(shape, lambda i, t: (0,) * len(shape))
    in_specs = [
        pl.BlockSpec((None, tr, D_MODEL), lambda i, t: (i, t, 0)),
        pl.BlockSpec((None, 6, D_MODEL), lambda i, t: (mod_row(i), 0, 0)),
        const((1, D_MODEL)),
        const((D_MODEL, IN_COLS)),
        const((1, ATTN_W)),
        const((1, KV_W)),
        const((ATTN_W, ATTN_W)),
        const((KV_W, KV_W)),
    ]
    args = [x, mods_l, g1.reshape(1, D_MODEL), w_in_bf,
            jnp.tile(qg, N_HEADS).reshape(1, ATTN_W), jnp.tile(kg, N_KV_HEADS).reshape(1, KV_W),
            _head_mean_matrix(ATTN_W), _head_mean_matrix(KV_W)]
    out_shape = [
        jax.ShapeDtypeStruct((b, seq, ATTN_W), BF16),
        jax.ShapeDtypeStruct((b, seq, 2 * KV_W), BF16),
        jax.ShapeDtypeStruct((b, seq, 2 * KV_W), BF16),
        jax.ShapeDtypeStruct((nbg, seq, SEQ_GROUP * SSM_W), F32),
        jax.ShapeDtypeStruct((b, seq, POOL_W), F32),
    ]
    out_specs = [
        pl.BlockSpec((None, tr, ATTN_W), lambda i, t: (i, t, 0)),
        pl.BlockSpec((None, tr, 2 * KV_W), lambda i, t: (i, t, 0)),
        pl.BlockSpec((None, tr, 2 * KV_W), lambda i, t: (i, t, 0)),
        pl.BlockSpec((None, tr, SSM_W), lambda i, t: (i // SEQ_GROUP, t, i % SEQ_GROUP)),
        pl.BlockSpec((None, tr, POOL_W), lambda i, t: (i, t, 0)),
    ]
    if latent:
        cos, sa, sb = _rope_tables(seq)
        in_specs += [pl.BlockSpec((tr, KV_W), lambda i, t: (t, 0))] * 3
        args += [cos, sa, sb]
    else:
        out_shape += [jax.ShapeDtypeStruct((b, seq, KV_W), F32)] * 2
        out_specs += [pl.BlockSpec((None, tr, KV_W), lambda i, t: (i, t, 0))] * 2
    return pl.pallas_call(
        functools.partial(_inproj_kernel, latent=latent),
        out_shape=out_shape, grid=grid, in_specs=in_specs, out_specs=out_specs,
        compiler_params=pltpu.CompilerParams(
            dimension_semantics=("arbitrary", "arbitrary"), vmem_limit_bytes=VMEM_LIMIT),
        name="inproj_latent" if latent else "inproj_context",
    )(*args)


def _attn_kernel(*refs, n_kv_src):
    q_ref = refs[0]
    k_refs = refs[1:1 + n_kv_src]
    v_refs = refs[1 + n_kv_src:1 + 2 * n_kv_src]
    o_ref = refs[1 + 2 * n_kv_src]
    tq = q_ref.shape[0]
    lane = lax.broadcasted_iota(jnp.int32, (tq, KV_W), 1)
    first = lane < HEAD_DIM
    for pair in range(N_HEADS // 2):
        kv = pair // 2
        cols = slice(pair * KV_W, (pair + 1) * KV_W)
        kcols = slice(kv * KV_W, (kv + 1) * KV_W)
        qp = q_ref[:, cols]
        halves = []
        for half in range(2):
            qm = jnp.where(first if half == 0 else jnp.logical_not(first), qp, jnp.zeros_like(qp))
            scores = [lax.dot_general(qm, k_ref[:, kcols], (((1,), (1,)), ((), ())),
                                      preferred_element_type=F32) for k_ref in k_refs]
            m = scores[0].max(axis=-1, keepdims=True)
            for s in scores[1:]:
                m = jnp.maximum(m, s.max(axis=-1, keepdims=True))
            den = jnp.zeros((tq, 1), F32)
            acc = jnp.zeros((tq, KV_W), F32)
            for s, v_ref in zip(scores, v_refs):
                p = jnp.exp(s - m)
                den = den + p.sum(axis=-1, keepdims=True)
                acc = acc + jnp.dot(p.astype(BF16), v_ref[:, kcols], preferred_element_type=F32)
            halves.append(acc / den)
        o_ref[:, cols] = jnp.where(first, halves[0], halves[1]).astype(o_ref.dtype)


def _attention(q, kds, vds):
    b, seq, _ = q.shape
    tq = 256
    n_src = len(kds)
    kv_spec = lambda a: pl.BlockSpec((None, a.shape[1], 2 * KV_W), lambda i, t: (i, 0, 0))
    return pl.pallas_call(
        functools.partial(_attn_kernel, n_kv_src=n_src),
        out_shape=jax.ShapeDtypeStruct((b, seq, ATTN_W), BF16),
        grid=(b, seq // tq),
        in_specs=[pl.BlockSpec((None, tq, ATTN_W), lambda i, t: (i, t, 0))]
                 + [kv_spec(a) for a in kds] + [kv_spec(a) for a in vds],
        out_specs=pl.BlockSpec((None, tq, ATTN_W), lambda i, t: (i, t, 0)),
        compiler_params=pltpu.CompilerParams(
            dimension_semantics=("arbitrary", "arbitrary"), vmem_limit_bytes=VMEM_LIMIT),
        name="attention_%dsrc" % n_src,
    )(q, *kds, *vds)


def _ssm_kernel(u_ref, bw_ref, cw_ref, lam_ref, h0_ref, d_ref, glu_ref,
                y_ref, hf_ref, bu_scr, hs_scr, st_scr):
    rows = u_ref.shape[0]
    crow = SSM_CHUNK * SEQ_GROUP
    nchunks = rows // crow
    w2 = 2 * SSM_GP

    st_scr[...] = h0_ref[...]

    def init_rows(i, c):
        r = pl.multiple_of(i * crow, crow)
        y_ref[pl.ds(r, crow), :] = u_ref[pl.ds(r, crow), :] * d_ref[...]
        return c
    lax.fori_loop(0, nchunks, init_rows, 0)

    def chunk(i, c):
        starts = (pl.multiple_of(i * crow, crow), pl.multiple_of((nchunks - 1 - i) * crow, crow))
        for d in range(2):
            ub = u_ref[pl.ds(starts[d], crow), :].astype(BF16)
            bu_scr[d] = jnp.dot(ub, bw_ref[:, d * w2:(d + 1) * w2], preferred_element_type=F32)
        for d in range(2):
            for cb in range(SSM_GP // SSM_COLBLK):
                re_c = slice(cb * SSM_COLBLK, (cb + 1) * SSM_COLBLK)
                im_c = slice(SSM_GP + cb * SSM_COLBLK, SSM_GP + (cb + 1) * SSM_COLBLK)
                l_re = lam_ref[2 * d, :, re_c]
                l_im = lam_ref[2 * d + 1, :, re_c]
                h_re = st_scr[:, d * w2 + cb * SSM_COLBLK:d * w2 + (cb + 1) * SSM_COLBLK]
                h_im = st_scr[:, d * w2 + SSM_GP + cb * SSM_COLBLK:d * w2 + SSM_GP + (cb + 1) * SSM_COLBLK]
                for s in range(SSM_CHUNK):
                    t = s if d == 0 else SSM_CHUNK - 1 - s
                    r = slice(t * SEQ_GROUP, (t + 1) * SEQ_GROUP)
                    n_re = l_re * h_re - l_im * h_im + bu_scr[d, r, re_c]
                    n_im = l_re * h_im + l_im * h_re + bu_scr[d, r, im_c]
                    hs_scr[d, r, re_c] = n_re
                    hs_scr[d, r, im_c] = n_im
                    h_re, h_im = n_re, n_im
                st_scr[:, d * w2 + cb * SSM_COLBLK:d * w2 + (cb + 1) * SSM_COLBLK] = h_re
                st_scr[:, d * w2 + SSM_GP + cb * SSM_COLBLK:d * w2 + SSM_GP + (cb + 1) * SSM_COLBLK] = h_im
        for d in range(2):
            contrib = jnp.dot(hs_scr[d].astype(BF16), cw_ref[d * w2:(d + 1) * w2, :],
                              preferred_element_type=F32)
            y_ref[pl.ds(starts[d], crow), :] += contrib
        return c
    lax.fori_loop(0, nchunks, chunk, 0)

    hf_ref[...] = st_scr[...]

    erow = 1024

    def epilogue(i, c):
        r = pl.multiple_of(i * erow, erow)
        z = jax.nn.gelu(y_ref[pl.ds(r, erow), :])
        g = jnp.dot(z.astype(BF16), glu_ref[...], preferred_element_type=F32)
        y_ref[pl.ds(r, erow), :] = g[:, :SSM_W] * jax.nn.sigmoid(g[:, SSM_W:])
        return c
    lax.fori_loop(0, rows // erow, epilogue, 0)


def _ssm(u_tm, bw, cw, lam, h0, ssm_d, glu_bf):
    nbg, rows, _ = u_tm.shape
    crow = SSM_CHUNK * SEQ_GROUP
    const = lambda shape: pl.BlockSpec(shape, lambda i: (0,) * len(shape))
    return pl.pallas_call(
        _ssm_kernel,
        out_shape=(jax.ShapeDtypeStruct((nbg, rows, SSM_W), F32),
                   jax.ShapeDtypeStruct((nbg, SEQ_GROUP, 4 * SSM_GP), F32)),
        grid=(nbg,),
        in_specs=[
            pl.BlockSpec((None, rows, SSM_W), lambda i: (i, 0, 0)),
            const((SSM_W, 4 * SSM_GP)),
            const((4 * SSM_GP, SSM_W)),
            const((4, SEQ_GROUP, SSM_GP)),
            pl.BlockSpec((None, SEQ_GROUP, 4 * SSM_GP), lambda i: (i, 0, 0)),
            const((1, SSM_W)),
            const((SSM_W, 2 * SSM_W)),
        ],
        out_specs=(pl.BlockSpec((None, rows, SSM_W), lambda i: (i, 0, 0)),
                   pl.BlockSpec((None, SEQ_GROUP, 4 * SSM_GP), lambda i: (i, 0, 0))),
        scratch_shapes=[pltpu.VMEM((2, crow, 2 * SSM_GP), F32),
                        pltpu.VMEM((2, crow, 2 * SSM_GP), F32),
                        pltpu.VMEM((SEQ_GROUP, 4 * SSM_GP), F32)],
        compiler_params=pltpu.CompilerParams(
            dimension_semantics=("arbitrary",), vmem_limit_bytes=VMEM_LIMIT),
        name="ssm_scan",
    )(u_tm, bw, cw, lam, h0, ssm_d.reshape(1, SSM_W), glu_bf)


def _mixout_kernel(*refs, route):
    if route:
        (x_ref, at_ref, ss_ref, up_ref, mod_ref, invc_ref, pw_ref, ps_ref, wo_ref, g2_ref, wr_ref, br_ref,
         x1_ref, h2_ref, gate_ref) = refs
    else:
        (x_ref, at_ref, ss_ref, up_ref, mod_ref, invc_ref, pw_ref, ps_ref, wo_ref, g2_ref,
         x1_ref, h2_ref) = refs
    u = up_ref[...]
    seq = u.shape[0]
    zpad = jnp.zeros((POOL_PAD, POOL_W), F32)
    ue = jnp.concatenate([zpad, u, zpad], axis=0)
    n_ext = seq + 2 * POOL_PAD
    back = lambda a, k: pltpu.roll(a, k, axis=0)
    ahead = lambda a, k: pltpu.roll(a, n_ext - k, axis=0)
    w2 = ue + back(ue, 1)
    w4 = back(w2, 1) + ahead(w2, 1)
    w8 = back(w4, 2) + ahead(w4, 2)
    w16 = back(w8, 4) + ahead(w8, 4)
    grp = lax.broadcasted_iota(jnp.int32, ue.shape, 1) // POOL_CH
    win = jnp.where(grp == 0, w2, jnp.where(grp == 1, w4, jnp.where(grp == 2, w8, w16)))
    pooled = win[POOL_PAD:POOL_PAD + seq] * invc_ref[...] - u
    pool = jnp.dot(pooled.astype(BF16), pw_ref[...], preferred_element_type=F32) * ps_ref[...]
    mix = jnp.concatenate([at_ref[...], ss_ref[...].astype(BF16), pool.astype(BF16)], axis=1)
    o = jnp.dot(mix, wo_ref[...], preferred_element_type=F32)
    mod = mod_ref[...]
    x1 = x_ref[...] + mod[2:3] * o
    x1_ref[...] = x1
    h2 = _rms(x1, g2_ref[...]) * (1.0 + mod[4:5]) + mod[3:4]
    h2_ref[...] = h2.astype(BF16)
    if route:
        logits = jnp.dot(h2, wr_ref[...], preferred_element_type=F32,
                         precision=lax.Precision.HIGHEST) + br_ref[...]
        lane = lax.broadcasted_iota(jnp.int32, logits.shape, 1).astype(F32)
        neg = float(np.finfo(np.float32).min)
        far = float(2 * KV_W)
        logits = jnp.where(lane < N_EXPERTS, logits, neg)
        m1 = logits.max(axis=-1, keepdims=True)
        i1 = jnp.where(logits == m1, lane, far).min(axis=-1, keepdims=True)
        rest = jnp.where(lane == i1, neg, logits)
        m2 = rest.max(axis=-1, keepdims=True)
        i2 = jnp.where(rest == m2, lane, far).min(axis=-1, keepdims=True)
        e2 = jnp.exp(m2 - m1)
        den = 1.0 + e2
        gate_ref[...] = jnp.where(lane == i1, 1.0 / den, 0.0) + jnp.where(lane == i2, e2 / den, 0.0)


def _pool_inv_count(seq):
    t = np.arange(seq)
    cols = []
    for win in POOL_WINDOWS:
        lo = np.clip(t - win // 2, 0, seq)
        hi = np.clip(t + win // 2, 0, seq)
        cols.append(np.repeat((hi - lo).astype(np.float32)[:, None], POOL_CH, axis=1))
    return np.concatenate(cols, axis=1)


def _block_diag_pool(pool_w):
    eye = jnp.eye(len(POOL_WINDOWS), dtype=F32)
    m = eye[:, None, :, None] * pool_w[:, :, None, :]
    return m.reshape(POOL_W, POOL_W)


def _mixout(x, attn, ssm_tm, u_pool, mods_l, pool_w, pool_scale, w_out_bf, g2, latent, router):
    b, seq, _ = x.shape
    route = router is not None
    mod_row = (lambda i: i) if latent else (lambda i: CTX_ROW)
    const = lambda shape: pl.BlockSpec(shape, lambda i: (0,) * len(shape))
    cnt = _pool_inv_count(seq)
    in_specs = [
        pl.BlockSpec((None, seq, D_MODEL), lambda i: (i, 0, 0)),
        pl.BlockSpec((None, seq, ATTN_W), lambda i: (i, 0, 0)),
        pl.BlockSpec((None, seq, SSM_W), lambda i: (i // SEQ_GROUP, 0, i % SEQ_GROUP)),
        pl.BlockSpec((None, seq, POOL_W), lambda i: (i, 0, 0)),
        pl.BlockSpec((None, 6, D_MODEL), lambda i: (mod_row(i), 0, 0)),
        const((seq, POOL_W)),
        const((POOL_W, POOL_W)),
        const((1, POOL_W)),
        const((D_MODEL, D_MODEL)),
        const((1, D_MODEL)),
    ]
    args = [x, attn, ssm_tm, u_pool, mods_l, jnp.asarray(1.0 / cnt, dtype=F32),
            _block_diag_pool(pool_w).astype(BF16), pool_scale.reshape(1, POOL_W), w_out_bf,
            g2.reshape(1, D_MODEL)]
    out_shape = [jax.ShapeDtypeStruct((b, seq, D_MODEL), F32), jax.ShapeDtypeStruct((b, seq, D_MODEL), BF16)]
    out_specs = [pl.BlockSpec((None, seq, D_MODEL), lambda i: (i, 0, 0)),
                 pl.BlockSpec((None, seq, D_MODEL), lambda i: (i, 0, 0))]
    if route:
        wr, br = router
        in_specs += [const((D_MODEL, KV_W)), const((1, KV_W))]
        args += [jnp.pad(wr, ((0, 0), (0, KV_W - N_EXPERTS))),
                 jnp.pad(br, (0, KV_W - N_EXPERTS)).reshape(1, KV_W)]
        out_shape.append(jax.ShapeDtypeStruct((b, seq, KV_W), F32))
        out_specs.append(pl.BlockSpec((None, seq, KV_W), lambda i: (i, 0, 0)))
    return pl.pallas_call(
        functools.partial(_mixout_kernel, route=route),
        out_shape=out_shape, grid=(b,), in_specs=in_specs, out_specs=out_specs,
        compiler_params=pltpu.CompilerParams(
            dimension_semantics=("arbitrary",), vmem_limit_bytes=VMEM_LIMIT),
        name="mixout_%s%s" % ("latent" if latent else "context", "_route" if route else ""),
    )(*args)


def _ffn_kernel(*refs, gated, final):
    refs = list(refs)
    h_ref, x_ref, mod_ref, wg_ref, wu_ref, wd_ref = refs[:6]
    pos = 6
    gate_ref = fg_ref = None
    if gated:
        gate_ref = refs[pos]
        pos += 1
    if final:
        fg_ref = refs[pos]
        pos += 1
    o_ref, acc_ref = refs[pos], refs[pos + 1]
    e = pl.program_id(1)

    @pl.when(e == 0)
    def _():
        acc_ref[...] = jnp.zeros_like(acc_ref)

    h = h_ref[...]
    a = _silu(jnp.dot(h, wg_ref[...], preferred_element_type=F32)) * \
        jnp.dot(h, wu_ref[...], preferred_element_type=F32)
    if gated:
        gates = gate_ref[...]
        lane = lax.broadcasted_iota(jnp.int32, gates.shape, 1)
        a = a * jnp.where(lane == e, gates, 0.0).sum(axis=-1, keepdims=True)
    acc_ref[...] += jnp.dot(a.astype(BF16), wd_ref[...], preferred_element_type=F32)

    @pl.when(e == pl.num_programs(1) - 1)
    def _():
        y = x_ref[...] + mod_ref[5:6] * acc_ref[...]
        if final:
            y = _rms(y, fg_ref[...])
        o_ref[...] = y


def _ffn(h2, x1, mods_l, wg, wu, wd, gates, final_g, latent, seq):
    rows = h2.shape[0]
    tm = 512
    gated = gates is not None
    final = final_g is not None
    if gated:
        n_chunk, ff = wg.shape[0], wg.shape[2]
        w_specs = [pl.BlockSpec((None, D_MODEL, ff), lambda i, e: (e, 0, 0)),
                   pl.BlockSpec((None, D_MODEL, ff), lambda i, e: (e, 0, 0)),
                   pl.BlockSpec((None, ff, D_MODEL), lambda i, e: (e, 0, 0))]
    else:
        ff = FF_EXPERT
        n_chunk = wg.shape[1] // ff
        w_specs = [pl.BlockSpec((D_MODEL, ff), lambda i, e: (0, e)),
                   pl.BlockSpec((D_MODEL, ff), lambda i, e: (0, e)),
                   pl.BlockSpec((ff, D_MODEL), lambda i, e: (e, 0))]
    mod_row = (lambda i: (i * tm) // seq) if latent else (lambda i: CTX_ROW)
    in_specs = [pl.BlockSpec((tm, D_MODEL), lambda i, e: (i, 0)),
                pl.BlockSpec((tm, D_MODEL), lambda i, e: (i, 0)),
                pl.BlockSpec((None, 6, D_MODEL), lambda i, e: (mod_row(i), 0, 0))] + w_specs
    args = [h2, x1, mods_l, wg, wu, wd]
    if gated:
        in_specs.append(pl.BlockSpec((tm, KV_W), lambda i, e: (i, 0)))
        args.append(gates)
    if final:
        in_specs.append(pl.BlockSpec((1, D_MODEL), lambda i, e: (0, 0)))
        args.append(final_g.reshape(1, D_MODEL))
    return pl.pallas_call(
        functools.partial(_ffn_kernel, gated=gated, final=final),
        out_shape=jax.ShapeDtypeStruct((rows, D_MODEL), F32),
        grid=(rows // tm, n_chunk),
        in_specs=in_specs,
        out_specs=pl.BlockSpec((tm, D_MODEL), lambda i, e: (i, 0)),
        scratch_shapes=[pltpu.VMEM((tm, D_MODEL), F32)],
        compiler_params=pltpu.CompilerParams(
            dimension_semantics=("arbitrary", "arbitrary"), vmem_limit_bytes=VMEM_LIMIT),
        name="ffn_%s%s" % ("experts" if gated else "dense", "_final" if final else ""),
    )(*args)


def _dup_heads(a):
    b, n = a.shape[:2]
    return jnp.repeat(a, 2, axis=2).reshape(b, n, 2 * KV_W).astype(BF16)


def kernel(x_prompt, x_sample, c, cache_k, cache_v, state_ssm_re, state_ssm_im, c_ctx, mod_w, mod_b, norm1_g, norm2_g, w_in, w_out, q_norm_g, k_norm_g, ssm_a_re, ssm_a_im, ssm_log_dt, ssm_b_re, ssm_b_im, ssm_c_re, ssm_c_im, ssm_d, ssm_glu_w, pool_w, pool_scale, ffn_w_gate, ffn_w_up, ffn_w_down, moe_router_w, moe_router_b, moe_w_gate, moe_w_up, moe_w_down, final_g):
    bp, lp, _ = x_prompt.shape
    bs, ls, _ = x_sample.shape
    assert bs == SEQ_GROUP and bp % SEQ_GROUP == 0

    cond = jnp.zeros((MOD_ROWS, D_MODEL), F32).at[:bs].set(c).at[CTX_ROW].set(c_ctx)
    mods = _modulation(cond, mod_w, mod_b).reshape(DEPTH, MOD_ROWS, 6, D_MODEL)

    lam_re, lam_im, bb_re, bb_im = _ssm_prep(ssm_a_re, ssm_a_im, ssm_log_dt, ssm_b_re, ssm_b_im)
    ssm_mats = _ssm_matrices(lam_re, lam_im, bb_re, bb_im, ssm_c_re, ssm_c_im)

    xp, xs = x_prompt, x_sample
    new_k, new_v, new_state = [], [], []
    for l in range(DEPTH):
        mods_l = mods[l]
        w_in_bf = w_in[l].astype(BF16)
        w_out_bf = w_out[l].astype(BF16)
        glu_bf = ssm_glu_w[l].astype(BF16)
        bw, cw, lam = ssm_mats[l]
        moe = l % 2 == 1
        i = l // 2
        router = (moe_router_w[i], moe_router_b[i]) if moe else None
        if moe:
            wg, wu, wd = (moe_w_gate[i].astype(BF16), moe_w_up[i].astype(BF16), moe_w_down[i].astype(BF16))
        else:
            wg, wu, wd = (ffn_w_gate[i].astype(BF16), ffn_w_up[i].astype(BF16), ffn_w_down[i].astype(BF16))
        fin = final_g if l == DEPTH - 1 else None

        for latent in (False, True):
            x = xs if latent else xp
            b, seq, _ = x.shape
            nbg = b // SEQ_GROUP
            outs = _inproj(x, mods_l, norm1_g[l], w_in_bf, q_norm_g[l], k_norm_g[l], latent)
            q, kd, vd, u_ssm, u_pool = outs[:5]
            if latent:
                kds = [kd, _dup_heads(cache_k[:, l])]
                vds = [vd, _dup_heads(cache_v[:, l])]
                h0 = jnp.concatenate([
                    state_ssm_re[:, l, 0].reshape(b, SSM_GP), state_ssm_im[:, l, 0].reshape(b, SSM_GP),
                    state_ssm_re[:, l, 1].reshape(b, SSM_GP), state_ssm_im[:, l, 1].reshape(b, SSM_GP)],
                    axis=1).reshape(nbg, SEQ_GROUP, 4 * SSM_GP)
            else:
                kds, vds = [kd], [vd]
                new_k.append(outs[5])
                new_v.append(outs[6])
                h0 = jnp.zeros((nbg, SEQ_GROUP, 4 * SSM_GP), F32)
            attn = _attention(q, kds, vds)
            y_ssm, hf = _ssm(u_ssm.reshape(nbg, seq * SEQ_GROUP, SSM_W), bw, cw, lam, h0, ssm_d[l], glu_bf)
            if not latent:
                new_state.append(hf.reshape(b, 2, 2, SSM_GROUPS, SSM_STATE))
            res = _mixout(x, attn, y_ssm.reshape(nbg, seq, SEQ_GROUP * SSM_W), u_pool, mods_l,
                          pool_w[l], pool_scale[l], w_out_bf, norm2_g[l], latent, router)
            x1, h2 = res[0], res[1]
            gates = res[2].reshape(b * seq, KV_W) if moe else None
            y = _ffn(h2.reshape(b * seq, D_MODEL), x1.reshape(b * seq, D_MODEL), mods_l, wg, wu, wd,
                     gates, fin, latent, seq).reshape(b, seq, D_MODEL)
            if latent:
                xs = y
            else:
                xp = y

    new_cache_k = jnp.stack(new_k, axis=1).reshape(bp, DEPTH, lp, N_KV_HEADS, HEAD_DIM)
    new_cache_v = jnp.stack(new_v, axis=1).reshape(bp, DEPTH, lp, N_KV_HEADS, HEAD_DIM)
    st = jnp.stack(new_state, axis=1)
    return (xp, xs, new_cache_k, new_cache_v, st[:, :, :, 0], st[:, :, :, 1])
```

```python
import functools

import numpy as np
import jax
import jax.numpy as jnp
from jax import lax
from jax.experimental import pallas as pl
from jax.experimental.pallas import tpu as pltpu

D_MODEL = 1024
DEPTH = 2
GRID_W = 64
ATTN_W = 512
HEAD_DIM = 64
N_HEADS = 8
N_KV_HEADS = 2
KV_W = 128
ROPE_THETA = 10000.0
SSM_W = 256
SSM_CH = 16
SSM_GROUPS = 16
SSM_STATE = 64
SSM_GP = SSM_GROUPS * SSM_STATE
POOL_W = 256
POOL_WINDOWS = (2, 4, 8, 16)
POOL_CH = 64
POOL_PAD = max(POOL_WINDOWS) // 2
IN_COLS = ATTN_W + 2 * KV_W + SSM_W + POOL_W
FF_DENSE = 2816
N_EXPERTS = 8
FF_EXPERT = 1408
EPS = 1e-6

SEQ_GROUP = 8
SSM_CHUNK = 32
SSM_COLBLK = 512
ROUTE_W = 128
MOE_BLOCK = 2048
MOE_TILE = 128
MOD_ROWS = 16
CTX_ROW = 8
VMEM_LIMIT = 56 * 1024 * 1024

F32 = jnp.float32
BF16 = jnp.bfloat16


def _silu(x):
    return x * jax.nn.sigmoid(x)


def _rms(x32, g):
    return x32 * lax.rsqrt(jnp.mean(x32 * x32, axis=-1, keepdims=True) + EPS) * g


def _mod_kernel(cond_ref, w_ref, b_ref, o_ref):
    s = _silu(cond_ref[...])
    o_ref[...] = jnp.dot(s, w_ref[...], preferred_element_type=F32,
                         precision=lax.Precision.HIGHEST) + b_ref[...]


def _modulation(cond, mod_w, mod_b):
    tn = 1536
    n = 6 * D_MODEL
    return pl.pallas_call(
        _mod_kernel,
        out_shape=jax.ShapeDtypeStruct((DEPTH, MOD_ROWS, n), F32),
        grid=(DEPTH, n // tn),
        in_specs=[
            pl.BlockSpec((MOD_ROWS, D_MODEL), lambda l, j: (0, 0)),
            pl.BlockSpec((None, D_MODEL, tn), lambda l, j: (l, 0, j)),
            pl.BlockSpec((None, 1, tn), lambda l, j: (l, 0, j)),
        ],
        out_specs=pl.BlockSpec((None, MOD_ROWS, tn), lambda l, j: (l, 0, j)),
        compiler_params=pltpu.CompilerParams(
            dimension_semantics=("arbitrary", "arbitrary"), vmem_limit_bytes=VMEM_LIMIT),
        name="modulation",
    )(cond, mod_w, mod_b.reshape(DEPTH, 1, n))


def _ssm_prep_kernel(are_ref, aim_ref, dt_ref, bre_ref, bim_ref,
                     lre_ref, lim_ref, bbre_ref, bbim_ref):
    a_re = are_ref[...]
    a_im = aim_ref[...]
    dt = jnp.exp(dt_ref[...])
    mag = jnp.exp(a_re * dt)
    l_re = mag * jnp.cos(a_im * dt)
    l_im = mag * jnp.sin(a_im * dt)
    lre_ref[...] = l_re
    lim_ref[...] = l_im
    x = l_re - 1.0
    y = l_im
    den = a_re * a_re + a_im * a_im
    c_re = (x * a_re + y * a_im) / den
    c_im = (y * a_re - x * a_im) / den
    b_re = bre_ref[...]
    b_im = bim_ref[...]
    bbre_ref[...] = c_re[:, None, :] * b_re - c_im[:, None, :] * b_im
    bbim_ref[...] = c_re[:, None, :] * b_im + c_im[:, None, :] * b_re


def _ssm_prep(a_re, a_im, log_dt, b_re, b_im):
    ld = DEPTH * 2
    are = a_re.reshape(ld, SSM_GP)
    aim = a_im.reshape(ld, SSM_GP)
    dt = jnp.repeat(log_dt.reshape(ld, SSM_GROUPS), SSM_STATE, axis=1)
    bre = b_re.reshape(ld, SSM_GROUPS, SSM_STATE, SSM_CH).transpose(0, 3, 1, 2).reshape(ld, SSM_CH, SSM_GP)
    bim = b_im.reshape(ld, SSM_GROUPS, SSM_STATE, SSM_CH).transpose(0, 3, 1, 2).reshape(ld, SSM_CH, SSM_GP)
    return pl.pallas_call(
        _ssm_prep_kernel,
        out_shape=(jax.ShapeDtypeStruct((ld, SSM_GP), F32), jax.ShapeDtypeStruct((ld, SSM_GP), F32),
                   jax.ShapeDtypeStruct((ld, SSM_CH, SSM_GP), F32),
                   jax.ShapeDtypeStruct((ld, SSM_CH, SSM_GP), F32)),
        name="ssm_prep",
    )(are, aim, dt, bre, bim)


def _ssm_matrices(lam_re, lam_im, bb_re, bb_im, c_re, c_im):
    eye = jnp.eye(SSM_GROUPS, dtype=F32)

    def b_block(bb):
        bb = bb.reshape(2, SSM_CH, SSM_GROUPS, SSM_STATE)
        m = eye[None, :, None, :, None] * bb[:, None, :, :, :]
        return m.reshape(2, SSM_W, SSM_GP)

    def c_block(cc):
        m = eye[None, :, None, :, None] * cc.transpose(0, 1, 3, 2)[:, :, :, None, :]
        return m.reshape(2, SSM_GP, SSM_W)

    out = []
    for l in range(DEPTH):
        sl = slice(2 * l, 2 * l + 2)
        bw = jnp.concatenate([b_block(bb_re[sl]), b_block(bb_im[sl])], axis=2)
        bw = jnp.concatenate([bw[0], bw[1]], axis=1).astype(BF16)
        cw = jnp.concatenate([c_block(c_re[l]), -c_block(c_im[l])], axis=1)
        cw = jnp.concatenate([cw[0], cw[1]], axis=0).astype(BF16)
        lam = jnp.stack([lam_re[2 * l], lam_im[2 * l], lam_re[2 * l + 1], lam_im[2 * l + 1]], axis=0)
        lam = jnp.broadcast_to(lam[:, None, :], (4, SEQ_GROUP, SSM_GP))
        out.append((bw, cw, lam))
    return out


def _inproj_kernel(*refs, latent):
    if latent:
        (x_ref, mod_ref, g_ref, w_ref, qg_ref, kg_ref, sq_ref, sk_ref, cos_ref, sa_ref, sb_ref,
         q_ref, kd_ref, vd_ref, us_ref, up_ref) = refs
    else:
        (x_ref, mod_ref, g_ref, w_ref, qg_ref, kg_ref, sq_ref, sk_ref,
         q_ref, kd_ref, vd_ref, us_ref, up_ref, ko_ref, vo_ref) = refs
    x = x_ref[...]
    mod = mod_ref[...]
    h = _rms(x, g_ref[...]) * (1.0 + mod[1:2]) + mod[0:1]
    p = jnp.dot(h.astype(BF16), w_ref[...], preferred_element_type=F32)
    q = p[:, :ATTN_W]
    k = p[:, ATTN_W:ATTN_W + KV_W]
    v = p[:, ATTN_W + KV_W:ATTN_W + 2 * KV_W]
    q_ms = jnp.dot((q * q).astype(BF16), sq_ref[...], preferred_element_type=F32)
    k_ms = jnp.dot((k * k).astype(BF16), sk_ref[...], preferred_element_type=F32)
    q = q * lax.rsqrt(q_ms + EPS) * qg_ref[...]
    k = k * lax.rsqrt(k_ms + EPS) * kg_ref[...]
    if latent:
        cos = cos_ref[...]
        sa = sa_ref[...]
        sb = sb_ref[...]
        cos4 = jnp.concatenate([cos] * 4, axis=1)
        sa4 = jnp.concatenate([sa] * 4, axis=1)
        sb4 = jnp.concatenate([sb] * 4, axis=1)
        q = (q * cos4 + pltpu.roll(q, ATTN_W - 16, axis=1) * sa4 + pltpu.roll(q, 16, axis=1) * sb4)
        k = (k * cos + pltpu.roll(k, KV_W - 16, axis=1) * sa + pltpu.roll(k, 16, axis=1) * sb)
    else:
        ko_ref[...] = k
        vo_ref[...] = v
    q_ref[...] = (q * (HEAD_DIM ** -0.5)).astype(BF16)
    lane = lax.broadcasted_iota(jnp.int32, k.shape, 1)
    first = lane < HEAD_DIM
    k_sw = pltpu.roll(k, HEAD_DIM, axis=1)
    v_sw = pltpu.roll(v, HEAD_DIM, axis=1)
    kd_ref[:, :KV_W] = jnp.where(first, k, k_sw).astype(BF16)
    kd_ref[:, KV_W:] = jnp.where(first, k_sw, k).astype(BF16)
    vd_ref[:, :KV_W] = jnp.where(first, v, v_sw).astype(BF16)
    vd_ref[:, KV_W:] = jnp.where(first, v_sw, v).astype(BF16)
    o3 = ATTN_W + 2 * KV_W
    us_ref[...] = p[:, o3:o3 + SSM_W]
    up_ref[...] = p[:, o3 + SSM_W:]


def _head_mean_matrix(width):
    i = np.arange(width) // HEAD_DIM
    return jnp.asarray((i[:, None] == i[None, :]).astype(np.float32) / HEAD_DIM, dtype=BF16)


def _rope_tables(seq):
    t = np.arange(seq)
    row = (t // GRID_W).astype(np.float64)
    col = (t % GRID_W).astype(np.float64)
    nf = HEAD_DIM // 4
    inv = ROPE_THETA ** (-np.arange(nf, dtype=np.float64) / nf)
    ang_r = row[:, None] * inv[None, :]
    ang_c = col[:, None] * inv[None, :]
    cos = np.concatenate([np.cos(ang_r), np.cos(ang_r), np.cos(ang_c), np.cos(ang_c)], axis=1)
    sin = np.concatenate([np.sin(ang_r), np.sin(ang_r), np.sin(ang_c), np.sin(ang_c)], axis=1)
    lower = np.tile(np.concatenate([np.ones(nf), np.zeros(nf)]), 2)[None, :]
    sa = -sin * lower
    sb = sin * (1.0 - lower)
    tile = lambda a: jnp.asarray(np.tile(a, (1, 2)), dtype=F32)
    return tile(cos), tile(sa), tile(sb)


def _inproj(x, mods_l, g1, w_in_bf, qg, kg, latent):
    b, seq, _ = x.shape
    tr = 256
    nbg = b // SEQ_GROUP
    grid = (b, seq // tr)
    mod_row = (lambda i: i) if latent else (lambda i: CTX_ROW)
    const = lambda shape: pl.BlockSpec(shape, lambda i, t: (0,) * len(shape))
    in_specs = [
        pl.BlockSpec((None, tr, D_MODEL), lambda i, t: (i, t, 0)),
        pl.BlockSpec((None, 6, D_MODEL), lambda i, t: (mod_row(i), 0, 0)),
        const((1, D_MODEL)),
        const((D_MODEL, IN_COLS)),
        const((1, ATTN_W)),
        const((1, KV_W)),
        const((ATTN_W, ATTN_W)),
        const((KV_W, KV_W)),
    ]
    args = [x, mods_l, g1.reshape(1, D_MODEL), w_in_bf,
            jnp.tile(qg, N_HEADS).reshape(1, ATTN_W), jnp.tile(kg, N_KV_HEADS).reshape(1, KV_W),
            _head_mean_matrix(ATTN_W), _head_mean_matrix(KV_W)]
    out_shape = [
        jax.ShapeDtypeStruct((b, seq, ATTN_W), BF16),
        jax.ShapeDtypeStruct((b, seq, 2 * KV_W), BF16),
        jax.ShapeDtypeStruct((b, seq, 2 * KV_W), BF16),
        jax.ShapeDtypeStruct((nbg, seq, SEQ_GROUP * SSM_W), F32),
        jax.ShapeDtypeStruct((b, seq, POOL_W), F32),
    ]
    out_specs = [
        pl.BlockSpec((None, tr, ATTN_W), lambda i, t: (i, t, 0)),
        pl.BlockSpec((None, tr, 2 * KV_W), lambda i, t: (i, t, 0)),
        pl.BlockSpec((None, tr, 2 * KV_W), lambda i, t: (i, t, 0)),
        pl.BlockSpec((None, tr, SSM_W), lambda i, t: (i // SEQ_GROUP, t, i % SEQ_GROUP)),
        pl.BlockSpec((None, tr, POOL_W), lambda i, t: (i, t, 0)),
    ]
    if latent:
        cos, sa, sb = _rope_tables(seq)
        in_specs += [pl.BlockSpec((tr, KV_W), lambda i, t: (t, 0))] * 3
        args += [cos, sa, sb]
    else:
        out_shape += [jax.ShapeDtypeStruct((b, seq, KV_W), F32)] * 2
        out_specs += [pl.BlockSpec((None, tr, KV_W), lambda i, t: (i, t, 0))] * 2
    return pl.pallas_call(
        functools.partial(_inproj_kernel, latent=latent),
        out_shape=out_shape, grid=grid, in_specs=in_specs, out_specs=out_specs,
        compiler_params=pltpu.CompilerParams(
            dimension_semantics=("arbitrary", "arbitrary"), vmem_limit_bytes=VMEM_LIMIT),
        name="inproj_latent" if latent else "inproj_context",
    )(*args)


def _attn_kernel(*refs, n_kv_src):
    q_ref = refs[0]
    k_refs = refs[1:1 + n_kv_src]
    v_refs = refs[1 + n_kv_src:1 + 2 * n_kv_src]
    o_ref = refs[1 + 2 * n_kv_src]
    tq = q_ref.shape[0]
    lane = lax.broadcasted_iota(jnp.int32, (tq, KV_W), 1)
    first = lane < HEAD_DIM
    for pair in range(N_HEADS // 2):
        kv = pair // 2
        cols = slice(pair * KV_W, (pair + 1) * KV_W)
        kcols = slice(kv * KV_W, (kv + 1) * KV_W)
        qp = q_ref[:, cols]
        halves = []
        for half in range(2):
            qm = jnp.where(first if half == 0 else jnp.logical_not(first), qp, jnp.zeros_like(qp))
            scores = [lax.dot_general(qm, k_ref[:, kcols], (((1,), (1,)), ((), ())),
                                      preferred_element_type=F32) for k_ref in k_refs]
            m = scores[0].max(axis=-1, keepdims=True)
            for s in scores[1:]:
                m = jnp.maximum(m, s.max(axis=-1, keepdims=True))
            den = jnp.zeros((tq, 1), F32)
            acc = jnp.zeros((tq, KV_W), F32)
            for s, v_ref in zip(scores, v_refs):
                p = jnp.exp(s - m)
                den = den + p.sum(axis=-1, keepdims=True)
                acc = acc + jnp.dot(p.astype(BF16), v_ref[:, kcols], preferred_element_type=F32)
            halves.append(acc / den)
        o_ref[:, cols] = jnp.where(first, halves[0], halves[1]).astype(o_ref.dtype)


def _attention(q, kds, vds):
    b, seq, _ = q.shape
    tq = 256
    n_src = len(kds)
    kv_spec = lambda a: pl.BlockSpec((None, a.shape[1], 2 * KV_W), lambda i, t: (i, 0, 0))
    return pl.pallas_call(
        functools.partial(_attn_kernel, n_kv_src=n_src),
        out_shape=jax.ShapeDtypeStruct((b, seq, ATTN_W), BF16),
        grid=(b, seq // tq),
        in_specs=[pl.BlockSpec((None, tq, ATTN_W), lambda i, t: (i, t, 0))]
                 + [kv_spec(a) for a in kds] + [kv_spec(a) for a in vds],
        out_specs=pl.BlockSpec((None, tq, ATTN_W), lambda i, t: (i, t, 0)),
        compiler_params=pltpu.CompilerParams(
            dimension_semantics=("arbitrary", "arbitrary"), vmem_limit_bytes=VMEM_LIMIT),
        name="attention_%dsrc" % n_src,
    )(q, *kds, *vds)


def _ssm_kernel(u_ref, bw_ref, cw_ref, lam_ref, h0_ref, d_ref, glu_ref,
                y_ref, hf_ref, bu_scr, hs_scr, st_scr):
    rows = u_ref.shape[0]
    crow = SSM_CHUNK * SEQ_GROUP
    nchunks = rows // crow
    w2 = 2 * SSM_GP

    st_scr[...] = h0_ref[...]

    def init_rows(i, c):
        r = pl.multiple_of(i * crow, crow)
        y_ref[pl.ds(r, crow), :] = u_ref[pl.ds(r, crow), :] * d_ref[...]
        return c
    lax.fori_loop(0, nchunks, init_rows, 0)

    def chunk(i, c):
        starts = (pl.multiple_of(i * crow, crow), pl.multiple_of((nchunks - 1 - i) * crow, crow))
        for d in range(2):
            ub = u_ref[pl.ds(starts[d], crow), :].astype(BF16)
            bu_scr[d] = jnp.dot(ub, bw_ref[:, d * w2:(d + 1) * w2], preferred_element_type=F32)
        for d in range(2):
            for cb in range(SSM_GP // SSM_COLBLK):
                re_c = slice(cb * SSM_COLBLK, (cb + 1) * SSM_COLBLK)
                im_c = slice(SSM_GP + cb * SSM_COLBLK, SSM_GP + (cb + 1) * SSM_COLBLK)
                l_re = lam_ref[2 * d, :, re_c]
                l_im = lam_ref[2 * d + 1, :, re_c]
                h_re = st_scr[:, d * w2 + cb * SSM_COLBLK:d * w2 + (cb + 1) * SSM_COLBLK]
                h_im = st_scr[:, d * w2 + SSM_GP + cb * SSM_COLBLK:d * w2 + SSM_GP + (cb + 1) * SSM_COLBLK]
                for s in range(SSM_CHUNK):
                    t = s if d == 0 else SSM_CHUNK - 1 - s
                    r = slice(t * SEQ_GROUP, (t + 1) * SEQ_GROUP)
                    n_re = l_re * h_re - l_im * h_im + bu_scr[d, r, re_c]
                    n_im = l_re * h_im + l_im * h_re + bu_scr[d, r, im_c]
                    hs_scr[d, r, re_c] = n_re
                    hs_scr[d, r, im_c] = n_im
                    h_re, h_im = n_re, n_im
                st_scr[:, d * w2 + cb * SSM_COLBLK:d * w2 + (cb + 1) * SSM_COLBLK] = h_re
                st_scr[:, d * w2 + SSM_GP + cb * SSM_COLBLK:d * w2 + SSM_GP + (cb + 1) * SSM_COLBLK] = h_im
        for d in range(2):
            contrib = jnp.dot(hs_scr[d].astype(BF16), cw_ref[d * w2:(d + 1) * w2, :],
                              preferred_element_type=F32)
            y_ref[pl.ds(starts[d], crow), :] += contrib
        return c
    lax.fori_loop(0, nchunks, chunk, 0)

    hf_ref[...] = st_scr[...]

    erow = 1024

    def epilogue(i, c):
        r = pl.multiple_of(i * erow, erow)
        z = jax.nn.gelu(y_ref[pl.ds(r, erow), :])
        g = jnp.dot(z.astype(BF16), glu_ref[...], preferred_element_type=F32)
        y_ref[pl.ds(r, erow), :] = g[:, :SSM_W] * jax.nn.sigmoid(g[:, SSM_W:])
        return c
    lax.fori_loop(0, rows // erow, epilogue, 0)


def _ssm(u_tm, bw, cw, lam, h0, ssm_d, glu_bf):
    nbg, rows, _ = u_tm.shape
    crow = SSM_CHUNK * SEQ_GROUP
    const = lambda shape: pl.BlockSpec(shape, lambda i: (0,) * len(shape))
    return pl.pallas_call(
        _ssm_kernel,
        out_shape=(jax.ShapeDtypeStruct((nbg, rows, SSM_W), F32),
                   jax.ShapeDtypeStruct((nbg, SEQ_GROUP, 4 * SSM_GP), F32)),
        grid=(nbg,),
        in_specs=[
            pl.BlockSpec((None, rows, SSM_W), lambda i: (i, 0, 0)),
            const((SSM_W, 4 * SSM_GP)),
            const((4 * SSM_GP, SSM_W)),
            const((4, SEQ_GROUP, SSM_GP)),
            pl.BlockSpec((None, SEQ_GROUP, 4 * SSM_GP), lambda i: (i, 0, 0)),
            const((1, SSM_W)),
            const((SSM_W, 2 * SSM_W)),
        ],
        out_specs=(pl.BlockSpec((None, rows, SSM_W), lambda i: (i, 0, 0)),
                   pl.BlockSpec((None, SEQ_GROUP, 4 * SSM_GP), lambda i: (i, 0, 0))),
        scratch_shapes=[pltpu.VMEM((2, crow, 2 * SSM_GP), F32),
                        pltpu.VMEM((2, crow, 2 * SSM_GP), F32),
                        pltpu.VMEM((SEQ_GROUP, 4 * SSM_GP), F32)],
        compiler_params=pltpu.CompilerParams(
            dimension_semantics=("arbitrary",), vmem_limit_bytes=VMEM_LIMIT),
        name="ssm_scan",
    )(u_tm, bw, cw, lam, h0, ssm_d.reshape(1, SSM_W), glu_bf)


def _pack2(lo, hi):
    return pltpu.pack_elementwise([lo, hi], packed_dtype=BF16)


def _unpack2(w, index):
    return pltpu.unpack_elementwise(w, index=index, packed_dtype=BF16, unpacked_dtype=F32)


def _mixout_kernel(*refs, route):
    if route:
        (x_ref, at_ref, ss_ref, up_ref, mod_ref, invc_ref, pw_ref, ps_ref, wo_ref, g2_ref, wr_ref, br_ref, tri_ref,
         x1_ref, h2_ref, route_ref, cnt_ref) = refs
    else:
        (x_ref, at_ref, ss_ref, up_ref, mod_ref, invc_ref, pw_ref, ps_ref, wo_ref, g2_ref,
         x1_ref, h2_ref) = refs
    u = up_ref[...]
    seq = u.shape[0]
    zpad = jnp.zeros((POOL_PAD, POOL_W), F32)
    ue = jnp.concatenate([zpad, u, zpad], axis=0)
    n_ext = seq + 2 * POOL_PAD
    back = lambda a, k: pltpu.roll(a, k, axis=0)
    ahead = lambda a, k: pltpu.roll(a, n_ext - k, axis=0)
    w2 = ue + back(ue, 1)
    w4 = back(w2, 1) + ahead(w2, 1)
    w8 = back(w4, 2) + ahead(w4, 2)
    w16 = back(w8, 4) + ahead(w8, 4)
    grp = lax.broadcasted_iota(jnp.int32, ue.shape, 1) // POOL_CH
    win = jnp.where(grp == 0, w2, jnp.where(grp == 1, w4, jnp.where(grp == 2, w8, w16)))
    pooled = win[POOL_PAD:POOL_PAD + seq] * invc_ref[...] - u
    pool = jnp.dot(pooled.astype(BF16), pw_ref[...], preferred_element_type=F32) * ps_ref[...]
    mix = jnp.concatenate([at_ref[...], ss_ref[...].astype(BF16), pool.astype(BF16)], axis=1)
    o = jnp.dot(mix, wo_ref[...], preferred_element_type=F32)
    mod = mod_ref[...]
    x1 = x_ref[...] + mod[2:3] * o
    x1_ref[...] = x1
    h2 = _rms(x1, g2_ref[...]) * (1.0 + mod[4:5]) + mod[3:4]
    h2_ref[...] = h2.astype(BF16)
    if not route:
        return
    logits = jnp.dot(h2, wr_ref[...], preferred_element_type=F32,
                     precision=lax.Precision.HIGHEST) + br_ref[...]
    lane = lax.broadcasted_iota(jnp.int32, logits.shape, 1).astype(F32)
    neg = float(np.finfo(np.float32).min)
    far = float(ROUTE_W)
    logits = jnp.where(lane < N_EXPERTS, logits, neg)
    m1 = logits.max(axis=-1, keepdims=True)
    i1 = jnp.where(logits == m1, lane, far).min(axis=-1, keepdims=True)
    rest = jnp.where(lane == i1, neg, logits)
    m2 = rest.max(axis=-1, keepdims=True)
    i2 = jnp.where(rest == m2, lane, far).min(axis=-1, keepdims=True)
    e2 = jnp.exp(m2 - m1)
    den = 1.0 + e2
    sel = jnp.where(lane == i1, 1.0, jnp.where(lane == i2, 1.0, 0.0))
    rank = jnp.dot(tri_ref[...], sel.astype(BF16), preferred_element_type=F32)
    r1 = jnp.where(lane == i1, rank, 0.0).sum(axis=-1, keepdims=True)
    r2 = jnp.where(lane == i2, rank, 0.0).sum(axis=-1, keepdims=True)
    out = jnp.zeros_like(logits)
    for j, col in enumerate((1.0 / den, e2 / den, i1, i2, r1, r2)):
        out = jnp.where(lane == j, col, out)
    route_ref[...] = out
    cnt_ref[...] = jnp.broadcast_to(sel.sum(axis=0, keepdims=True), cnt_ref.shape)


def _pool_inv_count(seq):
    t = np.arange(seq)
    cols = []
    for win in POOL_WINDOWS:
        lo = np.clip(t - win // 2, 0, seq)
        hi = np.clip(t + win // 2, 0, seq)
        cols.append(np.repeat((hi - lo).astype(np.float32)[:, None], POOL_CH, axis=1))
    return np.concatenate(cols, axis=1)


def _block_diag_pool(pool_w):
    eye = jnp.eye(len(POOL_WINDOWS), dtype=F32)
    m = eye[:, None, :, None] * pool_w[:, :, None, :]
    return m.reshape(POOL_W, POOL_W)


def _mixout(x, attn, ssm_tm, u_pool, mods_l, pool_w, pool_scale, w_out_bf, g2, latent, router):
    b, seq, _ = x.shape
    route = router is not None
    mod_row = (lambda i: i) if latent else (lambda i: CTX_ROW)
    const = lambda shape: pl.BlockSpec(shape, lambda i: (0,) * len(shape))
    cnt = _pool_inv_count(seq)
    in_specs = [
        pl.BlockSpec((None, seq, D_MODEL), lambda i: (i, 0, 0)),
        pl.BlockSpec((None, seq, ATTN_W), lambda i: (i, 0, 0)),
        pl.BlockSpec((None, seq, SSM_W), lambda i: (i // SEQ_GROUP, 0, i % SEQ_GROUP)),
        pl.BlockSpec((None, seq, POOL_W), lambda i: (i, 0, 0)),
        pl.BlockSpec((None, 6, D_MODEL), lambda i: (mod_row(i), 0, 0)),
        const((seq, POOL_W)),
        const((POOL_W, POOL_W)),
        const((1, POOL_W)),
        const((D_MODEL, D_MODEL)),
        const((1, D_MODEL)),
    ]
    args = [x, attn, ssm_tm, u_pool, mods_l, jnp.asarray(1.0 / cnt, dtype=F32),
            _block_diag_pool(pool_w).astype(BF16), pool_scale.reshape(1, POOL_W), w_out_bf,
            g2.reshape(1, D_MODEL)]
    row_spec = lambda width: pl.BlockSpec((None, seq, width), lambda i: (i, 0, 0))
    if route:
        wr, br = router
        tri = np.tril(np.ones((seq, seq), np.float32), -1)
        in_specs += [const((D_MODEL, ROUTE_W)), const((1, ROUTE_W)), const((seq, seq))]
        args += [jnp.pad(wr, ((0, 0), (0, ROUTE_W - N_EXPERTS))),
                 jnp.pad(br, (0, ROUTE_W - N_EXPERTS)).reshape(1, ROUTE_W),
                 jnp.asarray(tri, dtype=BF16)]
        out_shape = [jax.ShapeDtypeStruct((b, seq, D_MODEL), F32),
                     jax.ShapeDtypeStruct((b, seq, D_MODEL), BF16),
                     jax.ShapeDtypeStruct((b, seq, ROUTE_W), F32),
                     jax.ShapeDtypeStruct((b, 8, ROUTE_W), F32)]
        out_specs = [row_spec(D_MODEL), row_spec(D_MODEL), row_spec(ROUTE_W),
                     pl.BlockSpec((None, 8, ROUTE_W), lambda i: (i, 0, 0))]
    else:
        out_shape = [jax.ShapeDtypeStruct((b, seq, D_MODEL), F32), jax.ShapeDtypeStruct((b, seq, D_MODEL), BF16)]
        out_specs = [row_spec(D_MODEL), row_spec(D_MODEL)]
    return pl.pallas_call(
        functools.partial(_mixout_kernel, route=route),
        out_shape=out_shape, grid=(b,), in_specs=in_specs, out_specs=out_specs,
        compiler_params=pltpu.CompilerParams(
            dimension_semantics=("arbitrary",), vmem_limit_bytes=VMEM_LIMIT),
        name="mixout_%s%s" % ("latent" if latent else "context", "_route" if route else ""),
    )(*args)


def _ffn_kernel(h_ref, x_ref, mod_ref, wg_ref, wu_ref, wd_ref, o_ref, acc_ref):
    e = pl.program_id(1)

    @pl.when(e == 0)
    def _():
        acc_ref[...] = jnp.zeros_like(acc_ref)

    h = h_ref[...]
    a = _silu(jnp.dot(h, wg_ref[...], preferred_element_type=F32)) * \
        jnp.dot(h, wu_ref[...], preferred_element_type=F32)
    acc_ref[...] += jnp.dot(a.astype(BF16), wd_ref[...], preferred_element_type=F32)

    @pl.when(e == pl.num_programs(1) - 1)
    def _():
        o_ref[...] = x_ref[...] + mod_ref[5:6] * acc_ref[...]


def _ffn(h2, x1, mods_l, wg, wu, wd, latent, seq):
    rows = h2.shape[0]
    tm = 512
    ff = FF_EXPERT
    n_chunk = wg.shape[1] // ff
    mod_row = (lambda i: (i * tm) // seq) if latent else (lambda i: CTX_ROW)
    return pl.pallas_call(
        _ffn_kernel,
        out_shape=jax.ShapeDtypeStruct((rows, D_MODEL), F32),
        grid=(rows // tm, n_chunk),
        in_specs=[pl.BlockSpec((tm, D_MODEL), lambda i, e: (i, 0)),
                  pl.BlockSpec((tm, D_MODEL), lambda i, e: (i, 0)),
                  pl.BlockSpec((None, 6, D_MODEL), lambda i, e: (mod_row(i), 0, 0)),
                  pl.BlockSpec((D_MODEL, ff), lambda i, e: (0, e)),
                  pl.BlockSpec((D_MODEL, ff), lambda i, e: (0, e)),
                  pl.BlockSpec((ff, D_MODEL), lambda i, e: (e, 0))],
        out_specs=pl.BlockSpec((tm, D_MODEL), lambda i, e: (i, 0)),
        scratch_shapes=[pltpu.VMEM((tm, D_MODEL), F32)],
        compiler_params=pltpu.CompilerParams(
            dimension_semantics=("arbitrary", "arbitrary"), vmem_limit_bytes=VMEM_LIMIT),
        name="ffn_dense",
    )(h2, x1, mods_l, wg, wu, wd)


def _route_tables(route, cnt, seq):
    rows = route.shape[0]
    nb = rows // MOE_BLOCK
    per_block = MOE_BLOCK // seq
    c = cnt[:, 0, :N_EXPERTS].astype(jnp.int32).reshape(nb, per_block, N_EXPERTS)
    before = jnp.cumsum(c, axis=1) - c
    total = c.sum(axis=1)
    aligned = (total + 7) // 8 * 8
    starts = jnp.cumsum(aligned, axis=1) - aligned
    base = jnp.repeat((starts[:, None, :] + before).reshape(nb * per_block, N_EXPERTS), seq, axis=0)
    expert = route[:, 2:4].astype(jnp.int32)
    rank = route[:, 4:6].astype(jnp.int32)
    hit = expert[:, :, None] == jnp.arange(N_EXPERTS, dtype=jnp.int32)[None, None, :]
    slot = jnp.where(hit, base[:, None, :], 0).sum(axis=-1) + rank
    slots = slot.reshape(nb, MOE_BLOCK, 2).transpose(0, 2, 1)
    return starts.reshape(-1), total.reshape(-1), slots


MOE_SLOTS = 2 * MOE_BLOCK
MOE_ROWS = MOE_SLOTS + 8 * N_EXPERTS + MOE_TILE


MOE_GROUP = 16


def _moe_kernel(starts_ref, counts_ref, slots_ref, h_ref, route_ref, wg_ref, wu_ref, wd_ref, f_ref,
                xs_ref, stage_ref):
    b = pl.program_id(0)
    e = pl.program_id(1)
    half = D_MODEL // 2

    @pl.when(e == 0)
    def _():
        xs_ref[MOE_SLOTS:, :] = jnp.zeros((MOE_ROWS - MOE_SLOTS, half), jnp.uint32)
        zero_row = jnp.zeros((1, half), jnp.uint32)
        for g in range(N_EXPERTS):
            end = starts_ref[b * N_EXPERTS + g] + counts_ref[b * N_EXPERTS + g]
            for r in range(7):
                xs_ref[pl.ds(end + r, 1), :] = zero_row

        def put(g, c):
            t0 = pl.multiple_of(g * MOE_GROUP, MOE_GROUP)
            hb = h_ref[pl.ds(t0, MOE_GROUP), :].astype(F32)
            stage_ref[:MOE_GROUP, :] = _pack2(hb[:, :half], hb[:, half:])
            for r in range(MOE_GROUP):
                row = stage_ref[r:r + 1, :]
                xs_ref[pl.ds(slots_ref[0, t0 + r], 1), :] = row
                xs_ref[pl.ds(slots_ref[1, t0 + r], 1), :] = row
            return c
        lax.fori_loop(0, MOE_BLOCK // MOE_GROUP, put, 0)

    start = starts_ref[b * N_EXPERTS + e]
    count = counts_ref[b * N_EXPERTS + e]

    def tile(j, c):
        s = pl.multiple_of(start + j * MOE_TILE, 8)
        xg = xs_ref[pl.ds(s, MOE_TILE), :]
        x_lo32 = _unpack2(xg, 0)
        x_hi32 = _unpack2(xg, 1)
        x_lo = x_lo32.astype(BF16)
        x_hi = x_hi32.astype(BF16)
        hg = (jnp.dot(x_lo, wg_ref[:half, :], preferred_element_type=F32)
              + jnp.dot(x_hi, wg_ref[half:, :], preferred_element_type=F32))
        hu = (jnp.dot(x_lo, wu_ref[:half, :], preferred_element_type=F32)
              + jnp.dot(x_hi, wu_ref[half:, :], preferred_element_type=F32))
        a = (_silu(hg) * hu).astype(BF16)
        y = jnp.dot(a, wd_ref[...], preferred_element_type=F32)
        valid = lax.broadcasted_iota(jnp.int32, (MOE_TILE, half), 0) < count - j * MOE_TILE
        xs_ref[pl.ds(s, MOE_TILE), :] = _pack2(jnp.where(valid, y[:, :half], x_lo32),
                                               jnp.where(valid, y[:, half:], x_hi32))
        return c
    lax.fori_loop(0, (count + MOE_TILE - 1) // MOE_TILE, tile, 0)

    @pl.when(e == pl.num_programs(1) - 1)
    def _():
        def take(g, c):
            t0 = pl.multiple_of(g * MOE_GROUP, MOE_GROUP)
            for r in range(MOE_GROUP):
                stage_ref[r:r + 1, :] = xs_ref[pl.ds(slots_ref[0, t0 + r], 1), :]
                stage_ref[MOE_GROUP + r:MOE_GROUP + r + 1, :] = xs_ref[pl.ds(slots_ref[1, t0 + r], 1), :]
            z0 = stage_ref[:MOE_GROUP, :]
            z1 = stage_ref[MOE_GROUP:, :]
            route = route_ref[pl.ds(t0, MOE_GROUP), :]
            w1 = route[:, 0:1]
            w2 = route[:, 1:2]
            f_ref[pl.ds(t0, MOE_GROUP), :half] = (w1 * _unpack2(z0, 0) + w2 * _unpack2(z1, 0)).astype(BF16)
            f_ref[pl.ds(t0, MOE_GROUP), half:] = (w1 * _unpack2(z0, 1) + w2 * _unpack2(z1, 1)).astype(BF16)
            return c
        lax.fori_loop(0, MOE_BLOCK // MOE_GROUP, take, 0)


def _moe_experts(h2, route, starts, counts, slots, wg, wu, wd):
    rows = h2.shape[0]
    nb = rows // MOE_BLOCK
    ff = wg.shape[2]
    grid_spec = pltpu.PrefetchScalarGridSpec(
        num_scalar_prefetch=2,
        grid=(nb, N_EXPERTS),
        in_specs=[
            pl.BlockSpec((None, 2, MOE_BLOCK), lambda b, e, st, ct: (b, 0, 0), memory_space=pltpu.SMEM),
            pl.BlockSpec((MOE_BLOCK, D_MODEL), lambda b, e, st, ct: (b, 0)),
            pl.BlockSpec((MOE_BLOCK, ROUTE_W), lambda b, e, st, ct: (b, 0)),
            pl.BlockSpec((None, D_MODEL, ff), lambda b, e, st, ct: (e, 0, 0)),
            pl.BlockSpec((None, D_MODEL, ff), lambda b, e, st, ct: (e, 0, 0)),
            pl.BlockSpec((None, ff, D_MODEL), lambda b, e, st, ct: (e, 0, 0)),
        ],
        out_specs=pl.BlockSpec((MOE_BLOCK, D_MODEL), lambda b, e, st, ct: (b, 0)),
        scratch_shapes=[pltpu.VMEM((MOE_ROWS, D_MODEL // 2), jnp.uint32),
                        pltpu.VMEM((2 * MOE_GROUP, D_MODEL // 2), jnp.uint32)],
    )
    return pl.pallas_call(
        _moe_kernel,
        out_shape=jax.ShapeDtypeStruct((rows, D_MODEL), BF16),
        grid_spec=grid_spec,
        compiler_params=pltpu.CompilerParams(
            dimension_semantics=("arbitrary", "arbitrary"), vmem_limit_bytes=VMEM_LIMIT),
        name="moe_experts",
    )(starts, counts, slots, h2, route, wg, wu, wd)


def _final_kernel(x_ref, f_ref, mod_ref, fg_ref, o_ref):
    y = x_ref[...] + mod_ref[5:6] * f_ref[...].astype(F32)
    o_ref[...] = _rms(y, fg_ref[...])


def _final(x1, f, mods_l, final_g, latent, seq):
    rows = x1.shape[0]
    tr = 512
    mod_row = (lambda i: (i * tr) // seq) if latent else (lambda i: CTX_ROW)
    return pl.pallas_call(
        _final_kernel,
        out_shape=jax.ShapeDtypeStruct((rows, D_MODEL), F32),
        grid=(rows // tr,),
        in_specs=[pl.BlockSpec((tr, D_MODEL), lambda i: (i, 0)),
                  pl.BlockSpec((tr, D_MODEL), lambda i: (i, 0)),
                  pl.BlockSpec((None, 6, D_MODEL), lambda i: (mod_row(i), 0, 0)),
                  pl.BlockSpec((1, D_MODEL), lambda i: (0, 0))],
        out_specs=pl.BlockSpec((tr, D_MODEL), lambda i: (i, 0)),
        compiler_params=pltpu.CompilerParams(
            dimension_semantics=("arbitrary",), vmem_limit_bytes=VMEM_LIMIT),
        name="final_norm",
    )(x1, f, mods_l, final_g.reshape(1, D_MODEL))


def _dup_heads(a):
    b, n = a.shape[:2]
    return jnp.repeat(a, 2, axis=2).reshape(b, n, 2 * KV_W).astype(BF16)


def kernel(x_prompt, x_sample, c, cache_k, cache_v, state_ssm_re, state_ssm_im, c_ctx, mod_w, mod_b, norm1_g, norm2_g, w_in, w_out, q_norm_g, k_norm_g, ssm_a_re, ssm_a_im, ssm_log_dt, ssm_b_re, ssm_b_im, ssm_c_re, ssm_c_im, ssm_d, ssm_glu_w, pool_w, pool_scale, ffn_w_gate, ffn_w_up, ffn_w_down, moe_router_w, moe_router_b, moe_w_gate, moe_w_up, moe_w_down, final_g):
    bp, lp, _ = x_prompt.shape
    bs, ls, _ = x_sample.shape
    assert bs == SEQ_GROUP and bp % SEQ_GROUP == 0

    cond = jnp.zeros((MOD_ROWS, D_MODEL), F32).at[:bs].set(c).at[CTX_ROW].set(c_ctx)
    mods = _modulation(cond, mod_w, mod_b).reshape(DEPTH, MOD_ROWS, 6, D_MODEL)

    lam_re, lam_im, bb_re, bb_im = _ssm_prep(ssm_a_re, ssm_a_im, ssm_log_dt, ssm_b_re, ssm_b_im)
    ssm_mats = _ssm_matrices(lam_re, lam_im, bb_re, bb_im, ssm_c_re, ssm_c_im)

    xp, xs = x_prompt, x_sample
    new_k, new_v, new_state = [], [], []
    for l in range(DEPTH):
        mods_l = mods[l]
        w_in_bf = w_in[l].astype(BF16)
        w_out_bf = w_out[l].astype(BF16)
        glu_bf = ssm_glu_w[l].astype(BF16)
        bw, cw, lam = ssm_mats[l]
        moe = l % 2 == 1
        i = l // 2
        router = (moe_router_w[i], moe_router_b[i]) if moe else None
        if moe:
            wg, wu, wd = (moe_w_gate[i].astype(BF16), moe_w_up[i].astype(BF16), moe_w_down[i].astype(BF16))
        else:
            wg, wu, wd = (ffn_w_gate[i].astype(BF16), ffn_w_up[i].astype(BF16), ffn_w_down[i].astype(BF16))
        for latent in (False, True):
            x = xs if latent else xp
            b, seq, _ = x.shape
            nbg = b // SEQ_GROUP
            outs = _inproj(x, mods_l, norm1_g[l], w_in_bf, q_norm_g[l], k_norm_g[l], latent)
            q, kd, vd, u_ssm, u_pool = outs[:5]
            if latent:
                kds = [kd, _dup_heads(cache_k[:, l])]
                vds = [vd, _dup_heads(cache_v[:, l])]
                h0 = jnp.concatenate([
                    state_ssm_re[:, l, 0].reshape(b, SSM_GP), state_ssm_im[:, l, 0].reshape(b, SSM_GP),
                    state_ssm_re[:, l, 1].reshape(b, SSM_GP), state_ssm_im[:, l, 1].reshape(b, SSM_GP)],
                    axis=1).reshape(nbg, SEQ_GROUP, 4 * SSM_GP)
            else:
                kds, vds = [kd], [vd]
                new_k.append(outs[5])
                new_v.append(outs[6])
                h0 = jnp.zeros((nbg, SEQ_GROUP, 4 * SSM_GP), F32)
            attn = _attention(q, kds, vds)
            y_ssm, hf = _ssm(u_ssm.reshape(nbg, seq * SEQ_GROUP, SSM_W), bw, cw, lam, h0, ssm_d[l], glu_bf)
            if not latent:
                new_state.append(hf.reshape(b, 2, 2, SSM_GROUPS, SSM_STATE))
            res = _mixout(x, attn, y_ssm.reshape(nbg, seq, SEQ_GROUP * SSM_W), u_pool, mods_l,
                          pool_w[l], pool_scale[l], w_out_bf, norm2_g[l], latent, router)
            rows = b * seq
            x1 = res[0].reshape(rows, D_MODEL)
            if moe:
                route = res[2].reshape(rows, ROUTE_W)
                starts, counts, slots = _route_tables(route, res[3], seq)
                f = _moe_experts(res[1].reshape(rows, D_MODEL), route, starts, counts, slots, wg, wu, wd)
                y = _final(x1, f, mods_l, final_g, latent, seq)
            else:
                y = _ffn(res[1].reshape(rows, D_MODEL), x1, mods_l, wg, wu, wd, latent, seq)
            y = y.reshape(b, seq, D_MODEL)
            if latent:
                xs = y
            else:
                xp = y

    new_cache_k = jnp.stack(new_k, axis=1).reshape(bp, DEPTH, lp, N_KV_HEADS, HEAD_DIM)
    new_cache_v = jnp.stack(new_v, axis=1).reshape(bp, DEPTH, lp, N_KV_HEADS, HEAD_DIM)
    st = jnp.stack(new_state, axis=1)
    return (xp, xs, new_cache_k, new_cache_v, st[:, :, :, 0], st[:, :, :, 1])
```

```python
import functools

import numpy as np
import jax
import jax.numpy as jnp
from jax import lax
from jax.experimental import pallas as pl
from jax.experimental.pallas import tpu as pltpu

D_MODEL = 1024
DEPTH = 2
GRID_W = 64
ATTN_W = 512
HEAD_DIM = 64
N_HEADS = 8
N_KV_HEADS = 2
KV_W = 128
ROPE_THETA = 10000.0
SSM_W = 256
SSM_CH = 16
SSM_GROUPS = 16
SSM_STATE = 64
SSM_GP = SSM_GROUPS * SSM_STATE
POOL_W = 256
POOL_WINDOWS = (2, 4, 8, 16)
POOL_CH = 64
POOL_PAD = max(POOL_WINDOWS) // 2
IN_COLS = ATTN_W + 2 * KV_W + SSM_W + POOL_W
FF_DENSE = 2816
N_EXPERTS = 8
FF_EXPERT = 1408
EPS = 1e-6
LOG2_E = 1.4426950408889634

SEQ_GROUP = 8
SSM_CHUNK = 32
SSM_COLBLK = 512
ROUTE_W = 128
MOE_BLOCK = 2048
MOE_TILE = 128
MOD_ROWS = 16
CTX_ROW = 8
VMEM_LIMIT = 56 * 1024 * 1024

F32 = jnp.float32
BF16 = jnp.bfloat16


def _silu(x):
    return x * jax.nn.sigmoid(x)


def _rms(x32, g):
    return x32 * lax.rsqrt(jnp.mean(x32 * x32, axis=-1, keepdims=True) + EPS) * g


def _mod_kernel(cond_ref, w_ref, b_ref, o_ref):
    s = _silu(cond_ref[...])
    o_ref[...] = jnp.dot(s, w_ref[...], preferred_element_type=F32,
                         precision=lax.Precision.HIGHEST) + b_ref[...]


def _modulation(cond, mod_w, mod_b):
    tn = 1536
    n = 6 * D_MODEL
    return pl.pallas_call(
        _mod_kernel,
        out_shape=jax.ShapeDtypeStruct((DEPTH, MOD_ROWS, n), F32),
        grid=(DEPTH, n // tn),
        in_specs=[
            pl.BlockSpec((MOD_ROWS, D_MODEL), lambda l, j: (0, 0)),
            pl.BlockSpec((None, D_MODEL, tn), lambda l, j: (l, 0, j)),
            pl.BlockSpec((None, 1, tn), lambda l, j: (l, 0, j)),
        ],
        out_specs=pl.BlockSpec((None, MOD_ROWS, tn), lambda l, j: (l, 0, j)),
        compiler_params=pltpu.CompilerParams(
            dimension_semantics=("arbitrary", "arbitrary"), vmem_limit_bytes=VMEM_LIMIT),
        name="modulation",
    )(cond, mod_w, mod_b.reshape(DEPTH, 1, n))


def _ssm_prep_kernel(are_ref, aim_ref, dt_ref, bre_ref, bim_ref,
                     lre_ref, lim_ref, bbre_ref, bbim_ref):
    a_re = are_ref[...]
    a_im = aim_ref[...]
    dt = jnp.exp(dt_ref[...])
    mag = jnp.exp(a_re * dt)
    l_re = mag * jnp.cos(a_im * dt)
    l_im = mag * jnp.sin(a_im * dt)
    lre_ref[...] = l_re
    lim_ref[...] = l_im
    x = l_re - 1.0
    y = l_im
    den = a_re * a_re + a_im * a_im
    c_re = (x * a_re + y * a_im) / den
    c_im = (y * a_re - x * a_im) / den
    b_re = bre_ref[...]
    b_im = bim_ref[...]
    bbre_ref[...] = c_re[:, None, :] * b_re - c_im[:, None, :] * b_im
    bbim_ref[...] = c_re[:, None, :] * b_im + c_im[:, None, :] * b_re


def _ssm_prep(a_re, a_im, log_dt, b_re, b_im):
    ld = DEPTH * 2
    are = a_re.reshape(ld, SSM_GP)
    aim = a_im.reshape(ld, SSM_GP)
    dt = jnp.repeat(log_dt.reshape(ld, SSM_GROUPS), SSM_STATE, axis=1)
    bre = b_re.reshape(ld, SSM_GROUPS, SSM_STATE, SSM_CH).transpose(0, 3, 1, 2).reshape(ld, SSM_CH, SSM_GP)
    bim = b_im.reshape(ld, SSM_GROUPS, SSM_STATE, SSM_CH).transpose(0, 3, 1, 2).reshape(ld, SSM_CH, SSM_GP)
    return pl.pallas_call(
        _ssm_prep_kernel,
        out_shape=(jax.ShapeDtypeStruct((ld, SSM_GP), F32), jax.ShapeDtypeStruct((ld, SSM_GP), F32),
                   jax.ShapeDtypeStruct((ld, SSM_CH, SSM_GP), F32),
                   jax.ShapeDtypeStruct((ld, SSM_CH, SSM_GP), F32)),
        name="ssm_prep",
    )(are, aim, dt, bre, bim)


def _ssm_matrices(lam_re, lam_im, bb_re, bb_im, c_re, c_im):
    eye = jnp.eye(SSM_GROUPS, dtype=F32)

    def b_block(bb):
        bb = bb.reshape(2, SSM_CH, SSM_GROUPS, SSM_STATE)
        m = eye[None, :, None, :, None] * bb[:, None, :, :, :]
        return m.reshape(2, SSM_W, SSM_GP)

    def c_block(cc):
        m = eye[None, :, None, :, None] * cc.transpose(0, 1, 3, 2)[:, :, :, None, :]
        return m.reshape(2, SSM_GP, SSM_W)

    out = []
    for l in range(DEPTH):
        sl = slice(2 * l, 2 * l + 2)
        bw = jnp.concatenate([b_block(bb_re[sl]), b_block(bb_im[sl])], axis=2)
        bw = jnp.concatenate([bw[0], bw[1]], axis=1).astype(BF16)
        cw = jnp.concatenate([c_block(c_re[l]), -c_block(c_im[l])], axis=1)
        cw = jnp.concatenate([cw[0], cw[1]], axis=0).astype(BF16)
        lam = jnp.stack([lam_re[2 * l], lam_im[2 * l], lam_re[2 * l + 1], lam_im[2 * l + 1]], axis=0)
        lam = jnp.broadcast_to(lam[:, None, :], (4, SEQ_GROUP, SSM_GP))
        out.append((bw, cw, lam))
    return out


def _inproj_kernel(*refs, latent):
    if latent:
        (x_ref, mod_ref, g_ref, w_ref, qg_ref, kg_ref, sq_ref, sk_ref, cos_ref, sa_ref, sb_ref,
         q_ref, kd_ref, vd_ref, us_ref, up_ref) = refs
    else:
        (x_ref, mod_ref, g_ref, w_ref, qg_ref, kg_ref, sq_ref, sk_ref,
         q_ref, kd_ref, vd_ref, us_ref, up_ref, ko_ref, vo_ref) = refs
    x = x_ref[...]
    mod = mod_ref[...]
    h = _rms(x, g_ref[...]) * (1.0 + mod[1:2]) + mod[0:1]
    p = jnp.dot(h.astype(BF16), w_ref[...], preferred_element_type=F32)
    q = p[:, :ATTN_W]
    k = p[:, ATTN_W:ATTN_W + KV_W]
    v = p[:, ATTN_W + KV_W:ATTN_W + 2 * KV_W]
    q_ms = jnp.dot((q * q).astype(BF16), sq_ref[...], preferred_element_type=F32)
    k_ms = jnp.dot((k * k).astype(BF16), sk_ref[...], preferred_element_type=F32)
    q = q * lax.rsqrt(q_ms + EPS) * qg_ref[...]
    k = k * lax.rsqrt(k_ms + EPS) * kg_ref[...]
    if latent:
        cos = cos_ref[...]
        sa = sa_ref[...]
        sb = sb_ref[...]
        cos4 = jnp.concatenate([cos] * 4, axis=1)
        sa4 = jnp.concatenate([sa] * 4, axis=1)
        sb4 = jnp.concatenate([sb] * 4, axis=1)
        q = (q * cos4 + pltpu.roll(q, ATTN_W - 16, axis=1) * sa4 + pltpu.roll(q, 16, axis=1) * sb4)
        k = (k * cos + pltpu.roll(k, KV_W - 16, axis=1) * sa + pltpu.roll(k, 16, axis=1) * sb)
    else:
        ko_ref[...] = k
        vo_ref[...] = v
    q_ref[...] = (q * (HEAD_DIM ** -0.5 * LOG2_E)).astype(BF16)
    lane = lax.broadcasted_iota(jnp.int32, k.shape, 1)
    first = lane < HEAD_DIM
    k_sw = pltpu.roll(k, HEAD_DIM, axis=1)
    v_sw = pltpu.roll(v, HEAD_DIM, axis=1)
    kd_ref[:, :KV_W] = jnp.where(first, k, k_sw).astype(BF16)
    kd_ref[:, KV_W:] = jnp.where(first, k_sw, k).astype(BF16)
    vd_ref[:, :KV_W] = jnp.where(first, v, 1.0).astype(BF16)
    vd_ref[:, KV_W:] = jnp.where(first, v_sw, 1.0).astype(BF16)
    o3 = ATTN_W + 2 * KV_W
    us_ref[0] = p[:, o3:o3 + SSM_W // 2]
    us_ref[1] = p[:, o3 + SSM_W // 2:o3 + SSM_W]
    up_ref[...] = p[:, o3 + SSM_W:]


def _head_mean_matrix(width):
    i = np.arange(width) // HEAD_DIM
    return jnp.asarray((i[:, None] == i[None, :]).astype(np.float32) / HEAD_DIM, dtype=BF16)


def _rope_tables(seq):
    t = np.arange(seq)
    row = (t // GRID_W).astype(np.float64)
    col = (t % GRID_W).astype(np.float64)
    nf = HEAD_DIM // 4
    inv = ROPE_THETA ** (-np.arange(nf, dtype=np.float64) / nf)
    ang_r = row[:, None] * inv[None, :]
    ang_c = col[:, None] * inv[None, :]
    cos = np.concatenate([np.cos(ang_r), np.cos(ang_r), np.cos(ang_c), np.cos(ang_c)], axis=1)
    sin = np.concatenate([np.sin(ang_r), np.sin(ang_r), np.sin(ang_c), np.sin(ang_c)], axis=1)
    lower = np.tile(np.concatenate([np.ones(nf), np.zeros(nf)]), 2)[None, :]
    sa = -sin * lower
    sb = sin * (1.0 - lower)
    tile = lambda a: jnp.asarray(np.tile(a, (1, 2)), dtype=F32)
    return tile(cos), tile(sa), tile(sb)


def _inproj(x, mods_l, g1, w_in_bf, qg, kg, latent, seq):
    rows = x.shape[0]
    tr = 512
    tiles_per_seq = max(seq // tr, 1)
    mod_row = (lambda i: (i * tr) // seq) if latent else (lambda i: CTX_ROW)
    const = lambda shape: pl.BlockSpec(shape, lambda i: (0,) * len(shape))
    row_spec = lambda width: pl.BlockSpec((tr, width), lambda i: (i, 0))
    in_specs = [
        row_spec(D_MODEL),
        pl.BlockSpec((None, 6, D_MODEL), lambda i: (mod_row(i), 0, 0)),
        const((1, D_MODEL)),
        const((D_MODEL, IN_COLS)),
        const((1, ATTN_W)),
        const((1, KV_W)),
        const((ATTN_W, ATTN_W)),
        const((KV_W, KV_W)),
    ]
    args = [x, mods_l, g1.reshape(1, D_MODEL), w_in_bf,
            jnp.tile(qg, N_HEADS).reshape(1, ATTN_W), jnp.tile(kg, N_KV_HEADS).reshape(1, KV_W),
            _head_mean_matrix(ATTN_W), _head_mean_matrix(KV_W)]
    widths = [(ATTN_W, BF16), (2 * KV_W, BF16), (2 * KV_W, BF16), (None, F32), (POOL_W, F32)]
    if latent:
        assert seq % tr == 0
        cos, sa, sb = _rope_tables(seq)
        in_specs += [pl.BlockSpec((tr, KV_W), lambda i: (i % tiles_per_seq, 0))] * 3
        args += [cos, sa, sb]
    else:
        widths += [(KV_W, F32), (KV_W, F32)]
    out_shape = [jax.ShapeDtypeStruct((rows, w) if w else (2, rows, SSM_W // 2), dt) for w, dt in widths]
    out_specs = [row_spec(w) if w else pl.BlockSpec((2, tr, SSM_W // 2), lambda i: (0, i, 0)) for w, _ in widths]
    return pl.pallas_call(
        functools.partial(_inproj_kernel, latent=latent),
        out_shape=out_shape, grid=(rows // tr,), in_specs=in_specs, out_specs=out_specs,
        compiler_params=pltpu.CompilerParams(
            dimension_semantics=("arbitrary",), vmem_limit_bytes=VMEM_LIMIT),
        name="inproj_latent" if latent else "inproj_context",
    )(*args)


def _attn_kernel(*refs, n_kv_src):
    q_ref = refs[0]
    k_refs = refs[1:1 + n_kv_src]
    v_refs = refs[1 + n_kv_src:1 + 2 * n_kv_src]
    o_ref = refs[1 + 2 * n_kv_src]
    tq = q_ref.shape[0]
    lane = lax.broadcasted_iota(jnp.int32, (tq, KV_W), 1)
    first = lane < HEAD_DIM
    for pair in range(N_HEADS // 2):
        kv = pair // 2
        cols = slice(pair * KV_W, (pair + 1) * KV_W)
        kcols = slice(kv * KV_W, (kv + 1) * KV_W)
        qp = q_ref[:, cols]
        halves = []
        for half in range(2):
            qm = jnp.where(first if half == 0 else jnp.logical_not(first), qp, jnp.zeros_like(qp))
            scores = [lax.dot_general(qm, k_ref[:, kcols], (((1,), (1,)), ((), ())),
                                      preferred_element_type=F32) for k_ref in k_refs]
            m = scores[0].max(axis=-1, keepdims=True)
            for s in scores[1:]:
                m = jnp.maximum(m, s.max(axis=-1, keepdims=True))
            acc = jnp.zeros((tq, KV_W), F32)
            for s, v_ref in zip(scores, v_refs):
                p = jnp.exp2(s - m)
                acc = acc + jnp.dot(p.astype(BF16), v_ref[:, kcols], preferred_element_type=F32)
            swapped = pltpu.roll(acc, HEAD_DIM, axis=1)
            halves.append(acc / swapped if half == 0 else swapped / acc)
        o_ref[:, cols] = jnp.where(first, halves[0], halves[1]).astype(o_ref.dtype)


def _attention(q, kds, vds):
    b, seq, _ = q.shape
    tq = 256
    n_src = len(kds)
    kv_spec = lambda a: pl.BlockSpec((None, a.shape[1], 2 * KV_W), lambda i, t: (i, 0, 0))
    return pl.pallas_call(
        functools.partial(_attn_kernel, n_kv_src=n_src),
        out_shape=jax.ShapeDtypeStruct((b, seq, ATTN_W), BF16),
        grid=(b, seq // tq),
        in_specs=[pl.BlockSpec((None, tq, ATTN_W), lambda i, t: (i, t, 0))]
                 + [kv_spec(a) for a in kds] + [kv_spec(a) for a in vds],
        out_specs=pl.BlockSpec((None, tq, ATTN_W), lambda i, t: (i, t, 0)),
        compiler_params=pltpu.CompilerParams(
            dimension_semantics=("arbitrary", "arbitrary"), vmem_limit_bytes=VMEM_LIMIT),
        name="attention_%dsrc" % n_src,
    )(q, *kds, *vds)


def _ssm_kernel(u_ref, bw_ref, cw_ref, lam_ref, h0_ref, d_ref, glu_ref,
                o_ref, hf_ref, y_ref, bu_scr, hs_scr, st_scr):
    rows = u_ref.shape[1]
    seq = rows // SEQ_GROUP
    crow = SSM_CHUNK * SEQ_GROUP
    nchunks = rows // crow
    w2 = 2 * SSM_GP
    lanes = SSM_W // 2

    def time_major(c):
        t0 = c * SSM_CHUNK
        return jnp.concatenate(
            [jnp.concatenate([u_ref[h, pl.ds(t0 + s, SEQ_GROUP, stride=seq), :] for h in range(2)], axis=1)
             for s in range(SSM_CHUNK)], axis=0)

    def y_rows(r, n):
        return jnp.concatenate([y_ref[0, pl.ds(r, n), :], y_ref[1, pl.ds(r, n), :]], axis=1)

    def set_y_rows(r, n, val):
        y_ref[0, pl.ds(r, n), :] = val[:, :lanes]
        y_ref[1, pl.ds(r, n), :] = val[:, lanes:]

    st_scr[...] = h0_ref[...]

    def init_rows(i, c):
        r = pl.multiple_of(i * crow, crow)
        set_y_rows(r, crow, time_major(i) * d_ref[...])
        return c
    lax.fori_loop(0, nchunks, init_rows, 0)

    def chunk(i, c):
        cidx = (i, nchunks - 1 - i)
        starts = (pl.multiple_of(i * crow, crow), pl.multiple_of((nchunks - 1 - i) * crow, crow))
        for d in range(2):
            ub = time_major(cidx[d]).astype(BF16)
            bu_scr[d] = jnp.dot(ub, bw_ref[:, d * w2:(d + 1) * w2], preferred_element_type=F32)
        for d in range(2):
            for cb in range(SSM_GP // SSM_COLBLK):
                re_c = slice(cb * SSM_COLBLK, (cb + 1) * SSM_COLBLK)
                im_c = slice(SSM_GP + cb * SSM_COLBLK, SSM_GP + (cb + 1) * SSM_COLBLK)
                l_re = lam_ref[2 * d, :, re_c]
                l_im = lam_ref[2 * d + 1, :, re_c]
                h_re = st_scr[:, d * w2 + cb * SSM_COLBLK:d * w2 + (cb + 1) * SSM_COLBLK]
                h_im = st_scr[:, d * w2 + SSM_GP + cb * SSM_COLBLK:d * w2 + SSM_GP + (cb + 1) * SSM_COLBLK]
                for s in range(SSM_CHUNK):
                    t = s if d == 0 else SSM_CHUNK - 1 - s
                    r = slice(t * SEQ_GROUP, (t + 1) * SEQ_GROUP)
                    n_re = l_re * h_re - l_im * h_im + bu_scr[d, r, re_c]
                    n_im = l_re * h_im + l_im * h_re + bu_scr[d, r, im_c]
                    hs_scr[d, r, re_c] = n_re
                    hs_scr[d, r, im_c] = n_im
                    h_re, h_im = n_re, n_im
                st_scr[:, d * w2 + cb * SSM_COLBLK:d * w2 + (cb + 1) * SSM_COLBLK] = h_re
                st_scr[:, d * w2 + SSM_GP + cb * SSM_COLBLK:d * w2 + SSM_GP + (cb + 1) * SSM_COLBLK] = h_im
        for d in range(2):
            contrib = jnp.dot(hs_scr[d].astype(BF16), cw_ref[d * w2:(d + 1) * w2, :],
                              preferred_element_type=F32)
            set_y_rows(starts[d], crow, y_rows(starts[d], crow) + contrib)
        return c
    lax.fori_loop(0, nchunks, chunk, 0)

    hf_ref[...] = st_scr[...]

    esteps = 128
    erow = esteps * SEQ_GROUP

    def epilogue(i, c):
        r = pl.multiple_of(i * erow, erow)
        z = jax.nn.gelu(y_rows(r, erow))
        g = jnp.dot(z.astype(BF16), glu_ref[...], preferred_element_type=F32)
        set_y_rows(r, erow, g[:, :SSM_W] * jax.nn.sigmoid(g[:, SSM_W:]))
        t0 = pl.multiple_of(i * esteps, esteps)
        for b in range(SEQ_GROUP):
            for h in range(2):
                o_ref[pl.ds(b * seq + t0, esteps), h * lanes:(h + 1) * lanes] = (
                    y_ref[h, pl.ds(r + b, esteps, stride=SEQ_GROUP), :].astype(BF16))
        return c
    lax.fori_loop(0, rows // erow, epilogue, 0)


def _ssm(u, bw, cw, lam, h0, ssm_d, glu_bf, seq):
    rows = u.shape[1]
    grows = SEQ_GROUP * seq
    nbg = rows // grows
    crow = SSM_CHUNK * SEQ_GROUP
    const = lambda shape: pl.BlockSpec(shape, lambda i: (0,) * len(shape))
    return pl.pallas_call(
        _ssm_kernel,
        out_shape=(jax.ShapeDtypeStruct((rows, SSM_W), BF16),
                   jax.ShapeDtypeStruct((nbg, SEQ_GROUP, 4 * SSM_GP), F32)),
        grid=(nbg,),
        in_specs=[
            pl.BlockSpec((2, grows, SSM_W // 2), lambda i: (0, i, 0)),
            const((SSM_W, 4 * SSM_GP)),
            const((4 * SSM_GP, SSM_W)),
            const((4, SEQ_GROUP, SSM_GP)),
            pl.BlockSpec((None, SEQ_GROUP, 4 * SSM_GP), lambda i: (i, 0, 0)),
            const((1, SSM_W)),
            const((SSM_W, 2 * SSM_W)),
        ],
        out_specs=(pl.BlockSpec((grows, SSM_W), lambda i: (i, 0)),
                   pl.BlockSpec((None, SEQ_GROUP, 4 * SSM_GP), lambda i: (i, 0, 0))),
        scratch_shapes=[pltpu.VMEM((2, grows, SSM_W // 2), F32),
                        pltpu.VMEM((2, crow, 2 * SSM_GP), F32),
                        pltpu.VMEM((2, crow, 2 * SSM_GP), F32),
                        pltpu.VMEM((SEQ_GROUP, 4 * SSM_GP), F32)],
        compiler_params=pltpu.CompilerParams(
            dimension_semantics=("arbitrary",), vmem_limit_bytes=VMEM_LIMIT),
        name="ssm_scan",
    )(u, bw, cw, lam, h0, ssm_d.reshape(1, SSM_W), glu_bf)


def _pack2(lo, hi):
    return pltpu.pack_elementwise([lo, hi], packed_dtype=BF16)


def _unpack2(w, index):
    return pltpu.unpack_elementwise(w, index=index, packed_dtype=BF16, unpacked_dtype=F32)


def _mixout_kernel(*refs, route):
    if route:
        (x_ref, at_ref, ss_ref, up_ref, mod_ref, invc_ref, pw_ref, ps_ref, wo_ref, g2_ref, wr_ref, br_ref, tri_ref,
         x1_ref, h2_ref, route_ref, cnt_ref) = refs
    else:
        (x_ref, at_ref, ss_ref, up_ref, mod_ref, invc_ref, pw_ref, ps_ref, wo_ref, g2_ref,
         x1_ref, h2_ref) = refs
    u = up_ref[...]
    seq = u.shape[0]
    zpad = jnp.zeros((POOL_PAD, POOL_W), F32)
    ue = jnp.concatenate([zpad, u, zpad], axis=0)
    n_ext = seq + 2 * POOL_PAD
    back = lambda a, k: pltpu.roll(a, k, axis=0)
    ahead = lambda a, k: pltpu.roll(a, n_ext - k, axis=0)
    w2 = ue + back(ue, 1)
    w4 = back(w2, 1) + ahead(w2, 1)
    w8 = back(w4, 2) + ahead(w4, 2)
    w16 = back(w8, 4) + ahead(w8, 4)
    grp = lax.broadcasted_iota(jnp.int32, ue.shape, 1) // POOL_CH
    win = jnp.where(grp == 0, w2, jnp.where(grp == 1, w4, jnp.where(grp == 2, w8, w16)))
    pooled = win[POOL_PAD:POOL_PAD + seq] * invc_ref[...] - u
    pool = jnp.dot(pooled.astype(BF16), pw_ref[...], preferred_element_type=F32) * ps_ref[...]
    mix = jnp.concatenate([at_ref[...], ss_ref[...], pool.astype(BF16)], axis=1)
    o = jnp.dot(mix, wo_ref[...], preferred_element_type=F32)
    mod = mod_ref[...]
    x1 = x_ref[...] + mod[2:3] * o
    x1_ref[...] = x1
    h2 = _rms(x1, g2_ref[...]) * (1.0 + mod[4:5]) + mod[3:4]
    h2_ref[...] = h2.astype(BF16)
    if not route:
        return
    h_hi = h2.astype(BF16)
    h_lo = (h2 - h_hi.astype(F32)).astype(BF16)
    both = jnp.dot(h_hi, wr_ref[...], preferred_element_type=F32)
    logits = (both[:, :ROUTE_W] + both[:, ROUTE_W:]
              + jnp.dot(h_lo, wr_ref[:, :ROUTE_W], preferred_element_type=F32) + br_ref[...])
    lane = lax.broadcasted_iota(jnp.int32, logits.shape, 1).astype(F32)
    neg = float(np.finfo(np.float32).min)
    far = float(ROUTE_W)
    logits = jnp.where(lane < N_EXPERTS, logits, neg)
    m1 = logits.max(axis=-1, keepdims=True)
    i1 = jnp.where(logits == m1, lane, far).min(axis=-1, keepdims=True)
    rest = jnp.where(lane == i1, neg, logits)
    m2 = rest.max(axis=-1, keepdims=True)
    i2 = jnp.where(rest == m2, lane, far).min(axis=-1, keepdims=True)
    e2 = jnp.exp(m2 - m1)
    den = 1.0 + e2
    sel = jnp.where(lane == i1, 1.0, jnp.where(lane == i2, 1.0, 0.0))
    rank = jnp.dot(tri_ref[...], sel.astype(BF16), preferred_element_type=F32)
    r1 = jnp.where(lane == i1, rank, 0.0).sum(axis=-1, keepdims=True)
    r2 = jnp.where(lane == i2, rank, 0.0).sum(axis=-1, keepdims=True)
    out = jnp.zeros_like(logits)
    for j, col in enumerate((1.0 / den, e2 / den, i1, i2, r1, r2)):
        out = jnp.where(lane == j, col, out)
    route_ref[...] = out
    cnt_ref[...] = jnp.broadcast_to(sel.sum(axis=0, keepdims=True), cnt_ref.shape)


def _pool_inv_count(seq):
    t = np.arange(seq)
    cols = []
    for win in POOL_WINDOWS:
        lo = np.clip(t - win // 2, 0, seq)
        hi = np.clip(t + win // 2, 0, seq)
        cols.append(np.repeat((hi - lo).astype(np.float32)[:, None], POOL_CH, axis=1))
    return np.concatenate(cols, axis=1)


def _block_diag_pool(pool_w):
    eye = jnp.eye(len(POOL_WINDOWS), dtype=F32)
    m = eye[:, None, :, None] * pool_w[:, :, None, :]
    return m.reshape(POOL_W, POOL_W)


def _mixout(x, attn, ssm_tm, u_pool, mods_l, pool_w, pool_scale, w_out_bf, g2, latent, router):
    b, seq, _ = x.shape
    route = router is not None
    mod_row = (lambda i: i) if latent else (lambda i: CTX_ROW)
    const = lambda shape: pl.BlockSpec(shape, lambda i: (0,) * len(shape))
    cnt = _pool_inv_count(seq)
    in_specs = [
        pl.BlockSpec((None, seq, D_MODEL), lambda i: (i, 0, 0)),
        pl.BlockSpec((None, seq, ATTN_W), lambda i: (i, 0, 0)),
        pl.BlockSpec((None, seq, SSM_W), lambda i: (i, 0, 0)),
        pl.BlockSpec((None, seq, POOL_W), lambda i: (i, 0, 0)),
        pl.BlockSpec((None, 6, D_MODEL), lambda i: (mod_row(i), 0, 0)),
        const((seq, POOL_W)),
        const((POOL_W, POOL_W)),
        const((1, POOL_W)),
        const((D_MODEL, D_MODEL)),
        const((1, D_MODEL)),
    ]
    args = [x, attn, ssm_tm, u_pool, mods_l, jnp.asarray(1.0 / cnt, dtype=F32),
            _block_diag_pool(pool_w).astype(BF16), pool_scale.reshape(1, POOL_W), w_out_bf,
            g2.reshape(1, D_MODEL)]
    row_spec = lambda width: pl.BlockSpec((None, seq, width), lambda i: (i, 0, 0))
    if route:
        wr, br = router
        tri = np.tril(np.ones((seq, seq), np.float32), -1)
        wr_pad = jnp.pad(wr, ((0, 0), (0, ROUTE_W - N_EXPERTS)))
        wr_hi = wr_pad.astype(BF16)
        wr_lo = (wr_pad - wr_hi.astype(F32)).astype(BF16)
        in_specs += [const((D_MODEL, 2 * ROUTE_W)), const((1, ROUTE_W)), const((seq, seq))]
        args += [jnp.concatenate([wr_hi, wr_lo], axis=1),
                 jnp.pad(br, (0, ROUTE_W - N_EXPERTS)).reshape(1, ROUTE_W),
                 jnp.asarray(tri, dtype=BF16)]
        out_shape = [jax.ShapeDtypeStruct((b, seq, D_MODEL), F32),
                     jax.ShapeDtypeStruct((b, seq, D_MODEL), BF16),
                     jax.ShapeDtypeStruct((b, seq, ROUTE_W), F32),
                     jax.ShapeDtypeStruct((b, 8, ROUTE_W), F32)]
        out_specs = [row_spec(D_MODEL), row_spec(D_MODEL), row_spec(ROUTE_W),
                     pl.BlockSpec((None, 8, ROUTE_W), lambda i: (i, 0, 0))]
    else:
        out_shape = [jax.ShapeDtypeStruct((b, seq, D_MODEL), F32), jax.ShapeDtypeStruct((b, seq, D_MODEL), BF16)]
        out_specs = [row_spec(D_MODEL), row_spec(D_MODEL)]
    return pl.pallas_call(
        functools.partial(_mixout_kernel, route=route),
        out_shape=out_shape, grid=(b,), in_specs=in_specs, out_specs=out_specs,
        compiler_params=pltpu.CompilerParams(
            dimension_semantics=("arbitrary",), vmem_limit_bytes=VMEM_LIMIT),
        name="mixout_%s%s" % ("latent" if latent else "context", "_route" if route else ""),
    )(*args)


def _ffn_kernel(h_ref, x_ref, mod_ref, wg_ref, wu_ref, wd_ref, o_ref, acc_ref):
    e = pl.program_id(1)

    @pl.when(e == 0)
    def _():
        acc_ref[...] = jnp.zeros_like(acc_ref)

    h = h_ref[...]
    a = _silu(jnp.dot(h, wg_ref[...], preferred_element_type=F32)) * \
        jnp.dot(h, wu_ref[...], preferred_element_type=F32)
    acc_ref[...] += jnp.dot(a.astype(BF16), wd_ref[...], preferred_element_type=F32)

    @pl.when(e == pl.num_programs(1) - 1)
    def _():
        o_ref[...] = x_ref[...] + mod_ref[5:6] * acc_ref[...]


def _ffn(h2, x1, mods_l, wg, wu, wd, latent, seq):
    rows = h2.shape[0]
    tm = 512
    ff = FF_EXPERT
    n_chunk = wg.shape[1] // ff
    mod_row = (lambda i: (i * tm) // seq) if latent else (lambda i: CTX_ROW)
    return pl.pallas_call(
        _ffn_kernel,
        out_shape=jax.ShapeDtypeStruct((rows, D_MODEL), F32),
        grid=(rows // tm, n_chunk),
        in_specs=[pl.BlockSpec((tm, D_MODEL), lambda i, e: (i, 0)),
                  pl.BlockSpec((tm, D_MODEL), lambda i, e: (i, 0)),
                  pl.BlockSpec((None, 6, D_MODEL), lambda i, e: (mod_row(i), 0, 0)),
                  pl.BlockSpec((D_MODEL, ff), lambda i, e: (0, e)),
                  pl.BlockSpec((D_MODEL, ff), lambda i, e: (0, e)),
                  pl.BlockSpec((ff, D_MODEL), lambda i, e: (e, 0))],
        out_specs=pl.BlockSpec((tm, D_MODEL), lambda i, e: (i, 0)),
        scratch_shapes=[pltpu.VMEM((tm, D_MODEL), F32)],
        compiler_params=pltpu.CompilerParams(
            dimension_semantics=("arbitrary", "arbitrary"), vmem_limit_bytes=VMEM_LIMIT),
        name="ffn_dense",
    )(h2, x1, mods_l, wg, wu, wd)


def _route_tables(route, cnt, seq):
    rows = route.shape[0]
    nb = rows // MOE_BLOCK
    per_block = MOE_BLOCK // seq
    c = cnt[:, 0, :N_EXPERTS].astype(jnp.int32).reshape(nb, per_block, N_EXPERTS)
    before = jnp.cumsum(c, axis=1) - c
    total = c.sum(axis=1)
    aligned = (total + 7) // 8 * 8
    starts = jnp.cumsum(aligned, axis=1) - aligned
    base = jnp.repeat((starts[:, None, :] + before).reshape(nb * per_block, N_EXPERTS), seq, axis=0)
    expert = route[:, 2:4].astype(jnp.int32)
    rank = route[:, 4:6].astype(jnp.int32)
    hit = expert[:, :, None] == jnp.arange(N_EXPERTS, dtype=jnp.int32)[None, None, :]
    slot = jnp.where(hit, base[:, None, :], 0).sum(axis=-1) + rank
    slots = slot.reshape(nb, MOE_BLOCK, 2).transpose(0, 2, 1)
    return starts.reshape(-1), total.reshape(-1), slots


MOE_SLOTS = 2 * MOE_BLOCK
MOE_ROWS = MOE_SLOTS + 8 * N_EXPERTS + MOE_TILE


MOE_GROUP = 16


def _moe_kernel(starts_ref, counts_ref, slots_ref, h_ref, route_ref, wg_ref, wu_ref, wd_ref, f_ref,
                xs_ref, stage_ref):
    b = pl.program_id(0)
    e = pl.program_id(1)
    half = D_MODEL // 2

    @pl.when(e == 0)
    def _():
        xs_ref[MOE_SLOTS:, :] = jnp.zeros((MOE_ROWS - MOE_SLOTS, half), jnp.uint32)
        zero_row = jnp.zeros((1, half), jnp.uint32)
        for g in range(N_EXPERTS):
            end = starts_ref[b * N_EXPERTS + g] + counts_ref[b * N_EXPERTS + g]
            for r in range(7):
                xs_ref[pl.ds(end + r, 1), :] = zero_row

        def put(g, c):
            t0 = pl.multiple_of(g * MOE_GROUP, MOE_GROUP)
            hb = h_ref[pl.ds(t0, MOE_GROUP), :].astype(F32)
            stage_ref[:MOE_GROUP, :] = _pack2(hb[:, :half], hb[:, half:])
            for r in range(MOE_GROUP):
                row = stage_ref[r:r + 1, :]
                xs_ref[pl.ds(slots_ref[0, t0 + r], 1), :] = row
                xs_ref[pl.ds(slots_ref[1, t0 + r], 1), :] = row
            return c
        lax.fori_loop(0, MOE_BLOCK // MOE_GROUP, put, 0)

    start = starts_ref[b * N_EXPERTS + e]
    count = counts_ref[b * N_EXPERTS + e]

    def tile(j, c):
        s = pl.multiple_of(start + j * MOE_TILE, 8)
        xg = xs_ref[pl.ds(s, MOE_TILE), :]
        x_lo32 = _unpack2(xg, 0)
        x_hi32 = _unpack2(xg, 1)
        x_lo = x_lo32.astype(BF16)
        x_hi = x_hi32.astype(BF16)
        hg = (jnp.dot(x_lo, wg_ref[:half, :], preferred_element_type=F32)
              + jnp.dot(x_hi, wg_ref[half:, :], preferred_element_type=F32))
        hu = (jnp.dot(x_lo, wu_ref[:half, :], preferred_element_type=F32)
              + jnp.dot(x_hi, wu_ref[half:, :], preferred_element_type=F32))
        a = (_silu(hg) * hu).astype(BF16)
        y = jnp.dot(a, wd_ref[...], preferred_element_type=F32)
        valid = lax.broadcasted_iota(jnp.int32, (MOE_TILE, half), 0) < count - j * MOE_TILE
        xs_ref[pl.ds(s, MOE_TILE), :] = _pack2(jnp.where(valid, y[:, :half], x_lo32),
                                               jnp.where(valid, y[:, half:], x_hi32))
        return c
    lax.fori_loop(0, (count + MOE_TILE - 1) // MOE_TILE, tile, 0)

    @pl.when(e == pl.num_programs(1) - 1)
    def _():
        def take(g, c):
            t0 = pl.multiple_of(g * MOE_GROUP, MOE_GROUP)
            for r in range(MOE_GROUP):
                stage_ref[r:r + 1, :] = xs_ref[pl.ds(slots_ref[0, t0 + r], 1), :]
                stage_ref[MOE_GROUP + r:MOE_GROUP + r + 1, :] = xs_ref[pl.ds(slots_ref[1, t0 + r], 1), :]
            z0 = stage_ref[:MOE_GROUP, :]
            z1 = stage_ref[MOE_GROUP:, :]
            route = route_ref[pl.ds(t0, MOE_GROUP), :]
            w1 = route[:, 0:1]
            w2 = route[:, 1:2]
            f_ref[pl.ds(t0, MOE_GROUP), :half] = (w1 * _unpack2(z0, 0) + w2 * _unpack2(z1, 0)).astype(BF16)
            f_ref[pl.ds(t0, MOE_GROUP), half:] = (w1 * _unpack2(z0, 1) + w2 * _unpack2(z1, 1)).astype(BF16)
            return c
        lax.fori_loop(0, MOE_BLOCK // MOE_GROUP, take, 0)


def _moe_experts(h2, route, starts, counts, slots, wg, wu, wd):
    rows = h2.shape[0]
    nb = rows // MOE_BLOCK
    ff = wg.shape[2]
    grid_spec = pltpu.PrefetchScalarGridSpec(
        num_scalar_prefetch=2,
        grid=(nb, N_EXPERTS),
        in_specs=[
            pl.BlockSpec((None, 2, MOE_BLOCK), lambda b, e, st, ct: (b, 0, 0), memory_space=pltpu.SMEM),
            pl.BlockSpec((MOE_BLOCK, D_MODEL), lambda b, e, st, ct: (b, 0)),
            pl.BlockSpec((MOE_BLOCK, ROUTE_W), lambda b, e, st, ct: (b, 0)),
            pl.BlockSpec((None, D_MODEL, ff), lambda b, e, st, ct: (e, 0, 0)),
            pl.BlockSpec((None, D_MODEL, ff), lambda b, e, st, ct: (e, 0, 0)),
            pl.BlockSpec((None, ff, D_MODEL), lambda b, e, st, ct: (e, 0, 0)),
        ],
        out_specs=pl.BlockSpec((MOE_BLOCK, D_MODEL), lambda b, e, st, ct: (b, 0)),
        scratch_shapes=[pltpu.VMEM((MOE_ROWS, D_MODEL // 2), jnp.uint32),
                        pltpu.VMEM((2 * MOE_GROUP, D_MODEL // 2), jnp.uint32)],
    )
    return pl.pallas_call(
        _moe_kernel,
        out_shape=jax.ShapeDtypeStruct((rows, D_MODEL), BF16),
        grid_spec=grid_spec,
        compiler_params=pltpu.CompilerParams(
            dimension_semantics=("arbitrary", "arbitrary"), vmem_limit_bytes=VMEM_LIMIT),
        name="moe_experts",
    )(starts, counts, slots, h2, route, wg, wu, wd)


def _final_kernel(x_ref, f_ref, mod_ref, fg_ref, o_ref):
    y = x_ref[...] + mod_ref[5:6] * f_ref[...].astype(F32)
    o_ref[...] = _rms(y, fg_ref[...])


def _final(x1, f, mods_l, final_g, latent, seq):
    rows = x1.shape[0]
    tr = 512
    mod_row = (lambda i: (i * tr) // seq) if latent else (lambda i: CTX_ROW)
    return pl.pallas_call(
        _final_kernel,
        out_shape=jax.ShapeDtypeStruct((rows, D_MODEL), F32),
        grid=(rows // tr,),
        in_specs=[pl.BlockSpec((tr, D_MODEL), lambda i: (i, 0)),
                  pl.BlockSpec((tr, D_MODEL), lambda i: (i, 0)),
                  pl.BlockSpec((None, 6, D_MODEL), lambda i: (mod_row(i), 0, 0)),
                  pl.BlockSpec((1, D_MODEL), lambda i: (0, 0))],
        out_specs=pl.BlockSpec((tr, D_MODEL), lambda i: (i, 0)),
        compiler_params=pltpu.CompilerParams(
            dimension_semantics=("arbitrary",), vmem_limit_bytes=VMEM_LIMIT),
        name="final_norm",
    )(x1, f, mods_l, final_g.reshape(1, D_MODEL))


def _cache_keys(a):
    b, n = a.shape[:2]
    return jnp.repeat(a, 2, axis=2).reshape(b, n, 2 * KV_W).astype(BF16)


def _cache_values(a):
    b, n = a.shape[:2]
    return jnp.concatenate([a, jnp.ones_like(a)], axis=3).reshape(b, n, 2 * KV_W).astype(BF16)


def kernel(x_prompt, x_sample, c, cache_k, cache_v, state_ssm_re, state_ssm_im, c_ctx, mod_w, mod_b, norm1_g, norm2_g, w_in, w_out, q_norm_g, k_norm_g, ssm_a_re, ssm_a_im, ssm_log_dt, ssm_b_re, ssm_b_im, ssm_c_re, ssm_c_im, ssm_d, ssm_glu_w, pool_w, pool_scale, ffn_w_gate, ffn_w_up, ffn_w_down, moe_router_w, moe_router_b, moe_w_gate, moe_w_up, moe_w_down, final_g):
    bp, lp, _ = x_prompt.shape
    bs, ls, _ = x_sample.shape
    assert bs == SEQ_GROUP and bp % SEQ_GROUP == 0

    cond = jnp.zeros((MOD_ROWS, D_MODEL), F32).at[:bs].set(c).at[CTX_ROW].set(c_ctx)
    mods = _modulation(cond, mod_w, mod_b).reshape(DEPTH, MOD_ROWS, 6, D_MODEL)

    lam_re, lam_im, bb_re, bb_im = _ssm_prep(ssm_a_re, ssm_a_im, ssm_log_dt, ssm_b_re, ssm_b_im)
    ssm_mats = _ssm_matrices(lam_re, lam_im, bb_re, bb_im, ssm_c_re, ssm_c_im)

    xp, xs = x_prompt, x_sample
    new_k, new_v, new_state = [], [], []
    for l in range(DEPTH):
        mods_l = mods[l]
        w_in_bf = w_in[l].astype(BF16)
        w_out_bf = w_out[l].astype(BF16)
        glu_bf = ssm_glu_w[l].astype(BF16)
        bw, cw, lam = ssm_mats[l]
        moe = l % 2 == 1
        i = l // 2
        router = (moe_router_w[i], moe_router_b[i]) if moe else None
        if moe:
            wg, wu, wd = (moe_w_gate[i].astype(BF16), moe_w_up[i].astype(BF16), moe_w_down[i].astype(BF16))
        else:
            wg, wu, wd = (ffn_w_gate[i].astype(BF16), ffn_w_up[i].astype(BF16), ffn_w_down[i].astype(BF16))
        for latent in (False, True):
            x = xs if latent else xp
            b, seq, _ = x.shape
            nbg = b // SEQ_GROUP
            rows = b * seq
            outs = _inproj(x.reshape(rows, D_MODEL), mods_l, norm1_g[l], w_in_bf, q_norm_g[l], k_norm_g[l],
                           latent, seq)
            q, kd, vd = (a.reshape(b, seq, a.shape[-1]) for a in outs[:3])
            u_ssm = outs[3]
            u_pool = outs[4].reshape(b, seq, POOL_W)
            if latent:
                kds = [kd, _cache_keys(cache_k[:, l])]
                vds = [vd, _cache_values(cache_v[:, l])]
                h0 = jnp.concatenate([
                    state_ssm_re[:, l, 0].reshape(b, SSM_GP), state_ssm_im[:, l, 0].reshape(b, SSM_GP),
                    state_ssm_re[:, l, 1].reshape(b, SSM_GP), state_ssm_im[:, l, 1].reshape(b, SSM_GP)],
                    axis=1).reshape(nbg, SEQ_GROUP, 4 * SSM_GP)
            else:
                kds, vds = [kd], [vd]
                new_k.append(outs[5].reshape(b, seq, KV_W))
                new_v.append(outs[6].reshape(b, seq, KV_W))
                h0 = jnp.zeros((nbg, SEQ_GROUP, 4 * SSM_GP), F32)
            attn = _attention(q, kds, vds)
            y_ssm, hf = _ssm(u_ssm, bw, cw, lam, h0, ssm_d[l], glu_bf, seq)
            if not latent:
                new_state.append(hf.reshape(b, 2, 2, SSM_GROUPS, SSM_STATE))
            res = _mixout(x, attn, y_ssm.reshape(b, seq, SSM_W), u_pool, mods_l,
                          pool_w[l], pool_scale[l], w_out_bf, norm2_g[l], latent, router)
            x1 = res[0].reshape(rows, D_MODEL)
            if moe:
                route = res[2].reshape(rows, ROUTE_W)
                starts, counts, slots = _route_tables(route, res[3], seq)
                f = _moe_experts(res[1].reshape(rows, D_MODEL), route, starts, counts, slots, wg, wu, wd)
                y = _final(x1, f, mods_l, final_g, latent, seq)
            else:
                y = _ffn(res[1].reshape(rows, D_MODEL), x1, mods_l, wg, wu, wd, latent, seq)
            y = y.reshape(b, seq, D_MODEL)
            if latent:
                xs = y
            else:
                xp = y

    new_cache_k = jnp.stack(new_k, axis=1).reshape(bp, DEPTH, lp, N_KV_HEADS, HEAD_DIM)
    new_cache_v = jnp.stack(new_v, axis=1).reshape(bp, DEPTH, lp, N_KV_HEADS, HEAD_DIM)
    st = jnp.stack(new_state, axis=1)
    return (xp, xs, new_cache_k, new_cache_v, st[:, :, :, 0], st[:, :, :, 1])
```

```python
import functools

import numpy as np
import jax
import jax.numpy as jnp
from jax import lax
from jax.experimental import pallas as pl
from jax.experimental.pallas import tpu as pltpu

D_MODEL = 1024
DEPTH = 2
GRID_W = 64
ATTN_W = 512
HEAD_DIM = 64
N_HEADS = 8
N_KV_HEADS = 2
KV_W = 128
ROPE_THETA = 10000.0
SSM_W = 256
SSM_CH = 16
SSM_GROUPS = 16
SSM_STATE = 64
SSM_GP = SSM_GROUPS * SSM_STATE
POOL_W = 256
POOL_WINDOWS = (2, 4, 8, 16)
POOL_CH = 64
POOL_PAD = max(POOL_WINDOWS) // 2
IN_COLS = ATTN_W + 2 * KV_W + SSM_W + POOL_W
FF_DENSE = 2816
N_EXPERTS = 8
FF_EXPERT = 1408
EPS = 1e-6
LOG2_E = 1.4426950408889634

SEQ_GROUP = 8
SSM_CHUNK = 32
SSM_COLBLK = 512
ROUTE_W = 128
MOE_BLOCK = 2048
MOE_TILE = 128
MOD_ROWS = 16
CTX_ROW = 8
VMEM_LIMIT = 56 * 1024 * 1024

F32 = jnp.float32
BF16 = jnp.bfloat16


def _silu(x):
    return x * jax.nn.sigmoid(x)


def _rms(x32, g):
    return x32 * lax.rsqrt(jnp.mean(x32 * x32, axis=-1, keepdims=True) + EPS) * g


def _mod_kernel(cond_ref, w_ref, b_ref, o_ref):
    s = _silu(cond_ref[...])
    o_ref[...] = jnp.dot(s, w_ref[...], preferred_element_type=F32,
                         precision=lax.Precision.HIGHEST) + b_ref[...]


def _modulation(cond, mod_w, mod_b):
    tn = 1536
    n = 6 * D_MODEL
    return pl.pallas_call(
        _mod_kernel,
        out_shape=jax.ShapeDtypeStruct((DEPTH, MOD_ROWS, n), F32),
        grid=(DEPTH, n // tn),
        in_specs=[
            pl.BlockSpec((MOD_ROWS, D_MODEL), lambda l, j: (0, 0)),
            pl.BlockSpec((None, D_MODEL, tn), lambda l, j: (l, 0, j)),
            pl.BlockSpec((None, 1, tn), lambda l, j: (l, 0, j)),
        ],
        out_specs=pl.BlockSpec((None, MOD_ROWS, tn), lambda l, j: (l, 0, j)),
        compiler_params=pltpu.CompilerParams(
            dimension_semantics=("arbitrary", "arbitrary"), vmem_limit_bytes=VMEM_LIMIT),
        name="modulation",
    )(cond, mod_w, mod_b.reshape(DEPTH, 1, n))


def _ssm_prep_kernel(are_ref, aim_ref, dt_ref, bre_ref, bim_ref,
                     lre_ref, lim_ref, bbre_ref, bbim_ref):
    a_re = are_ref[...]
    a_im = aim_ref[...]
    dt = jnp.exp(dt_ref[...])
    mag = jnp.exp(a_re * dt)
    l_re = mag * jnp.cos(a_im * dt)
    l_im = mag * jnp.sin(a_im * dt)
    lre_ref[...] = l_re
    lim_ref[...] = l_im
    x = l_re - 1.0
    y = l_im
    den = a_re * a_re + a_im * a_im
    c_re = (x * a_re + y * a_im) / den
    c_im = (y * a_re - x * a_im) / den
    b_re = bre_ref[...]
    b_im = bim_ref[...]
    bbre_ref[...] = c_re[:, None, :] * b_re - c_im[:, None, :] * b_im
    bbim_ref[...] = c_re[:, None, :] * b_im + c_im[:, None, :] * b_re


def _ssm_prep(a_re, a_im, log_dt, b_re, b_im):
    ld = DEPTH * 2
    are = a_re.reshape(ld, SSM_GP)
    aim = a_im.reshape(ld, SSM_GP)
    dt = jnp.repeat(log_dt.reshape(ld, SSM_GROUPS), SSM_STATE, axis=1)
    bre = b_re.reshape(ld, SSM_GROUPS, SSM_STATE, SSM_CH).transpose(0, 3, 1, 2).reshape(ld, SSM_CH, SSM_GP)
    bim = b_im.reshape(ld, SSM_GROUPS, SSM_STATE, SSM_CH).transpose(0, 3, 1, 2).reshape(ld, SSM_CH, SSM_GP)
    return pl.pallas_call(
        _ssm_prep_kernel,
        out_shape=(jax.ShapeDtypeStruct((ld, SSM_GP), F32), jax.ShapeDtypeStruct((ld, SSM_GP), F32),
                   jax.ShapeDtypeStruct((ld, SSM_CH, SSM_GP), F32),
                   jax.ShapeDtypeStruct((ld, SSM_CH, SSM_GP), F32)),
        name="ssm_prep",
    )(are, aim, dt, bre, bim)


def _ssm_matrices(lam_re, lam_im, bb_re, bb_im, c_re, c_im):
    eye = jnp.eye(SSM_GROUPS, dtype=F32)

    def b_block(bb):
        bb = bb.reshape(2, SSM_CH, SSM_GROUPS, SSM_STATE)
        m = eye[None, :, None, :, None] * bb[:, None, :, :, :]
        return m.reshape(2, SSM_W, SSM_GP)

    def c_block(cc):
        m = eye[None, :, None, :, None] * cc.transpose(0, 1, 3, 2)[:, :, :, None, :]
        return m.reshape(2, SSM_GP, SSM_W)

    out = []
    for l in range(DEPTH):
        sl = slice(2 * l, 2 * l + 2)
        bw = jnp.concatenate([b_block(bb_re[sl]), b_block(bb_im[sl])], axis=2)
        bw = jnp.concatenate([bw[0], bw[1]], axis=1).astype(BF16)
        cw = jnp.concatenate([c_block(c_re[l]), -c_block(c_im[l])], axis=1)
        cw = jnp.concatenate([cw[0], cw[1]], axis=0).astype(BF16)
        lam = jnp.stack([lam_re[2 * l], lam_im[2 * l], lam_re[2 * l + 1], lam_im[2 * l + 1]], axis=0)
        lam = jnp.broadcast_to(lam[:, None, :], (4, SEQ_GROUP, SSM_GP))
        out.append((bw, cw, lam))
    return out


def _inproj_kernel(*refs, latent):
    if latent:
        (x_ref, mod_ref, g_ref, w_ref, qg_ref, kg_ref, sq_ref, sk_ref, cos_ref, sa_ref, sb_ref,
         q_ref, kd_ref, vd_ref, us_ref, up_ref) = refs
    else:
        (x_ref, mod_ref, g_ref, w_ref, qg_ref, kg_ref, sq_ref, sk_ref,
         q_ref, kd_ref, vd_ref, us_ref, up_ref, ko_ref, vo_ref) = refs
    x = x_ref[...]
    mod = mod_ref[...]
    h = _rms(x, g_ref[...]) * (1.0 + mod[1:2]) + mod[0:1]
    p = jnp.dot(h.astype(BF16), w_ref[...], preferred_element_type=F32)
    q = p[:, :ATTN_W]
    k = p[:, ATTN_W:ATTN_W + KV_W]
    v = p[:, ATTN_W + KV_W:ATTN_W + 2 * KV_W]
    q_ms = jnp.dot((q * q).astype(BF16), sq_ref[...], preferred_element_type=F32)
    k_ms = jnp.dot((k * k).astype(BF16), sk_ref[...], preferred_element_type=F32)
    q = q * lax.rsqrt(q_ms + EPS) * qg_ref[...]
    k = k * lax.rsqrt(k_ms + EPS) * kg_ref[...]
    if latent:
        cos = cos_ref[...]
        sa = sa_ref[...]
        sb = sb_ref[...]
        cos4 = jnp.concatenate([cos] * 4, axis=1)
        sa4 = jnp.concatenate([sa] * 4, axis=1)
        sb4 = jnp.concatenate([sb] * 4, axis=1)
        q = (q * cos4 + pltpu.roll(q, ATTN_W - 16, axis=1) * sa4 + pltpu.roll(q, 16, axis=1) * sb4)
        k = (k * cos + pltpu.roll(k, KV_W - 16, axis=1) * sa + pltpu.roll(k, 16, axis=1) * sb)
    else:
        seq = ko_ref.shape[-1]
        for s in range(ko_ref.shape[0]):
            ko_ref[s] = k[s * seq:(s + 1) * seq, :].T
            vo_ref[s] = v[s * seq:(s + 1) * seq, :].T
    q_ref[...] = (q * (HEAD_DIM ** -0.5 * LOG2_E)).astype(BF16)
    lane = lax.broadcasted_iota(jnp.int32, k.shape, 1)
    first = lane < HEAD_DIM
    k_sw = pltpu.roll(k, HEAD_DIM, axis=1)
    v_sw = pltpu.roll(v, HEAD_DIM, axis=1)
    kd_ref[:, :KV_W] = jnp.where(first, k, k_sw).astype(BF16)
    kd_ref[:, KV_W:] = jnp.where(first, k_sw, k).astype(BF16)
    vd_ref[:, :KV_W] = jnp.where(first, v, v_sw).astype(BF16)
    vd_ref[:, KV_W:] = jnp.where(first, v_sw, v).astype(BF16)
    o3 = ATTN_W + 2 * KV_W
    us_ref[0] = p[:, o3:o3 + SSM_W // 2]
    us_ref[1] = p[:, o3 + SSM_W // 2:o3 + SSM_W]
    up_ref[...] = p[:, o3 + SSM_W:]


def _head_mean_matrix(width):
    i = np.arange(width) // HEAD_DIM
    return jnp.asarray((i[:, None] == i[None, :]).astype(np.float32) / HEAD_DIM, dtype=BF16)


def _rope_tables(seq):
    t = np.arange(seq)
    row = (t // GRID_W).astype(np.float64)
    col = (t % GRID_W).astype(np.float64)
    nf = HEAD_DIM // 4
    inv = ROPE_THETA ** (-np.arange(nf, dtype=np.float64) / nf)
    ang_r = row[:, None] * inv[None, :]
    ang_c = col[:, None] * inv[None, :]
    cos = np.concatenate([np.cos(ang_r), np.cos(ang_r), np.cos(ang_c), np.cos(ang_c)], axis=1)
    sin = np.concatenate([np.sin(ang_r), np.sin(ang_r), np.sin(ang_c), np.sin(ang_c)], axis=1)
    lower = np.tile(np.concatenate([np.ones(nf), np.zeros(nf)]), 2)[None, :]
    sa = -sin * lower
    sb = sin * (1.0 - lower)
    tile = lambda a: jnp.asarray(np.tile(a, (1, 2)), dtype=F32)
    return tile(cos), tile(sa), tile(sb)


def _inproj(x, mods_l, g1, w_in_bf, qg, kg, latent, seq):
    rows = x.shape[0]
    tr = 512
    tiles_per_seq = max(seq // tr, 1)
    mod_row = (lambda i: (i * tr) // seq) if latent else (lambda i: CTX_ROW)
    const = lambda shape: pl.BlockSpec(shape, lambda i: (0,) * len(shape))
    row_spec = lambda width: pl.BlockSpec((tr, width), lambda i: (i, 0))
    in_specs = [
        row_spec(D_MODEL),
        pl.BlockSpec((None, 6, D_MODEL), lambda i: (mod_row(i), 0, 0)),
        const((1, D_MODEL)),
        const((D_MODEL, IN_COLS)),
        const((1, ATTN_W)),
        const((1, KV_W)),
        const((ATTN_W, ATTN_W)),
        const((KV_W, KV_W)),
    ]
    args = [x, mods_l, g1.reshape(1, D_MODEL), w_in_bf,
            jnp.tile(qg, N_HEADS).reshape(1, ATTN_W), jnp.tile(kg, N_KV_HEADS).reshape(1, KV_W),
            _head_mean_matrix(ATTN_W), _head_mean_matrix(KV_W)]
    widths = [(ATTN_W, BF16), (2 * KV_W, BF16), (2 * KV_W, BF16), (None, F32), (POOL_W, F32)]
    if latent:
        assert seq % tr == 0
        cos, sa, sb = _rope_tables(seq)
        in_specs += [pl.BlockSpec((tr, KV_W), lambda i: (i % tiles_per_seq, 0))] * 3
        args += [cos, sa, sb]
    out_shape = [jax.ShapeDtypeStruct((rows, w) if w else (2, rows, SSM_W // 2), dt) for w, dt in widths]
    out_specs = [row_spec(w) if w else pl.BlockSpec((2, tr, SSM_W // 2), lambda i: (0, i, 0)) for w, _ in widths]
    if not latent:
        assert tr % seq == 0
        out_shape += [jax.ShapeDtypeStruct((rows // seq, KV_W, seq), F32)] * 2
        out_specs += [pl.BlockSpec((tr // seq, KV_W, seq), lambda i: (i, 0, 0))] * 2
    return pl.pallas_call(
        functools.partial(_inproj_kernel, latent=latent),
        out_shape=out_shape, grid=(rows // tr,), in_specs=in_specs, out_specs=out_specs,
        compiler_params=pltpu.CompilerParams(
            dimension_semantics=("arbitrary",), vmem_limit_bytes=VMEM_LIMIT),
        name="inproj_latent" if latent else "inproj_context",
    )(*args)


def _attn_kernel(*refs, n_kv_src):
    q_ref = refs[0]
    k_refs = refs[1:1 + n_kv_src]
    v_refs = refs[1 + n_kv_src:1 + 2 * n_kv_src]
    o_ref = refs[1 + 2 * n_kv_src]
    tq = q_ref.shape[0]
    lane = lax.broadcasted_iota(jnp.int32, (tq, KV_W), 1)
    first = lane < HEAD_DIM
    for pair in range(N_HEADS // 2):
        kv = pair // 2
        cols = slice(pair * KV_W, (pair + 1) * KV_W)
        kcols = slice(kv * KV_W, (kv + 1) * KV_W)
        qp = q_ref[:, cols]
        halves = []
        for half in range(2):
            qm = jnp.where(first if half == 0 else jnp.logical_not(first), qp, jnp.zeros_like(qp))
            scores = [lax.dot_general(qm, k_ref[:, kcols], (((1,), (1,)), ((), ())),
                                      preferred_element_type=F32) for k_ref in k_refs]
            m = scores[0].max(axis=-1, keepdims=True)
            for s in scores[1:]:
                m = jnp.maximum(m, s.max(axis=-1, keepdims=True))
            den = jnp.zeros((tq, 1), F32)
            acc = jnp.zeros((tq, KV_W), F32)
            for s, v_ref in zip(scores, v_refs):
                p = jnp.exp2(s - m)
                den = den + p.sum(axis=-1, keepdims=True)
                acc = acc + jnp.dot(p.astype(BF16), v_ref[:, kcols], preferred_element_type=F32)
            halves.append(acc / den)
        o_ref[:, cols] = jnp.where(first, halves[0], halves[1]).astype(o_ref.dtype)


def _attention(q, kds, vds):
    b, seq, _ = q.shape
    tq = 256
    n_src = len(kds)
    kv_spec = lambda a: pl.BlockSpec((None, a.shape[1], 2 * KV_W), lambda i, t: (i, 0, 0))
    return pl.pallas_call(
        functools.partial(_attn_kernel, n_kv_src=n_src),
        out_shape=jax.ShapeDtypeStruct((b, seq, ATTN_W), BF16),
        grid=(b, seq // tq),
        in_specs=[pl.BlockSpec((None, tq, ATTN_W), lambda i, t: (i, t, 0))]
                 + [kv_spec(a) for a in kds] + [kv_spec(a) for a in vds],
        out_specs=pl.BlockSpec((None, tq, ATTN_W), lambda i, t: (i, t, 0)),
        compiler_params=pltpu.CompilerParams(
            dimension_semantics=("arbitrary", "arbitrary"), vmem_limit_bytes=VMEM_LIMIT),
        name="attention_%dsrc" % n_src,
    )(q, *kds, *vds)


def _ssm_kernel(u_ref, bw_ref, cw_ref, lam_ref, h0_ref, d_ref, glu_ref,
                o_ref, hf_ref, y_ref, utm_scr, bu_scr, hs_scr, st_scr):
    rows = u_ref.shape[1]
    seq = rows // SEQ_GROUP
    crow = SSM_CHUNK * SEQ_GROUP
    nchunks = rows // crow
    w2 = 2 * SSM_GP
    lanes = SSM_W // 2

    def time_major(c):
        t0 = c * SSM_CHUNK
        return jnp.concatenate(
            [jnp.concatenate([u_ref[h, pl.ds(t0 + s, SEQ_GROUP, stride=seq), :] for h in range(2)], axis=1)
             for s in range(SSM_CHUNK)], axis=0)

    def y_rows(r, n):
        return jnp.concatenate([y_ref[0, pl.ds(r, n), :], y_ref[1, pl.ds(r, n), :]], axis=1)

    def set_y_rows(r, n, val):
        y_ref[0, pl.ds(r, n), :] = val[:, :lanes]
        y_ref[1, pl.ds(r, n), :] = val[:, lanes:]

    st_scr[...] = h0_ref[...]

    def init_rows(i, c):
        r = pl.multiple_of(i * crow, crow)
        tm = time_major(i)
        set_y_rows(r, crow, tm * d_ref[...])
        utm_scr[pl.ds(r, crow), :] = tm.astype(BF16)
        return c
    lax.fori_loop(0, nchunks, init_rows, 0)

    def chunk(i, c):
        starts = (pl.multiple_of(i * crow, crow), pl.multiple_of((nchunks - 1 - i) * crow, crow))
        for d in range(2):
            bu_scr[d] = jnp.dot(utm_scr[pl.ds(starts[d], crow), :], bw_ref[:, d * w2:(d + 1) * w2],
                                preferred_element_type=F32)
        for d in range(2):
            for cb in range(SSM_GP // SSM_COLBLK):
                re_c = slice(cb * SSM_COLBLK, (cb + 1) * SSM_COLBLK)
                im_c = slice(SSM_GP + cb * SSM_COLBLK, SSM_GP + (cb + 1) * SSM_COLBLK)
                l_re = lam_ref[2 * d, :, re_c]
                l_im = lam_ref[2 * d + 1, :, re_c]
                h_re = st_scr[:, d * w2 + cb * SSM_COLBLK:d * w2 + (cb + 1) * SSM_COLBLK]
                h_im = st_scr[:, d * w2 + SSM_GP + cb * SSM_COLBLK:d * w2 + SSM_GP + (cb + 1) * SSM_COLBLK]
                for s in range(SSM_CHUNK):
                    t = s if d == 0 else SSM_CHUNK - 1 - s
                    r = slice(t * SEQ_GROUP, (t + 1) * SEQ_GROUP)
                    n_re = l_re * h_re - l_im * h_im + bu_scr[d, r, re_c]
                    n_im = l_re * h_im + l_im * h_re + bu_scr[d, r, im_c]
                    hs_scr[d, r, re_c] = n_re
                    hs_scr[d, r, im_c] = n_im
                    h_re, h_im = n_re, n_im
                st_scr[:, d * w2 + cb * SSM_COLBLK:d * w2 + (cb + 1) * SSM_COLBLK] = h_re
                st_scr[:, d * w2 + SSM_GP + cb * SSM_COLBLK:d * w2 + SSM_GP + (cb + 1) * SSM_COLBLK] = h_im
        for d in range(2):
            contrib = jnp.dot(hs_scr[d].astype(BF16), cw_ref[d * w2:(d + 1) * w2, :],
                              preferred_element_type=F32)
            set_y_rows(starts[d], crow, y_rows(starts[d], crow) + contrib)
        return c
    lax.fori_loop(0, nchunks, chunk, 0)

    hf_ref[...] = st_scr[...]

    esteps = 128
    erow = esteps * SEQ_GROUP

    def epilogue(i, c):
        r = pl.multiple_of(i * erow, erow)
        z = jax.nn.gelu(y_rows(r, erow))
        g = jnp.dot(z.astype(BF16), glu_ref[...], preferred_element_type=F32)
        set_y_rows(r, erow, g[:, :SSM_W] * jax.nn.sigmoid(g[:, SSM_W:]))
        t0 = pl.multiple_of(i * esteps, esteps)
        for b in range(SEQ_GROUP):
            for h in range(2):
                o_ref[pl.ds(b * seq + t0, esteps), h * lanes:(h + 1) * lanes] = (
                    y_ref[h, pl.ds(r + b, esteps, stride=SEQ_GROUP), :].astype(BF16))
        return c
    lax.fori_loop(0, rows // erow, epilogue, 0)


def _ssm(u, bw, cw, lam, h0, ssm_d, glu_bf, seq):
    rows = u.shape[1]
    grows = SEQ_GROUP * seq
    nbg = rows // grows
    crow = SSM_CHUNK * SEQ_GROUP
    const = lambda shape: pl.BlockSpec(shape, lambda i: (0,) * len(shape))
    return pl.pallas_call(
        _ssm_kernel,
        out_shape=(jax.ShapeDtypeStruct((rows, SSM_W), BF16),
                   jax.ShapeDtypeStruct((nbg, SEQ_GROUP, 4 * SSM_GP), F32)),
        grid=(nbg,),
        in_specs=[
            pl.BlockSpec((2, grows, SSM_W // 2), lambda i: (0, i, 0)),
            const((SSM_W, 4 * SSM_GP)),
            const((4 * SSM_GP, SSM_W)),
            const((4, SEQ_GROUP, SSM_GP)),
            pl.BlockSpec((None, SEQ_GROUP, 4 * SSM_GP), lambda i: (i, 0, 0)),
            const((1, SSM_W)),
            const((SSM_W, 2 * SSM_W)),
        ],
        out_specs=(pl.BlockSpec((grows, SSM_W), lambda i: (i, 0)),
                   pl.BlockSpec((None, SEQ_GROUP, 4 * SSM_GP), lambda i: (i, 0, 0))),
        scratch_shapes=[pltpu.VMEM((2, grows, SSM_W // 2), F32),
                        pltpu.VMEM((grows, SSM_W), BF16),
                        pltpu.VMEM((2, crow, 2 * SSM_GP), F32),
                        pltpu.VMEM((2, crow, 2 * SSM_GP), F32),
                        pltpu.VMEM((SEQ_GROUP, 4 * SSM_GP), F32)],
        compiler_params=pltpu.CompilerParams(
            dimension_semantics=("arbitrary",), vmem_limit_bytes=VMEM_LIMIT),
        name="ssm_scan",
    )(u, bw, cw, lam, h0, ssm_d.reshape(1, SSM_W), glu_bf)


def _pack2(lo, hi):
    return pltpu.pack_elementwise([lo, hi], packed_dtype=BF16)


def _unpack2(w, index):
    return pltpu.unpack_elementwise(w, index=index, packed_dtype=BF16, unpacked_dtype=F32)


def _mixout_kernel(*refs, route):
    if route:
        (x_ref, at_ref, ss_ref, up_ref, mod_ref, invc_ref, pw_ref, ps_ref, wo_ref, g2_ref, wr_ref, br_ref, tri_ref,
         x1_ref, h2_ref, route_ref, rt_ref, cnt_ref) = refs
    else:
        (x_ref, at_ref, ss_ref, up_ref, mod_ref, invc_ref, pw_ref, ps_ref, wo_ref, g2_ref,
         x1_ref, h2_ref) = refs
    u = up_ref[...]
    seq = u.shape[0]
    zpad = jnp.zeros((POOL_PAD, POOL_W), F32)
    ue = jnp.concatenate([zpad, u, zpad], axis=0)
    n_ext = seq + 2 * POOL_PAD
    back = lambda a, k: pltpu.roll(a, k, axis=0)
    ahead = lambda a, k: pltpu.roll(a, n_ext - k, axis=0)
    w2 = ue + back(ue, 1)
    w4 = back(w2, 1) + ahead(w2, 1)
    w8 = back(w4, 2) + ahead(w4, 2)
    w16 = back(w8, 4) + ahead(w8, 4)
    grp = lax.broadcasted_iota(jnp.int32, ue.shape, 1) // POOL_CH
    win = jnp.where(grp == 0, w2, jnp.where(grp == 1, w4, jnp.where(grp == 2, w8, w16)))
    pooled = win[POOL_PAD:POOL_PAD + seq] * invc_ref[...] - u
    pool = jnp.dot(pooled.astype(BF16), pw_ref[...], preferred_element_type=F32) * ps_ref[...]
    mix = jnp.concatenate([at_ref[...], ss_ref[...], pool.astype(BF16)], axis=1)
    o = jnp.dot(mix, wo_ref[...], preferred_element_type=F32)
    mod = mod_ref[...]
    x1 = x_ref[...] + mod[2:3] * o
    x1_ref[...] = x1
    h2 = _rms(x1, g2_ref[...]) * (1.0 + mod[4:5]) + mod[3:4]
    h2_ref[...] = h2.astype(BF16)
    if not route:
        return
    h_hi = h2.astype(BF16)
    h_lo = (h2 - h_hi.astype(F32)).astype(BF16)
    both = jnp.dot(h_hi, wr_ref[...], preferred_element_type=F32)
    logits = (both[:, :ROUTE_W] + both[:, ROUTE_W:]
              + jnp.dot(h_lo, wr_ref[:, :ROUTE_W], preferred_element_type=F32) + br_ref[...])
    lane = lax.broadcasted_iota(jnp.int32, logits.shape, 1).astype(F32)
    neg = float(np.finfo(np.float32).min)
    far = float(ROUTE_W)
    logits = jnp.where(lane < N_EXPERTS, logits, neg)
    m1 = logits.max(axis=-1, keepdims=True)
    i1 = jnp.where(logits == m1, lane, far).min(axis=-1, keepdims=True)
    rest = jnp.where(lane == i1, neg, logits)
    m2 = rest.max(axis=-1, keepdims=True)
    i2 = jnp.where(rest == m2, lane, far).min(axis=-1, keepdims=True)
    e2 = jnp.exp(m2 - m1)
    den = 1.0 + e2
    sel = jnp.where(lane == i1, 1.0, jnp.where(lane == i2, 1.0, 0.0))
    rank = jnp.dot(tri_ref[...], sel.astype(BF16), preferred_element_type=F32)
    r1 = jnp.where(lane == i1, rank, 0.0).sum(axis=-1, keepdims=True)
    r2 = jnp.where(lane == i2, rank, 0.0).sum(axis=-1, keepdims=True)
    out = jnp.zeros_like(logits)
    for j, col in enumerate((1.0 / den, e2 / den, i1, i2, r1, r2)):
        out = jnp.where(lane == j, col, out)
    route_ref[...] = out
    rt_ref[...] = out.T[:8, :]
    cnt_ref[...] = jnp.broadcast_to(sel.sum(axis=0, keepdims=True), cnt_ref.shape)


def _pool_inv_count(seq):
    t = np.arange(seq)
    cols = []
    for win in POOL_WINDOWS:
        lo = np.clip(t - win // 2, 0, seq)
        hi = np.clip(t + win // 2, 0, seq)
        cols.append(np.repeat((hi - lo).astype(np.float32)[:, None], POOL_CH, axis=1))
    return np.concatenate(cols, axis=1)


def _block_diag_pool(pool_w):
    eye = jnp.eye(len(POOL_WINDOWS), dtype=F32)
    m = eye[:, None, :, None] * pool_w[:, :, None, :]
    return m.reshape(POOL_W, POOL_W)


def _mixout(x, attn, ssm_tm, u_pool, mods_l, pool_w, pool_scale, w_out_bf, g2, latent, router):
    b, seq, _ = x.shape
    route = router is not None
    mod_row = (lambda i: i) if latent else (lambda i: CTX_ROW)
    const = lambda shape: pl.BlockSpec(shape, lambda i: (0,) * len(shape))
    cnt = _pool_inv_count(seq)
    in_specs = [
        pl.BlockSpec((None, seq, D_MODEL), lambda i: (i, 0, 0)),
        pl.BlockSpec((None, seq, ATTN_W), lambda i: (i, 0, 0)),
        pl.BlockSpec((None, seq, SSM_W), lambda i: (i, 0, 0)),
        pl.BlockSpec((None, seq, POOL_W), lambda i: (i, 0, 0)),
        pl.BlockSpec((None, 6, D_MODEL), lambda i: (mod_row(i), 0, 0)),
        const((seq, POOL_W)),
        const((POOL_W, POOL_W)),
        const((1, POOL_W)),
        const((D_MODEL, D_MODEL)),
        const((1, D_MODEL)),
    ]
    args = [x, attn, ssm_tm, u_pool, mods_l, jnp.asarray(1.0 / cnt, dtype=F32),
            _block_diag_pool(pool_w).astype(BF16), pool_scale.reshape(1, POOL_W), w_out_bf,
            g2.reshape(1, D_MODEL)]
    row_spec = lambda width: pl.BlockSpec((None, seq, width), lambda i: (i, 0, 0))
    if route:
        wr, br = router
        tri = np.tril(np.ones((seq, seq), np.float32), -1)
        wr_pad = jnp.pad(wr, ((0, 0), (0, ROUTE_W - N_EXPERTS)))
        wr_hi = wr_pad.astype(BF16)
        wr_lo = (wr_pad - wr_hi.astype(F32)).astype(BF16)
        in_specs += [const((D_MODEL, 2 * ROUTE_W)), const((1, ROUTE_W)), const((seq, seq))]
        args += [jnp.concatenate([wr_hi, wr_lo], axis=1),
                 jnp.pad(br, (0, ROUTE_W - N_EXPERTS)).reshape(1, ROUTE_W),
                 jnp.asarray(tri, dtype=BF16)]
        out_shape = [jax.ShapeDtypeStruct((b, seq, D_MODEL), F32),
                     jax.ShapeDtypeStruct((b, seq, D_MODEL), BF16),
                     jax.ShapeDtypeStruct((b, seq, ROUTE_W), F32),
                     jax.ShapeDtypeStruct((b, 8, seq), F32),
                     jax.ShapeDtypeStruct((b, 8, ROUTE_W), F32)]
        out_specs = [row_spec(D_MODEL), row_spec(D_MODEL), row_spec(ROUTE_W),
                     pl.BlockSpec((None, 8, seq), lambda i: (i, 0, 0)),
                     pl.BlockSpec((None, 8, ROUTE_W), lambda i: (i, 0, 0))]
    else:
        out_shape = [jax.ShapeDtypeStruct((b, seq, D_MODEL), F32), jax.ShapeDtypeStruct((b, seq, D_MODEL), BF16)]
        out_specs = [row_spec(D_MODEL), row_spec(D_MODEL)]
    return pl.pallas_call(
        functools.partial(_mixout_kernel, route=route),
        out_shape=out_shape, grid=(b,), in_specs=in_specs, out_specs=out_specs,
        compiler_params=pltpu.CompilerParams(
            dimension_semantics=("arbitrary",), vmem_limit_bytes=VMEM_LIMIT),
        name="mixout_%s%s" % ("latent" if latent else "context", "_route" if route else ""),
    )(*args)


def _ffn_kernel(h_ref, x_ref, mod_ref, wg_ref, wu_ref, wd_ref, o_ref, acc_ref):
    e = pl.program_id(1)

    @pl.when(e == 0)
    def _():
        acc_ref[...] = jnp.zeros_like(acc_ref)

    h = h_ref[...]
    a = _silu(jnp.dot(h, wg_ref[...], preferred_element_type=F32)) * \
        jnp.dot(h, wu_ref[...], preferred_element_type=F32)
    acc_ref[...] += jnp.dot(a.astype(BF16), wd_ref[...], preferred_element_type=F32)

    @pl.when(e == pl.num_programs(1) - 1)
    def _():
        o_ref[...] = x_ref[...] + mod_ref[5:6] * acc_ref[...]


def _ffn(h2, x1, mods_l, wg, wu, wd, latent, seq):
    rows = h2.shape[0]
    tm = 512
    ff = FF_EXPERT
    n_chunk = wg.shape[1] // ff
    mod_row = (lambda i: (i * tm) // seq) if latent else (lambda i: CTX_ROW)
    return pl.pallas_call(
        _ffn_kernel,
        out_shape=jax.ShapeDtypeStruct((rows, D_MODEL), F32),
        grid=(rows // tm, n_chunk),
        in_specs=[pl.BlockSpec((tm, D_MODEL), lambda i, e: (i, 0)),
                  pl.BlockSpec((tm, D_MODEL), lambda i, e: (i, 0)),
                  pl.BlockSpec((None, 6, D_MODEL), lambda i, e: (mod_row(i), 0, 0)),
                  pl.BlockSpec((D_MODEL, ff), lambda i, e: (0, e)),
                  pl.BlockSpec((D_MODEL, ff), lambda i, e: (0, e)),
                  pl.BlockSpec((ff, D_MODEL), lambda i, e: (e, 0))],
        out_specs=pl.BlockSpec((tm, D_MODEL), lambda i, e: (i, 0)),
        scratch_shapes=[pltpu.VMEM((tm, D_MODEL), F32)],
        compiler_params=pltpu.CompilerParams(
            dimension_semantics=("arbitrary", "arbitrary"), vmem_limit_bytes=VMEM_LIMIT),
        name="ffn_dense",
    )(h2, x1, mods_l, wg, wu, wd)


def _route_tables(route_t, cnt, seq):
    b = route_t.shape[0]
    per_block = MOE_BLOCK // seq
    nb = b // per_block
    c = cnt[:, 0, :N_EXPERTS].astype(jnp.int32).reshape(nb, per_block, N_EXPERTS)
    before = jnp.cumsum(c, axis=1) - c
    total = c.sum(axis=1)
    aligned = (total + 7) // 8 * 8
    starts = jnp.cumsum(aligned, axis=1) - aligned
    base = (starts[:, None, :] + before).reshape(b, N_EXPERTS)
    expert = route_t[:, 2:4, :].astype(jnp.int32)
    rank = route_t[:, 4:6, :].astype(jnp.int32)
    slot = rank
    for e in range(N_EXPERTS):
        slot = slot + jnp.where(expert == e, base[:, e][:, None, None], 0)
    slots = slot.reshape(nb, per_block, 2, seq).transpose(0, 2, 1, 3).reshape(nb, 2, MOE_BLOCK)
    return starts.reshape(-1), total.reshape(-1), slots


MOE_SLOTS = 2 * MOE_BLOCK
MOE_ROWS = MOE_SLOTS + 8 * N_EXPERTS + MOE_TILE


MOE_GROUP = 16


def _moe_kernel(starts_ref, counts_ref, slots_ref, h_ref, route_ref, wg_ref, wu_ref, wd_ref, f_ref,
                xs_ref, stage_ref):
    b = pl.program_id(0)
    e = pl.program_id(1)
    half = D_MODEL // 2

    @pl.when(e == 0)
    def _():
        xs_ref[MOE_SLOTS:, :] = jnp.zeros((MOE_ROWS - MOE_SLOTS, half), jnp.uint32)
        zero_row = jnp.zeros((1, half), jnp.uint32)
        for g in range(N_EXPERTS):
            end = starts_ref[b * N_EXPERTS + g] + counts_ref[b * N_EXPERTS + g]
            for r in range(7):
                xs_ref[pl.ds(end + r, 1), :] = zero_row

        def put(g, c):
            t0 = pl.multiple_of(g * MOE_GROUP, MOE_GROUP)
            hb = h_ref[pl.ds(t0, MOE_GROUP), :].astype(F32)
            stage_ref[:MOE_GROUP, :] = _pack2(hb[:, :half], hb[:, half:])
            for r in range(MOE_GROUP):
                row = stage_ref[r:r + 1, :]
                xs_ref[pl.ds(slots_ref[0, t0 + r], 1), :] = row
                xs_ref[pl.ds(slots_ref[1, t0 + r], 1), :] = row
            return c
        lax.fori_loop(0, MOE_BLOCK // MOE_GROUP, put, 0)

    start = starts_ref[b * N_EXPERTS + e]
    count = counts_ref[b * N_EXPERTS + e]

    def tile(j, c):
        s = pl.multiple_of(start + j * MOE_TILE, 8)
        xg = xs_ref[pl.ds(s, MOE_TILE), :]
        x_lo32 = _unpack2(xg, 0)
        x_hi32 = _unpack2(xg, 1)
        x_lo = x_lo32.astype(BF16)
        x_hi = x_hi32.astype(BF16)
        hg = (jnp.dot(x_lo, wg_ref[:half, :], preferred_element_type=F32)
              + jnp.dot(x_hi, wg_ref[half:, :], preferred_element_type=F32))
        hu = (jnp.dot(x_lo, wu_ref[:half, :], preferred_element_type=F32)
              + jnp.dot(x_hi, wu_ref[half:, :], preferred_element_type=F32))
        a = (_silu(hg) * hu).astype(BF16)
        y = jnp.dot(a, wd_ref[...], preferred_element_type=F32)
        valid = lax.broadcasted_iota(jnp.int32, (MOE_TILE, half), 0) < count - j * MOE_TILE
        xs_ref[pl.ds(s, MOE_TILE), :] = _pack2(jnp.where(valid, y[:, :half], x_lo32),
                                               jnp.where(valid, y[:, half:], x_hi32))
        return c
    lax.fori_loop(0, (count + MOE_TILE - 1) // MOE_TILE, tile, 0)

    @pl.when(e == pl.num_programs(1) - 1)
    def _():
        def take(g, c):
            t0 = pl.multiple_of(g * MOE_GROUP, MOE_GROUP)
            for r in range(MOE_GROUP):
                stage_ref[r:r + 1, :] = xs_ref[pl.ds(slots_ref[0, t0 + r], 1), :]
                stage_ref[MOE_GROUP + r:MOE_GROUP + r + 1, :] = xs_ref[pl.ds(slots_ref[1, t0 + r], 1), :]
            z0 = stage_ref[:MOE_GROUP, :]
            z1 = stage_ref[MOE_GROUP:, :]
            route = route_ref[pl.ds(t0, MOE_GROUP), :]
            w1 = route[:, 0:1]
            w2 = route[:, 1:2]
            f_ref[pl.ds(t0, MOE_GROUP), :half] = (w1 * _unpack2(z0, 0) + w2 * _unpack2(z1, 0)).astype(BF16)
            f_ref[pl.ds(t0, MOE_GROUP), half:] = (w1 * _unpack2(z0, 1) + w2 * _unpack2(z1, 1)).astype(BF16)
            return c
        lax.fori_loop(0, MOE_BLOCK // MOE_GROUP, take, 0)


def _moe_experts(h2, route, starts, counts, slots, wg, wu, wd):
    rows = h2.shape[0]
    nb = rows // MOE_BLOCK
    ff = wg.shape[2]
    grid_spec = pltpu.PrefetchScalarGridSpec(
        num_scalar_prefetch=2,
        grid=(nb, N_EXPERTS),
        in_specs=[
            pl.BlockSpec((None, 2, MOE_BLOCK), lambda b, e, st, ct: (b, 0, 0), memory_space=pltpu.SMEM),
            pl.BlockSpec((MOE_BLOCK, D_MODEL), lambda b, e, st, ct: (b, 0)),
            pl.BlockSpec((MOE_BLOCK, ROUTE_W), lambda b, e, st, ct: (b, 0)),
            pl.BlockSpec((None, D_MODEL, ff), lambda b, e, st, ct: (e, 0, 0)),
            pl.BlockSpec((None, D_MODEL, ff), lambda b, e, st, ct: (e, 0, 0)),
            pl.BlockSpec((None, ff, D_MODEL), lambda b, e, st, ct: (e, 0, 0)),
        ],
        out_specs=pl.BlockSpec((MOE_BLOCK, D_MODEL), lambda b, e, st, ct: (b, 0)),
        scratch_shapes=[pltpu.VMEM((MOE_ROWS, D_MODEL // 2), jnp.uint32),
                        pltpu.VMEM((2 * MOE_GROUP, D_MODEL // 2), jnp.uint32)],
    )
    return pl.pallas_call(
        _moe_kernel,
        out_shape=jax.ShapeDtypeStruct((rows, D_MODEL), BF16),
        grid_spec=grid_spec,
        compiler_params=pltpu.CompilerParams(
            dimension_semantics=("arbitrary", "arbitrary"), vmem_limit_bytes=VMEM_LIMIT),
        name="moe_experts",
    )(starts, counts, slots, h2, route, wg, wu, wd)


def _final_kernel(x_ref, f_ref, mod_ref, fg_ref, o_ref):
    y = x_ref[...] + mod_ref[5:6] * f_ref[...].astype(F32)
    o_ref[...] = _rms(y, fg_ref[...])


def _final(x1, f, mods_l, final_g, latent, seq):
    rows = x1.shape[0]
    tr = 512
    mod_row = (lambda i: (i * tr) // seq) if latent else (lambda i: CTX_ROW)
    return pl.pallas_call(
        _final_kernel,
        out_shape=jax.ShapeDtypeStruct((rows, D_MODEL), F32),
        grid=(rows // tr,),
        in_specs=[pl.BlockSpec((tr, D_MODEL), lambda i: (i, 0)),
                  pl.BlockSpec((tr, D_MODEL), lambda i: (i, 0)),
                  pl.BlockSpec((None, 6, D_MODEL), lambda i: (mod_row(i), 0, 0)),
                  pl.BlockSpec((1, D_MODEL), lambda i: (0, 0))],
        out_specs=pl.BlockSpec((tr, D_MODEL), lambda i: (i, 0)),
        compiler_params=pltpu.CompilerParams(
            dimension_semantics=("arbitrary",), vmem_limit_bytes=VMEM_LIMIT),
        name="final_norm",
    )(x1, f, mods_l, final_g.reshape(1, D_MODEL))


def _dup_heads(a):
    b, n = a.shape[:2]
    return jnp.repeat(a, 2, axis=2).reshape(b, n, 2 * KV_W).astype(BF16)


def kernel(x_prompt, x_sample, c, cache_k, cache_v, state_ssm_re, state_ssm_im, c_ctx, mod_w, mod_b, norm1_g, norm2_g, w_in, w_out, q_norm_g, k_norm_g, ssm_a_re, ssm_a_im, ssm_log_dt, ssm_b_re, ssm_b_im, ssm_c_re, ssm_c_im, ssm_d, ssm_glu_w, pool_w, pool_scale, ffn_w_gate, ffn_w_up, ffn_w_down, moe_router_w, moe_router_b, moe_w_gate, moe_w_up, moe_w_down, final_g):
    bp, lp, _ = x_prompt.shape
    bs, ls, _ = x_sample.shape
    assert bs == SEQ_GROUP and bp % SEQ_GROUP == 0

    cond = jnp.zeros((MOD_ROWS, D_MODEL), F32).at[:bs].set(c).at[CTX_ROW].set(c_ctx)
    mods = _modulation(cond, mod_w, mod_b).reshape(DEPTH, MOD_ROWS, 6, D_MODEL)

    lam_re, lam_im, bb_re, bb_im = _ssm_prep(ssm_a_re, ssm_a_im, ssm_log_dt, ssm_b_re, ssm_b_im)
    ssm_mats = _ssm_matrices(lam_re, lam_im, bb_re, bb_im, ssm_c_re, ssm_c_im)

    xp, xs = x_prompt, x_sample
    new_k, new_v, new_state = [], [], []
    for l in range(DEPTH):
        mods_l = mods[l]
        w_in_bf = w_in[l].astype(BF16)
        w_out_bf = w_out[l].astype(BF16)
        glu_bf = ssm_glu_w[l].astype(BF16)
        bw, cw, lam = ssm_mats[l]
        moe = l % 2 == 1
        i = l // 2
        router = (moe_router_w[i], moe_router_b[i]) if moe else None
        if moe:
            wg, wu, wd = (moe_w_gate[i].astype(BF16), moe_w_up[i].astype(BF16), moe_w_down[i].astype(BF16))
        else:
            wg, wu, wd = (ffn_w_gate[i].astype(BF16), ffn_w_up[i].astype(BF16), ffn_w_down[i].astype(BF16))
        for latent in (False, True):
            x = xs if latent else xp
            b, seq, _ = x.shape
            nbg = b // SEQ_GROUP
            rows = b * seq
            outs = _inproj(x.reshape(rows, D_MODEL), mods_l, norm1_g[l], w_in_bf, q_norm_g[l], k_norm_g[l],
                           latent, seq)
            q, kd, vd = (a.reshape(b, seq, a.shape[-1]) for a in outs[:3])
            u_ssm = outs[3]
            u_pool = outs[4].reshape(b, seq, POOL_W)
            if latent:
                kds = [kd, _dup_heads(cache_k[:, l])]
                vds = [vd, _dup_heads(cache_v[:, l])]
                h0 = jnp.concatenate([
                    state_ssm_re[:, l, 0].reshape(b, SSM_GP), state_ssm_im[:, l, 0].reshape(b, SSM_GP),
                    state_ssm_re[:, l, 1].reshape(b, SSM_GP), state_ssm_im[:, l, 1].reshape(b, SSM_GP)],
                    axis=1).reshape(nbg, SEQ_GROUP, 4 * SSM_GP)
            else:
                kds, vds = [kd], [vd]
                new_k.append(outs[5])
                new_v.append(outs[6])
                h0 = jnp.zeros((nbg, SEQ_GROUP, 4 * SSM_GP), F32)
            attn = _attention(q, kds, vds)
            y_ssm, hf = _ssm(u_ssm, bw, cw, lam, h0, ssm_d[l], glu_bf, seq)
            if not latent:
                new_state.append(hf.reshape(b, 2, 2, SSM_GROUPS, SSM_STATE))
            res = _mixout(x, attn, y_ssm.reshape(b, seq, SSM_W), u_pool, mods_l,
                          pool_w[l], pool_scale[l], w_out_bf, norm2_g[l], latent, router)
            x1 = res[0].reshape(rows, D_MODEL)
            if moe:
                route = res[2].reshape(rows, ROUTE_W)
                starts, counts, slots = _route_tables(res[3], res[4], seq)
                f = _moe_experts(res[1].reshape(rows, D_MODEL), route, starts, counts, slots, wg, wu, wd)
                y = _final(x1, f, mods_l, final_g, latent, seq)
            else:
                y = _ffn(res[1].reshape(rows, D_MODEL), x1, mods_l, wg, wu, wd, latent, seq)
            y = y.reshape(b, seq, D_MODEL)
            if latent:
                xs = y
            else:
                xp = y

    def cache_out(parts):
        a = jnp.stack(parts, axis=1).reshape(bp, DEPTH, N_KV_HEADS, HEAD_DIM, lp)
        return a.transpose(0, 1, 4, 2, 3)
    new_cache_k = cache_out(new_k)
    new_cache_v = cache_out(new_v)
    st = jnp.stack(new_state, axis=1)
    return (xp, xs, new_cache_k, new_cache_v, st[:, :, :, 0], st[:, :, :, 1])
```

```python
import functools

import numpy as np
import jax
import jax.numpy as jnp
from jax import lax
from jax.experimental import pallas as pl
from jax.experimental.pallas import tpu as pltpu

D_MODEL = 1024
DEPTH = 2
GRID_W = 64
ATTN_W = 512
HEAD_DIM = 64
N_HEADS = 8
N_KV_HEADS = 2
KV_W = 128
ROPE_THETA = 10000.0
SSM_W = 256
SSM_CH = 16
SSM_GROUPS = 16
SSM_STATE = 64
SSM_GP = SSM_GROUPS * SSM_STATE
POOL_W = 256
POOL_WINDOWS = (2, 4, 8, 16)
POOL_CH = 64
POOL_PAD = max(POOL_WINDOWS) // 2
IN_COLS = ATTN_W + 2 * KV_W + SSM_W + POOL_W
FF_DENSE = 2816
N_EXPERTS = 8
FF_EXPERT = 1408
EPS = 1e-6
LOG2_E = 1.4426950408889634

SEQ_GROUP = 8
SSM_CHUNK = 32
SSM_COLBLK = 512
ROUTE_W = 128
MOE_BLOCK = 2048
MOE_TILE = 128
MOD_ROWS = 16
CTX_ROW = 8
VMEM_LIMIT = 56 * 1024 * 1024

F32 = jnp.float32
BF16 = jnp.bfloat16


def _silu(x):
    return x * jax.nn.sigmoid(x)


def _rms(x32, g):
    return x32 * lax.rsqrt(jnp.mean(x32 * x32, axis=-1, keepdims=True) + EPS) * g


def _mod_kernel(cond_ref, w_ref, b_ref, o_ref):
    s = _silu(cond_ref[...])
    w = w_ref[...]
    s_hi = s.astype(BF16)
    s_lo = (s - s_hi.astype(F32)).astype(BF16)
    w_hi = w.astype(BF16)
    w_lo = (w - w_hi.astype(F32)).astype(BF16)
    o_ref[...] = (jnp.dot(s_hi, w_hi, preferred_element_type=F32)
                  + jnp.dot(s_hi, w_lo, preferred_element_type=F32)
                  + jnp.dot(s_lo, w_hi, preferred_element_type=F32) + b_ref[...])


def _modulation(cond, mod_w, mod_b):
    tn = 1536
    n = 6 * D_MODEL
    return pl.pallas_call(
        _mod_kernel,
        out_shape=jax.ShapeDtypeStruct((DEPTH, MOD_ROWS, n), F32),
        grid=(DEPTH, n // tn),
        in_specs=[
            pl.BlockSpec((MOD_ROWS, D_MODEL), lambda l, j: (0, 0)),
            pl.BlockSpec((None, D_MODEL, tn), lambda l, j: (l, 0, j)),
            pl.BlockSpec((None, 1, tn), lambda l, j: (l, 0, j)),
        ],
        out_specs=pl.BlockSpec((None, MOD_ROWS, tn), lambda l, j: (l, 0, j)),
        compiler_params=pltpu.CompilerParams(
            dimension_semantics=("arbitrary", "arbitrary"), vmem_limit_bytes=VMEM_LIMIT),
        name="modulation",
    )(cond, mod_w, mod_b.reshape(DEPTH, 1, n))


def _ssm_prep_kernel(are_ref, aim_ref, dt_ref, bre_ref, bim_ref, cre_ref, cim_ref, bw_ref, cw_ref, lam_ref):
    row_group = lax.broadcasted_iota(jnp.int32, (SSM_W, SSM_GP), 0) // SSM_CH
    lane_group = lax.broadcasted_iota(jnp.int32, (SSM_W, SSM_GP), 1) // SSM_STATE
    own = row_group == lane_group
    for ld in range(2 * DEPTH):
        l, d = divmod(ld, 2)
        a_re = are_ref[ld:ld + 1, :]
        a_im = aim_ref[ld:ld + 1, :]
        dt = jnp.exp(dt_ref[ld:ld + 1, :])
        mag = jnp.exp(a_re * dt)
        l_re = mag * jnp.cos(a_im * dt)
        l_im = mag * jnp.sin(a_im * dt)
        lam_ref[l, 2 * d] = jnp.broadcast_to(l_re, (SEQ_GROUP, SSM_GP))
        lam_ref[l, 2 * d + 1] = jnp.broadcast_to(l_im, (SEQ_GROUP, SSM_GP))
        x = l_re - 1.0
        den = a_re * a_re + a_im * a_im
        f_re = (x * a_re + l_im * a_im) / den
        f_im = (l_im * a_re - x * a_im) / den
        b_re = bre_ref[ld]
        b_im = bim_ref[ld]
        bbar = (f_re * b_re - f_im * b_im, f_re * b_im + f_im * b_re)
        cmat = (cre_ref[ld], -cim_ref[ld])
        for ri in range(2):
            c0 = (2 * d + ri) * SSM_GP
            tiled = jnp.concatenate([bbar[ri]] * SSM_GROUPS, axis=0)
            bw_ref[l, :, c0:c0 + SSM_GP] = jnp.where(own, tiled, 0.0).astype(BF16)
            cw_ref[l, c0:c0 + SSM_GP, :] = jnp.where(own, cmat[ri], 0.0).T.astype(BF16)


def _ssm_prep(a_re, a_im, log_dt, b_re, b_im, c_re, c_im):
    ld = DEPTH * 2
    are = a_re.reshape(ld, SSM_GP)
    aim = a_im.reshape(ld, SSM_GP)
    dt = jnp.repeat(log_dt.reshape(ld, SSM_GROUPS), SSM_STATE, axis=1)
    to_rows = lambda b: b.reshape(ld, SSM_GROUPS, SSM_STATE, SSM_CH).transpose(0, 3, 1, 2).reshape(ld, SSM_CH, SSM_GP)
    repeat_lanes = lambda c: jnp.tile(c.reshape(ld, SSM_W, SSM_STATE), (1, 1, SSM_GROUPS))
    return pl.pallas_call(
        _ssm_prep_kernel,
        out_shape=(jax.ShapeDtypeStruct((DEPTH, SSM_W, 4 * SSM_GP), BF16),
                   jax.ShapeDtypeStruct((DEPTH, 4 * SSM_GP, SSM_W), BF16),
                   jax.ShapeDtypeStruct((DEPTH, 4, SEQ_GROUP, SSM_GP), F32)),
        compiler_params=pltpu.CompilerParams(vmem_limit_bytes=VMEM_LIMIT),
        name="ssm_prep",
    )(are, aim, dt, to_rows(b_re), to_rows(b_im), repeat_lanes(c_re), repeat_lanes(c_im))


def _inproj_kernel(*refs, latent):
    if latent:
        (x_ref, mod_ref, g_ref, w_ref, qg_ref, kg_ref, sq_ref, sk_ref, cos_ref, sa_ref, sb_ref,
         q_ref, kd_ref, vd_ref, us_ref, up_ref) = refs
    else:
        (x_ref, mod_ref, g_ref, w_ref, qg_ref, kg_ref, sq_ref, sk_ref,
         q_ref, kd_ref, vd_ref, us_ref, up_ref, ko_ref, vo_ref) = refs
    x = x_ref[...]
    mod = mod_ref[...]
    h = _rms(x, g_ref[...]) * (1.0 + mod[1:2]) + mod[0:1]
    p = jnp.dot(h.astype(BF16), w_ref[...], preferred_element_type=F32)
    q = p[:, :ATTN_W]
    k = p[:, ATTN_W:ATTN_W + KV_W]
    v = p[:, ATTN_W + KV_W:ATTN_W + 2 * KV_W]
    q_ms = jnp.dot((q * q).astype(BF16), sq_ref[...], preferred_element_type=F32)
    k_ms = jnp.dot((k * k).astype(BF16), sk_ref[...], preferred_element_type=F32)
    q = q * lax.rsqrt(q_ms + EPS) * qg_ref[...]
    k = k * lax.rsqrt(k_ms + EPS) * kg_ref[...]
    if latent:
        cos = cos_ref[...]
        sa = sa_ref[...]
        sb = sb_ref[...]
        cos4 = jnp.concatenate([cos] * 4, axis=1)
        sa4 = jnp.concatenate([sa] * 4, axis=1)
        sb4 = jnp.concatenate([sb] * 4, axis=1)
        q = (q * cos4 + pltpu.roll(q, ATTN_W - 16, axis=1) * sa4 + pltpu.roll(q, 16, axis=1) * sb4)
        k = (k * cos + pltpu.roll(k, KV_W - 16, axis=1) * sa + pltpu.roll(k, 16, axis=1) * sb)
    else:
        seq = ko_ref.shape[-1]
        for s in range(ko_ref.shape[0]):
            ko_ref[s] = k[s * seq:(s + 1) * seq, :].T
            vo_ref[s] = v[s * seq:(s + 1) * seq, :].T
    q_ref[...] = (q * (HEAD_DIM ** -0.5 * LOG2_E)).astype(BF16)
    lane = lax.broadcasted_iota(jnp.int32, k.shape, 1)
    first = lane < HEAD_DIM
    k_sw = pltpu.roll(k, HEAD_DIM, axis=1)
    v_sw = pltpu.roll(v, HEAD_DIM, axis=1)
    kd_ref[:, :KV_W] = jnp.where(first, k, k_sw).astype(BF16)
    kd_ref[:, KV_W:] = jnp.where(first, k_sw, k).astype(BF16)
    vd_ref[:, :KV_W] = jnp.where(first, v, v_sw).astype(BF16)
    vd_ref[:, KV_W:] = jnp.where(first, v_sw, v).astype(BF16)
    o3 = ATTN_W + 2 * KV_W
    us_ref[0] = p[:, o3:o3 + SSM_W // 2]
    us_ref[1] = p[:, o3 + SSM_W // 2:o3 + SSM_W]
    up_ref[...] = p[:, o3 + SSM_W:]


def _head_mean_matrix(width):
    i = np.arange(width) // HEAD_DIM
    return jnp.asarray((i[:, None] == i[None, :]).astype(np.float32) / HEAD_DIM, dtype=BF16)


def _rope_tables(seq):
    t = np.arange(seq)
    row = (t // GRID_W).astype(np.float64)
    col = (t % GRID_W).astype(np.float64)
    nf = HEAD_DIM // 4
    inv = ROPE_THETA ** (-np.arange(nf, dtype=np.float64) / nf)
    ang_r = row[:, None] * inv[None, :]
    ang_c = col[:, None] * inv[None, :]
    cos = np.concatenate([np.cos(ang_r), np.cos(ang_r), np.cos(ang_c), np.cos(ang_c)], axis=1)
    sin = np.concatenate([np.sin(ang_r), np.sin(ang_r), np.sin(ang_c), np.sin(ang_c)], axis=1)
    lower = np.tile(np.concatenate([np.ones(nf), np.zeros(nf)]), 2)[None, :]
    sa = -sin * lower
    sb = sin * (1.0 - lower)
    tile = lambda a: jnp.asarray(np.tile(a, (1, 2)), dtype=F32)
    return tile(cos), tile(sa), tile(sb)


def _inproj(x, mods_l, g1, w_in_bf, qg, kg, latent, seq):
    rows = x.shape[0]
    tr = 512
    tiles_per_seq = max(seq // tr, 1)
    mod_row = (lambda i: (i * tr) // seq) if latent else (lambda i: CTX_ROW)
    const = lambda shape: pl.BlockSpec(shape, lambda i: (0,) * len(shape))
    row_spec = lambda width: pl.BlockSpec((tr, width), lambda i: (i, 0))
    in_specs = [
        row_spec(D_MODEL),
        pl.BlockSpec((None, 6, D_MODEL), lambda i: (mod_row(i), 0, 0)),
        const((1, D_MODEL)),
        const((D_MODEL, IN_COLS)),
        const((1, ATTN_W)),
        const((1, KV_W)),
        const((ATTN_W, ATTN_W)),
        const((KV_W, KV_W)),
    ]
    args = [x, mods_l, g1.reshape(1, D_MODEL), w_in_bf,
            jnp.tile(qg, N_HEADS).reshape(1, ATTN_W), jnp.tile(kg, N_KV_HEADS).reshape(1, KV_W),
            _head_mean_matrix(ATTN_W), _head_mean_matrix(KV_W)]
    widths = [(ATTN_W, BF16), (2 * KV_W, BF16), (2 * KV_W, BF16), (None, F32), (POOL_W, F32)]
    if latent:
        assert seq % tr == 0
        cos, sa, sb = _rope_tables(seq)
        in_specs += [pl.BlockSpec((tr, KV_W), lambda i: (i % tiles_per_seq, 0))] * 3
        args += [cos, sa, sb]
    out_shape = [jax.ShapeDtypeStruct((rows, w) if w else (2, rows, SSM_W // 2), dt) for w, dt in widths]
    out_specs = [row_spec(w) if w else pl.BlockSpec((2, tr, SSM_W // 2), lambda i: (0, i, 0)) for w, _ in widths]
    if not latent:
        assert tr % seq == 0
        out_shape += [jax.ShapeDtypeStruct((rows // seq, KV_W, seq), F32)] * 2
        out_specs += [pl.BlockSpec((tr // seq, KV_W, seq), lambda i: (i, 0, 0))] * 2
    return pl.pallas_call(
        functools.partial(_inproj_kernel, latent=latent),
        out_shape=out_shape, grid=(rows // tr,), in_specs=in_specs, out_specs=out_specs,
        compiler_params=pltpu.CompilerParams(
            dimension_semantics=("arbitrary",), vmem_limit_bytes=VMEM_LIMIT),
        name="inproj_latent" if latent else "inproj_context",
    )(*args)


def _attn_kernel(*refs, n_kv_src):
    q_ref = refs[0]
    k_refs = refs[1:1 + n_kv_src]
    v_refs = refs[1 + n_kv_src:1 + 2 * n_kv_src]
    o_ref = refs[1 + 2 * n_kv_src]
    tq = q_ref.shape[0]
    lane = lax.broadcasted_iota(jnp.int32, (tq, KV_W), 1)
    first = lane < HEAD_DIM
    for pair in range(N_HEADS // 2):
        kv = pair // 2
        cols = slice(pair * KV_W, (pair + 1) * KV_W)
        kcols = slice(kv * KV_W, (kv + 1) * KV_W)
        qp = q_ref[:, cols]
        halves = []
        for half in range(2):
            qm = jnp.where(first if half == 0 else jnp.logical_not(first), qp, jnp.zeros_like(qp))
            scores = [lax.dot_general(qm, k_ref[:, kcols], (((1,), (1,)), ((), ())),
                                      preferred_element_type=F32) for k_ref in k_refs]
            m = scores[0].max(axis=-1, keepdims=True)
            for s in scores[1:]:
                m = jnp.maximum(m, s.max(axis=-1, keepdims=True))
            den = jnp.zeros((tq, 1), F32)
            acc = jnp.zeros((tq, KV_W), F32)
            for s, v_ref in zip(scores, v_refs):
                p = jnp.exp2(s - m)
                den = den + p.sum(axis=-1, keepdims=True)
                acc = acc + jnp.dot(p.astype(BF16), v_ref[:, kcols], preferred_element_type=F32)
            halves.append(acc / den)
        o_ref[:, cols] = jnp.where(first, halves[0], halves[1]).astype(o_ref.dtype)


def _attention(q, kds, vds):
    b, seq, _ = q.shape
    tq = min(seq, 512)
    n_src = len(kds)
    kv_spec = lambda a: pl.BlockSpec((None, a.shape[1], 2 * KV_W), lambda i, t: (i, 0, 0))
    return pl.pallas_call(
        functools.partial(_attn_kernel, n_kv_src=n_src),
        out_shape=jax.ShapeDtypeStruct((b, seq, ATTN_W), BF16),
        grid=(b, seq // tq),
        in_specs=[pl.BlockSpec((None, tq, ATTN_W), lambda i, t: (i, t, 0))]
                 + [kv_spec(a) for a in kds] + [kv_spec(a) for a in vds],
        out_specs=pl.BlockSpec((None, tq, ATTN_W), lambda i, t: (i, t, 0)),
        compiler_params=pltpu.CompilerParams(
            dimension_semantics=("arbitrary", "arbitrary"), vmem_limit_bytes=VMEM_LIMIT),
        name="attention_%dsrc" % n_src,
    )(q, *kds, *vds)


def _ssm_kernel(u_ref, bw_ref, cw_ref, lam_ref, h0_ref, d_ref, glu_ref,
                o_ref, hf_ref, y_ref, utm_scr, bu_scr, hs_scr, st_scr):
    rows = u_ref.shape[1]
    seq = rows // SEQ_GROUP
    crow = SSM_CHUNK * SEQ_GROUP
    nchunks = rows // crow
    w2 = 2 * SSM_GP
    lanes = SSM_W // 2

    def time_major(c):
        t0 = c * SSM_CHUNK
        return jnp.concatenate(
            [jnp.concatenate([u_ref[h, pl.ds(t0 + s, SEQ_GROUP, stride=seq), :] for h in range(2)], axis=1)
             for s in range(SSM_CHUNK)], axis=0)

    def y_rows(r, n):
        return jnp.concatenate([y_ref[0, pl.ds(r, n), :], y_ref[1, pl.ds(r, n), :]], axis=1)

    def set_y_rows(r, n, val):
        y_ref[0, pl.ds(r, n), :] = val[:, :lanes]
        y_ref[1, pl.ds(r, n), :] = val[:, lanes:]

    st_scr[...] = h0_ref[...]

    def init_rows(i, c):
        r = pl.multiple_of(i * crow, crow)
        tm = time_major(i)
        set_y_rows(r, crow, tm * d_ref[...])
        utm_scr[pl.ds(r, crow), :] = tm.astype(BF16)
        return c
    lax.fori_loop(0, nchunks, init_rows, 0)

    def chunk(i, c):
        starts = (pl.multiple_of(i * crow, crow), pl.multiple_of((nchunks - 1 - i) * crow, crow))
        for d in range(2):
            bu_scr[d] = jnp.dot(utm_scr[pl.ds(starts[d], crow), :], bw_ref[:, d * w2:(d + 1) * w2],
                                preferred_element_type=F32)
        for d in range(2):
            for cb in range(SSM_GP // SSM_COLBLK):
                re_c = slice(cb * SSM_COLBLK, (cb + 1) * SSM_COLBLK)
                im_c = slice(SSM_GP + cb * SSM_COLBLK, SSM_GP + (cb + 1) * SSM_COLBLK)
                l_re = lam_ref[2 * d, :, re_c]
                l_im = lam_ref[2 * d + 1, :, re_c]
                h_re = st_scr[:, d * w2 + cb * SSM_COLBLK:d * w2 + (cb + 1) * SSM_COLBLK]
                h_im = st_scr[:, d * w2 + SSM_GP + cb * SSM_COLBLK:d * w2 + SSM_GP + (cb + 1) * SSM_COLBLK]
                for s in range(SSM_CHUNK):
                    t = s if d == 0 else SSM_CHUNK - 1 - s
                    r = slice(t * SEQ_GROUP, (t + 1) * SEQ_GROUP)
                    n_re = l_re * h_re - l_im * h_im + bu_scr[d, r, re_c]
                    n_im = l_re * h_im + l_im * h_re + bu_scr[d, r, im_c]
                    hs_scr[d, r, re_c] = n_re
                    hs_scr[d, r, im_c] = n_im
                    h_re, h_im = n_re, n_im
                st_scr[:, d * w2 + cb * SSM_COLBLK:d * w2 + (cb + 1) * SSM_COLBLK] = h_re
                st_scr[:, d * w2 + SSM_GP + cb * SSM_COLBLK:d * w2 + SSM_GP + (cb + 1) * SSM_COLBLK] = h_im
        for d in range(2):
            contrib = jnp.dot(hs_scr[d].astype(BF16), cw_ref[d * w2:(d + 1) * w2, :],
                              preferred_element_type=F32)
            set_y_rows(starts[d], crow, y_rows(starts[d], crow) + contrib)
        return c
    lax.fori_loop(0, nchunks, chunk, 0)

    hf_ref[...] = st_scr[...]

    esteps = 128
    erow = esteps * SEQ_GROUP

    def epilogue(i, c):
        r = pl.multiple_of(i * erow, erow)
        z = jax.nn.gelu(y_rows(r, erow))
        g = jnp.dot(z.astype(BF16), glu_ref[...], preferred_element_type=F32)
        set_y_rows(r, erow, g[:, :SSM_W] * jax.nn.sigmoid(g[:, SSM_W:]))
        t0 = pl.multiple_of(i * esteps, esteps)
        for b in range(SEQ_GROUP):
            for h in range(2):
                o_ref[pl.ds(b * seq + t0, esteps), h * lanes:(h + 1) * lanes] = (
                    y_ref[h, pl.ds(r + b, esteps, stride=SEQ_GROUP), :].astype(BF16))
        return c
    lax.fori_loop(0, rows // erow, epilogue, 0)


def _ssm(u, layer, bw, cw, lam, h0, ssm_d, glu_bf, seq):
    rows = u.shape[1]
    grows = SEQ_GROUP * seq
    nbg = rows // grows
    crow = SSM_CHUNK * SEQ_GROUP
    const = lambda shape: pl.BlockSpec(shape, lambda i: (0,) * len(shape))
    return pl.pallas_call(
        _ssm_kernel,
        out_shape=(jax.ShapeDtypeStruct((rows, SSM_W), BF16),
                   jax.ShapeDtypeStruct((nbg, SEQ_GROUP, 4 * SSM_GP), F32)),
        grid=(nbg,),
        in_specs=[
            pl.BlockSpec((2, grows, SSM_W // 2), lambda i: (0, i, 0)),
            pl.BlockSpec((None, SSM_W, 4 * SSM_GP), lambda i: (layer, 0, 0)),
            pl.BlockSpec((None, 4 * SSM_GP, SSM_W), lambda i: (layer, 0, 0)),
            pl.BlockSpec((None, 4, SEQ_GROUP, SSM_GP), lambda i: (layer, 0, 0, 0)),
            pl.BlockSpec((None, SEQ_GROUP, 4 * SSM_GP), lambda i: (i, 0, 0)),
            const((1, SSM_W)),
            const((SSM_W, 2 * SSM_W)),
        ],
        out_specs=(pl.BlockSpec((grows, SSM_W), lambda i: (i, 0)),
                   pl.BlockSpec((None, SEQ_GROUP, 4 * SSM_GP), lambda i: (i, 0, 0))),
        scratch_shapes=[pltpu.VMEM((2, grows, SSM_W // 2), F32),
                        pltpu.VMEM((grows, SSM_W), BF16),
                        pltpu.VMEM((2, crow, 2 * SSM_GP), F32),
                        pltpu.VMEM((2, crow, 2 * SSM_GP), F32),
                        pltpu.VMEM((SEQ_GROUP, 4 * SSM_GP), F32)],
        compiler_params=pltpu.CompilerParams(
            dimension_semantics=("arbitrary",), vmem_limit_bytes=VMEM_LIMIT),
        name="ssm_scan",
    )(u, bw, cw, lam, h0, ssm_d.reshape(1, SSM_W), glu_bf)


def _pack2(lo, hi):
    return pltpu.pack_elementwise([lo, hi], packed_dtype=BF16)


def _unpack2(w, index):
    return pltpu.unpack_elementwise(w, index=index, packed_dtype=BF16, unpacked_dtype=F32)


def _mixout_kernel(*refs, route):
    if route:
        (x_ref, at_ref, ss_ref, up_ref, mod_ref, invc_ref, pw_ref, ps_ref, wo_ref, g2_ref, wr_ref, br_ref, tri_ref,
         x1_ref, h2_ref, route_ref, rt_ref, cnt_ref) = refs
    else:
        (x_ref, at_ref, ss_ref, up_ref, mod_ref, invc_ref, pw_ref, ps_ref, wo_ref, g2_ref,
         x1_ref, h2_ref) = refs
    u = up_ref[...]
    seq = u.shape[0]
    zpad = jnp.zeros((POOL_PAD, POOL_W), F32)
    ue = jnp.concatenate([zpad, u, zpad], axis=0)
    n_ext = seq + 2 * POOL_PAD
    back = lambda a, k: pltpu.roll(a, k, axis=0)
    ahead = lambda a, k: pltpu.roll(a, n_ext - k, axis=0)
    w2 = ue + back(ue, 1)
    w4 = back(w2, 1) + ahead(w2, 1)
    w8 = back(w4, 2) + ahead(w4, 2)
    w16 = back(w8, 4) + ahead(w8, 4)
    grp = lax.broadcasted_iota(jnp.int32, ue.shape, 1) // POOL_CH
    win = jnp.where(grp == 0, w2, jnp.where(grp == 1, w4, jnp.where(grp == 2, w8, w16)))
    pooled = win[POOL_PAD:POOL_PAD + seq] * invc_ref[...] - u
    pool = jnp.dot(pooled.astype(BF16), pw_ref[...], preferred_element_type=F32) * ps_ref[...]
    mix = jnp.concatenate([at_ref[...], ss_ref[...], pool.astype(BF16)], axis=1)
    o = jnp.dot(mix, wo_ref[...], preferred_element_type=F32)
    mod = mod_ref[...]
    x1 = x_ref[...] + mod[2:3] * o
    x1_ref[...] = x1
    h2 = _rms(x1, g2_ref[...]) * (1.0 + mod[4:5]) + mod[3:4]
    h2_ref[...] = h2.astype(BF16)
    if not route:
        return
    h_hi = h2.astype(BF16)
    h_lo = (h2 - h_hi.astype(F32)).astype(BF16)
    both = jnp.dot(h_hi, wr_ref[...], preferred_element_type=F32)
    logits = (both[:, :ROUTE_W] + both[:, ROUTE_W:]
              + jnp.dot(h_lo, wr_ref[:, :ROUTE_W], preferred_element_type=F32) + br_ref[...])
    lane = lax.broadcasted_iota(jnp.int32, logits.shape, 1).astype(F32)
    neg = float(np.finfo(np.float32).min)
    far = float(ROUTE_W)
    logits = jnp.where(lane < N_EXPERTS, logits, neg)
    m1 = logits.max(axis=-1, keepdims=True)
    i1 = jnp.where(logits == m1, lane, far).min(axis=-1, keepdims=True)
    rest = jnp.where(lane == i1, neg, logits)
    m2 = rest.max(axis=-1, keepdims=True)
    i2 = jnp.where(rest == m2, lane, far).min(axis=-1, keepdims=True)
    e2 = jnp.exp(m2 - m1)
    den = 1.0 + e2
    sel = jnp.where(lane == i1, 1.0, jnp.where(lane == i2, 1.0, 0.0))
    rank = jnp.dot(tri_ref[...], sel.astype(BF16), preferred_element_type=F32)
    r1 = jnp.where(lane == i1, rank, 0.0).sum(axis=-1, keepdims=True)
    r2 = jnp.where(lane == i2, rank, 0.0).sum(axis=-1, keepdims=True)
    out = jnp.zeros_like(logits)
    for j, col in enumerate((1.0 / den, e2 / den, i1, i2, r1, r2)):
        out = jnp.where(lane == j, col, out)
    route_ref[...] = out
    rt_ref[...] = out.T[:8, :]
    cnt_ref[...] = jnp.broadcast_to(sel.sum(axis=0, keepdims=True), cnt_ref.shape)


def _pool_inv_count(seq):
    t = np.arange(seq)
    cols = []
    for win in POOL_WINDOWS:
        lo = np.clip(t - win // 2, 0, seq)
        hi = np.clip(t + win // 2, 0, seq)
        cols.append(np.repeat((hi - lo).astype(np.float32)[:, None], POOL_CH, axis=1))
    return np.concatenate(cols, axis=1)


def _block_diag_pool(pool_w):
    eye = jnp.eye(len(POOL_WINDOWS), dtype=F32)
    m = eye[:, None, :, None] * pool_w[:, :, None, :]
    return m.reshape(POOL_W, POOL_W)


def _mixout(x, attn, ssm_tm, u_pool, mods_l, pool_w, pool_scale, w_out_bf, g2, latent, router):
    b, seq, _ = x.shape
    route = router is not None
    mod_row = (lambda i: i) if latent else (lambda i: CTX_ROW)
    const = lambda shape: pl.BlockSpec(shape, lambda i: (0,) * len(shape))
    cnt = _pool_inv_count(seq)
    in_specs = [
        pl.BlockSpec((None, seq, D_MODEL), lambda i: (i, 0, 0)),
        pl.BlockSpec((None, seq, ATTN_W), lambda i: (i, 0, 0)),
        pl.BlockSpec((None, seq, SSM_W), lambda i: (i, 0, 0)),
        pl.BlockSpec((None, seq, POOL_W), lambda i: (i, 0, 0)),
        pl.BlockSpec((None, 6, D_MODEL), lambda i: (mod_row(i), 0, 0)),
        const((seq, POOL_W)),
        const((POOL_W, POOL_W)),
        const((1, POOL_W)),
        const((D_MODEL, D_MODEL)),
        const((1, D_MODEL)),
    ]
    args = [x, attn, ssm_tm, u_pool, mods_l, jnp.asarray(1.0 / cnt, dtype=F32),
            _block_diag_pool(pool_w).astype(BF16), pool_scale.reshape(1, POOL_W), w_out_bf,
            g2.reshape(1, D_MODEL)]
    row_spec = lambda width: pl.BlockSpec((None, seq, width), lambda i: (i, 0, 0))
    if route:
        wr, br = router
        tri = np.tril(np.ones((seq, seq), np.float32), -1)
        wr_pad = jnp.pad(wr, ((0, 0), (0, ROUTE_W - N_EXPERTS)))
        wr_hi = wr_pad.astype(BF16)
        wr_lo = (wr_pad - wr_hi.astype(F32)).astype(BF16)
        in_specs += [const((D_MODEL, 2 * ROUTE_W)), const((1, ROUTE_W)), const((seq, seq))]
        args += [jnp.concatenate([wr_hi, wr_lo], axis=1),
                 jnp.pad(br, (0, ROUTE_W - N_EXPERTS)).reshape(1, ROUTE_W),
                 jnp.asarray(tri, dtype=BF16)]
        out_shape = [jax.ShapeDtypeStruct((b, seq, D_MODEL), F32),
                     jax.ShapeDtypeStruct((b, seq, D_MODEL), BF16),
                     jax.ShapeDtypeStruct((b, seq, ROUTE_W), F32),
                     jax.ShapeDtypeStruct((b, 8, seq), F32),
                     jax.ShapeDtypeStruct((b, 8, ROUTE_W), F32)]
        out_specs = [row_spec(D_MODEL), row_spec(D_MODEL), row_spec(ROUTE_W),
                     pl.BlockSpec((None, 8, seq), lambda i: (i, 0, 0)),
                     pl.BlockSpec((None, 8, ROUTE_W), lambda i: (i, 0, 0))]
    else:
        out_shape = [jax.ShapeDtypeStruct((b, seq, D_MODEL), F32), jax.ShapeDtypeStruct((b, seq, D_MODEL), BF16)]
        out_specs = [row_spec(D_MODEL), row_spec(D_MODEL)]
    return pl.pallas_call(
        functools.partial(_mixout_kernel, route=route),
        out_shape=out_shape, grid=(b,), in_specs=in_specs, out_specs=out_specs,
        compiler_params=pltpu.CompilerParams(
            dimension_semantics=("arbitrary",), vmem_limit_bytes=VMEM_LIMIT),
        name="mixout_%s%s" % ("latent" if latent else "context", "_route" if route else ""),
    )(*args)


def _ffn_kernel(h_ref, x_ref, mod_ref, wg_ref, wu_ref, wd_ref, o_ref, acc_ref):
    e = pl.program_id(1)

    @pl.when(e == 0)
    def _():
        acc_ref[...] = jnp.zeros_like(acc_ref)

    h = h_ref[...]
    a = _silu(jnp.dot(h, wg_ref[...], preferred_element_type=F32)) * \
        jnp.dot(h, wu_ref[...], preferred_element_type=F32)
    acc_ref[...] += jnp.dot(a.astype(BF16), wd_ref[...], preferred_element_type=F32)

    @pl.when(e == pl.num_programs(1) - 1)
    def _():
        o_ref[...] = x_ref[...] + mod_ref[5:6] * acc_ref[...]


def _ffn(h2, x1, mods_l, wg, wu, wd, latent, seq):
    rows = h2.shape[0]
    tm = 512
    ff = FF_EXPERT
    n_chunk = wg.shape[1] // ff
    mod_row = (lambda i: (i * tm) // seq) if latent else (lambda i: CTX_ROW)
    return pl.pallas_call(
        _ffn_kernel,
        out_shape=jax.ShapeDtypeStruct((rows, D_MODEL), F32),
        grid=(rows // tm, n_chunk),
        in_specs=[pl.BlockSpec((tm, D_MODEL), lambda i, e: (i, 0)),
                  pl.BlockSpec((tm, D_MODEL), lambda i, e: (i, 0)),
                  pl.BlockSpec((None, 6, D_MODEL), lambda i, e: (mod_row(i), 0, 0)),
                  pl.BlockSpec((D_MODEL, ff), lambda i, e: (0, e)),
                  pl.BlockSpec((D_MODEL, ff), lambda i, e: (0, e)),
                  pl.BlockSpec((ff, D_MODEL), lambda i, e: (e, 0))],
        out_specs=pl.BlockSpec((tm, D_MODEL), lambda i, e: (i, 0)),
        scratch_shapes=[pltpu.VMEM((tm, D_MODEL), F32)],
        compiler_params=pltpu.CompilerParams(
            dimension_semantics=("arbitrary", "arbitrary"), vmem_limit_bytes=VMEM_LIMIT),
        name="ffn_dense",
    )(h2, x1, mods_l, wg, wu, wd)


def _route_tables(route_t, cnt, seq):
    b = route_t.shape[0]
    per_block = MOE_BLOCK // seq
    nb = b // per_block
    c = cnt[:, 0, :N_EXPERTS].astype(jnp.int32).reshape(nb, per_block, N_EXPERTS)
    before = jnp.cumsum(c, axis=1) - c
    total = c.sum(axis=1)
    aligned = (total + 7) // 8 * 8
    starts = jnp.cumsum(aligned, axis=1) - aligned
    base = (starts[:, None, :] + before).reshape(b, N_EXPERTS)
    expert = route_t[:, 2:4, :].astype(jnp.int32)
    rank = route_t[:, 4:6, :].astype(jnp.int32)
    slot = rank
    for e in range(N_EXPERTS):
        slot = slot + jnp.where(expert == e, base[:, e][:, None, None], 0)
    slots = slot.reshape(nb, per_block, 2, seq).transpose(0, 2, 1, 3).reshape(nb, 2, MOE_BLOCK)
    return starts.reshape(-1), total.reshape(-1), slots


MOE_SLOTS = 2 * MOE_BLOCK
MOE_ROWS = MOE_SLOTS + 8 * N_EXPERTS + MOE_TILE


MOE_GROUP = 16


def _moe_kernel(starts_ref, counts_ref, slots_ref, h_ref, route_ref, wg_ref, wu_ref, wd_ref, f_ref,
                xs_ref, stage_ref):
    b = pl.program_id(0)
    e = pl.program_id(1)
    half = D_MODEL // 2

    @pl.when(e == 0)
    def _():
        xs_ref[MOE_SLOTS:, :] = jnp.zeros((MOE_ROWS - MOE_SLOTS, half), jnp.uint32)
        zero_row = jnp.zeros((1, half), jnp.uint32)
        for g in range(N_EXPERTS):
            end = starts_ref[b * N_EXPERTS + g] + counts_ref[b * N_EXPERTS + g]
            for r in range(7):
                xs_ref[pl.ds(end + r, 1), :] = zero_row

        def put(g, c):
            t0 = pl.multiple_of(g * MOE_GROUP, MOE_GROUP)
            hb = h_ref[pl.ds(t0, MOE_GROUP), :].astype(F32)
            stage_ref[:MOE_GROUP, :] = _pack2(hb[:, :half], hb[:, half:])
            for r in range(MOE_GROUP):
                row = stage_ref[r:r + 1, :]
                xs_ref[pl.ds(slots_ref[0, t0 + r], 1), :] = row
                xs_ref[pl.ds(slots_ref[1, t0 + r], 1), :] = row
            return c
        lax.fori_loop(0, MOE_BLOCK // MOE_GROUP, put, 0)

    start = starts_ref[b * N_EXPERTS + e]
    count = counts_ref[b * N_EXPERTS + e]

    def tile(j, c):
        s = pl.multiple_of(start + j * MOE_TILE, 8)
        xg = xs_ref[pl.ds(s, MOE_TILE), :]
        x_lo32 = _unpack2(xg, 0)
        x_hi32 = _unpack2(xg, 1)
        x_lo = x_lo32.astype(BF16)
        x_hi = x_hi32.astype(BF16)
        hg = (jnp.dot(x_lo, wg_ref[:half, :], preferred_element_type=F32)
              + jnp.dot(x_hi, wg_ref[half:, :], preferred_element_type=F32))
        hu = (jnp.dot(x_lo, wu_ref[:half, :], preferred_element_type=F32)
              + jnp.dot(x_hi, wu_ref[half:, :], preferred_element_type=F32))
        a = (_silu(hg) * hu).astype(BF16)
        y = jnp.dot(a, wd_ref[...], preferred_element_type=F32)
        valid = lax.broadcasted_iota(jnp.int32, (MOE_TILE, half), 0) < count - j * MOE_TILE
        xs_ref[pl.ds(s, MOE_TILE), :] = _pack2(jnp.where(valid, y[:, :half], x_lo32),
                                               jnp.where(valid, y[:, half:], x_hi32))
        return c
    lax.fori_loop(0, (count + MOE_TILE - 1) // MOE_TILE, tile, 0)

    @pl.when(e == pl.num_programs(1) - 1)
    def _():
        def take(g, c):
            t0 = pl.multiple_of(g * MOE_GROUP, MOE_GROUP)
            for r in range(MOE_GROUP):
                stage_ref[r:r + 1, :] = xs_ref[pl.ds(slots_ref[0, t0 + r], 1), :]
                stage_ref[MOE_GROUP + r:MOE_GROUP + r + 1, :] = xs_ref[pl.ds(slots_ref[1, t0 + r], 1), :]
            z0 = stage_ref[:MOE_GROUP, :]
            z1 = stage_ref[MOE_GROUP:, :]
            route = route_ref[pl.ds(t0, MOE_GROUP), :]
            w1 = route[:, 0:1]
            w2 = route[:, 1:2]
            f_ref[pl.ds(t0, MOE_GROUP), :half] = (w1 * _unpack2(z0, 0) + w2 * _unpack2(z1, 0)).astype(BF16)
            f_ref[pl.ds(t0, MOE_GROUP), half:] = (w1 * _unpack2(z0, 1) + w2 * _unpack2(z1, 1)).astype(BF16)
            return c
        lax.fori_loop(0, MOE_BLOCK // MOE_GROUP, take, 0)


def _moe_experts(h2, route, starts, counts, slots, wg, wu, wd):
    rows = h2.shape[0]
    nb = rows // MOE_BLOCK
    ff = wg.shape[2]
    grid_spec = pltpu.PrefetchScalarGridSpec(
        num_scalar_prefetch=2,
        grid=(nb, N_EXPERTS),
        in_specs=[
            pl.BlockSpec((None, 2, MOE_BLOCK), lambda b, e, st, ct: (b, 0, 0), memory_space=pltpu.SMEM),
            pl.BlockSpec((MOE_BLOCK, D_MODEL), lambda b, e, st, ct: (b, 0)),
            pl.BlockSpec((MOE_BLOCK, ROUTE_W), lambda b, e, st, ct: (b, 0)),
            pl.BlockSpec((None, D_MODEL, ff), lambda b, e, st, ct: (e, 0, 0)),
            pl.BlockSpec((None, D_MODEL, ff), lambda b, e, st, ct: (e, 0, 0)),
            pl.BlockSpec((None, ff, D_MODEL), lambda b, e, st, ct: (e, 0, 0)),
        ],
        out_specs=pl.BlockSpec((MOE_BLOCK, D_MODEL), lambda b, e, st, ct: (b, 0)),
        scratch_shapes=[pltpu.VMEM((MOE_ROWS, D_MODEL // 2), jnp.uint32),
                        pltpu.VMEM((2 * MOE_GROUP, D_MODEL // 2), jnp.uint32)],
    )
    return pl.pallas_call(
        _moe_kernel,
        out_shape=jax.ShapeDtypeStruct((rows, D_MODEL), BF16),
        grid_spec=grid_spec,
        compiler_params=pltpu.CompilerParams(
            dimension_semantics=("arbitrary", "arbitrary"), vmem_limit_bytes=VMEM_LIMIT),
        name="moe_experts",
    )(starts, counts, slots, h2, route, wg, wu, wd)


def _final_kernel(x_ref, f_ref, mod_ref, fg_ref, o_ref):
    y = x_ref[...] + mod_ref[5:6] * f_ref[...].astype(F32)
    o_ref[...] = _rms(y, fg_ref[...])


def _final(x1, f, mods_l, final_g, latent, seq):
    rows = x1.shape[0]
    tr = 512
    mod_row = (lambda i: (i * tr) // seq) if latent else (lambda i: CTX_ROW)
    return pl.pallas_call(
        _final_kernel,
        out_shape=jax.ShapeDtypeStruct((rows, D_MODEL), F32),
        grid=(rows // tr,),
        in_specs=[pl.BlockSpec((tr, D_MODEL), lambda i: (i, 0)),
                  pl.BlockSpec((tr, D_MODEL), lambda i: (i, 0)),
                  pl.BlockSpec((None, 6, D_MODEL), lambda i: (mod_row(i), 0, 0)),
                  pl.BlockSpec((1, D_MODEL), lambda i: (0, 0))],
        out_specs=pl.BlockSpec((tr, D_MODEL), lambda i: (i, 0)),
        compiler_params=pltpu.CompilerParams(
            dimension_semantics=("arbitrary",), vmem_limit_bytes=VMEM_LIMIT),
        name="final_norm",
    )(x1, f, mods_l, final_g.reshape(1, D_MODEL))


def _dup_heads(a):
    b, n = a.shape[:2]
    return jnp.repeat(a, 2, axis=2).reshape(b, n, 2 * KV_W).astype(BF16)


def kernel(x_prompt, x_sample, c, cache_k, cache_v, state_ssm_re, state_ssm_im, c_ctx, mod_w, mod_b, norm1_g, norm2_g, w_in, w_out, q_norm_g, k_norm_g, ssm_a_re, ssm_a_im, ssm_log_dt, ssm_b_re, ssm_b_im, ssm_c_re, ssm_c_im, ssm_d, ssm_glu_w, pool_w, pool_scale, ffn_w_gate, ffn_w_up, ffn_w_down, moe_router_w, moe_router_b, moe_w_gate, moe_w_up, moe_w_down, final_g):
    bp, lp, _ = x_prompt.shape
    bs, ls, _ = x_sample.shape
    assert bs == SEQ_GROUP and bp % SEQ_GROUP == 0

    cond = jnp.zeros((MOD_ROWS, D_MODEL), F32).at[:bs].set(c).at[CTX_ROW].set(c_ctx)
    mods = _modulation(cond, mod_w, mod_b).reshape(DEPTH, MOD_ROWS, 6, D_MODEL)

    bw, cw, lam = _ssm_prep(ssm_a_re, ssm_a_im, ssm_log_dt, ssm_b_re, ssm_b_im, ssm_c_re, ssm_c_im)

    xp, xs = x_prompt, x_sample
    new_k, new_v, new_state = [], [], []
    for l in range(DEPTH):
        mods_l = mods[l]
        w_in_bf = w_in[l].astype(BF16)
        w_out_bf = w_out[l].astype(BF16)
        glu_bf = ssm_glu_w[l].astype(BF16)
        moe = l % 2 == 1
        i = l // 2
        router = (moe_router_w[i], moe_router_b[i]) if moe else None
        if moe:
            wg, wu, wd = (moe_w_gate[i].astype(BF16), moe_w_up[i].astype(BF16), moe_w_down[i].astype(BF16))
        else:
            wg, wu, wd = (ffn_w_gate[i].astype(BF16), ffn_w_up[i].astype(BF16), ffn_w_down[i].astype(BF16))
        for latent in (False, True):
            x = xs if latent else xp
            b, seq, _ = x.shape
            nbg = b // SEQ_GROUP
            rows = b * seq
            outs = _inproj(x.reshape(rows, D_MODEL), mods_l, norm1_g[l], w_in_bf, q_norm_g[l], k_norm_g[l],
                           latent, seq)
            q, kd, vd = (a.reshape(b, seq, a.shape[-1]) for a in outs[:3])
            u_ssm = outs[3]
            u_pool = outs[4].reshape(b, seq, POOL_W)
            if latent:
                kds = [kd, _dup_heads(cache_k[:, l])]
                vds = [vd, _dup_heads(cache_v[:, l])]
                h0 = jnp.concatenate([
                    state_ssm_re[:, l, 0].reshape(b, SSM_GP), state_ssm_im[:, l, 0].reshape(b, SSM_GP),
                    state_ssm_re[:, l, 1].reshape(b, SSM_GP), state_ssm_im[:, l, 1].reshape(b, SSM_GP)],
                    axis=1).reshape(nbg, SEQ_GROUP, 4 * SSM_GP)
            else:
                kds, vds = [kd], [vd]
                new_k.append(outs[5])
                new_v.append(outs[6])
                h0 = jnp.zeros((nbg, SEQ_GROUP, 4 * SSM_GP), F32)
            attn = _attention(q, kds, vds)
            y_ssm, hf = _ssm(u_ssm, l, bw, cw, lam, h0, ssm_d[l], glu_bf, seq)
            if not latent:
                new_state.append(hf.reshape(b, 2, 2, SSM_GROUPS, SSM_STATE))
            res = _mixout(x, attn, y_ssm.reshape(b, seq, SSM_W), u_pool, mods_l,
                          pool_w[l], pool_scale[l], w_out_bf, norm2_g[l], latent, router)
            x1 = res[0].reshape(rows, D_MODEL)
            if moe:
                route = res[2].reshape(rows, ROUTE_W)
                starts, counts, slots = _route_tables(res[3], res[4], seq)
                f = _moe_experts(res[1].reshape(rows, D_MODEL), route, starts, counts, slots, wg, wu, wd)
                y = _final(x1, f, mods_l, final_g, latent, seq)
            else:
                y = _ffn(res[1].reshape(rows, D_MODEL), x1, mods_l, wg, wu, wd, latent, seq)
            y = y.reshape(b, seq, D_MODEL)
            if latent:
                xs = y
            else:
                xp = y

    def cache_out(parts):
        a = jnp.stack(parts, axis=1).reshape(bp, DEPTH, N_KV_HEADS, HEAD_DIM, lp)
        return a.transpose(0, 1, 4, 2, 3)
    new_cache_k = cache_out(new_k)
    new_cache_v = cache_out(new_v)
    st = jnp.stack(new_state, axis=1)
    return (xp, xs, new_cache_k, new_cache_v, st[:, :, :, 0], st[:, :, :, 1])
```

```python
import functools

import numpy as np
import jax
import jax.numpy as jnp
from jax import lax
from jax.experimental import pallas as pl
from jax.experimental.pallas import tpu as pltpu

D_MODEL = 1024
DEPTH = 2
GRID_W = 64
ATTN_W = 512
HEAD_DIM = 64
N_HEADS = 8
N_KV_HEADS = 2
KV_W = 128
ROPE_THETA = 10000.0
SSM_W = 256
SSM_CH = 16
SSM_GROUPS = 16
SSM_STATE = 64
SSM_GP = SSM_GROUPS * SSM_STATE
POOL_W = 256
POOL_WINDOWS = (2, 4, 8, 16)
POOL_CH = 64
POOL_PAD = max(POOL_WINDOWS) // 2
IN_COLS = ATTN_W + 2 * KV_W + SSM_W + POOL_W
FF_DENSE = 2816
N_EXPERTS = 8
FF_EXPERT = 1408
EPS = 1e-6
LOG2_E = 1.4426950408889634

SEQ_GROUP = 8
SSM_CHUNK = 16
SSM_COLBLK = 512
ROUTE_W = 128
MOE_BLOCK = 2048
MOE_TILE = 128
MOD_ROWS = 16
CTX_ROW = 8
VMEM_LIMIT = 56 * 1024 * 1024

F32 = jnp.float32
BF16 = jnp.bfloat16


def _silu(x):
    return x * jax.nn.sigmoid(x)


def _rms(x32, g):
    return x32 * lax.rsqrt(jnp.mean(x32 * x32, axis=-1, keepdims=True) + EPS) * g


def _mod_kernel(cond_ref, w_ref, b_ref, o_ref):
    s = _silu(cond_ref[...])
    w = w_ref[...]
    s_hi = s.astype(BF16)
    s_lo = (s - s_hi.astype(F32)).astype(BF16)
    w_hi = w.astype(BF16)
    w_lo = (w - w_hi.astype(F32)).astype(BF16)
    o_ref[...] = (jnp.dot(s_hi, w_hi, preferred_element_type=F32)
                  + jnp.dot(s_hi, w_lo, preferred_element_type=F32)
                  + jnp.dot(s_lo, w_hi, preferred_element_type=F32) + b_ref[...])


def _modulation(cond, mod_w, mod_b):
    tn = 1536
    n = 6 * D_MODEL
    return pl.pallas_call(
        _mod_kernel,
        out_shape=jax.ShapeDtypeStruct((DEPTH, MOD_ROWS, n), F32),
        grid=(DEPTH, n // tn),
        in_specs=[
            pl.BlockSpec((MOD_ROWS, D_MODEL), lambda l, j: (0, 0)),
            pl.BlockSpec((None, D_MODEL, tn), lambda l, j: (l, 0, j)),
            pl.BlockSpec((None, 1, tn), lambda l, j: (l, 0, j)),
        ],
        out_specs=pl.BlockSpec((None, MOD_ROWS, tn), lambda l, j: (l, 0, j)),
        compiler_params=pltpu.CompilerParams(
            dimension_semantics=("arbitrary", "arbitrary"), vmem_limit_bytes=VMEM_LIMIT),
        name="modulation",
    )(cond, mod_w, mod_b.reshape(DEPTH, 1, n))


def _ssm_prep_kernel(are_ref, aim_ref, dt_ref, bre_ref, bim_ref, cre_ref, cim_ref, bw_ref, cw_ref, lam_ref):
    row_group = lax.broadcasted_iota(jnp.int32, (SSM_W, SSM_GP), 0) // SSM_CH
    lane_group = lax.broadcasted_iota(jnp.int32, (SSM_W, SSM_GP), 1) // SSM_STATE
    own = row_group == lane_group
    for ld in range(2 * DEPTH):
        l, d = divmod(ld, 2)
        a_re = are_ref[ld:ld + 1, :]
        a_im = aim_ref[ld:ld + 1, :]
        dt = jnp.exp(dt_ref[ld:ld + 1, :])
        mag = jnp.exp(a_re * dt)
        l_re = mag * jnp.cos(a_im * dt)
        l_im = mag * jnp.sin(a_im * dt)
        lam_ref[l, 2 * d] = jnp.broadcast_to(l_re, (SEQ_GROUP, SSM_GP))
        lam_ref[l, 2 * d + 1] = jnp.broadcast_to(l_im, (SEQ_GROUP, SSM_GP))
        x = l_re - 1.0
        den = a_re * a_re + a_im * a_im
        f_re = (x * a_re + l_im * a_im) / den
        f_im = (l_im * a_re - x * a_im) / den
        b_re = bre_ref[ld]
        b_im = bim_ref[ld]
        bbar = (f_re * b_re - f_im * b_im, f_re * b_im + f_im * b_re)
        cmat = (cre_ref[ld], -cim_ref[ld])
        for ri in range(2):
            c0 = (2 * d + ri) * SSM_GP
            tiled = jnp.concatenate([bbar[ri]] * SSM_GROUPS, axis=0)
            bw_ref[l, :, c0:c0 + SSM_GP] = jnp.where(own, tiled, 0.0).astype(BF16)
            cw_ref[l, c0:c0 + SSM_GP, :] = jnp.where(own, cmat[ri], 0.0).T.astype(BF16)


def _ssm_prep(a_re, a_im, log_dt, b_re, b_im, c_re, c_im):
    ld = DEPTH * 2
    are = a_re.reshape(ld, SSM_GP)
    aim = a_im.reshape(ld, SSM_GP)
    dt = jnp.repeat(log_dt.reshape(ld, SSM_GROUPS), SSM_STATE, axis=1)
    to_rows = lambda b: b.reshape(ld, SSM_GROUPS, SSM_STATE, SSM_CH).transpose(0, 3, 1, 2).reshape(ld, SSM_CH, SSM_GP)
    repeat_lanes = lambda c: jnp.tile(c.reshape(ld, SSM_W, SSM_STATE), (1, 1, SSM_GROUPS))
    return pl.pallas_call(
        _ssm_prep_kernel,
        out_shape=(jax.ShapeDtypeStruct((DEPTH, SSM_W, 4 * SSM_GP), BF16),
                   jax.ShapeDtypeStruct((DEPTH, 4 * SSM_GP, SSM_W), BF16),
                   jax.ShapeDtypeStruct((DEPTH, 4, SEQ_GROUP, SSM_GP), F32)),
        compiler_params=pltpu.CompilerParams(vmem_limit_bytes=VMEM_LIMIT),
        name="ssm_prep",
    )(are, aim, dt, to_rows(b_re), to_rows(b_im), repeat_lanes(c_re), repeat_lanes(c_im))


def _inproj_kernel(*refs, latent):
    if latent:
        (x_ref, mod_ref, g_ref, w_ref, qg_ref, kg_ref, sq_ref, sk_ref, cos_ref, sa_ref, sb_ref,
         q_ref, kd_ref, vd_ref, us_ref, up_ref) = refs
    else:
        (x_ref, mod_ref, g_ref, w_ref, qg_ref, kg_ref, sq_ref, sk_ref,
         q_ref, kd_ref, vd_ref, us_ref, up_ref, ko_ref, vo_ref) = refs
    x = x_ref[...]
    mod = mod_ref[...]
    h = _rms(x, g_ref[...]) * (1.0 + mod[1:2]) + mod[0:1]
    p = jnp.dot(h.astype(BF16), w_ref[...], preferred_element_type=F32)
    q = p[:, :ATTN_W]
    k = p[:, ATTN_W:ATTN_W + KV_W]
    v = p[:, ATTN_W + KV_W:ATTN_W + 2 * KV_W]
    q_ms = jnp.dot((q * q).astype(BF16), sq_ref[...], preferred_element_type=F32)
    k_ms = jnp.dot((k * k).astype(BF16), sk_ref[...], preferred_element_type=F32)
    q = q * lax.rsqrt(q_ms + EPS) * qg_ref[...]
    k = k * lax.rsqrt(k_ms + EPS) * kg_ref[...]
    if latent:
        cos = cos_ref[...]
        sa = sa_ref[...]
        sb = sb_ref[...]
        cos4 = jnp.concatenate([cos] * 4, axis=1)
        sa4 = jnp.concatenate([sa] * 4, axis=1)
        sb4 = jnp.concatenate([sb] * 4, axis=1)
        q = (q * cos4 + pltpu.roll(q, ATTN_W - 16, axis=1) * sa4 + pltpu.roll(q, 16, axis=1) * sb4)
        k = (k * cos + pltpu.roll(k, KV_W - 16, axis=1) * sa + pltpu.roll(k, 16, axis=1) * sb)
    else:
        seq = ko_ref.shape[-1]
        for s in range(ko_ref.shape[0]):
            ko_ref[s] = k[s * seq:(s + 1) * seq, :].T
            vo_ref[s] = v[s * seq:(s + 1) * seq, :].T
    q_ref[...] = (q * (HEAD_DIM ** -0.5 * LOG2_E)).astype(BF16)
    lane = lax.broadcasted_iota(jnp.int32, k.shape, 1)
    first = lane < HEAD_DIM
    k_sw = pltpu.roll(k, HEAD_DIM, axis=1)
    v_sw = pltpu.roll(v, HEAD_DIM, axis=1)
    kd_ref[:, :KV_W] = jnp.where(first, k, k_sw).astype(BF16)
    kd_ref[:, KV_W:] = jnp.where(first, k_sw, k).astype(BF16)
    vd_ref[:, :KV_W] = jnp.where(first, v, v_sw).astype(BF16)
    vd_ref[:, KV_W:] = jnp.where(first, v_sw, v).astype(BF16)
    o3 = ATTN_W + 2 * KV_W
    us_ref[0] = p[:, o3:o3 + SSM_W // 2]
    us_ref[1] = p[:, o3 + SSM_W // 2:o3 + SSM_W]
    up_ref[...] = p[:, o3 + SSM_W:]


def _head_mean_matrix(width):
    i = np.arange(width) // HEAD_DIM
    return jnp.asarray((i[:, None] == i[None, :]).astype(np.float32) / HEAD_DIM, dtype=BF16)


def _rope_tables(seq):
    t = np.arange(seq)
    row = (t // GRID_W).astype(np.float64)
    col = (t % GRID_W).astype(np.float64)
    nf = HEAD_DIM // 4
    inv = ROPE_THETA ** (-np.arange(nf, dtype=np.float64) / nf)
    ang_r = row[:, None] * inv[None, :]
    ang_c = col[:, None] * inv[None, :]
    cos = np.concatenate([np.cos(ang_r), np.cos(ang_r), np.cos(ang_c), np.cos(ang_c)], axis=1)
    sin = np.concatenate([np.sin(ang_r), np.sin(ang_r), np.sin(ang_c), np.sin(ang_c)], axis=1)
    lower = np.tile(np.concatenate([np.ones(nf), np.zeros(nf)]), 2)[None, :]
    sa = -sin * lower
    sb = sin * (1.0 - lower)
    tile = lambda a: jnp.asarray(np.tile(a, (1, 2)), dtype=F32)
    return tile(cos), tile(sa), tile(sb)


def _inproj(x, mods_l, g1, w_in_bf, qg, kg, latent, seq):
    rows = x.shape[0]
    tr = 512
    tiles_per_seq = max(seq // tr, 1)
    mod_row = (lambda i: (i * tr) // seq) if latent else (lambda i: CTX_ROW)
    const = lambda shape: pl.BlockSpec(shape, lambda i: (0,) * len(shape))
    row_spec = lambda width: pl.BlockSpec((tr, width), lambda i: (i, 0))
    in_specs = [
        row_spec(D_MODEL),
        pl.BlockSpec((None, 6, D_MODEL), lambda i: (mod_row(i), 0, 0)),
        const((1, D_MODEL)),
        const((D_MODEL, IN_COLS)),
        const((1, ATTN_W)),
        const((1, KV_W)),
        const((ATTN_W, ATTN_W)),
        const((KV_W, KV_W)),
    ]
    args = [x, mods_l, g1.reshape(1, D_MODEL), w_in_bf,
            jnp.tile(qg, N_HEADS).reshape(1, ATTN_W), jnp.tile(kg, N_KV_HEADS).reshape(1, KV_W),
            _head_mean_matrix(ATTN_W), _head_mean_matrix(KV_W)]
    widths = [(ATTN_W, BF16), (2 * KV_W, BF16), (2 * KV_W, BF16), (None, F32), (POOL_W, F32)]
    if latent:
        assert seq % tr == 0
        cos, sa, sb = _rope_tables(seq)
        in_specs += [pl.BlockSpec((tr, KV_W), lambda i: (i % tiles_per_seq, 0))] * 3
        args += [cos, sa, sb]
    out_shape = [jax.ShapeDtypeStruct((rows, w) if w else (2, rows, SSM_W // 2), dt) for w, dt in widths]
    out_specs = [row_spec(w) if w else pl.BlockSpec((2, tr, SSM_W // 2), lambda i: (0, i, 0)) for w, _ in widths]
    if not latent:
        assert tr % seq == 0
        out_shape += [jax.ShapeDtypeStruct((rows // seq, KV_W, seq), F32)] * 2
        out_specs += [pl.BlockSpec((tr // seq, KV_W, seq), lambda i: (i, 0, 0))] * 2
    return pl.pallas_call(
        functools.partial(_inproj_kernel, latent=latent),
        out_shape=out_shape, grid=(rows // tr,), in_specs=in_specs, out_specs=out_specs,
        compiler_params=pltpu.CompilerParams(
            dimension_semantics=("arbitrary",), vmem_limit_bytes=VMEM_LIMIT),
        name="inproj_latent" if latent else "inproj_context",
    )(*args)


def _attn_kernel(*refs, n_kv_src):
    q_ref = refs[0]
    k_refs = refs[1:1 + n_kv_src]
    v_refs = refs[1 + n_kv_src:1 + 2 * n_kv_src]
    o_ref = refs[1 + 2 * n_kv_src]
    tq = q_ref.shape[0]
    lane = lax.broadcasted_iota(jnp.int32, (tq, KV_W), 1)
    first = lane < HEAD_DIM
    for pair in range(N_HEADS // 2):
        kv = pair // 2
        cols = slice(pair * KV_W, (pair + 1) * KV_W)
        kcols = slice(kv * KV_W, (kv + 1) * KV_W)
        qp = q_ref[:, cols]
        halves = []
        for half in range(2):
            qm = jnp.where(first if half == 0 else jnp.logical_not(first), qp, jnp.zeros_like(qp))
            scores = [lax.dot_general(qm, k_ref[:, kcols], (((1,), (1,)), ((), ())),
                                      preferred_element_type=F32) for k_ref in k_refs]
            m = scores[0].max(axis=-1, keepdims=True)
            for s in scores[1:]:
                m = jnp.maximum(m, s.max(axis=-1, keepdims=True))
            den = jnp.zeros((tq, 1), F32)
            acc = jnp.zeros((tq, KV_W), F32)
            for s, v_ref in zip(scores, v_refs):
                p = jnp.exp2(s - m)
                den = den + p.sum(axis=-1, keepdims=True)
                acc = acc + jnp.dot(p.astype(BF16), v_ref[:, kcols], preferred_element_type=F32)
            halves.append(acc / den)
        o_ref[:, cols] = jnp.where(first, halves[0], halves[1]).astype(o_ref.dtype)


def _attention(q, kds, vds):
    b, seq, _ = q.shape
    tq = min(seq, 512)
    n_src = len(kds)
    kv_spec = lambda a: pl.BlockSpec((None, a.shape[1], 2 * KV_W), lambda i, t: (i, 0, 0))
    return pl.pallas_call(
        functools.partial(_attn_kernel, n_kv_src=n_src),
        out_shape=jax.ShapeDtypeStruct((b, seq, ATTN_W), BF16),
        grid=(b, seq // tq),
        in_specs=[pl.BlockSpec((None, tq, ATTN_W), lambda i, t: (i, t, 0))]
                 + [kv_spec(a) for a in kds] + [kv_spec(a) for a in vds],
        out_specs=pl.BlockSpec((None, tq, ATTN_W), lambda i, t: (i, t, 0)),
        compiler_params=pltpu.CompilerParams(
            dimension_semantics=("arbitrary", "arbitrary"), vmem_limit_bytes=VMEM_LIMIT),
        name="attention_%dsrc" % n_src,
    )(q, *kds, *vds)


def _ssm_kernel(u_ref, bw_ref, cw_ref, lam_ref, h0_ref, d_ref, glu_ref,
                o_ref, hf_ref, y_ref, utm_scr, bu0_scr, bu1_scr, hs0_scr, hs1_scr, st_scr):
    rows = u_ref.shape[1]
    seq = rows // SEQ_GROUP
    crow = SSM_CHUNK * SEQ_GROUP
    nchunks = rows // crow
    w2 = 2 * SSM_GP
    lanes = SSM_W // 2

    def time_major(c):
        t0 = c * SSM_CHUNK
        return jnp.concatenate(
            [jnp.concatenate([u_ref[h, pl.ds(t0 + s, SEQ_GROUP, stride=seq), :] for h in range(2)], axis=1)
             for s in range(SSM_CHUNK)], axis=0)

    def y_rows(r, n):
        return jnp.concatenate([y_ref[0, pl.ds(r, n), :], y_ref[1, pl.ds(r, n), :]], axis=1)

    def set_y_rows(r, n, val):
        y_ref[0, pl.ds(r, n), :] = val[:, :lanes]
        y_ref[1, pl.ds(r, n), :] = val[:, lanes:]

    st_scr[...] = h0_ref[...]

    def init_rows(i, c):
        r = pl.multiple_of(i * crow, crow)
        tm = time_major(i)
        set_y_rows(r, crow, tm * d_ref[...])
        utm_scr[pl.ds(r, crow), :] = tm.astype(BF16)
        return c
    lax.fori_loop(0, nchunks, init_rows, 0)

    last = nchunks - 1
    bu_scr = (bu0_scr, bu1_scr)
    hs_scr = (hs0_scr, hs1_scr)

    def project_in(i, buf):
        i = jnp.clip(i, 0, last)
        for d, c in ((0, i), (1, last - i)):
            r = pl.multiple_of(c * crow, crow)
            bu_scr[buf][d] = jnp.dot(utm_scr[pl.ds(r, crow), :], bw_ref[:, d * w2:(d + 1) * w2],
                                     preferred_element_type=F32)

    def recur(buf):
        for d in range(2):
            for cb in range(SSM_GP // SSM_COLBLK):
                re_c = slice(cb * SSM_COLBLK, (cb + 1) * SSM_COLBLK)
                im_c = slice(SSM_GP + cb * SSM_COLBLK, SSM_GP + (cb + 1) * SSM_COLBLK)
                l_re = lam_ref[2 * d, :, re_c]
                l_im = lam_ref[2 * d + 1, :, re_c]
                h_re = st_scr[:, d * w2 + cb * SSM_COLBLK:d * w2 + (cb + 1) * SSM_COLBLK]
                h_im = st_scr[:, d * w2 + SSM_GP + cb * SSM_COLBLK:d * w2 + SSM_GP + (cb + 1) * SSM_COLBLK]
                for s in range(SSM_CHUNK):
                    t = s if d == 0 else SSM_CHUNK - 1 - s
                    r = slice(t * SEQ_GROUP, (t + 1) * SEQ_GROUP)
                    n_re = l_re * h_re - l_im * h_im + bu_scr[buf][d, r, re_c]
                    n_im = l_re * h_im + l_im * h_re + bu_scr[buf][d, r, im_c]
                    hs_scr[buf][d, r, re_c] = n_re
                    hs_scr[buf][d, r, im_c] = n_im
                    h_re, h_im = n_re, n_im
                st_scr[:, d * w2 + cb * SSM_COLBLK:d * w2 + (cb + 1) * SSM_COLBLK] = h_re
                st_scr[:, d * w2 + SSM_GP + cb * SSM_COLBLK:d * w2 + SSM_GP + (cb + 1) * SSM_COLBLK] = h_im

    def project_out(i, buf):
        i = jnp.clip(i, 0, last)
        for d, c in ((0, i), (1, last - i)):
            r = pl.multiple_of(c * crow, crow)
            contrib = jnp.dot(hs_scr[buf][d].astype(BF16), cw_ref[d * w2:(d + 1) * w2, :],
                              preferred_element_type=F32)
            set_y_rows(r, crow, y_rows(r, crow) + contrib)

    hs1_scr[...] = jnp.zeros(hs1_scr.shape, F32)
    project_in(0, 0)

    def two_steps(j, c):
        i = 2 * j
        recur(0)
        project_in(i + 1, 1)
        project_out(i - 1, 1)
        recur(1)
        project_in(i + 2, 0)
        project_out(i, 0)
        return c
    lax.fori_loop(0, nchunks // 2, two_steps, 0)
    project_out(last, 1)

    hf_ref[...] = st_scr[...]

    esteps = 128
    erow = esteps * SEQ_GROUP

    def epilogue(i, c):
        r = pl.multiple_of(i * erow, erow)
        z = jax.nn.gelu(y_rows(r, erow))
        g = jnp.dot(z.astype(BF16), glu_ref[...], preferred_element_type=F32)
        set_y_rows(r, erow, g[:, :SSM_W] * jax.nn.sigmoid(g[:, SSM_W:]))
        t0 = pl.multiple_of(i * esteps, esteps)
        for b in range(SEQ_GROUP):
            for h in range(2):
                o_ref[pl.ds(b * seq + t0, esteps), h * lanes:(h + 1) * lanes] = (
                    y_ref[h, pl.ds(r + b, esteps, stride=SEQ_GROUP), :].astype(BF16))
        return c
    lax.fori_loop(0, rows // erow, epilogue, 0)


def _ssm(u, layer, bw, cw, lam, h0, ssm_d, glu_bf, seq):
    rows = u.shape[1]
    grows = SEQ_GROUP * seq
    nbg = rows // grows
    crow = SSM_CHUNK * SEQ_GROUP
    const = lambda shape: pl.BlockSpec(shape, lambda i: (0,) * len(shape))
    return pl.pallas_call(
        _ssm_kernel,
        out_shape=(jax.ShapeDtypeStruct((rows, SSM_W), BF16),
                   jax.ShapeDtypeStruct((nbg, SEQ_GROUP, 4 * SSM_GP), F32)),
        grid=(nbg,),
        in_specs=[
            pl.BlockSpec((2, grows, SSM_W // 2), lambda i: (0, i, 0)),
            pl.BlockSpec((None, SSM_W, 4 * SSM_GP), lambda i: (layer, 0, 0)),
            pl.BlockSpec((None, 4 * SSM_GP, SSM_W), lambda i: (layer, 0, 0)),
            pl.BlockSpec((None, 4, SEQ_GROUP, SSM_GP), lambda i: (layer, 0, 0, 0)),
            pl.BlockSpec((None, SEQ_GROUP, 4 * SSM_GP), lambda i: (i, 0, 0)),
            const((1, SSM_W)),
            const((SSM_W, 2 * SSM_W)),
        ],
        out_specs=(pl.BlockSpec((grows, SSM_W), lambda i: (i, 0)),
                   pl.BlockSpec((None, SEQ_GROUP, 4 * SSM_GP), lambda i: (i, 0, 0))),
        scratch_shapes=[pltpu.VMEM((2, grows, SSM_W // 2), F32),
                        pltpu.VMEM((grows, SSM_W), BF16),
                        pltpu.VMEM((2, crow, 2 * SSM_GP), F32),
                        pltpu.VMEM((2, crow, 2 * SSM_GP), F32),
                        pltpu.VMEM((2, crow, 2 * SSM_GP), F32),
                        pltpu.VMEM((2, crow, 2 * SSM_GP), F32),
                        pltpu.VMEM((SEQ_GROUP, 4 * SSM_GP), F32)],
        compiler_params=pltpu.CompilerParams(
            dimension_semantics=("arbitrary",), vmem_limit_bytes=VMEM_LIMIT),
        name="ssm_scan",
    )(u, bw, cw, lam, h0, ssm_d.reshape(1, SSM_W), glu_bf)


def _pack2(lo, hi):
    return pltpu.pack_elementwise([lo, hi], packed_dtype=BF16)


def _unpack2(w, index):
    return pltpu.unpack_elementwise(w, index=index, packed_dtype=BF16, unpacked_dtype=F32)


def _mixout_kernel(*refs, route):
    if route:
        (x_ref, at_ref, ss_ref, up_ref, mod_ref, invc_ref, pw_ref, ps_ref, wo_ref, g2_ref, wr_ref, br_ref, tri_ref,
         x1_ref, h2_ref, route_ref, rt_ref, cnt_ref) = refs
    else:
        (x_ref, at_ref, ss_ref, up_ref, mod_ref, invc_ref, pw_ref, ps_ref, wo_ref, g2_ref,
         x1_ref, h2_ref) = refs
    u = up_ref[...]
    seq = u.shape[0]
    zpad = jnp.zeros((POOL_PAD, POOL_W), F32)
    ue = jnp.concatenate([zpad, u, zpad], axis=0)
    n_ext = seq + 2 * POOL_PAD
    back = lambda a, k: pltpu.roll(a, k, axis=0)
    ahead = lambda a, k: pltpu.roll(a, n_ext - k, axis=0)
    w2 = ue + back(ue, 1)
    w4 = back(w2, 1) + ahead(w2, 1)
    w8 = back(w4, 2) + ahead(w4, 2)
    w16 = back(w8, 4) + ahead(w8, 4)
    grp = lax.broadcasted_iota(jnp.int32, ue.shape, 1) // POOL_CH
    win = jnp.where(grp == 0, w2, jnp.where(grp == 1, w4, jnp.where(grp == 2, w8, w16)))
    pooled = win[POOL_PAD:POOL_PAD + seq] * invc_ref[...] - u
    pool = jnp.dot(pooled.astype(BF16), pw_ref[...], preferred_element_type=F32) * ps_ref[...]
    mix = jnp.concatenate([at_ref[...], ss_ref[...], pool.astype(BF16)], axis=1)
    o = jnp.dot(mix, wo_ref[...], preferred_element_type=F32)
    mod = mod_ref[...]
    x1 = x_ref[...] + mod[2:3] * o
    x1_ref[...] = x1
    h2 = _rms(x1, g2_ref[...]) * (1.0 + mod[4:5]) + mod[3:4]
    h2_ref[...] = h2.astype(BF16)
    if not route:
        return
    h_hi = h2.astype(BF16)
    h_lo = (h2 - h_hi.astype(F32)).astype(BF16)
    both = jnp.dot(h_hi, wr_ref[...], preferred_element_type=F32)
    logits = (both[:, :ROUTE_W] + both[:, ROUTE_W:]
              + jnp.dot(h_lo, wr_ref[:, :ROUTE_W], preferred_element_type=F32) + br_ref[...])
    lane = lax.broadcasted_iota(jnp.int32, logits.shape, 1).astype(F32)
    neg = float(np.finfo(np.float32).min)
    far = float(ROUTE_W)
    logits = jnp.where(lane < N_EXPERTS, logits, neg)
    m1 = logits.max(axis=-1, keepdims=True)
    i1 = jnp.where(logits == m1, lane, far).min(axis=-1, keepdims=True)
    rest = jnp.where(lane == i1, neg, logits)
    m2 = rest.max(axis=-1, keepdims=True)
    i2 = jnp.where(rest == m2, lane, far).min(axis=-1, keepdims=True)
    e2 = jnp.exp(m2 - m1)
    den = 1.0 + e2
    sel = jnp.where(lane == i1, 1.0, jnp.where(lane == i2, 1.0, 0.0))
    rank = jnp.dot(tri_ref[...], sel.astype(BF16), preferred_element_type=F32)
    r1 = jnp.where(lane == i1, rank, 0.0).sum(axis=-1, keepdims=True)
    r2 = jnp.where(lane == i2, rank, 0.0).sum(axis=-1, keepdims=True)
    out = jnp.zeros_like(logits)
    for j, col in enumerate((1.0 / den, e2 / den, i1, i2, r1, r2)):
        out = jnp.where(lane == j, col, out)
    route_ref[...] = out
    rt_ref[...] = out.T[:8, :]
    cnt_ref[...] = jnp.broadcast_to(sel.sum(axis=0, keepdims=True), cnt_ref.shape)


def _pool_inv_count(seq):
    t = np.arange(seq)
    cols = []
    for win in POOL_WINDOWS:
        lo = np.clip(t - win // 2, 0, seq)
        hi = np.clip(t + win // 2, 0, seq)
        cols.append(np.repeat((hi - lo).astype(np.float32)[:, None], POOL_CH, axis=1))
    return np.concatenate(cols, axis=1)


def _block_diag_pool(pool_w):
    eye = jnp.eye(len(POOL_WINDOWS), dtype=F32)
    m = eye[:, None, :, None] * pool_w[:, :, None, :]
    return m.reshape(POOL_W, POOL_W)


def _mixout(x, attn, ssm_tm, u_pool, mods_l, pool_w, pool_scale, w_out_bf, g2, latent, router):
    b, seq, _ = x.shape
    route = router is not None
    mod_row = (lambda i: i) if latent else (lambda i: CTX_ROW)
    const = lambda shape: pl.BlockSpec(shape, lambda i: (0,) * len(shape))
    cnt = _pool_inv_count(seq)
    in_specs = [
        pl.BlockSpec((None, seq, D_MODEL), lambda i: (i, 0, 0)),
        pl.BlockSpec((None, seq, ATTN_W), lambda i: (i, 0, 0)),
        pl.BlockSpec((None, seq, SSM_W), lambda i: (i, 0, 0)),
        pl.BlockSpec((None, seq, POOL_W), lambda i: (i, 0, 0)),
        pl.BlockSpec((None, 6, D_MODEL), lambda i: (mod_row(i), 0, 0)),
        const((seq, POOL_W)),
        const((POOL_W, POOL_W)),
        const((1, POOL_W)),
        const((D_MODEL, D_MODEL)),
        const((1, D_MODEL)),
    ]
    args = [x, attn, ssm_tm, u_pool, mods_l, jnp.asarray(1.0 / cnt, dtype=F32),
            _block_diag_pool(pool_w).astype(BF16), pool_scale.reshape(1, POOL_W), w_out_bf,
            g2.reshape(1, D_MODEL)]
    row_spec = lambda width: pl.BlockSpec((None, seq, width), lambda i: (i, 0, 0))
    if route:
        wr, br = router
        tri = np.tril(np.ones((seq, seq), np.float32), -1)
        wr_pad = jnp.pad(wr, ((0, 0), (0, ROUTE_W - N_EXPERTS)))
        wr_hi = wr_pad.astype(BF16)
        wr_lo = (wr_pad - wr_hi.astype(F32)).astype(BF16)
        in_specs += [const((D_MODEL, 2 * ROUTE_W)), const((1, ROUTE_W)), const((seq, seq))]
        args += [jnp.concatenate([wr_hi, wr_lo], axis=1),
                 jnp.pad(br, (0, ROUTE_W - N_EXPERTS)).reshape(1, ROUTE_W),
                 jnp.asarray(tri, dtype=BF16)]
        out_shape = [jax.ShapeDtypeStruct((b, seq, D_MODEL), F32),
                     jax.ShapeDtypeStruct((b, seq, D_MODEL), BF16),
                     jax.ShapeDtypeStruct((b, seq, ROUTE_W), F32),
                     jax.ShapeDtypeStruct((b, 8, seq), F32),
                     jax.ShapeDtypeStruct((b, 8, ROUTE_W), F32)]
        out_specs = [row_spec(D_MODEL), row_spec(D_MODEL), row_spec(ROUTE_W),
                     pl.BlockSpec((None, 8, seq), lambda i: (i, 0, 0)),
                     pl.BlockSpec((None, 8, ROUTE_W), lambda i: (i, 0, 0))]
    else:
        out_shape = [jax.ShapeDtypeStruct((b, seq, D_MODEL), F32), jax.ShapeDtypeStruct((b, seq, D_MODEL), BF16)]
        out_specs = [row_spec(D_MODEL), row_spec(D_MODEL)]
    return pl.pallas_call(
        functools.partial(_mixout_kernel, route=route),
        out_shape=out_shape, grid=(b,), in_specs=in_specs, out_specs=out_specs,
        compiler_params=pltpu.CompilerParams(
            dimension_semantics=("arbitrary",), vmem_limit_bytes=VMEM_LIMIT),
        name="mixout_%s%s" % ("latent" if latent else "context", "_route" if route else ""),
    )(*args)


def _ffn_kernel(h_ref, x_ref, mod_ref, wg_ref, wu_ref, wd_ref, o_ref):
    h = h_ref[...]
    a = _silu(jnp.dot(h, wg_ref[...], preferred_element_type=F32)) * \
        jnp.dot(h, wu_ref[...], preferred_element_type=F32)
    f = jnp.dot(a.astype(BF16), wd_ref[...], preferred_element_type=F32)
    o_ref[...] = x_ref[...] + mod_ref[5:6] * f


def _ffn(h2, x1, mods_l, wg, wu, wd, latent, seq):
    rows = h2.shape[0]
    tm = 512
    ff = wg.shape[1]
    mod_row = (lambda i: (i * tm) // seq) if latent else (lambda i: CTX_ROW)
    resident = lambda shape: pl.BlockSpec(shape, lambda i: (0, 0), pipeline_mode=pl.Buffered(1))
    return pl.pallas_call(
        _ffn_kernel,
        out_shape=jax.ShapeDtypeStruct((rows, D_MODEL), F32),
        grid=(rows // tm,),
        in_specs=[pl.BlockSpec((tm, D_MODEL), lambda i: (i, 0)),
                  pl.BlockSpec((tm, D_MODEL), lambda i: (i, 0)),
                  pl.BlockSpec((None, 6, D_MODEL), lambda i: (mod_row(i), 0, 0)),
                  resident((D_MODEL, ff)), resident((D_MODEL, ff)), resident((ff, D_MODEL))],
        out_specs=pl.BlockSpec((tm, D_MODEL), lambda i: (i, 0)),
        compiler_params=pltpu.CompilerParams(
            dimension_semantics=("arbitrary",), vmem_limit_bytes=VMEM_LIMIT),
        name="ffn_dense",
    )(h2, x1, mods_l, wg, wu, wd)


def _route_tables(route_t, cnt, seq):
    b = route_t.shape[0]
    per_block = MOE_BLOCK // seq
    nb = b // per_block
    c = cnt[:, 0, :N_EXPERTS].astype(jnp.int32).reshape(nb, per_block, N_EXPERTS)
    before = jnp.cumsum(c, axis=1) - c
    total = c.sum(axis=1)
    aligned = (total + 7) // 8 * 8
    starts = jnp.cumsum(aligned, axis=1) - aligned
    base = (starts[:, None, :] + before).reshape(b, N_EXPERTS)
    expert = route_t[:, 2:4, :].astype(jnp.int32)
    rank = route_t[:, 4:6, :].astype(jnp.int32)
    slot = rank
    for e in range(N_EXPERTS):
        slot = slot + jnp.where(expert == e, base[:, e][:, None, None], 0)
    slots = slot.reshape(nb, per_block, 2, seq).transpose(0, 2, 1, 3).reshape(nb, 2, MOE_BLOCK)
    return starts.reshape(-1), total.reshape(-1), slots


MOE_SLOTS = 2 * MOE_BLOCK
MOE_ROWS = MOE_SLOTS + 8 * N_EXPERTS + MOE_TILE


MOE_GROUP = 16


def _moe_kernel(starts_ref, counts_ref, slots_ref, h_ref, route_ref, wgu_ref, wd_ref, f_ref,
                xs_ref, stage_ref):
    b = pl.program_id(0)
    e = pl.program_id(1)
    half = D_MODEL // 2

    @pl.when(e == 0)
    def _():
        xs_ref[MOE_SLOTS:, :] = jnp.zeros((MOE_ROWS - MOE_SLOTS, half), jnp.uint32)
        zero_row = jnp.zeros((1, half), jnp.uint32)
        for g in range(N_EXPERTS):
            end = starts_ref[b * N_EXPERTS + g] + counts_ref[b * N_EXPERTS + g]
            for r in range(7):
                xs_ref[pl.ds(end + r, 1), :] = zero_row

        def put(g, c):
            t0 = pl.multiple_of(g * MOE_GROUP, MOE_GROUP)
            hb = h_ref[pl.ds(t0, MOE_GROUP), :].astype(F32)
            stage_ref[:MOE_GROUP, :] = _pack2(hb[:, :half], hb[:, half:])
            for r in range(MOE_GROUP):
                row = stage_ref[r:r + 1, :]
                xs_ref[pl.ds(slots_ref[0, t0 + r], 1), :] = row
                xs_ref[pl.ds(slots_ref[1, t0 + r], 1), :] = row
            return c
        lax.fori_loop(0, MOE_BLOCK // MOE_GROUP, put, 0)

    start = starts_ref[b * N_EXPERTS + e]
    count = counts_ref[b * N_EXPERTS + e]

    def tile(j, c):
        s = pl.multiple_of(start + j * MOE_TILE, 8)
        xg = xs_ref[pl.ds(s, MOE_TILE), :]
        x_lo32 = _unpack2(xg, 0)
        x_hi32 = _unpack2(xg, 1)
        x_lo = x_lo32.astype(BF16)
        x_hi = x_hi32.astype(BF16)
        ff = wgu_ref.shape[1] // 2
        hgu = (jnp.dot(x_lo, wgu_ref[:half, :], preferred_element_type=F32)
               + jnp.dot(x_hi, wgu_ref[half:, :], preferred_element_type=F32))
        a = (_silu(hgu[:, :ff]) * hgu[:, ff:]).astype(BF16)
        y = jnp.dot(a, wd_ref[...], preferred_element_type=F32)
        valid = lax.broadcasted_iota(jnp.int32, (MOE_TILE, half), 0) < count - j * MOE_TILE
        xs_ref[pl.ds(s, MOE_TILE), :] = _pack2(jnp.where(valid, y[:, :half], x_lo32),
                                               jnp.where(valid, y[:, half:], x_hi32))
        return c
    lax.fori_loop(0, (count + MOE_TILE - 1) // MOE_TILE, tile, 0)

    @pl.when(e == pl.num_programs(1) - 1)
    def _():
        def take(g, c):
            t0 = pl.multiple_of(g * MOE_GROUP, MOE_GROUP)
            for r in range(MOE_GROUP):
                stage_ref[r:r + 1, :] = xs_ref[pl.ds(slots_ref[0, t0 + r], 1), :]
                stage_ref[MOE_GROUP + r:MOE_GROUP + r + 1, :] = xs_ref[pl.ds(slots_ref[1, t0 + r], 1), :]
            z0 = stage_ref[:MOE_GROUP, :]
            z1 = stage_ref[MOE_GROUP:, :]
            route = route_ref[pl.ds(t0, MOE_GROUP), :]
            w1 = route[:, 0:1]
            w2 = route[:, 1:2]
            f_ref[pl.ds(t0, MOE_GROUP), :half] = (w1 * _unpack2(z0, 0) + w2 * _unpack2(z1, 0)).astype(BF16)
            f_ref[pl.ds(t0, MOE_GROUP), half:] = (w1 * _unpack2(z0, 1) + w2 * _unpack2(z1, 1)).astype(BF16)
            return c
        lax.fori_loop(0, MOE_BLOCK // MOE_GROUP, take, 0)


def _moe_experts(h2, route, starts, counts, slots, wgu, wd):
    rows = h2.shape[0]
    nb = rows // MOE_BLOCK
    ff = wd.shape[1]
    grid_spec = pltpu.PrefetchScalarGridSpec(
        num_scalar_prefetch=2,
        grid=(nb, N_EXPERTS),
        in_specs=[
            pl.BlockSpec((None, 2, MOE_BLOCK), lambda b, e, st, ct: (b, 0, 0), memory_space=pltpu.SMEM),
            pl.BlockSpec((MOE_BLOCK, D_MODEL), lambda b, e, st, ct: (b, 0)),
            pl.BlockSpec((MOE_BLOCK, ROUTE_W), lambda b, e, st, ct: (b, 0)),
            pl.BlockSpec((None, D_MODEL, 2 * ff), lambda b, e, st, ct: (e, 0, 0)),
            pl.BlockSpec((None, ff, D_MODEL), lambda b, e, st, ct: (e, 0, 0)),
        ],
        out_specs=pl.BlockSpec((MOE_BLOCK, D_MODEL), lambda b, e, st, ct: (b, 0)),
        scratch_shapes=[pltpu.VMEM((MOE_ROWS, D_MODEL // 2), jnp.uint32),
                        pltpu.VMEM((2 * MOE_GROUP, D_MODEL // 2), jnp.uint32)],
    )
    return pl.pallas_call(
        _moe_kernel,
        out_shape=jax.ShapeDtypeStruct((rows, D_MODEL), BF16),
        grid_spec=grid_spec,
        compiler_params=pltpu.CompilerParams(
            dimension_semantics=("arbitrary", "arbitrary"), vmem_limit_bytes=VMEM_LIMIT),
        name="moe_experts",
    )(starts, counts, slots, h2, route, wgu, wd)


def _final_kernel(x_ref, f_ref, mod_ref, fg_ref, o_ref):
    y = x_ref[...] + mod_ref[5:6] * f_ref[...].astype(F32)
    o_ref[...] = _rms(y, fg_ref[...])


def _final(x1, f, mods_l, final_g, latent, seq):
    rows = x1.shape[0]
    tr = 512
    mod_row = (lambda i: (i * tr) // seq) if latent else (lambda i: CTX_ROW)
    return pl.pallas_call(
        _final_kernel,
        out_shape=jax.ShapeDtypeStruct((rows, D_MODEL), F32),
        grid=(rows // tr,),
        in_specs=[pl.BlockSpec((tr, D_MODEL), lambda i: (i, 0)),
                  pl.BlockSpec((tr, D_MODEL), lambda i: (i, 0)),
                  pl.BlockSpec((None, 6, D_MODEL), lambda i: (mod_row(i), 0, 0)),
                  pl.BlockSpec((1, D_MODEL), lambda i: (0, 0))],
        out_specs=pl.BlockSpec((tr, D_MODEL), lambda i: (i, 0)),
        compiler_params=pltpu.CompilerParams(
            dimension_semantics=("arbitrary",), vmem_limit_bytes=VMEM_LIMIT),
        name="final_norm",
    )(x1, f, mods_l, final_g.reshape(1, D_MODEL))


def _dup_heads(a):
    b, n = a.shape[:2]
    return jnp.repeat(a, 2, axis=2).reshape(b, n, 2 * KV_W).astype(BF16)


def kernel(x_prompt, x_sample, c, cache_k, cache_v, state_ssm_re, state_ssm_im, c_ctx, mod_w, mod_b, norm1_g, norm2_g, w_in, w_out, q_norm_g, k_norm_g, ssm_a_re, ssm_a_im, ssm_log_dt, ssm_b_re, ssm_b_im, ssm_c_re, ssm_c_im, ssm_d, ssm_glu_w, pool_w, pool_scale, ffn_w_gate, ffn_w_up, ffn_w_down, moe_router_w, moe_router_b, moe_w_gate, moe_w_up, moe_w_down, final_g):
    bp, lp, _ = x_prompt.shape
    bs, ls, _ = x_sample.shape
    assert bs == SEQ_GROUP and bp % SEQ_GROUP == 0

    cond = jnp.zeros((MOD_ROWS, D_MODEL), F32).at[:bs].set(c).at[CTX_ROW].set(c_ctx)
    mods = _modulation(cond, mod_w, mod_b).reshape(DEPTH, MOD_ROWS, 6, D_MODEL)

    bw, cw, lam = _ssm_prep(ssm_a_re, ssm_a_im, ssm_log_dt, ssm_b_re, ssm_b_im, ssm_c_re, ssm_c_im)

    xp, xs = x_prompt, x_sample
    new_k, new_v, new_state = [], [], []
    for l in range(DEPTH):
        mods_l = mods[l]
        w_in_bf = w_in[l].astype(BF16)
        w_out_bf = w_out[l].astype(BF16)
        glu_bf = ssm_glu_w[l].astype(BF16)
        moe = l % 2 == 1
        i = l // 2
        router = (moe_router_w[i], moe_router_b[i]) if moe else None
        if moe:
            wgu = jnp.concatenate([moe_w_gate[i].astype(BF16), moe_w_up[i].astype(BF16)], axis=2)
            wd = moe_w_down[i].astype(BF16)
        else:
            wg, wu, wd = (ffn_w_gate[i].astype(BF16), ffn_w_up[i].astype(BF16), ffn_w_down[i].astype(BF16))
        for latent in (False, True):
            x = xs if latent else xp
            b, seq, _ = x.shape
            nbg = b // SEQ_GROUP
            rows = b * seq
            outs = _inproj(x.reshape(rows, D_MODEL), mods_l, norm1_g[l], w_in_bf, q_norm_g[l], k_norm_g[l],
                           latent, seq)
            q, kd, vd = (a.reshape(b, seq, a.shape[-1]) for a in outs[:3])
            u_ssm = outs[3]
            u_pool = outs[4].reshape(b, seq, POOL_W)
            if latent:
                kds = [kd, _dup_heads(cache_k[:, l])]
                vds = [vd, _dup_heads(cache_v[:, l])]
                h0 = jnp.concatenate([
                    state_ssm_re[:, l, 0].reshape(b, SSM_GP), state_ssm_im[:, l, 0].reshape(b, SSM_GP),
                    state_ssm_re[:, l, 1].reshape(b, SSM_GP), state_ssm_im[:, l, 1].reshape(b, SSM_GP)],
                    axis=1).reshape(nbg, SEQ_GROUP, 4 * SSM_GP)
            else:
                kds, vds = [kd], [vd]
                new_k.append(outs[5])
                new_v.append(outs[6])
                h0 = jnp.zeros((nbg, SEQ_GROUP, 4 * SSM_GP), F32)
            attn = _attention(q, kds, vds)
            y_ssm, hf = _ssm(u_ssm, l, bw, cw, lam, h0, ssm_d[l], glu_bf, seq)
            if not latent:
                new_state.append(hf.reshape(b, 2, 2, SSM_GROUPS, SSM_STATE))
            res = _mixout(x, attn, y_ssm.reshape(b, seq, SSM_W), u_pool, mods_l,
                          pool_w[l], pool_scale[l], w_out_bf, norm2_g[l], latent, router)
            x1 = res[0].reshape(rows, D_MODEL)
            if moe:
                route = res[2].reshape(rows, ROUTE_W)
                starts, counts, slots = _route_tables(res[3], res[4], seq)
                f = _moe_experts(res[1].reshape(rows, D_MODEL), route, starts, counts, slots, wgu, wd)
                y = _final(x1, f, mods_l, final_g, latent, seq)
            else:
                y = _ffn(res[1].reshape(rows, D_MODEL), x1, mods_l, wg, wu, wd, latent, seq)
            y = y.reshape(b, seq, D_MODEL)
            if latent:
                xs = y
            else:
                xp = y

    def cache_out(parts):
        a = jnp.stack(parts, axis=1).reshape(bp, DEPTH, N_KV_HEADS, HEAD_DIM, lp)
        return a.transpose(0, 1, 4, 2, 3)
    new_cache_k = cache_out(new_k)
    new_cache_v = cache_out(new_v)
    st = jnp.stack(new_state, axis=1)
    return (xp, xs, new_cache_k, new_cache_v, st[:, :, :, 0], st[:, :, :, 1])
```

```python
import functools

import numpy as np
import jax
import jax.numpy as jnp
from jax import lax
from jax.experimental import pallas as pl
from jax.experimental.pallas import tpu as pltpu

D_MODEL = 1024
DEPTH = 2
GRID_W = 64
ATTN_W = 512
HEAD_DIM = 64
N_HEADS = 8
N_KV_HEADS = 2
KV_W = 128
ROPE_THETA = 10000.0
SSM_W = 256
SSM_CH = 16
SSM_GROUPS = 16
SSM_STATE = 64
SSM_GP = SSM_GROUPS * SSM_STATE
POOL_W = 256
POOL_WINDOWS = (2, 4, 8, 16)
POOL_CH = 64
POOL_PAD = max(POOL_WINDOWS) // 2
IN_COLS = ATTN_W + 2 * KV_W + SSM_W + POOL_W
FF_DENSE = 2816
N_EXPERTS = 8
FF_EXPERT = 1408
EPS = 1e-6
LOG2_E = 1.4426950408889634

SEQ_GROUP = 8
SSM_CHUNK = 16
SSM_COLBLK = 512
ROUTE_W = 128
MIX_ROWS = 1024
MOE_BLOCK = 2048
MOE_TILE = 128
MOD_ROWS = 16
CTX_ROW = 8
VMEM_LIMIT = 56 * 1024 * 1024

F32 = jnp.float32
BF16 = jnp.bfloat16


def _silu(x):
    return x * jax.nn.sigmoid(x)


def _rms(x32, g):
    return x32 * lax.rsqrt(jnp.mean(x32 * x32, axis=-1, keepdims=True) + EPS) * g


def _mod_kernel(cond_ref, w_ref, b_ref, o_ref):
    s = _silu(cond_ref[...])
    w = w_ref[...]
    s_hi = s.astype(BF16)
    s_lo = (s - s_hi.astype(F32)).astype(BF16)
    w_hi = w.astype(BF16)
    w_lo = (w - w_hi.astype(F32)).astype(BF16)
    o_ref[...] = (jnp.dot(s_hi, w_hi, preferred_element_type=F32)
                  + jnp.dot(s_hi, w_lo, preferred_element_type=F32)
                  + jnp.dot(s_lo, w_hi, preferred_element_type=F32) + b_ref[...])


def _modulation(cond, mod_w, mod_b):
    tn = 1536
    n = 6 * D_MODEL
    return pl.pallas_call(
        _mod_kernel,
        out_shape=jax.ShapeDtypeStruct((DEPTH, MOD_ROWS, n), F32),
        grid=(DEPTH, n // tn),
        in_specs=[
            pl.BlockSpec((MOD_ROWS, D_MODEL), lambda l, j: (0, 0)),
            pl.BlockSpec((None, D_MODEL, tn), lambda l, j: (l, 0, j)),
            pl.BlockSpec((None, 1, tn), lambda l, j: (l, 0, j)),
        ],
        out_specs=pl.BlockSpec((None, MOD_ROWS, tn), lambda l, j: (l, 0, j)),
        compiler_params=pltpu.CompilerParams(
            dimension_semantics=("arbitrary", "arbitrary"), vmem_limit_bytes=VMEM_LIMIT),
        name="modulation",
    )(cond, mod_w, mod_b.reshape(DEPTH, 1, n))


def _ssm_prep_kernel(are_ref, aim_ref, dt_ref, bre_ref, bim_ref, cre_ref, cim_ref, bw_ref, cw_ref, lam_ref):
    row_group = lax.broadcasted_iota(jnp.int32, (SSM_W, SSM_GP), 0) // SSM_CH
    lane_group = lax.broadcasted_iota(jnp.int32, (SSM_W, SSM_GP), 1) // SSM_STATE
    own = row_group == lane_group
    for ld in range(2 * DEPTH):
        l, d = divmod(ld, 2)
        a_re = are_ref[ld:ld + 1, :]
        a_im = aim_ref[ld:ld + 1, :]
        dt = jnp.exp(dt_ref[ld:ld + 1, :])
        mag = jnp.exp(a_re * dt)
        l_re = mag * jnp.cos(a_im * dt)
        l_im = mag * jnp.sin(a_im * dt)
        lam_ref[l, 2 * d] = jnp.broadcast_to(l_re, (SEQ_GROUP, SSM_GP))
        lam_ref[l, 2 * d + 1] = jnp.broadcast_to(l_im, (SEQ_GROUP, SSM_GP))
        x = l_re - 1.0
        den = a_re * a_re + a_im * a_im
        f_re = (x * a_re + l_im * a_im) / den
        f_im = (l_im * a_re - x * a_im) / den
        b_re = bre_ref[ld]
        b_im = bim_ref[ld]
        bbar = (f_re * b_re - f_im * b_im, f_re * b_im + f_im * b_re)
        cmat = (cre_ref[ld], -cim_ref[ld])
        for ri in range(2):
            c0 = (2 * d + ri) * SSM_GP
            tiled = jnp.concatenate([bbar[ri]] * SSM_GROUPS, axis=0)
            bw_ref[l, :, c0:c0 + SSM_GP] = jnp.where(own, tiled, 0.0).astype(BF16)
            cw_ref[l, c0:c0 + SSM_GP, :] = jnp.where(own, cmat[ri], 0.0).T.astype(BF16)


def _ssm_prep(a_re, a_im, log_dt, b_re, b_im, c_re, c_im):
    ld = DEPTH * 2
    are = a_re.reshape(ld, SSM_GP)
    aim = a_im.reshape(ld, SSM_GP)
    dt = jnp.repeat(log_dt.reshape(ld, SSM_GROUPS), SSM_STATE, axis=1)
    to_rows = lambda b: b.reshape(ld, SSM_GROUPS, SSM_STATE, SSM_CH).transpose(0, 3, 1, 2).reshape(ld, SSM_CH, SSM_GP)
    repeat_lanes = lambda c: jnp.tile(c.reshape(ld, SSM_W, SSM_STATE), (1, 1, SSM_GROUPS))
    return pl.pallas_call(
        _ssm_prep_kernel,
        out_shape=(jax.ShapeDtypeStruct((DEPTH, SSM_W, 4 * SSM_GP), BF16),
                   jax.ShapeDtypeStruct((DEPTH, 4 * SSM_GP, SSM_W), BF16),
                   jax.ShapeDtypeStruct((DEPTH, 4, SEQ_GROUP, SSM_GP), F32)),
        compiler_params=pltpu.CompilerParams(vmem_limit_bytes=VMEM_LIMIT),
        name="ssm_prep",
    )(are, aim, dt, to_rows(b_re), to_rows(b_im), repeat_lanes(c_re), repeat_lanes(c_im))


def _inproj_kernel(*refs, latent):
    if latent:
        (x_ref, mod_ref, g_ref, w_ref, qg_ref, kg_ref, sq_ref, sk_ref, cos_ref, sa_ref, sb_ref,
         q_ref, kd_ref, vd_ref, us_ref, up_ref) = refs
    else:
        (x_ref, mod_ref, g_ref, w_ref, qg_ref, kg_ref, sq_ref, sk_ref,
         q_ref, kd_ref, vd_ref, us_ref, up_ref, ko_ref, vo_ref) = refs
    x = x_ref[...]
    mod = mod_ref[...]
    h = _rms(x, g_ref[...]) * (1.0 + mod[1:2]) + mod[0:1]
    p = jnp.dot(h.astype(BF16), w_ref[...], preferred_element_type=F32)
    q = p[:, :ATTN_W]
    k = p[:, ATTN_W:ATTN_W + KV_W]
    v = p[:, ATTN_W + KV_W:ATTN_W + 2 * KV_W]
    q_ms = jnp.dot((q * q).astype(BF16), sq_ref[...], preferred_element_type=F32)
    k_ms = jnp.dot((k * k).astype(BF16), sk_ref[...], preferred_element_type=F32)
    q = q * lax.rsqrt(q_ms + EPS) * qg_ref[...]
    k = k * lax.rsqrt(k_ms + EPS) * kg_ref[...]
    if latent:
        cos = cos_ref[...]
        sa = sa_ref[...]
        sb = sb_ref[...]
        cos4 = jnp.concatenate([cos] * 4, axis=1)
        sa4 = jnp.concatenate([sa] * 4, axis=1)
        sb4 = jnp.concatenate([sb] * 4, axis=1)
        q = (q * cos4 + pltpu.roll(q, ATTN_W - 16, axis=1) * sa4 + pltpu.roll(q, 16, axis=1) * sb4)
        k = (k * cos + pltpu.roll(k, KV_W - 16, axis=1) * sa + pltpu.roll(k, 16, axis=1) * sb)
    else:
        seq = ko_ref.shape[-1]
        for s in range(ko_ref.shape[0]):
            ko_ref[s] = k[s * seq:(s + 1) * seq, :].T
            vo_ref[s] = v[s * seq:(s + 1) * seq, :].T
    q_ref[...] = (q * (HEAD_DIM ** -0.5 * LOG2_E)).astype(BF16)
    lane = lax.broadcasted_iota(jnp.int32, k.shape, 1)
    first = lane < HEAD_DIM
    k_sw = pltpu.roll(k, HEAD_DIM, axis=1)
    v_sw = pltpu.roll(v, HEAD_DIM, axis=1)
    kd_ref[:, :KV_W] = jnp.where(first, k, k_sw).astype(BF16)
    kd_ref[:, KV_W:] = jnp.where(first, k_sw, k).astype(BF16)
    vd_ref[:, :KV_W] = jnp.where(first, v, v_sw).astype(BF16)
    vd_ref[:, KV_W:] = jnp.where(first, v_sw, v).astype(BF16)
    o3 = ATTN_W + 2 * KV_W
    us_ref[0] = p[:, o3:o3 + SSM_W // 2]
    us_ref[1] = p[:, o3 + SSM_W // 2:o3 + SSM_W]
    up_ref[...] = p[:, o3 + SSM_W:]


def _head_mean_matrix(width):
    i = np.arange(width) // HEAD_DIM
    return jnp.asarray((i[:, None] == i[None, :]).astype(np.float32) / HEAD_DIM, dtype=BF16)


def _rope_tables(seq):
    t = np.arange(seq)
    row = (t // GRID_W).astype(np.float64)
    col = (t % GRID_W).astype(np.float64)
    nf = HEAD_DIM // 4
    inv = ROPE_THETA ** (-np.arange(nf, dtype=np.float64) / nf)
    ang_r = row[:, None] * inv[None, :]
    ang_c = col[:, None] * inv[None, :]
    cos = np.concatenate([np.cos(ang_r), np.cos(ang_r), np.cos(ang_c), np.cos(ang_c)], axis=1)
    sin = np.concatenate([np.sin(ang_r), np.sin(ang_r), np.sin(ang_c), np.sin(ang_c)], axis=1)
    lower = np.tile(np.concatenate([np.ones(nf), np.zeros(nf)]), 2)[None, :]
    sa = -sin * lower
    sb = sin * (1.0 - lower)
    tile = lambda a: jnp.asarray(np.tile(a, (1, 2)), dtype=F32)
    return tile(cos), tile(sa), tile(sb)


def _inproj(x, mods_l, g1, w_in_bf, qg, kg, latent, seq):
    rows = x.shape[0]
    tr = 512
    tiles_per_seq = max(seq // tr, 1)
    mod_row = (lambda i: (i * tr) // seq) if latent else (lambda i: CTX_ROW)
    const = lambda shape: pl.BlockSpec(shape, lambda i: (0,) * len(shape))
    row_spec = lambda width: pl.BlockSpec((tr, width), lambda i: (i, 0))
    in_specs = [
        row_spec(D_MODEL),
        pl.BlockSpec((None, 6, D_MODEL), lambda i: (mod_row(i), 0, 0)),
        const((1, D_MODEL)),
        const((D_MODEL, IN_COLS)),
        const((1, ATTN_W)),
        const((1, KV_W)),
        const((ATTN_W, ATTN_W)),
        const((KV_W, KV_W)),
    ]
    args = [x, mods_l, g1.reshape(1, D_MODEL), w_in_bf,
            jnp.tile(qg, N_HEADS).reshape(1, ATTN_W), jnp.tile(kg, N_KV_HEADS).reshape(1, KV_W),
            _head_mean_matrix(ATTN_W), _head_mean_matrix(KV_W)]
    widths = [(ATTN_W, BF16), (2 * KV_W, BF16), (2 * KV_W, BF16), (None, F32), (POOL_W, F32)]
    if latent:
        assert seq % tr == 0
        cos, sa, sb = _rope_tables(seq)
        in_specs += [pl.BlockSpec((tr, KV_W), lambda i: (i % tiles_per_seq, 0))] * 3
        args += [cos, sa, sb]
    out_shape = [jax.ShapeDtypeStruct((rows, w) if w else (2, rows, SSM_W // 2), dt) for w, dt in widths]
    out_specs = [row_spec(w) if w else pl.BlockSpec((2, tr, SSM_W // 2), lambda i: (0, i, 0)) for w, _ in widths]
    if not latent:
        assert tr % seq == 0
        out_shape += [jax.ShapeDtypeStruct((rows // seq, KV_W, seq), F32)] * 2
        out_specs += [pl.BlockSpec((tr // seq, KV_W, seq), lambda i: (i, 0, 0))] * 2
    return pl.pallas_call(
        functools.partial(_inproj_kernel, latent=latent),
        out_shape=out_shape, grid=(rows // tr,), in_specs=in_specs, out_specs=out_specs,
        compiler_params=pltpu.CompilerParams(
            dimension_semantics=("arbitrary",), vmem_limit_bytes=VMEM_LIMIT),
        name="inproj_latent" if latent else "inproj_context",
    )(*args)


def _attn_kernel(*refs, n_kv_src):
    q_ref = refs[0]
    k_refs = refs[1:1 + n_kv_src]
    v_refs = refs[1 + n_kv_src:1 + 2 * n_kv_src]
    o_ref = refs[1 + 2 * n_kv_src]
    tq = q_ref.shape[0]
    lane = lax.broadcasted_iota(jnp.int32, (tq, KV_W), 1)
    first = lane < HEAD_DIM
    for pair in range(N_HEADS // 2):
        kv = pair // 2
        cols = slice(pair * KV_W, (pair + 1) * KV_W)
        kcols = slice(kv * KV_W, (kv + 1) * KV_W)
        qp = q_ref[:, cols]
        halves = []
        for half in range(2):
            qm = jnp.where(first if half == 0 else jnp.logical_not(first), qp, jnp.zeros_like(qp))
            scores = [lax.dot_general(qm, k_ref[:, kcols], (((1,), (1,)), ((), ())),
                                      preferred_element_type=F32) for k_ref in k_refs]
            m = scores[0].max(axis=-1, keepdims=True)
            for s in scores[1:]:
                m = jnp.maximum(m, s.max(axis=-1, keepdims=True))
            den = jnp.zeros((tq, 1), F32)
            acc = jnp.zeros((tq, KV_W), F32)
            for s, v_ref in zip(scores, v_refs):
                p = jnp.exp2(s - m)
                den = den + p.sum(axis=-1, keepdims=True)
                acc = acc + jnp.dot(p.astype(BF16), v_ref[:, kcols], preferred_element_type=F32)
            halves.append(acc / den)
        o_ref[:, cols] = jnp.where(first, halves[0], halves[1]).astype(o_ref.dtype)


def _attention(q, kds, vds):
    b, seq, _ = q.shape
    tq = min(seq, 512)
    n_src = len(kds)
    kv_spec = lambda a: pl.BlockSpec((None, a.shape[1], 2 * KV_W), lambda i, t: (i, 0, 0))
    return pl.pallas_call(
        functools.partial(_attn_kernel, n_kv_src=n_src),
        out_shape=jax.ShapeDtypeStruct((b, seq, ATTN_W), BF16),
        grid=(b, seq // tq),
        in_specs=[pl.BlockSpec((None, tq, ATTN_W), lambda i, t: (i, t, 0))]
                 + [kv_spec(a) for a in kds] + [kv_spec(a) for a in vds],
        out_specs=pl.BlockSpec((None, tq, ATTN_W), lambda i, t: (i, t, 0)),
        compiler_params=pltpu.CompilerParams(
            dimension_semantics=("arbitrary", "arbitrary"), vmem_limit_bytes=VMEM_LIMIT),
        name="attention_%dsrc" % n_src,
    )(q, *kds, *vds)


def _ssm_kernel(u_ref, bw_ref, cw_ref, lam_ref, h0_ref, d_ref, glu_ref,
                o_ref, hf_ref, y_ref, utm_scr, bu0_scr, bu1_scr, hs0_scr, hs1_scr, st_scr):
    rows = u_ref.shape[1]
    seq = rows // SEQ_GROUP
    crow = SSM_CHUNK * SEQ_GROUP
    nchunks = rows // crow
    w2 = 2 * SSM_GP
    lanes = SSM_W // 2

    def time_major(c):
        t0 = c * SSM_CHUNK
        return jnp.concatenate(
            [jnp.concatenate([u_ref[h, pl.ds(t0 + s, SEQ_GROUP, stride=seq), :] for h in range(2)], axis=1)
             for s in range(SSM_CHUNK)], axis=0)

    def y_rows(r, n):
        return jnp.concatenate([y_ref[0, pl.ds(r, n), :], y_ref[1, pl.ds(r, n), :]], axis=1)

    def set_y_rows(r, n, val):
        y_ref[0, pl.ds(r, n), :] = val[:, :lanes]
        y_ref[1, pl.ds(r, n), :] = val[:, lanes:]

    st_scr[...] = h0_ref[...]

    def init_rows(i, c):
        r = pl.multiple_of(i * crow, crow)
        tm = time_major(i)
        set_y_rows(r, crow, tm * d_ref[...])
        utm_scr[pl.ds(r, crow), :] = tm.astype(BF16)
        return c
    lax.fori_loop(0, nchunks, init_rows, 0)

    last = nchunks - 1
    bu_scr = (bu0_scr, bu1_scr)
    hs_scr = (hs0_scr, hs1_scr)

    def project_in(i, buf):
        i = jnp.clip(i, 0, last)
        for d, c in ((0, i), (1, last - i)):
            r = pl.multiple_of(c * crow, crow)
            bu_scr[buf][d] = jnp.dot(utm_scr[pl.ds(r, crow), :], bw_ref[:, d * w2:(d + 1) * w2],
                                     preferred_element_type=F32)

    def recur(buf):
        for d in range(2):
            for cb in range(SSM_GP // SSM_COLBLK):
                re_c = slice(cb * SSM_COLBLK, (cb + 1) * SSM_COLBLK)
                im_c = slice(SSM_GP + cb * SSM_COLBLK, SSM_GP + (cb + 1) * SSM_COLBLK)
                l_re = lam_ref[2 * d, :, re_c]
                l_im = lam_ref[2 * d + 1, :, re_c]
                h_re = st_scr[:, d * w2 + cb * SSM_COLBLK:d * w2 + (cb + 1) * SSM_COLBLK]
                h_im = st_scr[:, d * w2 + SSM_GP + cb * SSM_COLBLK:d * w2 + SSM_GP + (cb + 1) * SSM_COLBLK]
                for s in range(SSM_CHUNK):
                    t = s if d == 0 else SSM_CHUNK - 1 - s
                    r = slice(t * SEQ_GROUP, (t + 1) * SEQ_GROUP)
                    n_re = l_re * h_re - l_im * h_im + bu_scr[buf][d, r, re_c]
                    n_im = l_re * h_im + l_im * h_re + bu_scr[buf][d, r, im_c]
                    hs_scr[buf][d, r, re_c] = n_re
                    hs_scr[buf][d, r, im_c] = n_im
                    h_re, h_im = n_re, n_im
                st_scr[:, d * w2 + cb * SSM_COLBLK:d * w2 + (cb + 1) * SSM_COLBLK] = h_re
                st_scr[:, d * w2 + SSM_GP + cb * SSM_COLBLK:d * w2 + SSM_GP + (cb + 1) * SSM_COLBLK] = h_im

    def project_out(i, buf):
        i = jnp.clip(i, 0, last)
        for d, c in ((0, i), (1, last - i)):
            r = pl.multiple_of(c * crow, crow)
            contrib = jnp.dot(hs_scr[buf][d].astype(BF16), cw_ref[d * w2:(d + 1) * w2, :],
                              preferred_element_type=F32)
            set_y_rows(r, crow, y_rows(r, crow) + contrib)

    hs1_scr[...] = jnp.zeros(hs1_scr.shape, F32)
    project_in(0, 0)

    def two_steps(j, c):
        i = 2 * j
        recur(0)
        project_in(i + 1, 1)
        project_out(i - 1, 1)
        recur(1)
        project_in(i + 2, 0)
        project_out(i, 0)
        return c
    lax.fori_loop(0, nchunks // 2, two_steps, 0)
    project_out(last, 1)

    hf_ref[...] = st_scr[...]

    esteps = 128
    erow = esteps * SEQ_GROUP

    def epilogue(i, c):
        r = pl.multiple_of(i * erow, erow)
        z = jax.nn.gelu(y_rows(r, erow))
        g = jnp.dot(z.astype(BF16), glu_ref[...], preferred_element_type=F32)
        set_y_rows(r, erow, g[:, :SSM_W] * jax.nn.sigmoid(g[:, SSM_W:]))
        t0 = pl.multiple_of(i * esteps, esteps)
        for b in range(SEQ_GROUP):
            for h in range(2):
                o_ref[pl.ds(b * seq + t0, esteps), h * lanes:(h + 1) * lanes] = (
                    y_ref[h, pl.ds(r + b, esteps, stride=SEQ_GROUP), :].astype(BF16))
        return c
    lax.fori_loop(0, rows // erow, epilogue, 0)


def _ssm(u, layer, bw, cw, lam, h0, ssm_d, glu_bf, seq):
    rows = u.shape[1]
    grows = SEQ_GROUP * seq
    nbg = rows // grows
    crow = SSM_CHUNK * SEQ_GROUP
    const = lambda shape: pl.BlockSpec(shape, lambda i: (0,) * len(shape))
    return pl.pallas_call(
        _ssm_kernel,
        out_shape=(jax.ShapeDtypeStruct((rows, SSM_W), BF16),
                   jax.ShapeDtypeStruct((nbg, SEQ_GROUP, 4 * SSM_GP), F32)),
        grid=(nbg,),
        in_specs=[
            pl.BlockSpec((2, grows, SSM_W // 2), lambda i: (0, i, 0)),
            pl.BlockSpec((None, SSM_W, 4 * SSM_GP), lambda i: (layer, 0, 0)),
            pl.BlockSpec((None, 4 * SSM_GP, SSM_W), lambda i: (layer, 0, 0)),
            pl.BlockSpec((None, 4, SEQ_GROUP, SSM_GP), lambda i: (layer, 0, 0, 0)),
            pl.BlockSpec((None, SEQ_GROUP, 4 * SSM_GP), lambda i: (i, 0, 0)),
            const((1, SSM_W)),
            const((SSM_W, 2 * SSM_W)),
        ],
        out_specs=(pl.BlockSpec((grows, SSM_W), lambda i: (i, 0)),
                   pl.BlockSpec((None, SEQ_GROUP, 4 * SSM_GP), lambda i: (i, 0, 0))),
        scratch_shapes=[pltpu.VMEM((2, grows, SSM_W // 2), F32),
                        pltpu.VMEM((grows, SSM_W), BF16),
                        pltpu.VMEM((2, crow, 2 * SSM_GP), F32),
                        pltpu.VMEM((2, crow, 2 * SSM_GP), F32),
                        pltpu.VMEM((2, crow, 2 * SSM_GP), F32),
                        pltpu.VMEM((2, crow, 2 * SSM_GP), F32),
                        pltpu.VMEM((SEQ_GROUP, 4 * SSM_GP), F32)],
        compiler_params=pltpu.CompilerParams(
            dimension_semantics=("arbitrary",), vmem_limit_bytes=VMEM_LIMIT),
        name="ssm_scan",
    )(u, bw, cw, lam, h0, ssm_d.reshape(1, SSM_W), glu_bf)


def _pack2(lo, hi):
    return pltpu.pack_elementwise([lo, hi], packed_dtype=BF16)


def _unpack2(w, index):
    return pltpu.unpack_elementwise(w, index=index, packed_dtype=BF16, unpacked_dtype=F32)


def _mixout_kernel(*refs, route):
    if route:
        (x_ref, at_ref, ss_ref, up_ref, mod_ref, invc_ref, pw_ref, ps_ref, wo_ref, g2_ref, wr_ref, br_ref, tri_ref,
         x1_ref, h2_ref, route_ref, rt_ref, cnt_ref) = refs
    else:
        (x_ref, at_ref, ss_ref, up_ref, mod_ref, invc_ref, pw_ref, ps_ref, wo_ref, g2_ref,
         x1_ref, h2_ref) = refs
    nseq, seq, _ = up_ref.shape
    rows = nseq * seq
    zpad = jnp.zeros((POOL_PAD, POOL_W), F32)
    n_ext = seq + 2 * POOL_PAD
    back = lambda a, k: pltpu.roll(a, k, axis=0)
    ahead = lambda a, k: pltpu.roll(a, n_ext - k, axis=0)
    grp = lax.broadcasted_iota(jnp.int32, (n_ext, POOL_W), 1) // POOL_CH
    pooled = []
    for s in range(nseq):
        u = up_ref[s]
        ue = jnp.concatenate([zpad, u, zpad], axis=0)
        w2 = ue + back(ue, 1)
        w4 = back(w2, 1) + ahead(w2, 1)
        w8 = back(w4, 2) + ahead(w4, 2)
        w16 = back(w8, 4) + ahead(w8, 4)
        win = jnp.where(grp == 0, w2, jnp.where(grp == 1, w4, jnp.where(grp == 2, w8, w16)))
        pooled.append(win[POOL_PAD:POOL_PAD + seq] * invc_ref[...] - u)
    pooled = jnp.concatenate(pooled, axis=0)
    pool = jnp.dot(pooled.astype(BF16), pw_ref[...], preferred_element_type=F32) * ps_ref[...]
    mix = jnp.concatenate([at_ref[...].reshape(rows, ATTN_W), ss_ref[...].reshape(rows, SSM_W),
                           pool.astype(BF16)], axis=1)
    o = jnp.dot(mix, wo_ref[...], preferred_element_type=F32)
    mod = mod_ref[...]
    x1 = x_ref[...].reshape(rows, D_MODEL) + mod[2:3] * o
    x1_ref[...] = x1.reshape(nseq, seq, D_MODEL)
    h2 = _rms(x1, g2_ref[...]) * (1.0 + mod[4:5]) + mod[3:4]
    h2_ref[...] = h2.astype(BF16).reshape(nseq, seq, D_MODEL)
    if not route:
        return
    h_hi = h2.astype(BF16)
    h_lo = (h2 - h_hi.astype(F32)).astype(BF16)
    both = jnp.dot(h_hi, wr_ref[...], preferred_element_type=F32)
    logits = (both[:, :ROUTE_W] + both[:, ROUTE_W:]
              + jnp.dot(h_lo, wr_ref[:, :ROUTE_W], preferred_element_type=F32) + br_ref[...])
    lane = lax.broadcasted_iota(jnp.int32, logits.shape, 1).astype(F32)
    neg = float(np.finfo(np.float32).min)
    far = float(ROUTE_W)
    logits = jnp.where(lane < N_EXPERTS, logits, neg)
    m1 = logits.max(axis=-1, keepdims=True)
    i1 = jnp.where(logits == m1, lane, far).min(axis=-1, keepdims=True)
    rest = jnp.where(lane == i1, neg, logits)
    m2 = rest.max(axis=-1, keepdims=True)
    i2 = jnp.where(rest == m2, lane, far).min(axis=-1, keepdims=True)
    e2 = jnp.exp(m2 - m1)
    den = 1.0 + e2
    sel = jnp.where(lane == i1, 1.0, jnp.where(lane == i2, 1.0, 0.0))
    sel_bf = sel.astype(BF16)
    rank = jnp.concatenate([jnp.dot(tri_ref[...], sel_bf[s * seq:(s + 1) * seq], preferred_element_type=F32)
                            for s in range(nseq)], axis=0)
    r1 = jnp.where(lane == i1, rank, 0.0).sum(axis=-1, keepdims=True)
    r2 = jnp.where(lane == i2, rank, 0.0).sum(axis=-1, keepdims=True)
    out = jnp.zeros_like(logits)
    for j, col in enumerate((1.0 / den, e2 / den, i1, i2, r1, r2)):
        out = jnp.where(lane == j, col, out)
    route_ref[...] = out.reshape(nseq, seq, ROUTE_W)
    for s in range(nseq):
        part = slice(s * seq, (s + 1) * seq)
        rt_ref[s] = out[part].T[:8, :]
        cnt_ref[s] = jnp.broadcast_to(sel[part].sum(axis=0, keepdims=True), cnt_ref.shape[1:])


def _pool_inv_count(seq):
    t = np.arange(seq)
    cols = []
    for win in POOL_WINDOWS:
        lo = np.clip(t - win // 2, 0, seq)
        hi = np.clip(t + win // 2, 0, seq)
        cols.append(np.repeat((hi - lo).astype(np.float32)[:, None], POOL_CH, axis=1))
    return np.concatenate(cols, axis=1)


def _block_diag_pool(pool_w):
    eye = jnp.eye(len(POOL_WINDOWS), dtype=F32)
    m = eye[:, None, :, None] * pool_w[:, :, None, :]
    return m.reshape(POOL_W, POOL_W)


def _mixout(x, attn, ssm_tm, u_pool, mods_l, pool_w, pool_scale, w_out_bf, g2, latent, router):
    b, seq, _ = x.shape
    route = router is not None
    nseq = 1 if latent else max(1, MIX_ROWS // seq)
    mod_row = (lambda i: i) if latent else (lambda i: CTX_ROW)
    const = lambda shape: pl.BlockSpec(shape, lambda i: (0,) * len(shape))
    cnt = _pool_inv_count(seq)
    row_spec = lambda width: pl.BlockSpec((nseq, seq, width), lambda i: (i, 0, 0))
    in_specs = [
        row_spec(D_MODEL), row_spec(ATTN_W), row_spec(SSM_W), row_spec(POOL_W),
        pl.BlockSpec((None, 6, D_MODEL), lambda i: (mod_row(i), 0, 0)),
        const((seq, POOL_W)),
        const((POOL_W, POOL_W)),
        const((1, POOL_W)),
        const((D_MODEL, D_MODEL)),
        const((1, D_MODEL)),
    ]
    args = [x, attn, ssm_tm, u_pool, mods_l, jnp.asarray(1.0 / cnt, dtype=F32),
            _block_diag_pool(pool_w).astype(BF16), pool_scale.reshape(1, POOL_W), w_out_bf,
            g2.reshape(1, D_MODEL)]
    if route:
        wr, br = router
        tri = np.tril(np.ones((seq, seq), np.float32), -1)
        wr_pad = jnp.pad(wr, ((0, 0), (0, ROUTE_W - N_EXPERTS)))
        wr_hi = wr_pad.astype(BF16)
        wr_lo = (wr_pad - wr_hi.astype(F32)).astype(BF16)
        in_specs += [const((D_MODEL, 2 * ROUTE_W)), const((1, ROUTE_W)), const((seq, seq))]
        args += [jnp.concatenate([wr_hi, wr_lo], axis=1),
                 jnp.pad(br, (0, ROUTE_W - N_EXPERTS)).reshape(1, ROUTE_W),
                 jnp.asarray(tri, dtype=BF16)]
        out_shape = [jax.ShapeDtypeStruct((b, seq, D_MODEL), F32),
                     jax.ShapeDtypeStruct((b, seq, D_MODEL), BF16),
                     jax.ShapeDtypeStruct((b, seq, ROUTE_W), F32),
                     jax.ShapeDtypeStruct((b, 8, seq), F32),
                     jax.ShapeDtypeStruct((b, 8, ROUTE_W), F32)]
        out_specs = [row_spec(D_MODEL), row_spec(D_MODEL), row_spec(ROUTE_W),
                     pl.BlockSpec((nseq, 8, seq), lambda i: (i, 0, 0)),
                     pl.BlockSpec((nseq, 8, ROUTE_W), lambda i: (i, 0, 0))]
    else:
        out_shape = [jax.ShapeDtypeStruct((b, seq, D_MODEL), F32), jax.ShapeDtypeStruct((b, seq, D_MODEL), BF16)]
        out_specs = [row_spec(D_MODEL), row_spec(D_MODEL)]
    return pl.pallas_call(
        functools.partial(_mixout_kernel, route=route),
        out_shape=out_shape, grid=(b // nseq,), in_specs=in_specs, out_specs=out_specs,
        compiler_params=pltpu.CompilerParams(
            dimension_semantics=("arbitrary",), vmem_limit_bytes=VMEM_LIMIT),
        name="mixout_%s%s" % ("latent" if latent else "context", "_route" if route else ""),
    )(*args)


def _ffn_kernel(h_ref, x_ref, mod_ref, wg_ref, wu_ref, wd_ref, o_ref):
    h = h_ref[...]
    a = _silu(jnp.dot(h, wg_ref[...], preferred_element_type=F32)) * \
        jnp.dot(h, wu_ref[...], preferred_element_type=F32)
    f = jnp.dot(a.astype(BF16), wd_ref[...], preferred_element_type=F32)
    o_ref[...] = x_ref[...] + mod_ref[5:6] * f


def _ffn(h2, x1, mods_l, wg, wu, wd, latent, seq):
    rows = h2.shape[0]
    tm = 512
    ff = wg.shape[1]
    mod_row = (lambda i: (i * tm) // seq) if latent else (lambda i: CTX_ROW)
    resident = lambda shape: pl.BlockSpec(shape, lambda i: (0, 0), pipeline_mode=pl.Buffered(1))
    return pl.pallas_call(
        _ffn_kernel,
        out_shape=jax.ShapeDtypeStruct((rows, D_MODEL), F32),
        grid=(rows // tm,),
        in_specs=[pl.BlockSpec((tm, D_MODEL), lambda i: (i, 0)),
                  pl.BlockSpec((tm, D_MODEL), lambda i: (i, 0)),
                  pl.BlockSpec((None, 6, D_MODEL), lambda i: (mod_row(i), 0, 0)),
                  resident((D_MODEL, ff)), resident((D_MODEL, ff)), resident((ff, D_MODEL))],
        out_specs=pl.BlockSpec((tm, D_MODEL), lambda i: (i, 0)),
        compiler_params=pltpu.CompilerParams(
            dimension_semantics=("arbitrary",), vmem_limit_bytes=VMEM_LIMIT),
        name="ffn_dense",
    )(h2, x1, mods_l, wg, wu, wd)


def _route_tables(route_t, cnt, seq):
    b = route_t.shape[0]
    per_block = MOE_BLOCK // seq
    nb = b // per_block
    c = cnt[:, 0, :N_EXPERTS].astype(jnp.int32).reshape(nb, per_block, N_EXPERTS)
    before = jnp.cumsum(c, axis=1) - c
    total = c.sum(axis=1)
    aligned = (total + 7) // 8 * 8
    starts = jnp.cumsum(aligned, axis=1) - aligned
    base = (starts[:, None, :] + before).reshape(b, N_EXPERTS)
    expert = route_t[:, 2:4, :].astype(jnp.int32)
    rank = route_t[:, 4:6, :].astype(jnp.int32)
    slot = rank
    for e in range(N_EXPERTS):
        slot = slot + jnp.where(expert == e, base[:, e][:, None, None], 0)
    slots = slot.reshape(nb, per_block, 2, seq).transpose(0, 2, 1, 3).reshape(nb, 2, MOE_BLOCK)
    return starts.reshape(-1), total.reshape(-1), slots


MOE_SLOTS = 2 * MOE_BLOCK
MOE_ROWS = MOE_SLOTS + 8 * N_EXPERTS + MOE_TILE


MOE_GROUP = 16


def _moe_kernel(starts_ref, counts_ref, slots_ref, h_ref, route_ref, wgu_ref, wd_ref, f_ref,
                xs_ref, stage_ref):
    b = pl.program_id(0)
    e = pl.program_id(1)
    half = D_MODEL // 2

    @pl.when(e == 0)
    def _():
        xs_ref[MOE_SLOTS:, :] = jnp.zeros((MOE_ROWS - MOE_SLOTS, half), jnp.uint32)
        zero_row = jnp.zeros((1, half), jnp.uint32)
        for g in range(N_EXPERTS):
            end = starts_ref[b * N_EXPERTS + g] + counts_ref[b * N_EXPERTS + g]
            for r in range(7):
                xs_ref[pl.ds(end + r, 1), :] = zero_row

        def put(g, c):
            t0 = pl.multiple_of(g * MOE_GROUP, MOE_GROUP)
            hb = h_ref[pl.ds(t0, MOE_GROUP), :].astype(F32)
            stage_ref[:MOE_GROUP, :] = _pack2(hb[:, :half], hb[:, half:])
            for r in range(MOE_GROUP):
                row = stage_ref[r:r + 1, :]
                xs_ref[pl.ds(slots_ref[0, t0 + r], 1), :] = row
                xs_ref[pl.ds(slots_ref[1, t0 + r], 1), :] = row
            return c
        lax.fori_loop(0, MOE_BLOCK // MOE_GROUP, put, 0)

    start = starts_ref[b * N_EXPERTS + e]
    count = counts_ref[b * N_EXPERTS + e]

    def tile(j, c):
        s = pl.multiple_of(start + j * MOE_TILE, 8)
        xg = xs_ref[pl.ds(s, MOE_TILE), :]
        x_lo32 = _unpack2(xg, 0)
        x_hi32 = _unpack2(xg, 1)
        x_lo = x_lo32.astype(BF16)
        x_hi = x_hi32.astype(BF16)
        ff = wgu_ref.shape[1] // 2
        hgu = (jnp.dot(x_lo, wgu_ref[:half, :], preferred_element_type=F32)
               + jnp.dot(x_hi, wgu_ref[half:, :], preferred_element_type=F32))
        a = (_silu(hgu[:, :ff]) * hgu[:, ff:]).astype(BF16)
        y = jnp.dot(a, wd_ref[...], preferred_element_type=F32)
        valid = lax.broadcasted_iota(jnp.int32, (MOE_TILE, half), 0) < count - j * MOE_TILE
        xs_ref[pl.ds(s, MOE_TILE), :] = _pack2(jnp.where(valid, y[:, :half], x_lo32),
                                               jnp.where(valid, y[:, half:], x_hi32))
        return c
    lax.fori_loop(0, (count + MOE_TILE - 1) // MOE_TILE, tile, 0)

    @pl.when(e == pl.num_programs(1) - 1)
    def _():
        def take(g, c):
            t0 = pl.multiple_of(g * MOE_GROUP, MOE_GROUP)
            for r in range(MOE_GROUP):
                stage_ref[r:r + 1, :] = xs_ref[pl.ds(slots_ref[0, t0 + r], 1), :]
                stage_ref[MOE_GROUP + r:MOE_GROUP + r + 1, :] = xs_ref[pl.ds(slots_ref[1, t0 + r], 1), :]
            z0 = stage_ref[:MOE_GROUP, :]
            z1 = stage_ref[MOE_GROUP:, :]
            route = route_ref[pl.ds(t0, MOE_GROUP), :]
            w1 = route[:, 0:1]
            w2 = route[:, 1:2]
            f_ref[pl.ds(t0, MOE_GROUP), :half] = (w1 * _unpack2(z0, 0) + w2 * _unpack2(z1, 0)).astype(BF16)
            f_ref[pl.ds(t0, MOE_GROUP), half:] = (w1 * _unpack2(z0, 1) + w2 * _unpack2(z1, 1)).astype(BF16)
            return c
        lax.fori_loop(0, MOE_BLOCK // MOE_GROUP, take, 0)


def _moe_experts(h2, route, starts, counts, slots, wgu, wd):
    rows = h2.shape[0]
    nb = rows // MOE_BLOCK
    ff = wd.shape[1]
    grid_spec = pltpu.PrefetchScalarGridSpec(
        num_scalar_prefetch=2,
        grid=(nb, N_EXPERTS),
        in_specs=[
            pl.BlockSpec((None, 2, MOE_BLOCK), lambda b, e, st, ct: (b, 0, 0), memory_space=pltpu.SMEM),
            pl.BlockSpec((MOE_BLOCK, D_MODEL), lambda b, e, st, ct: (b, 0)),
            pl.BlockSpec((MOE_BLOCK, ROUTE_W), lambda b, e, st, ct: (b, 0)),
            pl.BlockSpec((None, D_MODEL, 2 * ff), lambda b, e, st, ct: (e, 0, 0)),
            pl.BlockSpec((None, ff, D_MODEL), lambda b, e, st, ct: (e, 0, 0)),
        ],
        out_specs=pl.BlockSpec((MOE_BLOCK, D_MODEL), lambda b, e, st, ct: (b, 0)),
        scratch_shapes=[pltpu.VMEM((MOE_ROWS, D_MODEL // 2), jnp.uint32),
                        pltpu.VMEM((2 * MOE_GROUP, D_MODEL // 2), jnp.uint32)],
    )
    return pl.pallas_call(
        _moe_kernel,
        out_shape=jax.ShapeDtypeStruct((rows, D_MODEL), BF16),
        grid_spec=grid_spec,
        compiler_params=pltpu.CompilerParams(
            dimension_semantics=("arbitrary", "arbitrary"), vmem_limit_bytes=VMEM_LIMIT),
        name="moe_experts",
    )(starts, counts, slots, h2, route, wgu, wd)


def _final_kernel(x_ref, f_ref, mod_ref, fg_ref, o_ref):
    y = x_ref[...] + mod_ref[5:6] * f_ref[...].astype(F32)
    o_ref[...] = _rms(y, fg_ref[...])


def _final(x1, f, mods_l, final_g, latent, seq):
    rows = x1.shape[0]
    tr = 512
    mod_row = (lambda i: (i * tr) // seq) if latent else (lambda i: CTX_ROW)
    return pl.pallas_call(
        _final_kernel,
        out_shape=jax.ShapeDtypeStruct((rows, D_MODEL), F32),
        grid=(rows // tr,),
        in_specs=[pl.BlockSpec((tr, D_MODEL), lambda i: (i, 0)),
                  pl.BlockSpec((tr, D_MODEL), lambda i: (i, 0)),
                  pl.BlockSpec((None, 6, D_MODEL), lambda i: (mod_row(i), 0, 0)),
                  pl.BlockSpec((1, D_MODEL), lambda i: (0, 0))],
        out_specs=pl.BlockSpec((tr, D_MODEL), lambda i: (i, 0)),
        compiler_params=pltpu.CompilerParams(
            dimension_semantics=("arbitrary",), vmem_limit_bytes=VMEM_LIMIT),
        name="final_norm",
    )(x1, f, mods_l, final_g.reshape(1, D_MODEL))


def _dup_heads(a):
    b, n = a.shape[:2]
    return jnp.repeat(a, 2, axis=2).reshape(b, n, 2 * KV_W).astype(BF16)


def kernel(x_prompt, x_sample, c, cache_k, cache_v, state_ssm_re, state_ssm_im, c_ctx, mod_w, mod_b, norm1_g, norm2_g, w_in, w_out, q_norm_g, k_norm_g, ssm_a_re, ssm_a_im, ssm_log_dt, ssm_b_re, ssm_b_im, ssm_c_re, ssm_c_im, ssm_d, ssm_glu_w, pool_w, pool_scale, ffn_w_gate, ffn_w_up, ffn_w_down, moe_router_w, moe_router_b, moe_w_gate, moe_w_up, moe_w_down, final_g):
    bp, lp, _ = x_prompt.shape
    bs, ls, _ = x_sample.shape
    assert bs == SEQ_GROUP and bp % SEQ_GROUP == 0

    cond = jnp.zeros((MOD_ROWS, D_MODEL), F32).at[:bs].set(c).at[CTX_ROW].set(c_ctx)
    mods = _modulation(cond, mod_w, mod_b).reshape(DEPTH, MOD_ROWS, 6, D_MODEL)

    bw, cw, lam = _ssm_prep(ssm_a_re, ssm_a_im, ssm_log_dt, ssm_b_re, ssm_b_im, ssm_c_re, ssm_c_im)

    xp, xs = x_prompt, x_sample
    new_k, new_v, new_state = [], [], []
    for l in range(DEPTH):
        mods_l = mods[l]
        w_in_bf = w_in[l].astype(BF16)
        w_out_bf = w_out[l].astype(BF16)
        glu_bf = ssm_glu_w[l].astype(BF16)
        moe = l % 2 == 1
        i = l // 2
        router = (moe_router_w[i], moe_router_b[i]) if moe else None
        if moe:
            wgu = jnp.concatenate([moe_w_gate[i], moe_w_up[i]], axis=2).astype(BF16)
            wd = moe_w_down[i].astype(BF16)
        else:
            wg, wu, wd = (ffn_w_gate[i].astype(BF16), ffn_w_up[i].astype(BF16), ffn_w_down[i].astype(BF16))
        for latent in (False, True):
            x = xs if latent else xp
            b, seq, _ = x.shape
            nbg = b // SEQ_GROUP
            rows = b * seq
            outs = _inproj(x.reshape(rows, D_MODEL), mods_l, norm1_g[l], w_in_bf, q_norm_g[l], k_norm_g[l],
                           latent, seq)
            q, kd, vd = (a.reshape(b, seq, a.shape[-1]) for a in outs[:3])
            u_ssm = outs[3]
            u_pool = outs[4].reshape(b, seq, POOL_W)
            if latent:
                kds = [kd, _dup_heads(cache_k[:, l])]
                vds = [vd, _dup_heads(cache_v[:, l])]
                h0 = jnp.concatenate([
                    state_ssm_re[:, l, 0].reshape(b, SSM_GP), state_ssm_im[:, l, 0].reshape(b, SSM_GP),
                    state_ssm_re[:, l, 1].reshape(b, SSM_GP), state_ssm_im[:, l, 1].reshape(b, SSM_GP)],
                    axis=1).reshape(nbg, SEQ_GROUP, 4 * SSM_GP)
            else:
                kds, vds = [kd], [vd]
                new_k.append(outs[5])
                new_v.append(outs[6])
                h0 = jnp.zeros((nbg, SEQ_GROUP, 4 * SSM_GP), F32)
            attn = _attention(q, kds, vds)
            y_ssm, hf = _ssm(u_ssm, l, bw, cw, lam, h0, ssm_d[l], glu_bf, seq)
            if not latent:
                new_state.append(hf.reshape(b, 2, 2, SSM_GROUPS, SSM_STATE))
            res = _mixout(x, attn, y_ssm.reshape(b, seq, SSM_W), u_pool, mods_l,
                          pool_w[l], pool_scale[l], w_out_bf, norm2_g[l], latent, router)
            x1 = res[0].reshape(rows, D_MODEL)
            if moe:
                route = res[2].reshape(rows, ROUTE_W)
                starts, counts, slots = _route_tables(res[3], res[4], seq)
                f = _moe_experts(res[1].reshape(rows, D_MODEL), route, starts, counts, slots, wgu, wd)
                y = _final(x1, f, mods_l, final_g, latent, seq)
            else:
                y = _ffn(res[1].reshape(rows, D_MODEL), x1, mods_l, wg, wu, wd, latent, seq)
            y = y.reshape(b, seq, D_MODEL)
            if latent:
                xs = y
            else:
                xp = y

    def cache_out(parts):
        a = jnp.stack(parts, axis=1).reshape(bp, DEPTH, N_KV_HEADS, HEAD_DIM, lp)
        return a.transpose(0, 1, 4, 2, 3)
    new_cache_k = cache_out(new_k)
    new_cache_v = cache_out(new_v)
    st = jnp.stack(new_state, axis=1)
    return (xp, xs, new_cache_k, new_cache_v, st[:, :, :, 0], st[:, :, :, 1])
```

```python
import functools

import numpy as np
import jax
import jax.numpy as jnp
from jax import lax
from jax.experimental import pallas as pl
from jax.experimental.pallas import tpu as pltpu

D_MODEL = 1024
DEPTH = 2
GRID_W = 64
ATTN_W = 512
HEAD_DIM = 64
N_HEADS = 8
N_KV_HEADS = 2
KV_W = 128
ROPE_THETA = 10000.0
SSM_W = 256
SSM_CH = 16
SSM_GROUPS = 16
SSM_STATE = 64
SSM_GP = SSM_GROUPS * SSM_STATE
POOL_W = 256
POOL_WINDOWS = (2, 4, 8, 16)
POOL_CH = 64
POOL_PAD = max(POOL_WINDOWS) // 2
IN_COLS = ATTN_W + 2 * KV_W + SSM_W + POOL_W
FF_DENSE = 2816
N_EXPERTS = 8
FF_EXPERT = 1408
EPS = 1e-6
LOG2_E = 1.4426950408889634

SEQ_GROUP = 8
SSM_CHUNK = 16
SSM_COLBLK = 512
ROUTE_W = 128
MIX_ROWS = 1024
MOE_BLOCK = 2048
MOE_TILE = 128
MXU_TILE = 256
MOE_TAIL = FF_EXPERT % MXU_TILE
MOD_ROWS = 16
CTX_ROW = 8
VMEM_LIMIT = 56 * 1024 * 1024

F32 = jnp.float32
BF16 = jnp.bfloat16


def _silu(x):
    return x * jax.nn.sigmoid(x)


def _rms(x32, g):
    return x32 * lax.rsqrt(jnp.mean(x32 * x32, axis=-1, keepdims=True) + EPS) * g


def _mod_kernel(cond_ref, w_ref, b_ref, o_ref):
    s = _silu(cond_ref[...])
    w = w_ref[...]
    s_hi = s.astype(BF16)
    s_lo = (s - s_hi.astype(F32)).astype(BF16)
    w_hi = w.astype(BF16)
    w_lo = (w - w_hi.astype(F32)).astype(BF16)
    o_ref[...] = (jnp.dot(s_hi, w_hi, preferred_element_type=F32)
                  + jnp.dot(s_hi, w_lo, preferred_element_type=F32)
                  + jnp.dot(s_lo, w_hi, preferred_element_type=F32) + b_ref[...])


def _modulation(cond, mod_w, mod_b):
    tn = 1536
    n = 6 * D_MODEL
    return pl.pallas_call(
        _mod_kernel,
        out_shape=jax.ShapeDtypeStruct((DEPTH, MOD_ROWS, n), F32),
        grid=(DEPTH, n // tn),
        in_specs=[
            pl.BlockSpec((MOD_ROWS, D_MODEL), lambda l, j: (0, 0)),
            pl.BlockSpec((None, D_MODEL, tn), lambda l, j: (l, 0, j)),
            pl.BlockSpec((None, 1, tn), lambda l, j: (l, 0, j)),
        ],
        out_specs=pl.BlockSpec((None, MOD_ROWS, tn), lambda l, j: (l, 0, j)),
        compiler_params=pltpu.CompilerParams(
            dimension_semantics=("arbitrary", "arbitrary"), vmem_limit_bytes=VMEM_LIMIT),
        name="modulation",
    )(cond, mod_w, mod_b.reshape(DEPTH, 1, n))


def _ssm_prep_kernel(are_ref, aim_ref, dt_ref, bre_ref, bim_ref, cre_ref, cim_ref, bw_ref, cw_ref, lam_ref):
    row_group = lax.broadcasted_iota(jnp.int32, (SSM_W, SSM_GP), 0) // SSM_CH
    lane_group = lax.broadcasted_iota(jnp.int32, (SSM_W, SSM_GP), 1) // SSM_STATE
    own = row_group == lane_group
    for ld in range(2 * DEPTH):
        l, d = divmod(ld, 2)
        a_re = are_ref[ld:ld + 1, :]
        a_im = aim_ref[ld:ld + 1, :]
        dt = jnp.exp(dt_ref[ld:ld + 1, :])
        mag = jnp.exp(a_re * dt)
        l_re = mag * jnp.cos(a_im * dt)
        l_im = mag * jnp.sin(a_im * dt)
        lam_ref[l, 2 * d] = jnp.broadcast_to(l_re, (SEQ_GROUP, SSM_GP))
        lam_ref[l, 2 * d + 1] = jnp.broadcast_to(l_im, (SEQ_GROUP, SSM_GP))
        x = l_re - 1.0
        den = a_re * a_re + a_im * a_im
        f_re = (x * a_re + l_im * a_im) / den
        f_im = (l_im * a_re - x * a_im) / den
        b_re = bre_ref[ld]
        b_im = bim_ref[ld]
        bbar = (f_re * b_re - f_im * b_im, f_re * b_im + f_im * b_re)
        cmat = (cre_ref[ld], -cim_ref[ld])
        for ri in range(2):
            c0 = (2 * d + ri) * SSM_GP
            tiled = jnp.concatenate([bbar[ri]] * SSM_GROUPS, axis=0)
            bw_ref[l, :, c0:c0 + SSM_GP] = jnp.where(own, tiled, 0.0).astype(BF16)
            cw_ref[l, c0:c0 + SSM_GP, :] = jnp.where(own, cmat[ri], 0.0).T.astype(BF16)


def _ssm_prep(a_re, a_im, log_dt, b_re, b_im, c_re, c_im):
    ld = DEPTH * 2
    are = a_re.reshape(ld, SSM_GP)
    aim = a_im.reshape(ld, SSM_GP)
    dt = jnp.repeat(log_dt.reshape(ld, SSM_GROUPS), SSM_STATE, axis=1)
    to_rows = lambda b: b.reshape(ld, SSM_GROUPS, SSM_STATE, SSM_CH).transpose(0, 3, 1, 2).reshape(ld, SSM_CH, SSM_GP)
    repeat_lanes = lambda c: jnp.tile(c.reshape(ld, SSM_W, SSM_STATE), (1, 1, SSM_GROUPS))
    return pl.pallas_call(
        _ssm_prep_kernel,
        out_shape=(jax.ShapeDtypeStruct((DEPTH, SSM_W, 4 * SSM_GP), BF16),
                   jax.ShapeDtypeStruct((DEPTH, 4 * SSM_GP, SSM_W), BF16),
                   jax.ShapeDtypeStruct((DEPTH, 4, SEQ_GROUP, SSM_GP), F32)),
        compiler_params=pltpu.CompilerParams(vmem_limit_bytes=VMEM_LIMIT),
        name="ssm_prep",
    )(are, aim, dt, to_rows(b_re), to_rows(b_im), repeat_lanes(c_re), repeat_lanes(c_im))


def _inproj_kernel(*refs, latent):
    if latent:
        (x_ref, mod_ref, g_ref, w_ref, qg_ref, kg_ref, sq_ref, sk_ref, cos_ref, sa_ref, sb_ref,
         q_ref, kd_ref, vd_ref, us_ref, up_ref) = refs
    else:
        (x_ref, mod_ref, g_ref, w_ref, qg_ref, kg_ref, sq_ref, sk_ref,
         q_ref, kd_ref, vd_ref, us_ref, up_ref, ko_ref, vo_ref) = refs
    x = x_ref[...]
    mod = mod_ref[...]
    h = _rms(x, g_ref[...]) * (1.0 + mod[1:2]) + mod[0:1]
    p = jnp.dot(h.astype(BF16), w_ref[...], preferred_element_type=F32)
    q = p[:, :ATTN_W]
    k = p[:, ATTN_W:ATTN_W + KV_W]
    v = p[:, ATTN_W + KV_W:ATTN_W + 2 * KV_W]
    q_ms = jnp.dot((q * q).astype(BF16), sq_ref[...], preferred_element_type=F32)
    k_ms = jnp.dot((k * k).astype(BF16), sk_ref[...], preferred_element_type=F32)
    q = q * lax.rsqrt(q_ms + EPS) * qg_ref[...]
    k = k * lax.rsqrt(k_ms + EPS) * kg_ref[...]
    if latent:
        cos = cos_ref[...]
        sa = sa_ref[...]
        sb = sb_ref[...]
        cos4 = jnp.concatenate([cos] * 4, axis=1)
        sa4 = jnp.concatenate([sa] * 4, axis=1)
        sb4 = jnp.concatenate([sb] * 4, axis=1)
        q = (q * cos4 + pltpu.roll(q, ATTN_W - 16, axis=1) * sa4 + pltpu.roll(q, 16, axis=1) * sb4)
        k = (k * cos + pltpu.roll(k, KV_W - 16, axis=1) * sa + pltpu.roll(k, 16, axis=1) * sb)
    else:
        seq = ko_ref.shape[-1]
        for s in range(ko_ref.shape[0]):
            ko_ref[s] = k[s * seq:(s + 1) * seq, :].T
            vo_ref[s] = v[s * seq:(s + 1) * seq, :].T
    q_ref[...] = (q * (HEAD_DIM ** -0.5 * LOG2_E)).astype(BF16)
    lane = lax.broadcasted_iota(jnp.int32, k.shape, 1)
    first = lane < HEAD_DIM
    k_sw = pltpu.roll(k, HEAD_DIM, axis=1)
    v_sw = pltpu.roll(v, HEAD_DIM, axis=1)
    kd_ref[:, :KV_W] = jnp.where(first, k, k_sw).astype(BF16)
    kd_ref[:, KV_W:] = jnp.where(first, k_sw, k).astype(BF16)
    vd_ref[:, :KV_W] = jnp.where(first, v, v_sw).astype(BF16)
    vd_ref[:, KV_W:] = jnp.where(first, v_sw, v).astype(BF16)
    o3 = ATTN_W + 2 * KV_W
    us_ref[0] = p[:, o3:o3 + SSM_W // 2]
    us_ref[1] = p[:, o3 + SSM_W // 2:o3 + SSM_W]
    up_ref[...] = p[:, o3 + SSM_W:]


def _head_mean_matrix(width):
    i = np.arange(width) // HEAD_DIM
    return jnp.asarray((i[:, None] == i[None, :]).astype(np.float32) / HEAD_DIM, dtype=BF16)


def _rope_tables(seq):
    t = np.arange(seq)
    row = (t // GRID_W).astype(np.float64)
    col = (t % GRID_W).astype(np.float64)
    nf = HEAD_DIM // 4
    inv = ROPE_THETA ** (-np.arange(nf, dtype=np.float64) / nf)
    ang_r = row[:, None] * inv[None, :]
    ang_c = col[:, None] * inv[None, :]
    cos = np.concatenate([np.cos(ang_r), np.cos(ang_r), np.cos(ang_c), np.cos(ang_c)], axis=1)
    sin = np.concatenate([np.sin(ang_r), np.sin(ang_r), np.sin(ang_c), np.sin(ang_c)], axis=1)
    lower = np.tile(np.concatenate([np.ones(nf), np.zeros(nf)]), 2)[None, :]
    sa = -sin * lower
    sb = sin * (1.0 - lower)
    tile = lambda a: jnp.asarray(np.tile(a, (1, 2)), dtype=F32)
    return tile(cos), tile(sa), tile(sb)


def _inproj(x, mods_l, g1, w_in_bf, qg, kg, latent, seq):
    rows = x.shape[0]
    tr = 512
    tiles_per_seq = max(seq // tr, 1)
    mod_row = (lambda i: (i * tr) // seq) if latent else (lambda i: CTX_ROW)
    const = lambda shape: pl.BlockSpec(shape, lambda i: (0,) * len(shape))
    row_spec = lambda width: pl.BlockSpec((tr, width), lambda i: (i, 0))
    in_specs = [
        row_spec(D_MODEL),
        pl.BlockSpec((None, 6, D_MODEL), lambda i: (mod_row(i), 0, 0)),
        const((1, D_MODEL)),
        const((D_MODEL, IN_COLS)),
        const((1, ATTN_W)),
        const((1, KV_W)),
        const((ATTN_W, ATTN_W)),
        const((KV_W, KV_W)),
    ]
    args = [x, mods_l, g1.reshape(1, D_MODEL), w_in_bf,
            jnp.tile(qg, N_HEADS).reshape(1, ATTN_W), jnp.tile(kg, N_KV_HEADS).reshape(1, KV_W),
            _head_mean_matrix(ATTN_W), _head_mean_matrix(KV_W)]
    widths = [(ATTN_W, BF16), (2 * KV_W, BF16), (2 * KV_W, BF16), (None, F32), (POOL_W, F32)]
    if latent:
        assert seq % tr == 0
        cos, sa, sb = _rope_tables(seq)
        in_specs += [pl.BlockSpec((tr, KV_W), lambda i: (i % tiles_per_seq, 0))] * 3
        args += [cos, sa, sb]
    out_shape = [jax.ShapeDtypeStruct((rows, w) if w else (2, rows, SSM_W // 2), dt) for w, dt in widths]
    out_specs = [row_spec(w) if w else pl.BlockSpec((2, tr, SSM_W // 2), lambda i: (0, i, 0)) for w, _ in widths]
    if not latent:
        assert tr % seq == 0
        out_shape += [jax.ShapeDtypeStruct((rows // seq, KV_W, seq), F32)] * 2
        out_specs += [pl.BlockSpec((tr // seq, KV_W, seq), lambda i: (i, 0, 0))] * 2
    return pl.pallas_call(
        functools.partial(_inproj_kernel, latent=latent),
        out_shape=out_shape, grid=(rows // tr,), in_specs=in_specs, out_specs=out_specs,
        compiler_params=pltpu.CompilerParams(
            dimension_semantics=("arbitrary",), vmem_limit_bytes=VMEM_LIMIT),
        name="inproj_latent" if latent else "inproj_context",
    )(*args)


def _attn_kernel(*refs, n_kv_src):
    q_ref = refs[0]
    k_refs = refs[1:1 + n_kv_src]
    v_refs = refs[1 + n_kv_src:1 + 2 * n_kv_src]
    o_ref = refs[1 + 2 * n_kv_src]
    tq = q_ref.shape[0]
    lane = lax.broadcasted_iota(jnp.int32, (tq, KV_W), 1)
    first = lane < HEAD_DIM
    for pair in range(N_HEADS // 2):
        kv = pair // 2
        cols = slice(pair * KV_W, (pair + 1) * KV_W)
        kcols = slice(kv * KV_W, (kv + 1) * KV_W)
        qp = q_ref[:, cols]
        halves = []
        for half in range(2):
            qm = jnp.where(first if half == 0 else jnp.logical_not(first), qp, jnp.zeros_like(qp))
            scores = [lax.dot_general(qm, k_ref[:, kcols], (((1,), (1,)), ((), ())),
                                      preferred_element_type=F32) for k_ref in k_refs]
            m = scores[0].max(axis=-1, keepdims=True)
            for s in scores[1:]:
                m = jnp.maximum(m, s.max(axis=-1, keepdims=True))
            den = jnp.zeros((tq, 1), F32)
            acc = jnp.zeros((tq, KV_W), F32)
            for s, v_ref in zip(scores, v_refs):
                p = jnp.exp2(s - m)
                den = den + p.sum(axis=-1, keepdims=True)
                acc = acc + jnp.dot(p.astype(BF16), v_ref[:, kcols], preferred_element_type=F32)
            halves.append(acc / den)
        o_ref[:, cols] = jnp.where(first, halves[0], halves[1]).astype(o_ref.dtype)


def _attention(q, kds, vds):
    b, seq, _ = q.shape
    tq = min(seq, 512)
    n_src = len(kds)
    kv_spec = lambda a: pl.BlockSpec((None, a.shape[1], 2 * KV_W), lambda i, t: (i, 0, 0))
    return pl.pallas_call(
        functools.partial(_attn_kernel, n_kv_src=n_src),
        out_shape=jax.ShapeDtypeStruct((b, seq, ATTN_W), BF16),
        grid=(b, seq // tq),
        in_specs=[pl.BlockSpec((None, tq, ATTN_W), lambda i, t: (i, t, 0))]
                 + [kv_spec(a) for a in kds] + [kv_spec(a) for a in vds],
        out_specs=pl.BlockSpec((None, tq, ATTN_W), lambda i, t: (i, t, 0)),
        compiler_params=pltpu.CompilerParams(
            dimension_semantics=("arbitrary", "arbitrary"), vmem_limit_bytes=VMEM_LIMIT),
        name="attention_%dsrc" % n_src,
    )(q, *kds, *vds)


def _ssm_kernel(u_ref, bw_ref, cw_ref, lam_ref, h0_ref, d_ref, glu_ref,
                o_ref, hf_ref, y_ref, utm_scr, bu0_scr, bu1_scr, hs0_scr, hs1_scr, st_scr):
    rows = u_ref.shape[1]
    seq = rows // SEQ_GROUP
    crow = SSM_CHUNK * SEQ_GROUP
    nchunks = rows // crow
    w2 = 2 * SSM_GP
    lanes = SSM_W // 2

    def time_major(c):
        t0 = c * SSM_CHUNK
        return jnp.concatenate(
            [jnp.concatenate([u_ref[h, pl.ds(t0 + s, SEQ_GROUP, stride=seq), :] for h in range(2)], axis=1)
             for s in range(SSM_CHUNK)], axis=0)

    def y_rows(r, n):
        return jnp.concatenate([y_ref[0, pl.ds(r, n), :], y_ref[1, pl.ds(r, n), :]], axis=1)

    def set_y_rows(r, n, val):
        y_ref[0, pl.ds(r, n), :] = val[:, :lanes]
        y_ref[1, pl.ds(r, n), :] = val[:, lanes:]

    st_scr[...] = h0_ref[...]

    def init_rows(i, c):
        r = pl.multiple_of(i * crow, crow)
        tm = time_major(i)
        set_y_rows(r, crow, tm * d_ref[...])
        utm_scr[pl.ds(r, crow), :] = tm.astype(BF16)
        return c
    lax.fori_loop(0, nchunks, init_rows, 0)

    last = nchunks - 1
    bu_scr = (bu0_scr, bu1_scr)
    hs_scr = (hs0_scr, hs1_scr)

    def project_in(i, buf):
        i = jnp.clip(i, 0, last)
        for d, c in ((0, i), (1, last - i)):
            r = pl.multiple_of(c * crow, crow)
            bu_scr[buf][d] = jnp.dot(utm_scr[pl.ds(r, crow), :], bw_ref[:, d * w2:(d + 1) * w2],
                                     preferred_element_type=F32)

    def recur(buf):
        for d in range(2):
            for cb in range(SSM_GP // SSM_COLBLK):
                re_c = slice(cb * SSM_COLBLK, (cb + 1) * SSM_COLBLK)
                im_c = slice(SSM_GP + cb * SSM_COLBLK, SSM_GP + (cb + 1) * SSM_COLBLK)
                l_re = lam_ref[2 * d, :, re_c]
                l_im = lam_ref[2 * d + 1, :, re_c]
                h_re = st_scr[:, d * w2 + cb * SSM_COLBLK:d * w2 + (cb + 1) * SSM_COLBLK]
                h_im = st_scr[:, d * w2 + SSM_GP + cb * SSM_COLBLK:d * w2 + SSM_GP + (cb + 1) * SSM_COLBLK]
                for s in range(SSM_CHUNK):
                    t = s if d == 0 else SSM_CHUNK - 1 - s
                    r = slice(t * SEQ_GROUP, (t + 1) * SEQ_GROUP)
                    n_re = l_re * h_re - l_im * h_im + bu_scr[buf][d, r, re_c]
                    n_im = l_re * h_im + l_im * h_re + bu_scr[buf][d, r, im_c]
                    hs_scr[buf][d, r, re_c] = n_re
                    hs_scr[buf][d, r, im_c] = n_im
                    h_re, h_im = n_re, n_im
                st_scr[:, d * w2 + cb * SSM_COLBLK:d * w2 + (cb + 1) * SSM_COLBLK] = h_re
                st_scr[:, d * w2 + SSM_GP + cb * SSM_COLBLK:d * w2 + SSM_GP + (cb + 1) * SSM_COLBLK] = h_im

    def project_out(i, buf):
        i = jnp.clip(i, 0, last)
        for d, c in ((0, i), (1, last - i)):
            r = pl.multiple_of(c * crow, crow)
            contrib = jnp.dot(hs_scr[buf][d].astype(BF16), cw_ref[d * w2:(d + 1) * w2, :],
                              preferred_element_type=F32)
            set_y_rows(r, crow, y_rows(r, crow) + contrib)

    hs1_scr[...] = jnp.zeros(hs1_scr.shape, F32)
    project_in(0, 0)

    def two_steps(j, c):
        i = 2 * j
        recur(0)
        project_in(i + 1, 1)
        project_out(i - 1, 1)
        recur(1)
        project_in(i + 2, 0)
        project_out(i, 0)
        return c
    lax.fori_loop(0, nchunks // 2, two_steps, 0)
    project_out(last, 1)

    hf_ref[...] = st_scr[...]

    esteps = 128
    erow = esteps * SEQ_GROUP

    def epilogue(i, c):
        r = pl.multiple_of(i * erow, erow)
        z = jax.nn.gelu(y_rows(r, erow))
        g = jnp.dot(z.astype(BF16), glu_ref[...], preferred_element_type=F32)
        set_y_rows(r, erow, g[:, :SSM_W] * jax.nn.sigmoid(g[:, SSM_W:]))
        t0 = pl.multiple_of(i * esteps, esteps)
        for b in range(SEQ_GROUP):
            for h in range(2):
                o_ref[pl.ds(b * seq + t0, esteps), h * lanes:(h + 1) * lanes] = (
                    y_ref[h, pl.ds(r + b, esteps, stride=SEQ_GROUP), :].astype(BF16))
        return c
    lax.fori_loop(0, rows // erow, epilogue, 0)


def _ssm(u, layer, bw, cw, lam, h0, ssm_d, glu_bf, seq):
    rows = u.shape[1]
    grows = SEQ_GROUP * seq
    nbg = rows // grows
    crow = SSM_CHUNK * SEQ_GROUP
    const = lambda shape: pl.BlockSpec(shape, lambda i: (0,) * len(shape))
    return pl.pallas_call(
        _ssm_kernel,
        out_shape=(jax.ShapeDtypeStruct((rows, SSM_W), BF16),
                   jax.ShapeDtypeStruct((nbg, SEQ_GROUP, 4 * SSM_GP), F32)),
        grid=(nbg,),
        in_specs=[
            pl.BlockSpec((2, grows, SSM_W // 2), lambda i: (0, i, 0)),
            pl.BlockSpec((None, SSM_W, 4 * SSM_GP), lambda i: (layer, 0, 0)),
            pl.BlockSpec((None, 4 * SSM_GP, SSM_W), lambda i: (layer, 0, 0)),
            pl.BlockSpec((None, 4, SEQ_GROUP, SSM_GP), lambda i: (layer, 0, 0, 0)),
            pl.BlockSpec((None, SEQ_GROUP, 4 * SSM_GP), lambda i: (i, 0, 0)),
            const((1, SSM_W)),
            const((SSM_W, 2 * SSM_W)),
        ],
        out_specs=(pl.BlockSpec((grows, SSM_W), lambda i: (i, 0)),
                   pl.BlockSpec((None, SEQ_GROUP, 4 * SSM_GP), lambda i: (i, 0, 0))),
        scratch_shapes=[pltpu.VMEM((2, grows, SSM_W // 2), F32),
                        pltpu.VMEM((grows, SSM_W), BF16),
                        pltpu.VMEM((2, crow, 2 * SSM_GP), F32),
                        pltpu.VMEM((2, crow, 2 * SSM_GP), F32),
                        pltpu.VMEM((2, crow, 2 * SSM_GP), F32),
                        pltpu.VMEM((2, crow, 2 * SSM_GP), F32),
                        pltpu.VMEM((SEQ_GROUP, 4 * SSM_GP), F32)],
        compiler_params=pltpu.CompilerParams(
            dimension_semantics=("arbitrary",), vmem_limit_bytes=VMEM_LIMIT),
        name="ssm_scan",
    )(u, bw, cw, lam, h0, ssm_d.reshape(1, SSM_W), glu_bf)


def _pack2(lo, hi):
    return pltpu.pack_elementwise([lo, hi], packed_dtype=BF16)


def _unpack2(w, index):
    return pltpu.unpack_elementwise(w, index=index, packed_dtype=BF16, unpacked_dtype=F32)


def _mixout_kernel(*refs, route):
    if route:
        (x_ref, at_ref, ss_ref, up_ref, mod_ref, invc_ref, pw_ref, ps_ref, wo_ref, g2_ref, wr_ref, br_ref, tri_ref,
         x1_ref, h2_ref, route_ref, rt_ref, cnt_ref) = refs
    else:
        (x_ref, at_ref, ss_ref, up_ref, mod_ref, invc_ref, pw_ref, ps_ref, wo_ref, g2_ref,
         x1_ref, h2_ref) = refs
    nseq, seq, _ = up_ref.shape
    rows = nseq * seq
    zpad = jnp.zeros((POOL_PAD, POOL_W), F32)
    n_ext = seq + 2 * POOL_PAD
    back = lambda a, k: pltpu.roll(a, k, axis=0)
    ahead = lambda a, k: pltpu.roll(a, n_ext - k, axis=0)
    grp = lax.broadcasted_iota(jnp.int32, (n_ext, POOL_W), 1) // POOL_CH
    pooled = []
    for s in range(nseq):
        u = up_ref[s]
        ue = jnp.concatenate([zpad, u, zpad], axis=0)
        w2 = ue + back(ue, 1)
        w4 = back(w2, 1) + ahead(w2, 1)
        w8 = back(w4, 2) + ahead(w4, 2)
        w16 = back(w8, 4) + ahead(w8, 4)
        win = jnp.where(grp == 0, w2, jnp.where(grp == 1, w4, jnp.where(grp == 2, w8, w16)))
        pooled.append(win[POOL_PAD:POOL_PAD + seq] * invc_ref[...] - u)
    pooled = jnp.concatenate(pooled, axis=0)
    pool = jnp.dot(pooled.astype(BF16), pw_ref[...], preferred_element_type=F32) * ps_ref[...]
    mix = jnp.concatenate([at_ref[...].reshape(rows, ATTN_W), ss_ref[...].reshape(rows, SSM_W),
                           pool.astype(BF16)], axis=1)
    o = jnp.dot(mix, wo_ref[...], preferred_element_type=F32)
    mod = mod_ref[...]
    x1 = x_ref[...].reshape(rows, D_MODEL) + mod[2:3] * o
    x1_ref[...] = x1.reshape(nseq, seq, D_MODEL)
    h2 = _rms(x1, g2_ref[...]) * (1.0 + mod[4:5]) + mod[3:4]
    h2_ref[...] = h2.astype(BF16).reshape(nseq, seq, D_MODEL)
    if not route:
        return
    h_hi = h2.astype(BF16)
    h_lo = (h2 - h_hi.astype(F32)).astype(BF16)
    both = jnp.dot(h_hi, wr_ref[...], preferred_element_type=F32)
    logits = (both[:, :ROUTE_W] + both[:, ROUTE_W:]
              + jnp.dot(h_lo, wr_ref[:, :ROUTE_W], preferred_element_type=F32) + br_ref[...])
    lane = lax.broadcasted_iota(jnp.int32, logits.shape, 1).astype(F32)
    neg = float(np.finfo(np.float32).min)
    far = float(ROUTE_W)
    logits = jnp.where(lane < N_EXPERTS, logits, neg)
    m1 = logits.max(axis=-1, keepdims=True)
    i1 = jnp.where(logits == m1, lane, far).min(axis=-1, keepdims=True)
    rest = jnp.where(lane == i1, neg, logits)
    m2 = rest.max(axis=-1, keepdims=True)
    i2 = jnp.where(rest == m2, lane, far).min(axis=-1, keepdims=True)
    e2 = jnp.exp(m2 - m1)
    den = 1.0 + e2
    sel = jnp.where(lane == i1, 1.0, jnp.where(lane == i2, 1.0, 0.0))
    sel_bf = sel.astype(BF16)
    rank = jnp.concatenate([jnp.dot(tri_ref[...], sel_bf[s * seq:(s + 1) * seq], preferred_element_type=F32)
                            for s in range(nseq)], axis=0)
    r1 = jnp.where(lane == i1, rank, 0.0).sum(axis=-1, keepdims=True)
    r2 = jnp.where(lane == i2, rank, 0.0).sum(axis=-1, keepdims=True)
    out = jnp.zeros_like(logits)
    for j, col in enumerate((1.0 / den, e2 / den, i1, i2, r1, r2)):
        out = jnp.where(lane == j, col, out)
    route_ref[...] = out.reshape(nseq, seq, ROUTE_W)
    for s in range(nseq):
        part = slice(s * seq, (s + 1) * seq)
        rt_ref[s] = out[part].T[:8, :]
        cnt_ref[s] = jnp.broadcast_to(sel[part].sum(axis=0, keepdims=True), cnt_ref.shape[1:])


def _pool_inv_count(seq):
    t = np.arange(seq)
    cols = []
    for win in POOL_WINDOWS:
        lo = np.clip(t - win // 2, 0, seq)
        hi = np.clip(t + win // 2, 0, seq)
        cols.append(np.repeat((hi - lo).astype(np.float32)[:, None], POOL_CH, axis=1))
    return np.concatenate(cols, axis=1)


def _block_diag_pool(pool_w):
    eye = jnp.eye(len(POOL_WINDOWS), dtype=F32)
    m = eye[:, None, :, None] * pool_w[:, :, None, :]
    return m.reshape(POOL_W, POOL_W)


def _mixout(x, attn, ssm_tm, u_pool, mods_l, pool_w, pool_scale, w_out_bf, g2, latent, router):
    b, seq, _ = x.shape
    route = router is not None
    nseq = 1 if latent else max(1, MIX_ROWS // seq)
    mod_row = (lambda i: i) if latent else (lambda i: CTX_ROW)
    const = lambda shape: pl.BlockSpec(shape, lambda i: (0,) * len(shape))
    cnt = _pool_inv_count(seq)
    row_spec = lambda width: pl.BlockSpec((nseq, seq, width), lambda i: (i, 0, 0))
    in_specs = [
        row_spec(D_MODEL), row_spec(ATTN_W), row_spec(SSM_W), row_spec(POOL_W),
        pl.BlockSpec((None, 6, D_MODEL), lambda i: (mod_row(i), 0, 0)),
        const((seq, POOL_W)),
        const((POOL_W, POOL_W)),
        const((1, POOL_W)),
        const((D_MODEL, D_MODEL)),
        const((1, D_MODEL)),
    ]
    args = [x, attn, ssm_tm, u_pool, mods_l, jnp.asarray(1.0 / cnt, dtype=F32),
            _block_diag_pool(pool_w).astype(BF16), pool_scale.reshape(1, POOL_W), w_out_bf,
            g2.reshape(1, D_MODEL)]
    if route:
        wr, br = router
        tri = np.tril(np.ones((seq, seq), np.float32), -1)
        wr_pad = jnp.pad(wr, ((0, 0), (0, ROUTE_W - N_EXPERTS)))
        wr_hi = wr_pad.astype(BF16)
        wr_lo = (wr_pad - wr_hi.astype(F32)).astype(BF16)
        in_specs += [const((D_MODEL, 2 * ROUTE_W)), const((1, ROUTE_W)), const((seq, seq))]
        args += [jnp.concatenate([wr_hi, wr_lo], axis=1),
                 jnp.pad(br, (0, ROUTE_W - N_EXPERTS)).reshape(1, ROUTE_W),
                 jnp.asarray(tri, dtype=BF16)]
        out_shape = [jax.ShapeDtypeStruct((b, seq, D_MODEL), F32),
                     jax.ShapeDtypeStruct((b, seq, D_MODEL), BF16),
                     jax.ShapeDtypeStruct((b, seq, ROUTE_W), F32),
                     jax.ShapeDtypeStruct((b, 8, seq), F32),
                     jax.ShapeDtypeStruct((b, 8, ROUTE_W), F32)]
        out_specs = [row_spec(D_MODEL), row_spec(D_MODEL), row_spec(ROUTE_W),
                     pl.BlockSpec((nseq, 8, seq), lambda i: (i, 0, 0)),
                     pl.BlockSpec((nseq, 8, ROUTE_W), lambda i: (i, 0, 0))]
    else:
        out_shape = [jax.ShapeDtypeStruct((b, seq, D_MODEL), F32), jax.ShapeDtypeStruct((b, seq, D_MODEL), BF16)]
        out_specs = [row_spec(D_MODEL), row_spec(D_MODEL)]
    return pl.pallas_call(
        functools.partial(_mixout_kernel, route=route),
        out_shape=out_shape, grid=(b // nseq,), in_specs=in_specs, out_specs=out_specs,
        compiler_params=pltpu.CompilerParams(
            dimension_semantics=("arbitrary",), vmem_limit_bytes=VMEM_LIMIT),
        name="mixout_%s%s" % ("latent" if latent else "context", "_route" if route else ""),
    )(*args)


def _ffn_kernel(h_ref, x_ref, mod_ref, wg_ref, wu_ref, wd_ref, o_ref):
    h = h_ref[...]
    a = _silu(jnp.dot(h, wg_ref[...], preferred_element_type=F32)) * \
        jnp.dot(h, wu_ref[...], preferred_element_type=F32)
    f = jnp.dot(a.astype(BF16), wd_ref[...], preferred_element_type=F32)
    o_ref[...] = x_ref[...] + mod_ref[5:6] * f


def _ffn(h2, x1, mods_l, wg, wu, wd, latent, seq):
    rows = h2.shape[0]
    tm = 512
    ff = wg.shape[1]
    mod_row = (lambda i: (i * tm) // seq) if latent else (lambda i: CTX_ROW)
    resident = lambda shape: pl.BlockSpec(shape, lambda i: (0, 0), pipeline_mode=pl.Buffered(1))
    return pl.pallas_call(
        _ffn_kernel,
        out_shape=jax.ShapeDtypeStruct((rows, D_MODEL), F32),
        grid=(rows // tm,),
        in_specs=[pl.BlockSpec((tm, D_MODEL), lambda i: (i, 0)),
                  pl.BlockSpec((tm, D_MODEL), lambda i: (i, 0)),
                  pl.BlockSpec((None, 6, D_MODEL), lambda i: (mod_row(i), 0, 0)),
                  resident((D_MODEL, ff)), resident((D_MODEL, ff)), resident((ff, D_MODEL))],
        out_specs=pl.BlockSpec((tm, D_MODEL), lambda i: (i, 0)),
        compiler_params=pltpu.CompilerParams(
            dimension_semantics=("arbitrary",), vmem_limit_bytes=VMEM_LIMIT),
        name="ffn_dense",
    )(h2, x1, mods_l, wg, wu, wd)


def _route_tables(route_t, cnt, seq):
    b = route_t.shape[0]
    per_block = MOE_BLOCK // seq
    nb = b // per_block
    c = cnt[:, 0, :N_EXPERTS].astype(jnp.int32).reshape(nb, per_block, N_EXPERTS)
    before = jnp.cumsum(c, axis=1) - c
    total = c.sum(axis=1)
    aligned = (total + 7) // 8 * 8
    starts = jnp.cumsum(aligned, axis=1) - aligned
    base = (starts[:, None, :] + before).reshape(b, N_EXPERTS)
    expert = route_t[:, 2:4, :].astype(jnp.int32)
    rank = route_t[:, 4:6, :].astype(jnp.int32)
    slot = rank
    for e in range(N_EXPERTS):
        slot = slot + jnp.where(expert == e, base[:, e][:, None, None], 0)
    slots = slot.reshape(nb, per_block, 2, seq).transpose(0, 2, 1, 3).reshape(nb, 2, MOE_BLOCK)
    return starts.reshape(-1), total.reshape(-1), slots


MOE_SLOTS = 2 * MOE_BLOCK
MOE_ROWS = MOE_SLOTS + 8 * N_EXPERTS + MOE_TILE


MOE_GROUP = 16


def _moe_kernel(starts_ref, counts_ref, slots_ref, h_ref, route_ref, wg_ref, wu_ref, wt_ref, wd_ref, f_ref,
                xs_ref, stage_ref):
    b = pl.program_id(0)
    e = pl.program_id(1)
    half = D_MODEL // 2

    @pl.when(e == 0)
    def _():
        xs_ref[MOE_SLOTS:, :] = jnp.zeros((MOE_ROWS - MOE_SLOTS, half), jnp.uint32)
        zero_row = jnp.zeros((1, half), jnp.uint32)
        for g in range(N_EXPERTS):
            end = starts_ref[b * N_EXPERTS + g] + counts_ref[b * N_EXPERTS + g]
            for r in range(7):
                xs_ref[pl.ds(end + r, 1), :] = zero_row

        def put(g, c):
            t0 = pl.multiple_of(g * MOE_GROUP, MOE_GROUP)
            hb = h_ref[pl.ds(t0, MOE_GROUP), :].astype(F32)
            stage_ref[:MOE_GROUP, :] = _pack2(hb[:, :half], hb[:, half:])
            for r in range(MOE_GROUP):
                row = stage_ref[r:r + 1, :]
                xs_ref[pl.ds(slots_ref[0, t0 + r], 1), :] = row
                xs_ref[pl.ds(slots_ref[1, t0 + r], 1), :] = row
            return c
        lax.fori_loop(0, MOE_BLOCK // MOE_GROUP, put, 0)

    start = starts_ref[b * N_EXPERTS + e]
    count = counts_ref[b * N_EXPERTS + e]

    def tile(j, c):
        s = pl.multiple_of(start + j * MOE_TILE, 8)
        xg = xs_ref[pl.ds(s, MOE_TILE), :]
        x_lo32 = _unpack2(xg, 0)
        x_hi32 = _unpack2(xg, 1)
        x_lo = x_lo32.astype(BF16)
        x_hi = x_hi32.astype(BF16)
        main = FF_EXPERT - MOE_TAIL

        def proj(w_ref, cols):
            return (jnp.dot(x_lo, w_ref[:half, cols], preferred_element_type=F32)
                    + jnp.dot(x_hi, w_ref[half:, cols], preferred_element_type=F32))
        tail = proj(wt_ref, slice(None))
        rest = tail.shape[1] // 2
        a = jnp.concatenate([_silu(proj(wg_ref, slice(0, main))) * proj(wu_ref, slice(0, main)),
                             _silu(tail[:, :rest]) * tail[:, rest:]], axis=1).astype(BF16)
        y = jnp.dot(a, wd_ref[...], preferred_element_type=F32)
        valid = lax.broadcasted_iota(jnp.int32, (MOE_TILE, half), 0) < count - j * MOE_TILE
        xs_ref[pl.ds(s, MOE_TILE), :] = _pack2(jnp.where(valid, y[:, :half], x_lo32),
                                               jnp.where(valid, y[:, half:], x_hi32))
        return c
    lax.fori_loop(0, (count + MOE_TILE - 1) // MOE_TILE, tile, 0)

    @pl.when(e == pl.num_programs(1) - 1)
    def _():
        def take(g, c):
            t0 = pl.multiple_of(g * MOE_GROUP, MOE_GROUP)
            for r in range(MOE_GROUP):
                stage_ref[r:r + 1, :] = xs_ref[pl.ds(slots_ref[0, t0 + r], 1), :]
                stage_ref[MOE_GROUP + r:MOE_GROUP + r + 1, :] = xs_ref[pl.ds(slots_ref[1, t0 + r], 1), :]
            z0 = stage_ref[:MOE_GROUP, :]
            z1 = stage_ref[MOE_GROUP:, :]
            route = route_ref[pl.ds(t0, MOE_GROUP), :]
            w1 = route[:, 0:1]
            w2 = route[:, 1:2]
            f_ref[pl.ds(t0, MOE_GROUP), :half] = (w1 * _unpack2(z0, 0) + w2 * _unpack2(z1, 0)).astype(BF16)
            f_ref[pl.ds(t0, MOE_GROUP), half:] = (w1 * _unpack2(z0, 1) + w2 * _unpack2(z1, 1)).astype(BF16)
            return c
        lax.fori_loop(0, MOE_BLOCK // MOE_GROUP, take, 0)


def _moe_experts(h2, route, starts, counts, slots, wg, wu, wd):
    rows = h2.shape[0]
    nb = rows // MOE_BLOCK
    ff = wd.shape[1]
    wt = jnp.concatenate([wg[:, :, ff - MOE_TAIL:], wu[:, :, ff - MOE_TAIL:]], axis=2)
    grid_spec = pltpu.PrefetchScalarGridSpec(
        num_scalar_prefetch=2,
        grid=(nb, N_EXPERTS),
        in_specs=[
            pl.BlockSpec((None, 2, MOE_BLOCK), lambda b, e, st, ct: (b, 0, 0), memory_space=pltpu.SMEM),
            pl.BlockSpec((MOE_BLOCK, D_MODEL), lambda b, e, st, ct: (b, 0)),
            pl.BlockSpec((MOE_BLOCK, ROUTE_W), lambda b, e, st, ct: (b, 0)),
            pl.BlockSpec((None, D_MODEL, ff), lambda b, e, st, ct: (e, 0, 0)),
            pl.BlockSpec((None, D_MODEL, ff), lambda b, e, st, ct: (e, 0, 0)),
            pl.BlockSpec((None, D_MODEL, 2 * MOE_TAIL), lambda b, e, st, ct: (e, 0, 0)),
            pl.BlockSpec((None, ff, D_MODEL), lambda b, e, st, ct: (e, 0, 0)),
        ],
        out_specs=pl.BlockSpec((MOE_BLOCK, D_MODEL), lambda b, e, st, ct: (b, 0)),
        scratch_shapes=[pltpu.VMEM((MOE_ROWS, D_MODEL // 2), jnp.uint32),
                        pltpu.VMEM((2 * MOE_GROUP, D_MODEL // 2), jnp.uint32)],
    )
    return pl.pallas_call(
        _moe_kernel,
        out_shape=jax.ShapeDtypeStruct((rows, D_MODEL), BF16),
        grid_spec=grid_spec,
        compiler_params=pltpu.CompilerParams(
            dimension_semantics=("arbitrary", "arbitrary"), vmem_limit_bytes=VMEM_LIMIT),
        name="moe_experts",
    )(starts, counts, slots, h2, route, wg, wu, wt, wd)


def _final_kernel(x_ref, f_ref, mod_ref, fg_ref, o_ref):
    y = x_ref[...] + mod_ref[5:6] * f_ref[...].astype(F32)
    o_ref[...] = _rms(y, fg_ref[...])


def _final(x1, f, mods_l, final_g, latent, seq):
    rows = x1.shape[0]
    tr = 512
    mod_row = (lambda i: (i * tr) // seq) if latent else (lambda i: CTX_ROW)
    return pl.pallas_call(
        _final_kernel,
        out_shape=jax.ShapeDtypeStruct((rows, D_MODEL), F32),
        grid=(rows // tr,),
        in_specs=[pl.BlockSpec((tr, D_MODEL), lambda i: (i, 0)),
                  pl.BlockSpec((tr, D_MODEL), lambda i: (i, 0)),
                  pl.BlockSpec((None, 6, D_MODEL), lambda i: (mod_row(i), 0, 0)),
                  pl.BlockSpec((1, D_MODEL), lambda i: (0, 0))],
        out_specs=pl.BlockSpec((tr, D_MODEL), lambda i: (i, 0)),
        compiler_params=pltpu.CompilerParams(
            dimension_semantics=("arbitrary",), vmem_limit_bytes=VMEM_LIMIT),
        name="final_norm",
    )(x1, f, mods_l, final_g.reshape(1, D_MODEL))


def _dup_heads(a):
    b, n = a.shape[:2]
    return jnp.repeat(a, 2, axis=2).reshape(b, n, 2 * KV_W).astype(BF16)


def kernel(x_prompt, x_sample, c, cache_k, cache_v, state_ssm_re, state_ssm_im, c_ctx, mod_w, mod_b, norm1_g, norm2_g, w_in, w_out, q_norm_g, k_norm_g, ssm_a_re, ssm_a_im, ssm_log_dt, ssm_b_re, ssm_b_im, ssm_c_re, ssm_c_im, ssm_d, ssm_glu_w, pool_w, pool_scale, ffn_w_gate, ffn_w_up, ffn_w_down, moe_router_w, moe_router_b, moe_w_gate, moe_w_up, moe_w_down, final_g):
    bp, lp, _ = x_prompt.shape
    bs, ls, _ = x_sample.shape
    assert bs == SEQ_GROUP and bp % SEQ_GROUP == 0

    cond = jnp.zeros((MOD_ROWS, D_MODEL), F32).at[:bs].set(c).at[CTX_ROW].set(c_ctx)
    mods = _modulation(cond, mod_w, mod_b).reshape(DEPTH, MOD_ROWS, 6, D_MODEL)

    bw, cw, lam = _ssm_prep(ssm_a_re, ssm_a_im, ssm_log_dt, ssm_b_re, ssm_b_im, ssm_c_re, ssm_c_im)

    xp, xs = x_prompt, x_sample
    new_k, new_v, new_state = [], [], []
    for l in range(DEPTH):
        mods_l = mods[l]
        w_in_bf = w_in[l].astype(BF16)
        w_out_bf = w_out[l].astype(BF16)
        glu_bf = ssm_glu_w[l].astype(BF16)
        moe = l % 2 == 1
        i = l // 2
        router = (moe_router_w[i], moe_router_b[i]) if moe else None
        if moe:
            wg, wu, wd = (moe_w_gate[i].astype(BF16), moe_w_up[i].astype(BF16), moe_w_down[i].astype(BF16))
        else:
            wg, wu, wd = (ffn_w_gate[i].astype(BF16), ffn_w_up[i].astype(BF16), ffn_w_down[i].astype(BF16))
        for latent in (False, True):
            x = xs if latent else xp
            b, seq, _ = x.shape
            nbg = b // SEQ_GROUP
            rows = b * seq
            outs = _inproj(x.reshape(rows, D_MODEL), mods_l, norm1_g[l], w_in_bf, q_norm_g[l], k_norm_g[l],
                           latent, seq)
            q, kd, vd = (a.reshape(b, seq, a.shape[-1]) for a in outs[:3])
            u_ssm = outs[3]
            u_pool = outs[4].reshape(b, seq, POOL_W)
            if latent:
                kds = [kd, _dup_heads(cache_k[:, l])]
                vds = [vd, _dup_heads(cache_v[:, l])]
                h0 = jnp.concatenate([
                    state_ssm_re[:, l, 0].reshape(b, SSM_GP), state_ssm_im[:, l, 0].reshape(b, SSM_GP),
                    state_ssm_re[:, l, 1].reshape(b, SSM_GP), state_ssm_im[:, l, 1].reshape(b, SSM_GP)],
                    axis=1).reshape(nbg, SEQ_GROUP, 4 * SSM_GP)
            else:
                kds, vds = [kd], [vd]
                new_k.append(outs[5])
                new_v.append(outs[6])
                h0 = jnp.zeros((nbg, SEQ_GROUP, 4 * SSM_GP), F32)
            attn = _attention(q, kds, vds)
            y_ssm, hf = _ssm(u_ssm, l, bw, cw, lam, h0, ssm_d[l], glu_bf, seq)
            if not latent:
                new_state.append(hf.reshape(b, 2, 2, SSM_GROUPS, SSM_STATE))
            res = _mixout(x, attn, y_ssm.reshape(b, seq, SSM_W), u_pool, mods_l,
                          pool_w[l], pool_scale[l], w_out_bf, norm2_g[l], latent, router)
            x1 = res[0].reshape(rows, D_MODEL)
            if moe:
                route = res[2].reshape(rows, ROUTE_W)
                starts, counts, slots = _route_tables(res[3], res[4], seq)
                f = _moe_experts(res[1].reshape(rows, D_MODEL), route, starts, counts, slots, wg, wu, wd)
                y = _final(x1, f, mods_l, final_g, latent, seq)
            else:
                y = _ffn(res[1].reshape(rows, D_MODEL), x1, mods_l, wg, wu, wd, latent, seq)
            y = y.reshape(b, seq, D_MODEL)
            if latent:
                xs = y
            else:
                xp = y

    def cache_out(parts):
        a = jnp.stack(parts, axis=1).reshape(bp, DEPTH, N_KV_HEADS, HEAD_DIM, lp)
        return a.transpose(0, 1, 4, 2, 3)
    new_cache_k = cache_out(new_k)
    new_cache_v = cache_out(new_v)
    st = jnp.stack(new_state, axis=1)
    return (xp, xs, new_cache_k, new_cache_v, st[:, :, :, 0], st[:, :, :, 1])
```

```python
import functools

import numpy as np
import jax
import jax.numpy as jnp
from jax import lax
from jax.experimental import pallas as pl
from jax.experimental.pallas import tpu as pltpu

D_MODEL = 1024
DEPTH = 2
GRID_W = 64
ATTN_W = 512
HEAD_DIM = 64
N_HEADS = 8
N_KV_HEADS = 2
KV_W = 128
ROPE_THETA = 10000.0
SSM_W = 256
SSM_CH = 16
SSM_GROUPS = 16
SSM_STATE = 64
SSM_GP = SSM_GROUPS * SSM_STATE
POOL_W = 256
POOL_WINDOWS = (2, 4, 8, 16)
POOL_CH = 64
POOL_PAD = max(POOL_WINDOWS) // 2
IN_COLS = ATTN_W + 2 * KV_W + SSM_W + POOL_W
FF_DENSE = 2816
N_EXPERTS = 8
FF_EXPERT = 1408
EPS = 1e-6
LOG2_E = 1.4426950408889634

SEQ_GROUP = 8
SSM_CHUNK = 16
SSM_COLBLK = 512
ROUTE_W = 128
MIX_ROWS = 1024
MOE_BLOCK = 2048
MOE_TILE = 128
MXU_TILE = 256
MOE_TAIL = FF_EXPERT % MXU_TILE
MOD_ROWS = 16
CTX_ROW = 8
VMEM_LIMIT = 56 * 1024 * 1024

F32 = jnp.float32
BF16 = jnp.bfloat16


def _silu(x):
    return x * jax.nn.sigmoid(x)


def _rms(x32, g):
    return x32 * lax.rsqrt(jnp.mean(x32 * x32, axis=-1, keepdims=True) + EPS) * g


def _mod_kernel(cond_ref, w_ref, b_ref, o_ref):
    s = _silu(cond_ref[...])
    w = w_ref[...]
    s_hi = s.astype(BF16)
    s_lo = (s - s_hi.astype(F32)).astype(BF16)
    w_hi = w.astype(BF16)
    w_lo = (w - w_hi.astype(F32)).astype(BF16)
    o_ref[...] = (jnp.dot(s_hi, w_hi, preferred_element_type=F32)
                  + jnp.dot(s_hi, w_lo, preferred_element_type=F32)
                  + jnp.dot(s_lo, w_hi, preferred_element_type=F32) + b_ref[...])


def _modulation(cond, mod_w, mod_b):
    tn = 1536
    n = 6 * D_MODEL
    return pl.pallas_call(
        _mod_kernel,
        out_shape=jax.ShapeDtypeStruct((DEPTH, MOD_ROWS, n), F32),
        grid=(DEPTH, n // tn),
        in_specs=[
            pl.BlockSpec((MOD_ROWS, D_MODEL), lambda l, j: (0, 0)),
            pl.BlockSpec((None, D_MODEL, tn), lambda l, j: (l, 0, j)),
            pl.BlockSpec((None, 1, tn), lambda l, j: (l, 0, j)),
        ],
        out_specs=pl.BlockSpec((None, MOD_ROWS, tn), lambda l, j: (l, 0, j)),
        compiler_params=pltpu.CompilerParams(
            dimension_semantics=("arbitrary", "arbitrary"), vmem_limit_bytes=VMEM_LIMIT),
        name="modulation",
    )(cond, mod_w, mod_b.reshape(DEPTH, 1, n))


def _ssm_prep_kernel(are_ref, aim_ref, dt_ref, bre_ref, bim_ref, cre_ref, cim_ref, bw_ref, cw_ref, lam_ref):
    row_group = lax.broadcasted_iota(jnp.int32, (SSM_W, SSM_GP), 0) // SSM_CH
    lane_group = lax.broadcasted_iota(jnp.int32, (SSM_W, SSM_GP), 1) // SSM_STATE
    own = row_group == lane_group
    for ld in range(2 * DEPTH):
        l, d = divmod(ld, 2)
        a_re = are_ref[ld:ld + 1, :]
        a_im = aim_ref[ld:ld + 1, :]
        dt = jnp.exp(dt_ref[ld:ld + 1, :])
        mag = jnp.exp(a_re * dt)
        l_re = mag * jnp.cos(a_im * dt)
        l_im = mag * jnp.sin(a_im * dt)
        lam_ref[l, 2 * d] = jnp.broadcast_to(l_re, (SEQ_GROUP, SSM_GP))
        lam_ref[l, 2 * d + 1] = jnp.broadcast_to(l_im, (SEQ_GROUP, SSM_GP))
        x = l_re - 1.0
        den = a_re * a_re + a_im * a_im
        f_re = (x * a_re + l_im * a_im) / den
        f_im = (l_im * a_re - x * a_im) / den
        b_re = bre_ref[ld]
        b_im = bim_ref[ld]
        bbar = (f_re * b_re - f_im * b_im, f_re * b_im + f_im * b_re)
        cmat = (cre_ref[ld], -cim_ref[ld])
        for ri in range(2):
            c0 = (2 * d + ri) * SSM_GP
            tiled = jnp.concatenate([bbar[ri]] * SSM_GROUPS, axis=0)
            bw_ref[l, :, c0:c0 + SSM_GP] = jnp.where(own, tiled, 0.0).astype(BF16)
            cw_ref[l, c0:c0 + SSM_GP, :] = jnp.where(own, cmat[ri], 0.0).T.astype(BF16)


def _ssm_prep(a_re, a_im, log_dt, b_re, b_im, c_re, c_im):
    ld = DEPTH * 2
    are = a_re.reshape(ld, SSM_GP)
    aim = a_im.reshape(ld, SSM_GP)
    dt = jnp.repeat(log_dt.reshape(ld, SSM_GROUPS), SSM_STATE, axis=1)
    to_rows = lambda b: b.reshape(ld, SSM_GROUPS, SSM_STATE, SSM_CH).transpose(0, 3, 1, 2).reshape(ld, SSM_CH, SSM_GP)
    repeat_lanes = lambda c: jnp.tile(c.reshape(ld, SSM_W, SSM_STATE), (1, 1, SSM_GROUPS))
    return pl.pallas_call(
        _ssm_prep_kernel,
        out_shape=(jax.ShapeDtypeStruct((DEPTH, SSM_W, 4 * SSM_GP), BF16),
                   jax.ShapeDtypeStruct((DEPTH, 4 * SSM_GP, SSM_W), BF16),
                   jax.ShapeDtypeStruct((DEPTH, 4, SEQ_GROUP, SSM_GP), F32)),
        compiler_params=pltpu.CompilerParams(vmem_limit_bytes=VMEM_LIMIT),
        name="ssm_prep",
    )(are, aim, dt, to_rows(b_re), to_rows(b_im), repeat_lanes(c_re), repeat_lanes(c_im))


def _inproj_kernel(*refs, latent):
    if latent:
        (x_ref, mod_ref, g_ref, w_ref, qg_ref, kg_ref, sq_ref, sk_ref, cos_ref, sa_ref, sb_ref,
         q_ref, kd_ref, vd_ref, us_ref, up_ref) = refs
    else:
        (x_ref, mod_ref, g_ref, w_ref, qg_ref, kg_ref, sq_ref, sk_ref,
         q_ref, kd_ref, vd_ref, us_ref, up_ref, ko_ref, vo_ref) = refs
    x = x_ref[...]
    mod = mod_ref[...]
    h = _rms(x, g_ref[...]) * (1.0 + mod[1:2]) + mod[0:1]
    p = jnp.dot(h.astype(BF16), w_ref[...], preferred_element_type=F32)
    q = p[:, :ATTN_W]
    k = p[:, ATTN_W:ATTN_W + KV_W]
    v = p[:, ATTN_W + KV_W:ATTN_W + 2 * KV_W]
    q_ms = jnp.dot((q * q).astype(BF16), sq_ref[...], preferred_element_type=F32)
    k_ms = jnp.dot((k * k).astype(BF16), sk_ref[...], preferred_element_type=F32)
    q = q * lax.rsqrt(q_ms + EPS) * qg_ref[...]
    k = k * lax.rsqrt(k_ms + EPS) * kg_ref[...]
    if latent:
        cos = cos_ref[...]
        sa = sa_ref[...]
        sb = sb_ref[...]
        cos4 = jnp.concatenate([cos] * 4, axis=1)
        sa4 = jnp.concatenate([sa] * 4, axis=1)
        sb4 = jnp.concatenate([sb] * 4, axis=1)
        q = (q * cos4 + pltpu.roll(q, ATTN_W - 16, axis=1) * sa4 + pltpu.roll(q, 16, axis=1) * sb4)
        k = (k * cos + pltpu.roll(k, KV_W - 16, axis=1) * sa + pltpu.roll(k, 16, axis=1) * sb)
    else:
        seq = ko_ref.shape[-1]
        for s in range(ko_ref.shape[0]):
            ko_ref[s] = k[s * seq:(s + 1) * seq, :].T
            vo_ref[s] = v[s * seq:(s + 1) * seq, :].T
    q_ref[...] = (q * (HEAD_DIM ** -0.5 * LOG2_E)).astype(BF16)
    lane = lax.broadcasted_iota(jnp.int32, k.shape, 1)
    first = lane < HEAD_DIM
    k_sw = pltpu.roll(k, HEAD_DIM, axis=1)
    v_sw = pltpu.roll(v, HEAD_DIM, axis=1)
    kd_ref[:, :KV_W] = jnp.where(first, k, k_sw).astype(BF16)
    kd_ref[:, KV_W:] = jnp.where(first, k_sw, k).astype(BF16)
    vd_ref[:, :KV_W] = jnp.where(first, v, v_sw).astype(BF16)
    vd_ref[:, KV_W:] = jnp.where(first, v_sw, v).astype(BF16)
    o3 = ATTN_W + 2 * KV_W
    us_ref[0] = p[:, o3:o3 + SSM_W // 2]
    us_ref[1] = p[:, o3 + SSM_W // 2:o3 + SSM_W]
    up_ref[...] = p[:, o3 + SSM_W:]


def _head_mean_matrix(width):
    i = np.arange(width) // HEAD_DIM
    return jnp.asarray((i[:, None] == i[None, :]).astype(np.float32) / HEAD_DIM, dtype=BF16)


def _rope_tables(seq):
    t = np.arange(seq)
    row = (t // GRID_W).astype(np.float64)
    col = (t % GRID_W).astype(np.float64)
    nf = HEAD_DIM // 4
    inv = ROPE_THETA ** (-np.arange(nf, dtype=np.float64) / nf)
    ang_r = row[:, None] * inv[None, :]
    ang_c = col[:, None] * inv[None, :]
    cos = np.concatenate([np.cos(ang_r), np.cos(ang_r), np.cos(ang_c), np.cos(ang_c)], axis=1)
    sin = np.concatenate([np.sin(ang_r), np.sin(ang_r), np.sin(ang_c), np.sin(ang_c)], axis=1)
    lower = np.tile(np.concatenate([np.ones(nf), np.zeros(nf)]), 2)[None, :]
    sa = -sin * lower
    sb = sin * (1.0 - lower)
    tile = lambda a: jnp.asarray(np.tile(a, (1, 2)), dtype=F32)
    return tile(cos), tile(sa), tile(sb)


def _inproj(x, mods_l, g1, w_in_bf, qg, kg, latent, seq):
    rows = x.shape[0]
    tr = 1024
    tiles_per_seq = max(seq // tr, 1)
    mod_row = (lambda i: (i * tr) // seq) if latent else (lambda i: CTX_ROW)
    const = lambda shape: pl.BlockSpec(shape, lambda i: (0,) * len(shape))
    row_spec = lambda width: pl.BlockSpec((tr, width), lambda i: (i, 0))
    in_specs = [
        row_spec(D_MODEL),
        pl.BlockSpec((None, 6, D_MODEL), lambda i: (mod_row(i), 0, 0)),
        const((1, D_MODEL)),
        const((D_MODEL, IN_COLS)),
        const((1, ATTN_W)),
        const((1, KV_W)),
        const((ATTN_W, ATTN_W)),
        const((KV_W, KV_W)),
    ]
    args = [x, mods_l, g1.reshape(1, D_MODEL), w_in_bf,
            jnp.tile(qg, N_HEADS).reshape(1, ATTN_W), jnp.tile(kg, N_KV_HEADS).reshape(1, KV_W),
            _head_mean_matrix(ATTN_W), _head_mean_matrix(KV_W)]
    widths = [(ATTN_W, BF16), (2 * KV_W, BF16), (2 * KV_W, BF16), (None, F32), (POOL_W, F32)]
    if latent:
        assert seq % tr == 0
        cos, sa, sb = _rope_tables(seq)
        in_specs += [pl.BlockSpec((tr, KV_W), lambda i: (i % tiles_per_seq, 0))] * 3
        args += [cos, sa, sb]
    out_shape = [jax.ShapeDtypeStruct((rows, w) if w else (2, rows, SSM_W // 2), dt) for w, dt in widths]
    out_specs = [row_spec(w) if w else pl.BlockSpec((2, tr, SSM_W // 2), lambda i: (0, i, 0)) for w, _ in widths]
    if not latent:
        assert tr % seq == 0
        out_shape += [jax.ShapeDtypeStruct((rows // seq, KV_W, seq), F32)] * 2
        out_specs += [pl.BlockSpec((tr // seq, KV_W, seq), lambda i: (i, 0, 0))] * 2
    return pl.pallas_call(
        functools.partial(_inproj_kernel, latent=latent),
        out_shape=out_shape, grid=(rows // tr,), in_specs=in_specs, out_specs=out_specs,
        compiler_params=pltpu.CompilerParams(
            dimension_semantics=("arbitrary",), vmem_limit_bytes=VMEM_LIMIT),
        name="inproj_latent" if latent else "inproj_context",
    )(*args)


def _attn_kernel(*refs, n_kv_src):
    q_ref = refs[0]
    k_refs = refs[1:1 + n_kv_src]
    v_refs = refs[1 + n_kv_src:1 + 2 * n_kv_src]
    o_ref = refs[1 + 2 * n_kv_src]
    tq = q_ref.shape[0]
    lane = lax.broadcasted_iota(jnp.int32, (tq, KV_W), 1)
    first = lane < HEAD_DIM
    for pair in range(N_HEADS // 2):
        kv = pair // 2
        cols = slice(pair * KV_W, (pair + 1) * KV_W)
        kcols = slice(kv * KV_W, (kv + 1) * KV_W)
        qp = q_ref[:, cols]
        halves = []
        for half in range(2):
            qm = jnp.where(first if half == 0 else jnp.logical_not(first), qp, jnp.zeros_like(qp))
            scores = [lax.dot_general(qm, k_ref[:, kcols], (((1,), (1,)), ((), ())),
                                      preferred_element_type=F32) for k_ref in k_refs]
            m = scores[0].max(axis=-1, keepdims=True)
            for s in scores[1:]:
                m = jnp.maximum(m, s.max(axis=-1, keepdims=True))
            den = jnp.zeros((tq, 1), F32)
            acc = jnp.zeros((tq, KV_W), F32)
            for s, v_ref in zip(scores, v_refs):
                p = jnp.exp2(s - m)
                den = den + p.sum(axis=-1, keepdims=True)
                acc = acc + jnp.dot(p.astype(BF16), v_ref[:, kcols], preferred_element_type=F32)
            halves.append(acc / den)
        o_ref[:, cols] = jnp.where(first, halves[0], halves[1]).astype(o_ref.dtype)


def _attention(q, kds, vds):
    b, seq, _ = q.shape
    tq = min(seq, 512)
    n_src = len(kds)
    kv_spec = lambda a: pl.BlockSpec((None, a.shape[1], 2 * KV_W), lambda i, t: (i, 0, 0))
    return pl.pallas_call(
        functools.partial(_attn_kernel, n_kv_src=n_src),
        out_shape=jax.ShapeDtypeStruct((b, seq, ATTN_W), BF16),
        grid=(b, seq // tq),
        in_specs=[pl.BlockSpec((None, tq, ATTN_W), lambda i, t: (i, t, 0))]
                 + [kv_spec(a) for a in kds] + [kv_spec(a) for a in vds],
        out_specs=pl.BlockSpec((None, tq, ATTN_W), lambda i, t: (i, t, 0)),
        compiler_params=pltpu.CompilerParams(
            dimension_semantics=("arbitrary", "arbitrary"), vmem_limit_bytes=VMEM_LIMIT),
        name="attention_%dsrc" % n_src,
    )(q, *kds, *vds)


def _ssm_kernel(u_ref, bw_ref, cw_ref, lam_ref, h0_ref, d_ref, glu_ref,
                o_ref, hf_ref, y_ref, utm_scr, bu0_scr, bu1_scr, hs0_scr, hs1_scr, st_scr):
    rows = u_ref.shape[1]
    seq = rows // SEQ_GROUP
    crow = SSM_CHUNK * SEQ_GROUP
    nchunks = rows // crow
    w2 = 2 * SSM_GP
    lanes = SSM_W // 2

    def time_major(c):
        t0 = c * SSM_CHUNK
        return jnp.concatenate(
            [jnp.concatenate([u_ref[h, pl.ds(t0 + s, SEQ_GROUP, stride=seq), :] for h in range(2)], axis=1)
             for s in range(SSM_CHUNK)], axis=0)

    def y_rows(r, n):
        return jnp.concatenate([y_ref[0, pl.ds(r, n), :], y_ref[1, pl.ds(r, n), :]], axis=1)

    def set_y_rows(r, n, val):
        y_ref[0, pl.ds(r, n), :] = val[:, :lanes]
        y_ref[1, pl.ds(r, n), :] = val[:, lanes:]

    st_scr[...] = h0_ref[...]

    def init_rows(i, c):
        r = pl.multiple_of(i * crow, crow)
        tm = time_major(i)
        set_y_rows(r, crow, tm * d_ref[...])
        utm_scr[pl.ds(r, crow), :] = tm.astype(BF16)
        return c
    lax.fori_loop(0, nchunks, init_rows, 0)

    last = nchunks - 1
    bu_scr = (bu0_scr, bu1_scr)
    hs_scr = (hs0_scr, hs1_scr)

    def project_in(i, buf):
        i = jnp.clip(i, 0, last)
        for d, c in ((0, i), (1, last - i)):
            r = pl.multiple_of(c * crow, crow)
            bu_scr[buf][d] = jnp.dot(utm_scr[pl.ds(r, crow), :], bw_ref[:, d * w2:(d + 1) * w2],
                                     preferred_element_type=F32)

    def recur(buf):
        for d in range(2):
            for cb in range(SSM_GP // SSM_COLBLK):
                re_c = slice(cb * SSM_COLBLK, (cb + 1) * SSM_COLBLK)
                im_c = slice(SSM_GP + cb * SSM_COLBLK, SSM_GP + (cb + 1) * SSM_COLBLK)
                l_re = lam_ref[2 * d, :, re_c]
                l_im = lam_ref[2 * d + 1, :, re_c]
                h_re = st_scr[:, d * w2 + cb * SSM_COLBLK:d * w2 + (cb + 1) * SSM_COLBLK]
                h_im = st_scr[:, d * w2 + SSM_GP + cb * SSM_COLBLK:d * w2 + SSM_GP + (cb + 1) * SSM_COLBLK]
                for s in range(SSM_CHUNK):
                    t = s if d == 0 else SSM_CHUNK - 1 - s
                    r = slice(t * SEQ_GROUP, (t + 1) * SEQ_GROUP)
                    n_re = l_re * h_re - l_im * h_im + bu_scr[buf][d, r, re_c]
                    n_im = l_re * h_im + l_im * h_re + bu_scr[buf][d, r, im_c]
                    hs_scr[buf][d, r, re_c] = n_re
                    hs_scr[buf][d, r, im_c] = n_im
                    h_re, h_im = n_re, n_im
                st_scr[:, d * w2 + cb * SSM_COLBLK:d * w2 + (cb + 1) * SSM_COLBLK] = h_re
                st_scr[:, d * w2 + SSM_GP + cb * SSM_COLBLK:d * w2 + SSM_GP + (cb + 1) * SSM_COLBLK] = h_im

    def project_out(i, buf):
        i = jnp.clip(i, 0, last)
        for d, c in ((0, i), (1, last - i)):
            r = pl.multiple_of(c * crow, crow)
            contrib = jnp.dot(hs_scr[buf][d].astype(BF16), cw_ref[d * w2:(d + 1) * w2, :],
                              preferred_element_type=F32)
            set_y_rows(r, crow, y_rows(r, crow) + contrib)

    hs1_scr[...] = jnp.zeros(hs1_scr.shape, F32)
    project_in(0, 0)

    def two_steps(j, c):
        i = 2 * j
        recur(0)
        project_in(i + 1, 1)
        project_out(i - 1, 1)
        recur(1)
        project_in(i + 2, 0)
        project_out(i, 0)
        return c
    lax.fori_loop(0, nchunks // 2, two_steps, 0)
    project_out(last, 1)

    hf_ref[...] = st_scr[...]

    esteps = 128
    erow = esteps * SEQ_GROUP

    def epilogue(i, c):
        r = pl.multiple_of(i * erow, erow)
        z = jax.nn.gelu(y_rows(r, erow))
        g = jnp.dot(z.astype(BF16), glu_ref[...], preferred_element_type=F32)
        set_y_rows(r, erow, g[:, :SSM_W] * jax.nn.sigmoid(g[:, SSM_W:]))
        t0 = pl.multiple_of(i * esteps, esteps)
        for b in range(SEQ_GROUP):
            for h in range(2):
                o_ref[pl.ds(b * seq + t0, esteps), h * lanes:(h + 1) * lanes] = (
                    y_ref[h, pl.ds(r + b, esteps, stride=SEQ_GROUP), :].astype(BF16))
        return c
    lax.fori_loop(0, rows // erow, epilogue, 0)


def _ssm(u, layer, bw, cw, lam, h0, ssm_d, glu_bf, seq):
    rows = u.shape[1]
    grows = SEQ_GROUP * seq
    nbg = rows // grows
    crow = SSM_CHUNK * SEQ_GROUP
    const = lambda shape: pl.BlockSpec(shape, lambda i: (0,) * len(shape))
    return pl.pallas_call(
        _ssm_kernel,
        out_shape=(jax.ShapeDtypeStruct((rows, SSM_W), BF16),
                   jax.ShapeDtypeStruct((nbg, SEQ_GROUP, 4 * SSM_GP), F32)),
        grid=(nbg,),
        in_specs=[
            pl.BlockSpec((2, grows, SSM_W // 2), lambda i: (0, i, 0)),
            pl.BlockSpec((None, SSM_W, 4 * SSM_GP), lambda i: (layer, 0, 0)),
            pl.BlockSpec((None, 4 * SSM_GP, SSM_W), lambda i: (layer, 0, 0)),
            pl.BlockSpec((None, 4, SEQ_GROUP, SSM_GP), lambda i: (layer, 0, 0, 0)),
            pl.BlockSpec((None, SEQ_GROUP, 4 * SSM_GP), lambda i: (i, 0, 0)),
            const((1, SSM_W)),
            const((SSM_W, 2 * SSM_W)),
        ],
        out_specs=(pl.BlockSpec((grows, SSM_W), lambda i: (i, 0)),
                   pl.BlockSpec((None, SEQ_GROUP, 4 * SSM_GP), lambda i: (i, 0, 0))),
        scratch_shapes=[pltpu.VMEM((2, grows, SSM_W // 2), F32),
                        pltpu.VMEM((grows, SSM_W), BF16),
                        pltpu.VMEM((2, crow, 2 * SSM_GP), F32),
                        pltpu.VMEM((2, crow, 2 * SSM_GP), F32),
                        pltpu.VMEM((2, crow, 2 * SSM_GP), F32),
                        pltpu.VMEM((2, crow, 2 * SSM_GP), F32),
                        pltpu.VMEM((SEQ_GROUP, 4 * SSM_GP), F32)],
        compiler_params=pltpu.CompilerParams(
            dimension_semantics=("arbitrary",), vmem_limit_bytes=VMEM_LIMIT),
        name="ssm_scan",
    )(u, bw, cw, lam, h0, ssm_d.reshape(1, SSM_W), glu_bf)


def _pack2(lo, hi):
    return pltpu.pack_elementwise([lo, hi], packed_dtype=BF16)


def _unpack2(w, index):
    return pltpu.unpack_elementwise(w, index=index, packed_dtype=BF16, unpacked_dtype=F32)


def _mixout_kernel(*refs, route):
    if route:
        (x_ref, at_ref, ss_ref, up_ref, mod_ref, invc_ref, pw_ref, ps_ref, wo_ref, g2_ref, wr_ref, br_ref, tri_ref,
         x1_ref, h2_ref, route_ref, rt_ref, cnt_ref) = refs
    else:
        (x_ref, at_ref, ss_ref, up_ref, mod_ref, invc_ref, pw_ref, ps_ref, wo_ref, g2_ref,
         x1_ref, h2_ref) = refs
    nseq, seq, _ = up_ref.shape
    rows = nseq * seq
    zpad = jnp.zeros((POOL_PAD, POOL_W), F32)
    n_ext = seq + 2 * POOL_PAD
    back = lambda a, k: pltpu.roll(a, k, axis=0)
    ahead = lambda a, k: pltpu.roll(a, n_ext - k, axis=0)
    grp = lax.broadcasted_iota(jnp.int32, (n_ext, POOL_W), 1) // POOL_CH
    pooled = []
    for s in range(nseq):
        u = up_ref[s]
        ue = jnp.concatenate([zpad, u, zpad], axis=0)
        w2 = ue + back(ue, 1)
        w4 = back(w2, 1) + ahead(w2, 1)
        w8 = back(w4, 2) + ahead(w4, 2)
        w16 = back(w8, 4) + ahead(w8, 4)
        win = jnp.where(grp == 0, w2, jnp.where(grp == 1, w4, jnp.where(grp == 2, w8, w16)))
        pooled.append(win[POOL_PAD:POOL_PAD + seq] * invc_ref[...] - u)
    pooled = jnp.concatenate(pooled, axis=0)
    pool = jnp.dot(pooled.astype(BF16), pw_ref[...], preferred_element_type=F32) * ps_ref[...]
    mix = jnp.concatenate([at_ref[...].reshape(rows, ATTN_W), ss_ref[...].reshape(rows, SSM_W),
                           pool.astype(BF16)], axis=1)
    o = jnp.dot(mix, wo_ref[...], preferred_element_type=F32)
    mod = mod_ref[...]
    x1 = x_ref[...].reshape(rows, D_MODEL) + mod[2:3] * o
    x1_ref[...] = x1.reshape(nseq, seq, D_MODEL)
    h2 = _rms(x1, g2_ref[...]) * (1.0 + mod[4:5]) + mod[3:4]
    h2_ref[...] = h2.astype(BF16).reshape(nseq, seq, D_MODEL)
    if not route:
        return
    h_hi = h2.astype(BF16)
    h_lo = (h2 - h_hi.astype(F32)).astype(BF16)
    both = jnp.dot(h_hi, wr_ref[...], preferred_element_type=F32)
    logits = (both[:, :ROUTE_W] + both[:, ROUTE_W:]
              + jnp.dot(h_lo, wr_ref[:, :ROUTE_W], preferred_element_type=F32) + br_ref[...])
    lane = lax.broadcasted_iota(jnp.int32, logits.shape, 1).astype(F32)
    neg = float(np.finfo(np.float32).min)
    far = float(ROUTE_W)
    logits = jnp.where(lane < N_EXPERTS, logits, neg)
    m1 = logits.max(axis=-1, keepdims=True)
    i1 = jnp.where(logits == m1, lane, far).min(axis=-1, keepdims=True)
    rest = jnp.where(lane == i1, neg, logits)
    m2 = rest.max(axis=-1, keepdims=True)
    i2 = jnp.where(rest == m2, lane, far).min(axis=-1, keepdims=True)
    e2 = jnp.exp(m2 - m1)
    den = 1.0 + e2
    sel = jnp.where(lane == i1, 1.0, jnp.where(lane == i2, 1.0, 0.0))
    sel_bf = sel.astype(BF16)
    rank = jnp.concatenate([jnp.dot(tri_ref[...], sel_bf[s * seq:(s + 1) * seq], preferred_element_type=F32)
                            for s in range(nseq)], axis=0)
    r1 = jnp.where(lane == i1, rank, 0.0).sum(axis=-1, keepdims=True)
    r2 = jnp.where(lane == i2, rank, 0.0).sum(axis=-1, keepdims=True)
    out = jnp.zeros_like(logits)
    for j, col in enumerate((1.0 / den, e2 / den, i1, i2, r1, r2)):
        out = jnp.where(lane == j, col, out)
    route_ref[...] = out.reshape(nseq, seq, ROUTE_W)
    for s in range(nseq):
        part = slice(s * seq, (s + 1) * seq)
        rt_ref[s] = out[part].T[:8, :]
        cnt_ref[s] = jnp.broadcast_to(sel[part].sum(axis=0, keepdims=True), cnt_ref.shape[1:])


def _pool_inv_count(seq):
    t = np.arange(seq)
    cols = []
    for win in POOL_WINDOWS:
        lo = np.clip(t - win // 2, 0, seq)
        hi = np.clip(t + win // 2, 0, seq)
        cols.append(np.repeat((hi - lo).astype(np.float32)[:, None], POOL_CH, axis=1))
    return np.concatenate(cols, axis=1)


def _block_diag_pool(pool_w):
    eye = jnp.eye(len(POOL_WINDOWS), dtype=F32)
    m = eye[:, None, :, None] * pool_w[:, :, None, :]
    return m.reshape(POOL_W, POOL_W)


def _mixout(x, attn, ssm_tm, u_pool, mods_l, pool_w, pool_scale, w_out_bf, g2, latent, router):
    b, seq, _ = x.shape
    route = router is not None
    nseq = 1 if latent else max(1, MIX_ROWS // seq)
    mod_row = (lambda i: i) if latent else (lambda i: CTX_ROW)
    const = lambda shape: pl.BlockSpec(shape, lambda i: (0,) * len(shape))
    cnt = _pool_inv_count(seq)
    row_spec = lambda width: pl.BlockSpec((nseq, seq, width), lambda i: (i, 0, 0))
    in_specs = [
        row_spec(D_MODEL), row_spec(ATTN_W), row_spec(SSM_W), row_spec(POOL_W),
        pl.BlockSpec((None, 6, D_MODEL), lambda i: (mod_row(i), 0, 0)),
        const((seq, POOL_W)),
        const((POOL_W, POOL_W)),
        const((1, POOL_W)),
        const((D_MODEL, D_MODEL)),
        const((1, D_MODEL)),
    ]
    args = [x, attn, ssm_tm, u_pool, mods_l, jnp.asarray(1.0 / cnt, dtype=F32),
            _block_diag_pool(pool_w).astype(BF16), pool_scale.reshape(1, POOL_W), w_out_bf,
            g2.reshape(1, D_MODEL)]
    if route:
        wr, br = router
        tri = np.tril(np.ones((seq, seq), np.float32), -1)
        wr_pad = jnp.pad(wr, ((0, 0), (0, ROUTE_W - N_EXPERTS)))
        wr_hi = wr_pad.astype(BF16)
        wr_lo = (wr_pad - wr_hi.astype(F32)).astype(BF16)
        in_specs += [const((D_MODEL, 2 * ROUTE_W)), const((1, ROUTE_W)), const((seq, seq))]
        args += [jnp.concatenate([wr_hi, wr_lo], axis=1),
                 jnp.pad(br, (0, ROUTE_W - N_EXPERTS)).reshape(1, ROUTE_W),
                 jnp.asarray(tri, dtype=BF16)]
        out_shape = [jax.ShapeDtypeStruct((b, seq, D_MODEL), F32),
                     jax.ShapeDtypeStruct((b, seq, D_MODEL), BF16),
                     jax.ShapeDtypeStruct((b, seq, ROUTE_W), F32),
                     jax.ShapeDtypeStruct((b, 8, seq), F32),
                     jax.ShapeDtypeStruct((b, 8, ROUTE_W), F32)]
        out_specs = [row_spec(D_MODEL), row_spec(D_MODEL), row_spec(ROUTE_W),
                     pl.BlockSpec((nseq, 8, seq), lambda i: (i, 0, 0)),
                     pl.BlockSpec((nseq, 8, ROUTE_W), lambda i: (i, 0, 0))]
    else:
        out_shape = [jax.ShapeDtypeStruct((b, seq, D_MODEL), F32), jax.ShapeDtypeStruct((b, seq, D_MODEL), BF16)]
        out_specs = [row_spec(D_MODEL), row_spec(D_MODEL)]
    return pl.pallas_call(
        functools.partial(_mixout_kernel, route=route),
        out_shape=out_shape, grid=(b // nseq,), in_specs=in_specs, out_specs=out_specs,
        compiler_params=pltpu.CompilerParams(
            dimension_semantics=("arbitrary",), vmem_limit_bytes=VMEM_LIMIT),
        name="mixout_%s%s" % ("latent" if latent else "context", "_route" if route else ""),
    )(*args)


def _ffn_kernel(h_ref, x_ref, mod_ref, wg_ref, wu_ref, wd_ref, o_ref):
    h = h_ref[...]
    a = _silu(jnp.dot(h, wg_ref[...], preferred_element_type=F32)) * \
        jnp.dot(h, wu_ref[...], preferred_element_type=F32)
    f = jnp.dot(a.astype(BF16), wd_ref[...], preferred_element_type=F32)
    o_ref[...] = x_ref[...] + mod_ref[5:6] * f


def _ffn(h2, x1, mods_l, wg, wu, wd, latent, seq):
    rows = h2.shape[0]
    tm = 512
    ff = wg.shape[1]
    mod_row = (lambda i: (i * tm) // seq) if latent else (lambda i: CTX_ROW)
    resident = lambda shape: pl.BlockSpec(shape, lambda i: (0, 0), pipeline_mode=pl.Buffered(1))
    return pl.pallas_call(
        _ffn_kernel,
        out_shape=jax.ShapeDtypeStruct((rows, D_MODEL), F32),
        grid=(rows // tm,),
        in_specs=[pl.BlockSpec((tm, D_MODEL), lambda i: (i, 0)),
                  pl.BlockSpec((tm, D_MODEL), lambda i: (i, 0)),
                  pl.BlockSpec((None, 6, D_MODEL), lambda i: (mod_row(i), 0, 0)),
                  resident((D_MODEL, ff)), resident((D_MODEL, ff)), resident((ff, D_MODEL))],
        out_specs=pl.BlockSpec((tm, D_MODEL), lambda i: (i, 0)),
        compiler_params=pltpu.CompilerParams(
            dimension_semantics=("arbitrary",), vmem_limit_bytes=VMEM_LIMIT),
        name="ffn_dense",
    )(h2, x1, mods_l, wg, wu, wd)


def _route_tables(route_t, cnt, seq):
    b = route_t.shape[0]
    per_block = MOE_BLOCK // seq
    nb = b // per_block
    c = cnt[:, 0, :N_EXPERTS].astype(jnp.int32).reshape(nb, per_block, N_EXPERTS)
    before = jnp.cumsum(c, axis=1) - c
    total = c.sum(axis=1)
    aligned = (total + 7) // 8 * 8
    starts = jnp.cumsum(aligned, axis=1) - aligned
    base = (starts[:, None, :] + before).reshape(b, N_EXPERTS)
    expert = route_t[:, 2:4, :].astype(jnp.int32)
    rank = route_t[:, 4:6, :].astype(jnp.int32)
    slot = rank
    for e in range(N_EXPERTS):
        slot = slot + jnp.where(expert == e, base[:, e][:, None, None], 0)
    slots = slot.reshape(nb, per_block, 2, seq).transpose(0, 2, 1, 3).reshape(nb, 2, MOE_BLOCK)
    return starts.reshape(-1), total.reshape(-1), slots


MOE_SLOTS = 2 * MOE_BLOCK
MOE_ROWS = MOE_SLOTS + 8 * N_EXPERTS + MOE_TILE


MOE_GROUP = 16


MOE_PLANE = MOE_BLOCK + 8
MOE_SPARE_ROW = MOE_BLOCK
MOE_SPARE_SLOT = MOE_ROWS


def _moe_kernel(starts_ref, counts_ref, slots_ref, h_ref, route_ref, wg_ref, wu_ref, wt_ref, wd_ref, f_ref,
                xs_ref, stage_ref, z_ref, yb_ref, dst_ref, state_ref):
    b = pl.program_id(0)
    e = pl.program_id(1)
    half = D_MODEL // 2

    def send_rows(pending, buf):
        for r in range(MOE_TILE):
            z_ref[pl.ds(dst_ref[pending + r], 1), :] = yb_ref[buf, pl.ds(r, 1), :]

    @pl.when(e == 0)
    def _():
        xs_ref[MOE_SLOTS:, :] = jnp.zeros((MOE_ROWS - MOE_SLOTS, half), jnp.uint32)
        yb_ref[...] = jnp.zeros(yb_ref.shape, jnp.uint32)
        zero_row = jnp.zeros((1, half), jnp.uint32)
        for g in range(N_EXPERTS):
            end = starts_ref[b * N_EXPERTS + g] + counts_ref[b * N_EXPERTS + g]
            for r in range(7):
                xs_ref[pl.ds(end + r, 1), :] = zero_row
                dst_ref[end + r] = MOE_SPARE_ROW

        def spare(i, c):
            dst_ref[MOE_SLOTS + i] = MOE_SPARE_ROW
            return c
        lax.fori_loop(0, MOE_ROWS + MOE_TILE - MOE_SLOTS, spare, 0)

        def put(g, c):
            t0 = pl.multiple_of(g * MOE_GROUP, MOE_GROUP)
            hb = h_ref[pl.ds(t0, MOE_GROUP), :].astype(F32)
            stage_ref[...] = _pack2(hb[:, :half], hb[:, half:])
            for r in range(MOE_GROUP):
                row = stage_ref[r:r + 1, :]
                s0 = slots_ref[0, t0 + r]
                s1 = slots_ref[1, t0 + r]
                xs_ref[pl.ds(s0, 1), :] = row
                xs_ref[pl.ds(s1, 1), :] = row
                dst_ref[s0] = t0 + r
                dst_ref[s1] = t0 + r + MOE_PLANE
            return c
        lax.fori_loop(0, MOE_BLOCK // MOE_GROUP, put, 0)
        state_ref[0] = 0
        state_ref[1] = MOE_SPARE_SLOT

    start = starts_ref[b * N_EXPERTS + e]
    count = counts_ref[b * N_EXPERTS + e]

    def tile(j, carry):
        buf, pending = carry
        send_rows(pending, 1 - buf)
        s = pl.multiple_of(start + j * MOE_TILE, 8)
        xg = xs_ref[pl.ds(s, MOE_TILE), :]
        x_lo = _unpack2(xg, 0).astype(BF16)
        x_hi = _unpack2(xg, 1).astype(BF16)
        main = FF_EXPERT - MOE_TAIL

        def proj(w_ref, cols):
            return (jnp.dot(x_lo, w_ref[:half, cols], preferred_element_type=F32)
                    + jnp.dot(x_hi, w_ref[half:, cols], preferred_element_type=F32))
        tail = proj(wt_ref, slice(None))
        rest = tail.shape[1] // 2
        a = jnp.concatenate([_silu(proj(wg_ref, slice(0, main))) * proj(wu_ref, slice(0, main)),
                             _silu(tail[:, :rest]) * tail[:, rest:]], axis=1).astype(BF16)
        y = jnp.dot(a, wd_ref[...], preferred_element_type=F32)
        yb_ref[buf] = _pack2(y[:, :half], y[:, half:])
        return 1 - buf, s
    buf, pending = lax.fori_loop(0, (count + MOE_TILE - 1) // MOE_TILE, tile, (state_ref[0], state_ref[1]))
    state_ref[0] = buf
    state_ref[1] = pending

    @pl.when(e == pl.num_programs(1) - 1)
    def _():
        send_rows(pending, 1 - buf)
        rows = 256

        def blend(i, c):
            t0 = pl.multiple_of(i * rows, rows)
            z0 = z_ref[pl.ds(t0, rows), :]
            z1 = z_ref[pl.ds(pl.multiple_of(MOE_PLANE + t0, 8), rows), :]
            route = route_ref[pl.ds(t0, rows), :]
            w1 = route[:, 0:1]
            w2 = route[:, 1:2]
            f_ref[pl.ds(t0, rows), :half] = (w1 * _unpack2(z0, 0) + w2 * _unpack2(z1, 0)).astype(BF16)
            f_ref[pl.ds(t0, rows), half:] = (w1 * _unpack2(z0, 1) + w2 * _unpack2(z1, 1)).astype(BF16)
            return c
        lax.fori_loop(0, MOE_BLOCK // rows, blend, 0)


def _moe_experts(h2, route, starts, counts, slots, wg, wu, wd):
    rows = h2.shape[0]
    nb = rows // MOE_BLOCK
    ff = wd.shape[1]
    wt = jnp.concatenate([wg[:, :, ff - MOE_TAIL:], wu[:, :, ff - MOE_TAIL:]], axis=2)
    grid_spec = pltpu.PrefetchScalarGridSpec(
        num_scalar_prefetch=2,
        grid=(nb, N_EXPERTS),
        in_specs=[
            pl.BlockSpec((None, 2, MOE_BLOCK), lambda b, e, st, ct: (b, 0, 0), memory_space=pltpu.SMEM),
            pl.BlockSpec((MOE_BLOCK, D_MODEL), lambda b, e, st, ct: (b, 0), pipeline_mode=pl.Buffered(1)),
            pl.BlockSpec((MOE_BLOCK, ROUTE_W), lambda b, e, st, ct: (b, 0), pipeline_mode=pl.Buffered(1)),
            pl.BlockSpec((None, D_MODEL, ff), lambda b, e, st, ct: (e, 0, 0)),
            pl.BlockSpec((None, D_MODEL, ff), lambda b, e, st, ct: (e, 0, 0)),
            pl.BlockSpec((None, D_MODEL, 2 * MOE_TAIL), lambda b, e, st, ct: (e, 0, 0)),
            pl.BlockSpec((None, ff, D_MODEL), lambda b, e, st, ct: (e, 0, 0)),
        ],
        out_specs=pl.BlockSpec((MOE_BLOCK, D_MODEL), lambda b, e, st, ct: (b, 0)),
        scratch_shapes=[pltpu.VMEM((MOE_ROWS, D_MODEL // 2), jnp.uint32),
                        pltpu.VMEM((MOE_GROUP, D_MODEL // 2), jnp.uint32),
                        pltpu.VMEM((2 * MOE_PLANE, D_MODEL // 2), jnp.uint32),
                        pltpu.VMEM((2, MOE_TILE, D_MODEL // 2), jnp.uint32),
                        pltpu.SMEM((MOE_ROWS + MOE_TILE,), jnp.int32),
                        pltpu.SMEM((2,), jnp.int32)],
    )
    return pl.pallas_call(
        _moe_kernel,
        out_shape=jax.ShapeDtypeStruct((rows, D_MODEL), BF16),
        grid_spec=grid_spec,
        compiler_params=pltpu.CompilerParams(
            dimension_semantics=("arbitrary", "arbitrary"), vmem_limit_bytes=VMEM_LIMIT),
        name="moe_experts",
    )(starts, counts, slots, h2, route, wg, wu, wt, wd)


def _final_kernel(x_ref, f_ref, mod_ref, fg_ref, o_ref):
    y = x_ref[...] + mod_ref[5:6] * f_ref[...].astype(F32)
    o_ref[...] = _rms(y, fg_ref[...])


def _final(x1, f, mods_l, final_g, latent, seq):
    rows = x1.shape[0]
    tr = 512
    mod_row = (lambda i: (i * tr) // seq) if latent else (lambda i: CTX_ROW)
    return pl.pallas_call(
        _final_kernel,
        out_shape=jax.ShapeDtypeStruct((rows, D_MODEL), F32),
        grid=(rows // tr,),
        in_specs=[pl.BlockSpec((tr, D_MODEL), lambda i: (i, 0)),
                  pl.BlockSpec((tr, D_MODEL), lambda i: (i, 0)),
                  pl.BlockSpec((None, 6, D_MODEL), lambda i: (mod_row(i), 0, 0)),
                  pl.BlockSpec((1, D_MODEL), lambda i: (0, 0))],
        out_specs=pl.BlockSpec((tr, D_MODEL), lambda i: (i, 0)),
        compiler_params=pltpu.CompilerParams(
            dimension_semantics=("arbitrary",), vmem_limit_bytes=VMEM_LIMIT),
        name="final_norm",
    )(x1, f, mods_l, final_g.reshape(1, D_MODEL))


def _dup_heads(a):
    b, n = a.shape[:2]
    return jnp.repeat(a, 2, axis=2).reshape(b, n, 2 * KV_W).astype(BF16)


def kernel(x_prompt, x_sample, c, cache_k, cache_v, state_ssm_re, state_ssm_im, c_ctx, mod_w, mod_b, norm1_g, norm2_g, w_in, w_out, q_norm_g, k_norm_g, ssm_a_re, ssm_a_im, ssm_log_dt, ssm_b_re, ssm_b_im, ssm_c_re, ssm_c_im, ssm_d, ssm_glu_w, pool_w, pool_scale, ffn_w_gate, ffn_w_up, ffn_w_down, moe_router_w, moe_router_b, moe_w_gate, moe_w_up, moe_w_down, final_g):
    bp, lp, _ = x_prompt.shape
    bs, ls, _ = x_sample.shape
    assert bs == SEQ_GROUP and bp % SEQ_GROUP == 0

    cond = jnp.zeros((MOD_ROWS, D_MODEL), F32).at[:bs].set(c).at[CTX_ROW].set(c_ctx)
    mods = _modulation(cond, mod_w, mod_b).reshape(DEPTH, MOD_ROWS, 6, D_MODEL)

    bw, cw, lam = _ssm_prep(ssm_a_re, ssm_a_im, ssm_log_dt, ssm_b_re, ssm_b_im, ssm_c_re, ssm_c_im)

    xp, xs = x_prompt, x_sample
    new_k, new_v, new_state = [], [], []
    for l in range(DEPTH):
        mods_l = mods[l]
        w_in_bf = w_in[l].astype(BF16)
        w_out_bf = w_out[l].astype(BF16)
        glu_bf = ssm_glu_w[l].astype(BF16)
        moe = l % 2 == 1
        i = l // 2
        router = (moe_router_w[i], moe_router_b[i]) if moe else None
        if moe:
            wg, wu, wd = (moe_w_gate[i].astype(BF16), moe_w_up[i].astype(BF16), moe_w_down[i].astype(BF16))
        else:
            wg, wu, wd = (ffn_w_gate[i].astype(BF16), ffn_w_up[i].astype(BF16), ffn_w_down[i].astype(BF16))
        for latent in (False, True):
            x = xs if latent else xp
            b, seq, _ = x.shape
            nbg = b // SEQ_GROUP
            rows = b * seq
            outs = _inproj(x.reshape(rows, D_MODEL), mods_l, norm1_g[l], w_in_bf, q_norm_g[l], k_norm_g[l],
                           latent, seq)
            q, kd, vd = (a.reshape(b, seq, a.shape[-1]) for a in outs[:3])
            u_ssm = outs[3]
            u_pool = outs[4].reshape(b, seq, POOL_W)
            if latent:
                kds = [kd, _dup_heads(cache_k[:, l])]
                vds = [vd, _dup_heads(cache_v[:, l])]
                h0 = jnp.concatenate([
                    state_ssm_re[:, l, 0].reshape(b, SSM_GP), state_ssm_im[:, l, 0].reshape(b, SSM_GP),
                    state_ssm_re[:, l, 1].reshape(b, SSM_GP), state_ssm_im[:, l, 1].reshape(b, SSM_GP)],
                    axis=1).reshape(nbg, SEQ_GROUP, 4 * SSM_GP)
            else:
                kds, vds = [kd], [vd]
                new_k.append(outs[5])
                new_v.append(outs[6])
                h0 = jnp.zeros((nbg, SEQ_GROUP, 4 * SSM_GP), F32)
            attn = _attention(q, kds, vds)
            y_ssm, hf = _ssm(u_ssm, l, bw, cw, lam, h0, ssm_d[l], glu_bf, seq)
            if not latent:
                new_state.append(hf.reshape(b, 2, 2, SSM_GROUPS, SSM_STATE))
            res = _mixout(x, attn, y_ssm.reshape(b, seq, SSM_W), u_pool, mods_l,
                          pool_w[l], pool_scale[l], w_out_bf, norm2_g[l], latent, router)
            x1 = res[0].reshape(rows, D_MODEL)
            if moe:
                route = res[2].reshape(rows, ROUTE_W)
                starts, counts, slots = _route_tables(res[3], res[4], seq)
                f = _moe_experts(res[1].reshape(rows, D_MODEL), route, starts, counts, slots, wg, wu, wd)
                y = _final(x1, f, mods_l, final_g, latent, seq)
            else:
                y = _ffn(res[1].reshape(rows, D_MODEL), x1, mods_l, wg, wu, wd, latent, seq)
            y = y.reshape(b, seq, D_MODEL)
            if latent:
                xs = y
            else:
                xp = y

    def cache_out(parts):
        a = jnp.stack(parts, axis=1).reshape(bp, DEPTH, N_KV_HEADS, HEAD_DIM, lp)
        return a.transpose(0, 1, 4, 2, 3)
    new_cache_k = cache_out(new_k)
    new_cache_v = cache_out(new_v)
    st = jnp.stack(new_state, axis=1)
    return (xp, xs, new_cache_k, new_cache_v, st[:, :, :, 0], st[:, :, :, 1])
```

```python
import functools

import numpy as np
import jax
import jax.numpy as jnp
from jax import lax
from jax.experimental import pallas as pl
from jax.experimental.pallas import tpu as pltpu

D_MODEL = 1024
DEPTH = 2
GRID_W = 64
ATTN_W = 512
HEAD_DIM = 64
N_HEADS = 8
N_KV_HEADS = 2
KV_W = 128
ROPE_THETA = 10000.0
SSM_W = 256
SSM_CH = 16
SSM_GROUPS = 16
SSM_STATE = 64
SSM_GP = SSM_GROUPS * SSM_STATE
POOL_W = 256
POOL_WINDOWS = (2, 4, 8, 16)
POOL_CH = 64
POOL_PAD = max(POOL_WINDOWS) // 2
IN_COLS = ATTN_W + 2 * KV_W + SSM_W + POOL_W
FF_DENSE = 2816
N_EXPERTS = 8
FF_EXPERT = 1408
EPS = 1e-6
LOG2_E = 1.4426950408889634

SEQ_GROUP = 8
SSM_CHUNK = 16
SSM_COLBLK = 512
ROUTE_W = 128
ATTN_ROWS = 512
MIX_ROWS = 1024
MOE_BLOCK = 2048
MOE_TILE = 128
MXU_TILE = 256
MOE_TAIL = FF_EXPERT % MXU_TILE
MOD_ROWS = 16
CTX_ROW = 8
VMEM_LIMIT = 56 * 1024 * 1024

F32 = jnp.float32
BF16 = jnp.bfloat16


def _silu(x):
    return x * jax.nn.sigmoid(x)


def _rms(x32, g):
    return x32 * lax.rsqrt(jnp.mean(x32 * x32, axis=-1, keepdims=True) + EPS) * g


def _mod_kernel(cond_ref, w_ref, b_ref, o_ref):
    s = _silu(cond_ref[...])
    w = w_ref[...]
    s_hi = s.astype(BF16)
    s_lo = (s - s_hi.astype(F32)).astype(BF16)
    w_hi = w.astype(BF16)
    w_lo = (w - w_hi.astype(F32)).astype(BF16)
    o_ref[...] = (jnp.dot(s_hi, w_hi, preferred_element_type=F32)
                  + jnp.dot(s_hi, w_lo, preferred_element_type=F32)
                  + jnp.dot(s_lo, w_hi, preferred_element_type=F32) + b_ref[...])


def _modulation(cond, mod_w, mod_b):
    tn = 1536
    n = 6 * D_MODEL
    return pl.pallas_call(
        _mod_kernel,
        out_shape=jax.ShapeDtypeStruct((DEPTH, MOD_ROWS, n), F32),
        grid=(DEPTH, n // tn),
        in_specs=[
            pl.BlockSpec((MOD_ROWS, D_MODEL), lambda l, j: (0, 0)),
            pl.BlockSpec((None, D_MODEL, tn), lambda l, j: (l, 0, j)),
            pl.BlockSpec((None, 1, tn), lambda l, j: (l, 0, j)),
        ],
        out_specs=pl.BlockSpec((None, MOD_ROWS, tn), lambda l, j: (l, 0, j)),
        compiler_params=pltpu.CompilerParams(
            dimension_semantics=("arbitrary", "arbitrary"), vmem_limit_bytes=VMEM_LIMIT),
        name="modulation",
    )(cond, mod_w, mod_b.reshape(DEPTH, 1, n))


def _ssm_prep_kernel(are_ref, aim_ref, dt_ref, bre_ref, bim_ref, cre_ref, cim_ref, bw_ref, cw_ref, lam_ref):
    row_group = lax.broadcasted_iota(jnp.int32, (SSM_W, SSM_GP), 0) // SSM_CH
    lane_group = lax.broadcasted_iota(jnp.int32, (SSM_W, SSM_GP), 1) // SSM_STATE
    own = row_group == lane_group
    for ld in range(2 * DEPTH):
        l, d = divmod(ld, 2)
        a_re = are_ref[ld:ld + 1, :]
        a_im = aim_ref[ld:ld + 1, :]
        dt = jnp.exp(dt_ref[ld:ld + 1, :])
        mag = jnp.exp(a_re * dt)
        l_re = mag * jnp.cos(a_im * dt)
        l_im = mag * jnp.sin(a_im * dt)
        lam_ref[l, 2 * d] = jnp.broadcast_to(l_re, (SEQ_GROUP, SSM_GP))
        lam_ref[l, 2 * d + 1] = jnp.broadcast_to(l_im, (SEQ_GROUP, SSM_GP))
        x = l_re - 1.0
        den = a_re * a_re + a_im * a_im
        f_re = (x * a_re + l_im * a_im) / den
        f_im = (l_im * a_re - x * a_im) / den
        b_re = bre_ref[ld]
        b_im = bim_ref[ld]
        bbar = (f_re * b_re - f_im * b_im, f_re * b_im + f_im * b_re)
        cmat = (cre_ref[ld], -cim_ref[ld])
        for ri in range(2):
            c0 = (2 * d + ri) * SSM_GP
            tiled = jnp.concatenate([bbar[ri]] * SSM_GROUPS, axis=0)
            bw_ref[l, :, c0:c0 + SSM_GP] = jnp.where(own, tiled, 0.0).astype(BF16)
            cw_ref[l, c0:c0 + SSM_GP, :] = jnp.where(own, cmat[ri], 0.0).T.astype(BF16)


def _ssm_prep(a_re, a_im, log_dt, b_re, b_im, c_re, c_im):
    ld = DEPTH * 2
    are = a_re.reshape(ld, SSM_GP)
    aim = a_im.reshape(ld, SSM_GP)
    dt = jnp.repeat(log_dt.reshape(ld, SSM_GROUPS), SSM_STATE, axis=1)
    to_rows = lambda b: b.reshape(ld, SSM_GROUPS, SSM_STATE, SSM_CH).transpose(0, 3, 1, 2).reshape(ld, SSM_CH, SSM_GP)
    repeat_lanes = lambda c: jnp.tile(c.reshape(ld, SSM_W, SSM_STATE), (1, 1, SSM_GROUPS))
    return pl.pallas_call(
        _ssm_prep_kernel,
        out_shape=(jax.ShapeDtypeStruct((DEPTH, SSM_W, 4 * SSM_GP), BF16),
                   jax.ShapeDtypeStruct((DEPTH, 4 * SSM_GP, SSM_W), BF16),
                   jax.ShapeDtypeStruct((DEPTH, 4, SEQ_GROUP, SSM_GP), F32)),
        compiler_params=pltpu.CompilerParams(vmem_limit_bytes=VMEM_LIMIT),
        name="ssm_prep",
    )(are, aim, dt, to_rows(b_re), to_rows(b_im), repeat_lanes(c_re), repeat_lanes(c_im))


def _inproj_kernel(*refs, latent):
    if latent:
        (x_ref, mod_ref, g_ref, w_ref, qg_ref, kg_ref, sq_ref, sk_ref, cos_ref, sa_ref, sb_ref,
         q_ref, kd_ref, vd_ref, us_ref, up_ref) = refs
    else:
        (x_ref, mod_ref, g_ref, w_ref, qg_ref, kg_ref, sq_ref, sk_ref,
         q_ref, kd_ref, vd_ref, us_ref, up_ref, ko_ref, vo_ref) = refs
    x = x_ref[...]
    mod = mod_ref[...]
    h = _rms(x, g_ref[...]) * (1.0 + mod[1:2]) + mod[0:1]
    p = jnp.dot(h.astype(BF16), w_ref[...], preferred_element_type=F32)
    q = p[:, :ATTN_W]
    k = p[:, ATTN_W:ATTN_W + KV_W]
    v = p[:, ATTN_W + KV_W:ATTN_W + 2 * KV_W]
    q_ms = jnp.dot((q * q).astype(BF16), sq_ref[...], preferred_element_type=F32)
    k_ms = jnp.dot((k * k).astype(BF16), sk_ref[...], preferred_element_type=F32)
    q = q * lax.rsqrt(q_ms + EPS) * qg_ref[...]
    k = k * lax.rsqrt(k_ms + EPS) * kg_ref[...]
    if latent:
        cos = cos_ref[...]
        sa = sa_ref[...]
        sb = sb_ref[...]
        cos4 = jnp.concatenate([cos] * 4, axis=1)
        sa4 = jnp.concatenate([sa] * 4, axis=1)
        sb4 = jnp.concatenate([sb] * 4, axis=1)
        q = (q * cos4 + pltpu.roll(q, ATTN_W - 16, axis=1) * sa4 + pltpu.roll(q, 16, axis=1) * sb4)
        k = (k * cos + pltpu.roll(k, KV_W - 16, axis=1) * sa + pltpu.roll(k, 16, axis=1) * sb)
    else:
        seq = ko_ref.shape[-1]
        for s in range(ko_ref.shape[0]):
            ko_ref[s] = k[s * seq:(s + 1) * seq, :].T
            vo_ref[s] = v[s * seq:(s + 1) * seq, :].T
    q_ref[...] = (q * (HEAD_DIM ** -0.5 * LOG2_E)).astype(BF16)
    lane = lax.broadcasted_iota(jnp.int32, k.shape, 1)
    first = lane < HEAD_DIM
    k_sw = pltpu.roll(k, HEAD_DIM, axis=1)
    v_sw = pltpu.roll(v, HEAD_DIM, axis=1)
    kd_ref[:, :KV_W] = jnp.where(first, k, k_sw).astype(BF16)
    kd_ref[:, KV_W:] = jnp.where(first, k_sw, k).astype(BF16)
    vd_ref[:, :KV_W] = jnp.where(first, v, v_sw).astype(BF16)
    vd_ref[:, KV_W:] = jnp.where(first, v_sw, v).astype(BF16)
    o3 = ATTN_W + 2 * KV_W
    us_ref[0] = p[:, o3:o3 + SSM_W // 2]
    us_ref[1] = p[:, o3 + SSM_W // 2:o3 + SSM_W]
    up_ref[...] = p[:, o3 + SSM_W:]


def _head_mean_matrix(width):
    i = np.arange(width) // HEAD_DIM
    return jnp.asarray((i[:, None] == i[None, :]).astype(np.float32) / HEAD_DIM, dtype=BF16)


def _rope_tables(seq):
    t = np.arange(seq)
    row = (t // GRID_W).astype(np.float64)
    col = (t % GRID_W).astype(np.float64)
    nf = HEAD_DIM // 4
    inv = ROPE_THETA ** (-np.arange(nf, dtype=np.float64) / nf)
    ang_r = row[:, None] * inv[None, :]
    ang_c = col[:, None] * inv[None, :]
    cos = np.concatenate([np.cos(ang_r), np.cos(ang_r), np.cos(ang_c), np.cos(ang_c)], axis=1)
    sin = np.concatenate([np.sin(ang_r), np.sin(ang_r), np.sin(ang_c), np.sin(ang_c)], axis=1)
    lower = np.tile(np.concatenate([np.ones(nf), np.zeros(nf)]), 2)[None, :]
    sa = -sin * lower
    sb = sin * (1.0 - lower)
    tile = lambda a: jnp.asarray(np.tile(a, (1, 2)), dtype=F32)
    return tile(cos), tile(sa), tile(sb)


def _inproj(x, mods_l, g1, w_in_bf, qg, kg, latent, seq):
    rows = x.shape[0]
    tr = 1024
    tiles_per_seq = max(seq // tr, 1)
    mod_row = (lambda i: (i * tr) // seq) if latent else (lambda i: CTX_ROW)
    const = lambda shape: pl.BlockSpec(shape, lambda i: (0,) * len(shape))
    row_spec = lambda width: pl.BlockSpec((tr, width), lambda i: (i, 0))
    in_specs = [
        row_spec(D_MODEL),
        pl.BlockSpec((None, 6, D_MODEL), lambda i: (mod_row(i), 0, 0)),
        const((1, D_MODEL)),
        const((D_MODEL, IN_COLS)),
        const((1, ATTN_W)),
        const((1, KV_W)),
        const((ATTN_W, ATTN_W)),
        const((KV_W, KV_W)),
    ]
    args = [x, mods_l, g1.reshape(1, D_MODEL), w_in_bf,
            jnp.tile(qg, N_HEADS).reshape(1, ATTN_W), jnp.tile(kg, N_KV_HEADS).reshape(1, KV_W),
            _head_mean_matrix(ATTN_W), _head_mean_matrix(KV_W)]
    widths = [(ATTN_W, BF16), (2 * KV_W, BF16), (2 * KV_W, BF16), (None, F32), (POOL_W, F32)]
    if latent:
        assert seq % tr == 0
        cos, sa, sb = _rope_tables(seq)
        in_specs += [pl.BlockSpec((tr, KV_W), lambda i: (i % tiles_per_seq, 0))] * 3
        args += [cos, sa, sb]
    out_shape = [jax.ShapeDtypeStruct((rows, w) if w else (2, rows, SSM_W // 2), dt) for w, dt in widths]
    out_specs = [row_spec(w) if w else pl.BlockSpec((2, tr, SSM_W // 2), lambda i: (0, i, 0)) for w, _ in widths]
    if not latent:
        assert tr % seq == 0
        out_shape += [jax.ShapeDtypeStruct((rows // seq, KV_W, seq), F32)] * 2
        out_specs += [pl.BlockSpec((tr // seq, KV_W, seq), lambda i: (i, 0, 0))] * 2
    return pl.pallas_call(
        functools.partial(_inproj_kernel, latent=latent),
        out_shape=out_shape, grid=(rows // tr,), in_specs=in_specs, out_specs=out_specs,
        compiler_params=pltpu.CompilerParams(
            dimension_semantics=("arbitrary",), vmem_limit_bytes=VMEM_LIMIT),
        name="inproj_latent" if latent else "inproj_context",
    )(*args)


def _attn_kernel(*refs, n_kv_src):
    q_ref = refs[0]
    k_refs = refs[1:1 + n_kv_src]
    v_refs = refs[1 + n_kv_src:1 + 2 * n_kv_src]
    o_ref = refs[1 + 2 * n_kv_src]
    nseq, tq, _ = q_ref.shape
    lane = lax.broadcasted_iota(jnp.int32, (tq, KV_W), 1)
    first = lane < HEAD_DIM
    for sq in range(nseq):
        for pair in range(N_HEADS // 2):
            kv = pair // 2
            cols = slice(pair * KV_W, (pair + 1) * KV_W)
            kcols = slice(kv * KV_W, (kv + 1) * KV_W)
            qp = q_ref[sq, :, cols]
            halves = []
            for half in range(2):
                qm = jnp.where(first if half == 0 else jnp.logical_not(first), qp, jnp.zeros_like(qp))
                scores = [lax.dot_general(qm, k_ref[sq, :, kcols], (((1,), (1,)), ((), ())),
                                          preferred_element_type=F32) for k_ref in k_refs]
                m = scores[0].max(axis=-1, keepdims=True)
                for s in scores[1:]:
                    m = jnp.maximum(m, s.max(axis=-1, keepdims=True))
                den = jnp.zeros((tq, 1), F32)
                acc = jnp.zeros((tq, KV_W), F32)
                for s, v_ref in zip(scores, v_refs):
                    p = jnp.exp2(s - m)
                    den = den + p.sum(axis=-1, keepdims=True)
                    acc = acc + jnp.dot(p.astype(BF16), v_ref[sq, :, kcols], preferred_element_type=F32)
                halves.append(acc / den)
            o_ref[sq, :, cols] = jnp.where(first, halves[0], halves[1]).astype(o_ref.dtype)


def _attention(q, kds, vds):
    b, seq, _ = q.shape
    tq = min(seq, ATTN_ROWS)
    nseq = max(1, ATTN_ROWS // seq)
    n_src = len(kds)
    kv_spec = lambda a: pl.BlockSpec((nseq, a.shape[1], 2 * KV_W), lambda i, t: (i, 0, 0))
    return pl.pallas_call(
        functools.partial(_attn_kernel, n_kv_src=n_src),
        out_shape=jax.ShapeDtypeStruct((b, seq, ATTN_W), BF16),
        grid=(b // nseq, seq // tq),
        in_specs=[pl.BlockSpec((nseq, tq, ATTN_W), lambda i, t: (i, t, 0))]
                 + [kv_spec(a) for a in kds] + [kv_spec(a) for a in vds],
        out_specs=pl.BlockSpec((nseq, tq, ATTN_W), lambda i, t: (i, t, 0)),
        compiler_params=pltpu.CompilerParams(
            dimension_semantics=("arbitrary", "arbitrary"), vmem_limit_bytes=VMEM_LIMIT),
        name="attention_%dsrc" % n_src,
    )(q, *kds, *vds)


def _ssm_kernel(u_ref, bw_ref, cw_ref, lam_ref, h0_ref, d_ref, glu_ref,
                o_ref, hf_ref, y_ref, utm_scr, bu0_scr, bu1_scr, hs0_scr, hs1_scr, st_scr):
    rows = u_ref.shape[1]
    seq = rows // SEQ_GROUP
    crow = SSM_CHUNK * SEQ_GROUP
    nchunks = rows // crow
    w2 = 2 * SSM_GP
    lanes = SSM_W // 2

    def time_major(c):
        t0 = c * SSM_CHUNK
        return jnp.concatenate(
            [jnp.concatenate([u_ref[h, pl.ds(t0 + s, SEQ_GROUP, stride=seq), :] for h in range(2)], axis=1)
             for s in range(SSM_CHUNK)], axis=0)

    def y_rows(r, n):
        return jnp.concatenate([y_ref[0, pl.ds(r, n), :], y_ref[1, pl.ds(r, n), :]], axis=1)

    def set_y_rows(r, n, val):
        y_ref[0, pl.ds(r, n), :] = val[:, :lanes]
        y_ref[1, pl.ds(r, n), :] = val[:, lanes:]

    st_scr[...] = h0_ref[...]

    def init_rows(i, c):
        r = pl.multiple_of(i * crow, crow)
        tm = time_major(i)
        set_y_rows(r, crow, tm * d_ref[...])
        utm_scr[pl.ds(r, crow), :] = tm.astype(BF16)
        return c
    lax.fori_loop(0, nchunks, init_rows, 0)

    last = nchunks - 1
    bu_scr = (bu0_scr, bu1_scr)
    hs_scr = (hs0_scr, hs1_scr)

    def project_in(i, buf):
        i = jnp.clip(i, 0, last)
        for d, c in ((0, i), (1, last - i)):
            r = pl.multiple_of(c * crow, crow)
            bu_scr[buf][d] = jnp.dot(utm_scr[pl.ds(r, crow), :], bw_ref[:, d * w2:(d + 1) * w2],
                                     preferred_element_type=F32)

    def recur(buf):
        for d in range(2):
            for cb in range(SSM_GP // SSM_COLBLK):
                re_c = slice(cb * SSM_COLBLK, (cb + 1) * SSM_COLBLK)
                im_c = slice(SSM_GP + cb * SSM_COLBLK, SSM_GP + (cb + 1) * SSM_COLBLK)
                l_re = lam_ref[2 * d, :, re_c]
                l_im = lam_ref[2 * d + 1, :, re_c]
                h_re = st_scr[:, d * w2 + cb * SSM_COLBLK:d * w2 + (cb + 1) * SSM_COLBLK]
                h_im = st_scr[:, d * w2 + SSM_GP + cb * SSM_COLBLK:d * w2 + SSM_GP + (cb + 1) * SSM_COLBLK]
                for s in range(SSM_CHUNK):
                    t = s if d == 0 else SSM_CHUNK - 1 - s
                    r = slice(t * SEQ_GROUP, (t + 1) * SEQ_GROUP)
                    n_re = l_re * h_re - l_im * h_im + bu_scr[buf][d, r, re_c]
                    n_im = l_re * h_im + l_im * h_re + bu_scr[buf][d, r, im_c]
                    hs_scr[buf][d, r, re_c] = n_re
                    hs_scr[buf][d, r, im_c] = n_im
                    h_re, h_im = n_re, n_im
                st_scr[:, d * w2 + cb * SSM_COLBLK:d * w2 + (cb + 1) * SSM_COLBLK] = h_re
                st_scr[:, d * w2 + SSM_GP + cb * SSM_COLBLK:d * w2 + SSM_GP + (cb + 1) * SSM_COLBLK] = h_im

    def project_out(i, buf):
        i = jnp.clip(i, 0, last)
        for d, c in ((0, i), (1, last - i)):
            r = pl.multiple_of(c * crow, crow)
            contrib = jnp.dot(hs_scr[buf][d].astype(BF16), cw_ref[d * w2:(d + 1) * w2, :],
                              preferred_element_type=F32)
            set_y_rows(r, crow, y_rows(r, crow) + contrib)

    hs1_scr[...] = jnp.zeros(hs1_scr.shape, F32)
    project_in(0, 0)

    def two_steps(j, c):
        i = 2 * j
        recur(0)
        project_in(i + 1, 1)
        project_out(i - 1, 1)
        recur(1)
        project_in(i + 2, 0)
        project_out(i, 0)
        return c
    lax.fori_loop(0, nchunks // 2, two_steps, 0)
    project_out(last, 1)

    hf_ref[...] = st_scr[...]

    esteps = 128
    erow = esteps * SEQ_GROUP

    def epilogue(i, c):
        r = pl.multiple_of(i * erow, erow)
        z = jax.nn.gelu(y_rows(r, erow))
        g = jnp.dot(z.astype(BF16), glu_ref[...], preferred_element_type=F32)
        set_y_rows(r, erow, g[:, :SSM_W] * jax.nn.sigmoid(g[:, SSM_W:]))
        t0 = pl.multiple_of(i * esteps, esteps)
        for b in range(SEQ_GROUP):
            for h in range(2):
                o_ref[pl.ds(b * seq + t0, esteps), h * lanes:(h + 1) * lanes] = (
                    y_ref[h, pl.ds(r + b, esteps, stride=SEQ_GROUP), :].astype(BF16))
        return c
    lax.fori_loop(0, rows // erow, epilogue, 0)


def _ssm(u, layer, bw, cw, lam, h0, ssm_d, glu_bf, seq):
    rows = u.shape[1]
    grows = SEQ_GROUP * seq
    nbg = rows // grows
    crow = SSM_CHUNK * SEQ_GROUP
    const = lambda shape: pl.BlockSpec(shape, lambda i: (0,) * len(shape))
    return pl.pallas_call(
        _ssm_kernel,
        out_shape=(jax.ShapeDtypeStruct((rows, SSM_W), BF16),
                   jax.ShapeDtypeStruct((nbg, SEQ_GROUP, 4 * SSM_GP), F32)),
        grid=(nbg,),
        in_specs=[
            pl.BlockSpec((2, grows, SSM_W // 2), lambda i: (0, i, 0)),
            pl.BlockSpec((None, SSM_W, 4 * SSM_GP), lambda i: (layer, 0, 0)),
            pl.BlockSpec((None, 4 * SSM_GP, SSM_W), lambda i: (layer, 0, 0)),
            pl.BlockSpec((None, 4, SEQ_GROUP, SSM_GP), lambda i: (layer, 0, 0, 0)),
            pl.BlockSpec((None, SEQ_GROUP, 4 * SSM_GP), lambda i: (i, 0, 0)),
            const((1, SSM_W)),
            const((SSM_W, 2 * SSM_W)),
        ],
        out_specs=(pl.BlockSpec((grows, SSM_W), lambda i: (i, 0)),
                   pl.BlockSpec((None, SEQ_GROUP, 4 * SSM_GP), lambda i: (i, 0, 0))),
        scratch_shapes=[pltpu.VMEM((2, grows, SSM_W // 2), F32),
                        pltpu.VMEM((grows, SSM_W), BF16),
                        pltpu.VMEM((2, crow, 2 * SSM_GP), F32),
                        pltpu.VMEM((2, crow, 2 * SSM_GP), F32),
                        pltpu.VMEM((2, crow, 2 * SSM_GP), F32),
                        pltpu.VMEM((2, crow, 2 * SSM_GP), F32),
                        pltpu.VMEM((SEQ_GROUP, 4 * SSM_GP), F32)],
        compiler_params=pltpu.CompilerParams(
            dimension_semantics=("arbitrary",), vmem_limit_bytes=VMEM_LIMIT),
        name="ssm_scan",
    )(u, bw, cw, lam, h0, ssm_d.reshape(1, SSM_W), glu_bf)


def _pack2(lo, hi):
    return pltpu.pack_elementwise([lo, hi], packed_dtype=BF16)


def _unpack2(w, index):
    return pltpu.unpack_elementwise(w, index=index, packed_dtype=BF16, unpacked_dtype=F32)


def _mixout_kernel(*refs, route):
    if route:
        (x_ref, at_ref, ss_ref, up_ref, mod_ref, invc_ref, pw_ref, ps_ref, wo_ref, g2_ref, wr_ref, br_ref, tri_ref,
         x1_ref, h2_ref, route_ref, rt_ref, cnt_ref) = refs
    else:
        (x_ref, at_ref, ss_ref, up_ref, mod_ref, invc_ref, pw_ref, ps_ref, wo_ref, g2_ref,
         x1_ref, h2_ref) = refs
    nseq, seq, _ = up_ref.shape
    rows = nseq * seq
    zpad = jnp.zeros((POOL_PAD, POOL_W), F32)
    n_ext = seq + 2 * POOL_PAD
    back = lambda a, k: pltpu.roll(a, k, axis=0)
    ahead = lambda a, k: pltpu.roll(a, n_ext - k, axis=0)
    grp = lax.broadcasted_iota(jnp.int32, (n_ext, POOL_W), 1) // POOL_CH
    pooled = []
    for s in range(nseq):
        u = up_ref[s]
        ue = jnp.concatenate([zpad, u, zpad], axis=0)
        w2 = ue + back(ue, 1)
        w4 = back(w2, 1) + ahead(w2, 1)
        w8 = back(w4, 2) + ahead(w4, 2)
        w16 = back(w8, 4) + ahead(w8, 4)
        win = jnp.where(grp == 0, w2, jnp.where(grp == 1, w4, jnp.where(grp == 2, w8, w16)))
        pooled.append(win[POOL_PAD:POOL_PAD + seq] * invc_ref[...] - u)
    pooled = jnp.concatenate(pooled, axis=0)
    pool = jnp.dot(pooled.astype(BF16), pw_ref[...], preferred_element_type=F32) * ps_ref[...]
    mix = jnp.concatenate([at_ref[...].reshape(rows, ATTN_W), ss_ref[...].reshape(rows, SSM_W),
                           pool.astype(BF16)], axis=1)
    o = jnp.dot(mix, wo_ref[...], preferred_element_type=F32)
    mod = mod_ref[...]
    x1 = x_ref[...].reshape(rows, D_MODEL) + mod[2:3] * o
    x1_ref[...] = x1.reshape(nseq, seq, D_MODEL)
    h2 = _rms(x1, g2_ref[...]) * (1.0 + mod[4:5]) + mod[3:4]
    h2_ref[...] = h2.astype(BF16).reshape(nseq, seq, D_MODEL)
    if not route:
        return
    h_hi = h2.astype(BF16)
    h_lo = (h2 - h_hi.astype(F32)).astype(BF16)
    both = jnp.dot(h_hi, wr_ref[...], preferred_element_type=F32)
    logits = (both[:, :ROUTE_W] + both[:, ROUTE_W:]
              + jnp.dot(h_lo, wr_ref[:, :ROUTE_W], preferred_element_type=F32) + br_ref[...])
    lane = lax.broadcasted_iota(jnp.int32, logits.shape, 1).astype(F32)
    neg = float(np.finfo(np.float32).min)
    far = float(ROUTE_W)
    logits = jnp.where(lane < N_EXPERTS, logits, neg)
    m1 = logits.max(axis=-1, keepdims=True)
    i1 = jnp.where(logits == m1, lane, far).min(axis=-1, keepdims=True)
    rest = jnp.where(lane == i1, neg, logits)
    m2 = rest.max(axis=-1, keepdims=True)
    i2 = jnp.where(rest == m2, lane, far).min(axis=-1, keepdims=True)
    e2 = jnp.exp(m2 - m1)
    den = 1.0 + e2
    sel = jnp.where(lane == i1, 1.0, jnp.where(lane == i2, 1.0, 0.0))
    sel_bf = sel.astype(BF16)
    rank = jnp.concatenate([jnp.dot(tri_ref[...], sel_bf[s * seq:(s + 1) * seq], preferred_element_type=F32)
                            for s in range(nseq)], axis=0)
    r1 = jnp.where(lane == i1, rank, 0.0).sum(axis=-1, keepdims=True)
    r2 = jnp.where(lane == i2, rank, 0.0).sum(axis=-1, keepdims=True)
    out = jnp.zeros_like(logits)
    for j, col in enumerate((1.0 / den, e2 / den, i1, i2, r1, r2)):
        out = jnp.where(lane == j, col, out)
    route_ref[...] = out.reshape(nseq, seq, ROUTE_W)
    for s in range(nseq):
        part = slice(s * seq, (s + 1) * seq)
        rt_ref[s] = out[part].T[:8, :]
        cnt_ref[s] = jnp.broadcast_to(sel[part].sum(axis=0, keepdims=True), cnt_ref.shape[1:])


def _pool_inv_count(seq):
    t = np.arange(seq)
    cols = []
    for win in POOL_WINDOWS:
        lo = np.clip(t - win // 2, 0, seq)
        hi = np.clip(t + win // 2, 0, seq)
        cols.append(np.repeat((hi - lo).astype(np.float32)[:, None], POOL_CH, axis=1))
    return np.concatenate(cols, axis=1)


def _block_diag_pool(pool_w):
    eye = jnp.eye(len(POOL_WINDOWS), dtype=F32)
    m = eye[:, None, :, None] * pool_w[:, :, None, :]
    return m.reshape(POOL_W, POOL_W)


def _mixout(x, attn, ssm_tm, u_pool, mods_l, pool_w, pool_scale, w_out_bf, g2, latent, router):
    b, seq, _ = x.shape
    route = router is not None
    nseq = 1 if latent else max(1, MIX_ROWS // seq)
    mod_row = (lambda i: i) if latent else (lambda i: CTX_ROW)
    const = lambda shape: pl.BlockSpec(shape, lambda i: (0,) * len(shape))
    cnt = _pool_inv_count(seq)
    row_spec = lambda width: pl.BlockSpec((nseq, seq, width), lambda i: (i, 0, 0))
    in_specs = [
        row_spec(D_MODEL), row_spec(ATTN_W), row_spec(SSM_W), row_spec(POOL_W),
        pl.BlockSpec((None, 6, D_MODEL), lambda i: (mod_row(i), 0, 0)),
        const((seq, POOL_W)),
        const((POOL_W, POOL_W)),
        const((1, POOL_W)),
        const((D_MODEL, D_MODEL)),
        const((1, D_MODEL)),
    ]
    args = [x, attn, ssm_tm, u_pool, mods_l, jnp.asarray(1.0 / cnt, dtype=F32),
            _block_diag_pool(pool_w).astype(BF16), pool_scale.reshape(1, POOL_W), w_out_bf,
            g2.reshape(1, D_MODEL)]
    if route:
        wr, br = router
        tri = np.tril(np.ones((seq, seq), np.float32), -1)
        wr_pad = jnp.pad(wr, ((0, 0), (0, ROUTE_W - N_EXPERTS)))
        wr_hi = wr_pad.astype(BF16)
        wr_lo = (wr_pad - wr_hi.astype(F32)).astype(BF16)
        in_specs += [const((D_MODEL, 2 * ROUTE_W)), const((1, ROUTE_W)), const((seq, seq))]
        args += [jnp.concatenate([wr_hi, wr_lo], axis=1),
                 jnp.pad(br, (0, ROUTE_W - N_EXPERTS)).reshape(1, ROUTE_W),
                 jnp.asarray(tri, dtype=BF16)]
        out_shape = [jax.ShapeDtypeStruct((b, seq, D_MODEL), F32),
                     jax.ShapeDtypeStruct((b, seq, D_MODEL), BF16),
                     jax.ShapeDtypeStruct((b, seq, ROUTE_W), F32),
                     jax.ShapeDtypeStruct((b, 8, seq), F32),
                     jax.ShapeDtypeStruct((b, 8, ROUTE_W), F32)]
        out_specs = [row_spec(D_MODEL), row_spec(D_MODEL), row_spec(ROUTE_W),
                     pl.BlockSpec((nseq, 8, seq), lambda i: (i, 0, 0)),
                     pl.BlockSpec((nseq, 8, ROUTE_W), lambda i: (i, 0, 0))]
    else:
        out_shape = [jax.ShapeDtypeStruct((b, seq, D_MODEL), F32), jax.ShapeDtypeStruct((b, seq, D_MODEL), BF16)]
        out_specs = [row_spec(D_MODEL), row_spec(D_MODEL)]
    return pl.pallas_call(
        functools.partial(_mixout_kernel, route=route),
        out_shape=out_shape, grid=(b // nseq,), in_specs=in_specs, out_specs=out_specs,
        compiler_params=pltpu.CompilerParams(
            dimension_semantics=("arbitrary",), vmem_limit_bytes=VMEM_LIMIT),
        name="mixout_%s%s" % ("latent" if latent else "context", "_route" if route else ""),
    )(*args)


def _ffn_kernel(h_ref, x_ref, mod_ref, wg_ref, wu_ref, wd_ref, o_ref):
    h = h_ref[...]
    a = _silu(jnp.dot(h, wg_ref[...], preferred_element_type=F32)) * \
        jnp.dot(h, wu_ref[...], preferred_element_type=F32)
    f = jnp.dot(a.astype(BF16), wd_ref[...], preferred_element_type=F32)
    o_ref[...] = x_ref[...] + mod_ref[5:6] * f


def _ffn(h2, x1, mods_l, wg, wu, wd, latent, seq):
    rows = h2.shape[0]
    tm = 512
    ff = wg.shape[1]
    mod_row = (lambda i: (i * tm) // seq) if latent else (lambda i: CTX_ROW)
    resident = lambda shape: pl.BlockSpec(shape, lambda i: (0, 0), pipeline_mode=pl.Buffered(1))
    return pl.pallas_call(
        _ffn_kernel,
        out_shape=jax.ShapeDtypeStruct((rows, D_MODEL), F32),
        grid=(rows // tm,),
        in_specs=[pl.BlockSpec((tm, D_MODEL), lambda i: (i, 0)),
                  pl.BlockSpec((tm, D_MODEL), lambda i: (i, 0)),
                  pl.BlockSpec((None, 6, D_MODEL), lambda i: (mod_row(i), 0, 0)),
                  resident((D_MODEL, ff)), resident((D_MODEL, ff)), resident((ff, D_MODEL))],
        out_specs=pl.BlockSpec((tm, D_MODEL), lambda i: (i, 0)),
        compiler_params=pltpu.CompilerParams(
            dimension_semantics=("arbitrary",), vmem_limit_bytes=VMEM_LIMIT),
        name="ffn_dense",
    )(h2, x1, mods_l, wg, wu, wd)


def _route_tables(route_t, cnt, seq):
    b = route_t.shape[0]
    per_block = MOE_BLOCK // seq
    nb = b // per_block
    c = cnt[:, 0, :N_EXPERTS].astype(jnp.int32).reshape(nb, per_block, N_EXPERTS)
    before = jnp.cumsum(c, axis=1) - c
    total = c.sum(axis=1)
    aligned = (total + 7) // 8 * 8
    starts = jnp.cumsum(aligned, axis=1) - aligned
    base = (starts[:, None, :] + before).reshape(b, N_EXPERTS)
    expert = route_t[:, 2:4, :].astype(jnp.int32)
    rank = route_t[:, 4:6, :].astype(jnp.int32)
    slot = rank
    for e in range(N_EXPERTS):
        slot = slot + jnp.where(expert == e, base[:, e][:, None, None], 0)
    slots = slot.reshape(nb, per_block, 2, seq).transpose(0, 2, 1, 3).reshape(nb, 2, MOE_BLOCK)
    return starts.reshape(-1), total.reshape(-1), slots


MOE_SLOTS = 2 * MOE_BLOCK
MOE_ROWS = MOE_SLOTS + 8 * N_EXPERTS + MOE_TILE


MOE_GROUP = 16


MOE_PLANE = MOE_BLOCK + 8
MOE_SPARE_ROW = MOE_BLOCK
MOE_SPARE_SLOT = MOE_ROWS


def _moe_kernel(starts_ref, counts_ref, slots_ref, h_ref, route_ref, wg_ref, wu_ref, wt_ref, wd_ref, f_ref,
                xs_ref, stage_ref, z_ref, yb_ref, dst_ref, state_ref):
    b = pl.program_id(0)
    e = pl.program_id(1)
    half = D_MODEL // 2

    def send_rows(pending, buf):
        for r in range(MOE_TILE):
            z_ref[pl.ds(dst_ref[pending + r], 1), :] = yb_ref[buf, pl.ds(r, 1), :]

    @pl.when(e == 0)
    def _():
        xs_ref[MOE_SLOTS:, :] = jnp.zeros((MOE_ROWS - MOE_SLOTS, half), jnp.uint32)
        yb_ref[...] = jnp.zeros(yb_ref.shape, jnp.uint32)
        zero_row = jnp.zeros((1, half), jnp.uint32)
        for g in range(N_EXPERTS):
            end = starts_ref[b * N_EXPERTS + g] + counts_ref[b * N_EXPERTS + g]
            for r in range(7):
                xs_ref[pl.ds(end + r, 1), :] = zero_row
                dst_ref[end + r] = MOE_SPARE_ROW

        def spare(i, c):
            dst_ref[MOE_SLOTS + i] = MOE_SPARE_ROW
            return c
        lax.fori_loop(0, MOE_ROWS + MOE_TILE - MOE_SLOTS, spare, 0)

        def put(g, c):
            t0 = pl.multiple_of(g * MOE_GROUP, MOE_GROUP)
            hb = h_ref[pl.ds(t0, MOE_GROUP), :].astype(F32)
            stage_ref[...] = _pack2(hb[:, :half], hb[:, half:])
            for r in range(MOE_GROUP):
                row = stage_ref[r:r + 1, :]
                s0 = slots_ref[0, t0 + r]
                s1 = slots_ref[1, t0 + r]
                xs_ref[pl.ds(s0, 1), :] = row
                xs_ref[pl.ds(s1, 1), :] = row
                dst_ref[s0] = t0 + r
                dst_ref[s1] = t0 + r + MOE_PLANE
            return c
        lax.fori_loop(0, MOE_BLOCK // MOE_GROUP, put, 0)
        state_ref[0] = 0
        state_ref[1] = MOE_SPARE_SLOT

    start = starts_ref[b * N_EXPERTS + e]
    count = counts_ref[b * N_EXPERTS + e]

    def tile(j, carry):
        buf, pending = carry
        send_rows(pending, 1 - buf)
        s = pl.multiple_of(start + j * MOE_TILE, 8)
        xg = xs_ref[pl.ds(s, MOE_TILE), :]
        x_lo = _unpack2(xg, 0).astype(BF16)
        x_hi = _unpack2(xg, 1).astype(BF16)
        main = FF_EXPERT - MOE_TAIL

        def proj(w_ref, cols):
            return (jnp.dot(x_lo, w_ref[:half, cols], preferred_element_type=F32)
                    + jnp.dot(x_hi, w_ref[half:, cols], preferred_element_type=F32))
        tail = proj(wt_ref, slice(None))
        rest = tail.shape[1] // 2
        a = jnp.concatenate([_silu(proj(wg_ref, slice(0, main))) * proj(wu_ref, slice(0, main)),
                             _silu(tail[:, :rest]) * tail[:, rest:]], axis=1).astype(BF16)
        y = jnp.dot(a, wd_ref[...], preferred_element_type=F32)
        yb_ref[buf] = _pack2(y[:, :half], y[:, half:])
        return 1 - buf, s
    buf, pending = lax.fori_loop(0, (count + MOE_TILE - 1) // MOE_TILE, tile, (state_ref[0], state_ref[1]))
    state_ref[0] = buf
    state_ref[1] = pending

    @pl.when(e == pl.num_programs(1) - 1)
    def _():
        send_rows(pending, 1 - buf)
        rows = 256

        def blend(i, c):
            t0 = pl.multiple_of(i * rows, rows)
            z0 = z_ref[pl.ds(t0, rows), :]
            z1 = z_ref[pl.ds(pl.multiple_of(MOE_PLANE + t0, 8), rows), :]
            route = route_ref[pl.ds(t0, rows), :]
            w1 = route[:, 0:1]
            w2 = route[:, 1:2]
            f_ref[pl.ds(t0, rows), :half] = (w1 * _unpack2(z0, 0) + w2 * _unpack2(z1, 0)).astype(BF16)
            f_ref[pl.ds(t0, rows), half:] = (w1 * _unpack2(z0, 1) + w2 * _unpack2(z1, 1)).astype(BF16)
            return c
        lax.fori_loop(0, MOE_BLOCK // rows, blend, 0)


def _moe_experts(h2, route, starts, counts, slots, wg, wu, wd):
    rows = h2.shape[0]
    nb = rows // MOE_BLOCK
    ff = wd.shape[1]
    wt = jnp.concatenate([wg[:, :, ff - MOE_TAIL:], wu[:, :, ff - MOE_TAIL:]], axis=2)
    grid_spec = pltpu.PrefetchScalarGridSpec(
        num_scalar_prefetch=2,
        grid=(nb, N_EXPERTS),
        in_specs=[
            pl.BlockSpec((None, 2, MOE_BLOCK), lambda b, e, st, ct: (b, 0, 0), memory_space=pltpu.SMEM),
            pl.BlockSpec((MOE_BLOCK, D_MODEL), lambda b, e, st, ct: (b, 0), pipeline_mode=pl.Buffered(1)),
            pl.BlockSpec((MOE_BLOCK, ROUTE_W), lambda b, e, st, ct: (b, 0), pipeline_mode=pl.Buffered(1)),
            pl.BlockSpec((None, D_MODEL, ff), lambda b, e, st, ct: (e, 0, 0)),
            pl.BlockSpec((None, D_MODEL, ff), lambda b, e, st, ct: (e, 0, 0)),
            pl.BlockSpec((None, D_MODEL, 2 * MOE_TAIL), lambda b, e, st, ct: (e, 0, 0)),
            pl.BlockSpec((None, ff, D_MODEL), lambda b, e, st, ct: (e, 0, 0)),
        ],
        out_specs=pl.BlockSpec((MOE_BLOCK, D_MODEL), lambda b, e, st, ct: (b, 0)),
        scratch_shapes=[pltpu.VMEM((MOE_ROWS, D_MODEL // 2), jnp.uint32),
                        pltpu.VMEM((MOE_GROUP, D_MODEL // 2), jnp.uint32),
                        pltpu.VMEM((2 * MOE_PLANE, D_MODEL // 2), jnp.uint32),
                        pltpu.VMEM((2, MOE_TILE, D_MODEL // 2), jnp.uint32),
                        pltpu.SMEM((MOE_ROWS + MOE_TILE,), jnp.int32),
                        pltpu.SMEM((2,), jnp.int32)],
    )
    return pl.pallas_call(
        _moe_kernel,
        out_shape=jax.ShapeDtypeStruct((rows, D_MODEL), BF16),
        grid_spec=grid_spec,
        compiler_params=pltpu.CompilerParams(
            dimension_semantics=("arbitrary", "arbitrary"), vmem_limit_bytes=VMEM_LIMIT),
        name="moe_experts",
    )(starts, counts, slots, h2, route, wg, wu, wt, wd)


def _final_kernel(x_ref, f_ref, mod_ref, fg_ref, o_ref):
    y = x_ref[...] + mod_ref[5:6] * f_ref[...].astype(F32)
    o_ref[...] = _rms(y, fg_ref[...])


def _final(x1, f, mods_l, final_g, latent, seq):
    rows = x1.shape[0]
    tr = 1024
    mod_row = (lambda i: (i * tr) // seq) if latent else (lambda i: CTX_ROW)
    return pl.pallas_call(
        _final_kernel,
        out_shape=jax.ShapeDtypeStruct((rows, D_MODEL), F32),
        grid=(rows // tr,),
        in_specs=[pl.BlockSpec((tr, D_MODEL), lambda i: (i, 0)),
                  pl.BlockSpec((tr, D_MODEL), lambda i: (i, 0)),
                  pl.BlockSpec((None, 6, D_MODEL), lambda i: (mod_row(i), 0, 0)),
                  pl.BlockSpec((1, D_MODEL), lambda i: (0, 0))],
        out_specs=pl.BlockSpec((tr, D_MODEL), lambda i: (i, 0)),
        compiler_params=pltpu.CompilerParams(
            dimension_semantics=("arbitrary",), vmem_limit_bytes=VMEM_LIMIT),
        name="final_norm",
    )(x1, f, mods_l, final_g.reshape(1, D_MODEL))


def _dup_heads(a):
    b, n = a.shape[:2]
    return jnp.repeat(a, 2, axis=2).reshape(b, n, 2 * KV_W).astype(BF16)


def kernel(x_prompt, x_sample, c, cache_k, cache_v, state_ssm_re, state_ssm_im, c_ctx, mod_w, mod_b, norm1_g, norm2_g, w_in, w_out, q_norm_g, k_norm_g, ssm_a_re, ssm_a_im, ssm_log_dt, ssm_b_re, ssm_b_im, ssm_c_re, ssm_c_im, ssm_d, ssm_glu_w, pool_w, pool_scale, ffn_w_gate, ffn_w_up, ffn_w_down, moe_router_w, moe_router_b, moe_w_gate, moe_w_up, moe_w_down, final_g):
    bp, lp, _ = x_prompt.shape
    bs, ls, _ = x_sample.shape
    assert bs == SEQ_GROUP and bp % SEQ_GROUP == 0

    cond = jnp.zeros((MOD_ROWS, D_MODEL), F32).at[:bs].set(c).at[CTX_ROW].set(c_ctx)
    mods = _modulation(cond, mod_w, mod_b).reshape(DEPTH, MOD_ROWS, 6, D_MODEL)

    bw, cw, lam = _ssm_prep(ssm_a_re, ssm_a_im, ssm_log_dt, ssm_b_re, ssm_b_im, ssm_c_re, ssm_c_im)

    xp, xs = x_prompt, x_sample
    new_k, new_v, new_state = [], [], []
    for l in range(DEPTH):
        mods_l = mods[l]
        w_in_bf = w_in[l].astype(BF16)
        w_out_bf = w_out[l].astype(BF16)
        glu_bf = ssm_glu_w[l].astype(BF16)
        moe = l % 2 == 1
        i = l // 2
        router = (moe_router_w[i], moe_router_b[i]) if moe else None
        if moe:
            wg, wu, wd = (moe_w_gate[i].astype(BF16), moe_w_up[i].astype(BF16), moe_w_down[i].astype(BF16))
        else:
            wg, wu, wd = (ffn_w_gate[i].astype(BF16), ffn_w_up[i].astype(BF16), ffn_w_down[i].astype(BF16))
        for latent in (False, True):
            x = xs if latent else xp
            b, seq, _ = x.shape
            nbg = b // SEQ_GROUP
            rows = b * seq
            outs = _inproj(x.reshape(rows, D_MODEL), mods_l, norm1_g[l], w_in_bf, q_norm_g[l], k_norm_g[l],
                           latent, seq)
            q, kd, vd = (a.reshape(b, seq, a.shape[-1]) for a in outs[:3])
            u_ssm = outs[3]
            u_pool = outs[4].reshape(b, seq, POOL_W)
            if latent:
                kds = [kd, _dup_heads(cache_k[:, l])]
                vds = [vd, _dup_heads(cache_v[:, l])]
                h0 = jnp.concatenate([
                    state_ssm_re[:, l, 0].reshape(b, SSM_GP), state_ssm_im[:, l, 0].reshape(b, SSM_GP),
                    state_ssm_re[:, l, 1].reshape(b, SSM_GP), state_ssm_im[:, l, 1].reshape(b, SSM_GP)],
                    axis=1).reshape(nbg, SEQ_GROUP, 4 * SSM_GP)
            else:
                kds, vds = [kd], [vd]
                new_k.append(outs[5])
                new_v.append(outs[6])
                h0 = jnp.zeros((nbg, SEQ_GROUP, 4 * SSM_GP), F32)
            attn = _attention(q, kds, vds)
            y_ssm, hf = _ssm(u_ssm, l, bw, cw, lam, h0, ssm_d[l], glu_bf, seq)
            if not latent:
                new_state.append(hf.reshape(b, 2, 2, SSM_GROUPS, SSM_STATE))
            res = _mixout(x, attn, y_ssm.reshape(b, seq, SSM_W), u_pool, mods_l,
                          pool_w[l], pool_scale[l], w_out_bf, norm2_g[l], latent, router)
            x1 = res[0].reshape(rows, D_MODEL)
            if moe:
                route = res[2].reshape(rows, ROUTE_W)
                starts, counts, slots = _route_tables(res[3], res[4], seq)
                f = _moe_experts(res[1].reshape(rows, D_MODEL), route, starts, counts, slots, wg, wu, wd)
                y = _final(x1, f, mods_l, final_g, latent, seq)
            else:
                y = _ffn(res[1].reshape(rows, D_MODEL), x1, mods_l, wg, wu, wd, latent, seq)
            y = y.reshape(b, seq, D_MODEL)
            if latent:
                xs = y
            else:
                xp = y

    def cache_out(parts):
        a = jnp.stack(parts, axis=1).reshape(bp, DEPTH, N_KV_HEADS, HEAD_DIM, lp)
        return a.transpose(0, 1, 4, 2, 3)
    new_cache_k = cache_out(new_k)
    new_cache_v = cache_out(new_v)
    st = jnp.stack(new_state, axis=1)
    return (xp, xs, new_cache_k, new_cache_v, st[:, :, :, 0], st[:, :, :, 1])
```

```python
import functools

import numpy as np
import jax
import jax.numpy as jnp
from jax import lax
from jax.experimental import pallas as pl
from jax.experimental.pallas import tpu as pltpu

D_MODEL = 1024
DEPTH = 2
GRID_W = 64
ATTN_W = 512
HEAD_DIM = 64
N_HEADS = 8
N_KV_HEADS = 2
KV_W = 128
ROPE_THETA = 10000.0
SSM_W = 256
SSM_CH = 16
SSM_GROUPS = 16
SSM_STATE = 64
SSM_GP = SSM_GROUPS * SSM_STATE
POOL_W = 256
POOL_WINDOWS = (2, 4, 8, 16)
POOL_CH = 64
POOL_PAD = max(POOL_WINDOWS) // 2
IN_COLS = ATTN_W + 2 * KV_W + SSM_W + POOL_W
FF_DENSE = 2816
N_EXPERTS = 8
FF_EXPERT = 1408
EPS = 1e-6
LOG2_E = 1.4426950408889634

SEQ_GROUP = 8
SSM_CHUNK = 16
SSM_COLBLK = 512
ROUTE_W = 128
ATTN_ROWS = 512
MIX_ROWS = 1024
MOE_BLOCK = 2048
MOE_TILE = 128
MXU_TILE = 256
MOE_TAIL = FF_EXPERT % MXU_TILE
MOD_ROWS = 16
CTX_ROW = 8
VMEM_LIMIT = 56 * 1024 * 1024

F32 = jnp.float32
BF16 = jnp.bfloat16


def _silu(x):
    return x * jax.nn.sigmoid(x)


def _rms(x32, g):
    return x32 * lax.rsqrt(jnp.mean(x32 * x32, axis=-1, keepdims=True) + EPS) * g


def _mod_kernel(cond_ref, w_ref, b_ref, o_ref):
    s = _silu(cond_ref[...])
    w = w_ref[...]
    s_hi = s.astype(BF16)
    s_lo = (s - s_hi.astype(F32)).astype(BF16)
    w_hi = w.astype(BF16)
    w_lo = (w - w_hi.astype(F32)).astype(BF16)
    o_ref[...] = (jnp.dot(s_hi, w_hi, preferred_element_type=F32)
                  + jnp.dot(s_hi, w_lo, preferred_element_type=F32)
                  + jnp.dot(s_lo, w_hi, preferred_element_type=F32) + b_ref[...])


def _modulation(cond, mod_w, mod_b):
    tn = 3072
    n = 6 * D_MODEL
    return pl.pallas_call(
        _mod_kernel,
        out_shape=jax.ShapeDtypeStruct((DEPTH, MOD_ROWS, n), F32),
        grid=(DEPTH, n // tn),
        in_specs=[
            pl.BlockSpec((MOD_ROWS, D_MODEL), lambda l, j: (0, 0)),
            pl.BlockSpec((None, D_MODEL, tn), lambda l, j: (l, 0, j)),
            pl.BlockSpec((None, 1, tn), lambda l, j: (l, 0, j)),
        ],
        out_specs=pl.BlockSpec((None, MOD_ROWS, tn), lambda l, j: (l, 0, j)),
        compiler_params=pltpu.CompilerParams(
            dimension_semantics=("arbitrary", "arbitrary"), vmem_limit_bytes=VMEM_LIMIT),
        name="modulation",
    )(cond, mod_w, mod_b.reshape(DEPTH, 1, n))


def _ssm_prep_kernel(are_ref, aim_ref, dt_ref, bre_ref, bim_ref, cre_ref, cim_ref, bw_ref, cw_ref, lam_ref):
    row_group = lax.broadcasted_iota(jnp.int32, (SSM_W, SSM_GP), 0) // SSM_CH
    lane_group = lax.broadcasted_iota(jnp.int32, (SSM_W, SSM_GP), 1) // SSM_STATE
    own = row_group == lane_group
    for ld in range(2 * DEPTH):
        l, d = divmod(ld, 2)
        a_re = are_ref[ld:ld + 1, :]
        a_im = aim_ref[ld:ld + 1, :]
        dt = jnp.exp(dt_ref[ld:ld + 1, :])
        mag = jnp.exp(a_re * dt)
        l_re = mag * jnp.cos(a_im * dt)
        l_im = mag * jnp.sin(a_im * dt)
        lam_ref[l, 2 * d] = jnp.broadcast_to(l_re, (SEQ_GROUP, SSM_GP))
        lam_ref[l, 2 * d + 1] = jnp.broadcast_to(l_im, (SEQ_GROUP, SSM_GP))
        x = l_re - 1.0
        den = a_re * a_re + a_im * a_im
        f_re = (x * a_re + l_im * a_im) / den
        f_im = (l_im * a_re - x * a_im) / den
        b_re = bre_ref[ld]
        b_im = bim_ref[ld]
        bbar = (f_re * b_re - f_im * b_im, f_re * b_im + f_im * b_re)
        cmat = (cre_ref[ld], -cim_ref[ld])
        for ri in range(2):
            c0 = (2 * d + ri) * SSM_GP
            tiled = jnp.concatenate([bbar[ri]] * SSM_GROUPS, axis=0)
            bw_ref[l, :, c0:c0 + SSM_GP] = jnp.where(own, tiled, 0.0).astype(BF16)
            cw_ref[l, c0:c0 + SSM_GP, :] = jnp.where(own, cmat[ri], 0.0).T.astype(BF16)


def _ssm_prep(a_re, a_im, log_dt, b_re, b_im, c_re, c_im):
    ld = DEPTH * 2
    are = a_re.reshape(ld, SSM_GP)
    aim = a_im.reshape(ld, SSM_GP)
    dt = jnp.repeat(log_dt.reshape(ld, SSM_GROUPS), SSM_STATE, axis=1)
    to_rows = lambda b: b.reshape(ld, SSM_GROUPS, SSM_STATE, SSM_CH).transpose(0, 3, 1, 2).reshape(ld, SSM_CH, SSM_GP)
    repeat_lanes = lambda c: jnp.tile(c.reshape(ld, SSM_W, SSM_STATE), (1, 1, SSM_GROUPS))
    return pl.pallas_call(
        _ssm_prep_kernel,
        out_shape=(jax.ShapeDtypeStruct((DEPTH, SSM_W, 4 * SSM_GP), BF16),
                   jax.ShapeDtypeStruct((DEPTH, 4 * SSM_GP, SSM_W), BF16),
                   jax.ShapeDtypeStruct((DEPTH, 4, SEQ_GROUP, SSM_GP), F32)),
        compiler_params=pltpu.CompilerParams(vmem_limit_bytes=VMEM_LIMIT),
        name="ssm_prep",
    )(are, aim, dt, to_rows(b_re), to_rows(b_im), repeat_lanes(c_re), repeat_lanes(c_im))


def _inproj_kernel(*refs, latent):
    if latent:
        (x_ref, mod_ref, g_ref, w_ref, qg_ref, kg_ref, sq_ref, sk_ref, cos_ref, sa_ref, sb_ref,
         q_ref, kd_ref, vd_ref, us_ref, up_ref) = refs
    else:
        (x_ref, mod_ref, g_ref, w_ref, qg_ref, kg_ref, sq_ref, sk_ref,
         q_ref, kd_ref, vd_ref, us_ref, up_ref, ko_ref, vo_ref) = refs
    x = x_ref[...]
    mod = mod_ref[...]
    h = _rms(x, g_ref[...]) * (1.0 + mod[1:2]) + mod[0:1]
    p = jnp.dot(h.astype(BF16), w_ref[...], preferred_element_type=F32)
    q = p[:, :ATTN_W]
    k = p[:, ATTN_W:ATTN_W + KV_W]
    v = p[:, ATTN_W + KV_W:ATTN_W + 2 * KV_W]
    q_ms = jnp.dot((q * q).astype(BF16), sq_ref[...], preferred_element_type=F32)
    k_ms = jnp.dot((k * k).astype(BF16), sk_ref[...], preferred_element_type=F32)
    q = q * lax.rsqrt(q_ms + EPS) * qg_ref[...]
    k = k * lax.rsqrt(k_ms + EPS) * kg_ref[...]
    if latent:
        cos = cos_ref[...]
        sa = sa_ref[...]
        sb = sb_ref[...]
        cos4 = jnp.concatenate([cos] * 4, axis=1)
        sa4 = jnp.concatenate([sa] * 4, axis=1)
        sb4 = jnp.concatenate([sb] * 4, axis=1)
        q = (q * cos4 + pltpu.roll(q, ATTN_W - 16, axis=1) * sa4 + pltpu.roll(q, 16, axis=1) * sb4)
        k = (k * cos + pltpu.roll(k, KV_W - 16, axis=1) * sa + pltpu.roll(k, 16, axis=1) * sb)
    else:
        seq = ko_ref.shape[-1]
        for s in range(ko_ref.shape[0]):
            ko_ref[s] = k[s * seq:(s + 1) * seq, :].T
            vo_ref[s] = v[s * seq:(s + 1) * seq, :].T
    q_ref[...] = (q * (HEAD_DIM ** -0.5 * LOG2_E)).astype(BF16)
    lane = lax.broadcasted_iota(jnp.int32, k.shape, 1)
    first = lane < HEAD_DIM
    k_sw = pltpu.roll(k, HEAD_DIM, axis=1)
    v_sw = pltpu.roll(v, HEAD_DIM, axis=1)
    kd_ref[:, :KV_W] = jnp.where(first, k, k_sw).astype(BF16)
    kd_ref[:, KV_W:] = jnp.where(first, k_sw, k).astype(BF16)
    vd_ref[:, :KV_W] = jnp.where(first, v, v_sw).astype(BF16)
    vd_ref[:, KV_W:] = jnp.where(first, v_sw, v).astype(BF16)
    o3 = ATTN_W + 2 * KV_W
    us_ref[0] = p[:, o3:o3 + SSM_W // 2]
    us_ref[1] = p[:, o3 + SSM_W // 2:o3 + SSM_W]
    up_ref[...] = p[:, o3 + SSM_W:]


def _head_mean_matrix(width):
    i = np.arange(width) // HEAD_DIM
    return jnp.asarray((i[:, None] == i[None, :]).astype(np.float32) / HEAD_DIM, dtype=BF16)


def _rope_tables(seq):
    t = np.arange(seq)
    row = (t // GRID_W).astype(np.float64)
    col = (t % GRID_W).astype(np.float64)
    nf = HEAD_DIM // 4
    inv = ROPE_THETA ** (-np.arange(nf, dtype=np.float64) / nf)
    ang_r = row[:, None] * inv[None, :]
    ang_c = col[:, None] * inv[None, :]
    cos = np.concatenate([np.cos(ang_r), np.cos(ang_r), np.cos(ang_c), np.cos(ang_c)], axis=1)
    sin = np.concatenate([np.sin(ang_r), np.sin(ang_r), np.sin(ang_c), np.sin(ang_c)], axis=1)
    lower = np.tile(np.concatenate([np.ones(nf), np.zeros(nf)]), 2)[None, :]
    sa = -sin * lower
    sb = sin * (1.0 - lower)
    tile = lambda a: jnp.asarray(np.tile(a, (1, 2)), dtype=F32)
    return tile(cos), tile(sa), tile(sb)


def _inproj(x, mods_l, g1, w_in_bf, qg, kg, latent, seq):
    rows = x.shape[0]
    tr = 1024
    tiles_per_seq = max(seq // tr, 1)
    mod_row = (lambda i: (i * tr) // seq) if latent else (lambda i: CTX_ROW)
    const = lambda shape: pl.BlockSpec(shape, lambda i: (0,) * len(shape))
    row_spec = lambda width: pl.BlockSpec((tr, width), lambda i: (i, 0))
    in_specs = [
        row_spec(D_MODEL),
        pl.BlockSpec((None, 6, D_MODEL), lambda i: (mod_row(i), 0, 0)),
        const((1, D_MODEL)),
        const((D_MODEL, IN_COLS)),
        const((1, ATTN_W)),
        const((1, KV_W)),
        const((ATTN_W, ATTN_W)),
        const((KV_W, KV_W)),
    ]
    args = [x, mods_l, g1.reshape(1, D_MODEL), w_in_bf,
            jnp.tile(qg, N_HEADS).reshape(1, ATTN_W), jnp.tile(kg, N_KV_HEADS).reshape(1, KV_W),
            _head_mean_matrix(ATTN_W), _head_mean_matrix(KV_W)]
    widths = [(ATTN_W, BF16), (2 * KV_W, BF16), (2 * KV_W, BF16), (None, F32), (POOL_W, F32)]
    if latent:
        assert seq % tr == 0
        cos, sa, sb = _rope_tables(seq)
        in_specs += [pl.BlockSpec((tr, KV_W), lambda i: (i % tiles_per_seq, 0))] * 3
        args += [cos, sa, sb]
    out_shape = [jax.ShapeDtypeStruct((rows, w) if w else (2, rows, SSM_W // 2), dt) for w, dt in widths]
    out_specs = [row_spec(w) if w else pl.BlockSpec((2, tr, SSM_W // 2), lambda i: (0, i, 0)) for w, _ in widths]
    if not latent:
        assert tr % seq == 0
        out_shape += [jax.ShapeDtypeStruct((rows // seq, KV_W, seq), F32)] * 2
        out_specs += [pl.BlockSpec((tr // seq, KV_W, seq), lambda i: (i, 0, 0))] * 2
    return pl.pallas_call(
        functools.partial(_inproj_kernel, latent=latent),
        out_shape=out_shape, grid=(rows // tr,), in_specs=in_specs, out_specs=out_specs,
        compiler_params=pltpu.CompilerParams(
            dimension_semantics=("arbitrary",), vmem_limit_bytes=VMEM_LIMIT),
        name="inproj_latent" if latent else "inproj_context",
    )(*args)


def _attn_kernel(*refs, n_kv_src):
    q_ref = refs[0]
    k_refs = refs[1:1 + n_kv_src]
    v_refs = refs[1 + n_kv_src:1 + 2 * n_kv_src]
    o_ref = refs[1 + 2 * n_kv_src]
    nseq, tq, _ = q_ref.shape
    lane = lax.broadcasted_iota(jnp.int32, (tq, KV_W), 1)
    first = lane < HEAD_DIM
    for sq in range(nseq):
        for pair in range(N_HEADS // 2):
            kv = pair // 2
            cols = slice(pair * KV_W, (pair + 1) * KV_W)
            kcols = slice(kv * KV_W, (kv + 1) * KV_W)
            qp = q_ref[sq, :, cols]
            halves = []
            for half in range(2):
                qm = jnp.where(first if half == 0 else jnp.logical_not(first), qp, jnp.zeros_like(qp))
                scores = [lax.dot_general(qm, k_ref[sq, :, kcols], (((1,), (1,)), ((), ())),
                                          preferred_element_type=F32) for k_ref in k_refs]
                m = scores[0].max(axis=-1, keepdims=True)
                for s in scores[1:]:
                    m = jnp.maximum(m, s.max(axis=-1, keepdims=True))
                den = jnp.zeros((tq, 1), F32)
                acc = jnp.zeros((tq, KV_W), F32)
                for s, v_ref in zip(scores, v_refs):
                    p = jnp.exp2(s - m)
                    den = den + p.sum(axis=-1, keepdims=True)
                    acc = acc + jnp.dot(p.astype(BF16), v_ref[sq, :, kcols], preferred_element_type=F32)
                halves.append(acc / den)
            o_ref[sq, :, cols] = jnp.where(first, halves[0], halves[1]).astype(o_ref.dtype)


def _attention(q, kds, vds):
    b, seq, _ = q.shape
    tq = min(seq, ATTN_ROWS)
    nseq = max(1, 2 * ATTN_ROWS // seq)
    n_src = len(kds)
    kv_spec = lambda a: pl.BlockSpec((nseq, a.shape[1], 2 * KV_W), lambda i, t: (i, 0, 0))
    return pl.pallas_call(
        functools.partial(_attn_kernel, n_kv_src=n_src),
        out_shape=jax.ShapeDtypeStruct((b, seq, ATTN_W), BF16),
        grid=(b // nseq, seq // tq),
        in_specs=[pl.BlockSpec((nseq, tq, ATTN_W), lambda i, t: (i, t, 0))]
                 + [kv_spec(a) for a in kds] + [kv_spec(a) for a in vds],
        out_specs=pl.BlockSpec((nseq, tq, ATTN_W), lambda i, t: (i, t, 0)),
        compiler_params=pltpu.CompilerParams(
            dimension_semantics=("arbitrary", "arbitrary"), vmem_limit_bytes=VMEM_LIMIT),
        name="attention_%dsrc" % n_src,
    )(q, *kds, *vds)


def _ssm_kernel(u_ref, bw_ref, cw_ref, lam_ref, h0_ref, d_ref, glu_ref,
                o_ref, hf_ref, y_ref, utm_scr, bu0_scr, bu1_scr, hs0_scr, hs1_scr, st_scr):
    rows = u_ref.shape[1]
    seq = rows // SEQ_GROUP
    crow = SSM_CHUNK * SEQ_GROUP
    nchunks = rows // crow
    w2 = 2 * SSM_GP
    lanes = SSM_W // 2

    def time_major(c):
        t0 = c * SSM_CHUNK
        return jnp.concatenate(
            [jnp.concatenate([u_ref[h, pl.ds(t0 + s, SEQ_GROUP, stride=seq), :] for h in range(2)], axis=1)
             for s in range(SSM_CHUNK)], axis=0)

    def y_rows(r, n):
        return jnp.concatenate([y_ref[0, pl.ds(r, n), :], y_ref[1, pl.ds(r, n), :]], axis=1)

    def set_y_rows(r, n, val):
        y_ref[0, pl.ds(r, n), :] = val[:, :lanes]
        y_ref[1, pl.ds(r, n), :] = val[:, lanes:]

    st_scr[...] = h0_ref[...]

    def init_rows(i, c):
        r = pl.multiple_of(i * crow, crow)
        tm = time_major(i)
        set_y_rows(r, crow, tm * d_ref[...])
        utm_scr[pl.ds(r, crow), :] = tm.astype(BF16)
        return c
    lax.fori_loop(0, nchunks, init_rows, 0)

    last = nchunks - 1
    bu_scr = (bu0_scr, bu1_scr)
    hs_scr = (hs0_scr, hs1_scr)

    def project_in(i, buf):
        i = jnp.clip(i, 0, last)
        for d, c in ((0, i), (1, last - i)):
            r = pl.multiple_of(c * crow, crow)
            bu_scr[buf][d] = jnp.dot(utm_scr[pl.ds(r, crow), :], bw_ref[:, d * w2:(d + 1) * w2],
                                     preferred_element_type=F32)

    def recur(buf):
        for d in range(2):
            for cb in range(SSM_GP // SSM_COLBLK):
                re_c = slice(cb * SSM_COLBLK, (cb + 1) * SSM_COLBLK)
                im_c = slice(SSM_GP + cb * SSM_COLBLK, SSM_GP + (cb + 1) * SSM_COLBLK)
                l_re = lam_ref[2 * d, :, re_c]
                l_im = lam_ref[2 * d + 1, :, re_c]
                h_re = st_scr[:, d * w2 + cb * SSM_COLBLK:d * w2 + (cb + 1) * SSM_COLBLK]
                h_im = st_scr[:, d * w2 + SSM_GP + cb * SSM_COLBLK:d * w2 + SSM_GP + (cb + 1) * SSM_COLBLK]
                for s in range(SSM_CHUNK):
                    t = s if d == 0 else SSM_CHUNK - 1 - s
                    r = slice(t * SEQ_GROUP, (t + 1) * SEQ_GROUP)
                    n_re = l_re * h_re - l_im * h_im + bu_scr[buf][d, r, re_c]
                    n_im = l_re * h_im + l_im * h_re + bu_scr[buf][d, r, im_c]
                    hs_scr[buf][d, r, re_c] = n_re
                    hs_scr[buf][d, r, im_c] = n_im
                    h_re, h_im = n_re, n_im
                st_scr[:, d * w2 + cb * SSM_COLBLK:d * w2 + (cb + 1) * SSM_COLBLK] = h_re
                st_scr[:, d * w2 + SSM_GP + cb * SSM_COLBLK:d * w2 + SSM_GP + (cb + 1) * SSM_COLBLK] = h_im

    def project_out(i, buf):
        i = jnp.clip(i, 0, last)
        for d, c in ((0, i), (1, last - i)):
            r = pl.multiple_of(c * crow, crow)
            contrib = jnp.dot(hs_scr[buf][d].astype(BF16), cw_ref[d * w2:(d + 1) * w2, :],
                              preferred_element_type=F32)
            set_y_rows(r, crow, y_rows(r, crow) + contrib)

    hs1_scr[...] = jnp.zeros(hs1_scr.shape, F32)
    project_in(0, 0)

    def two_steps(j, c):
        i = 2 * j
        recur(0)
        project_in(i + 1, 1)
        project_out(i - 1, 1)
        recur(1)
        project_in(i + 2, 0)
        project_out(i, 0)
        return c
    lax.fori_loop(0, nchunks // 2, two_steps, 0)
    project_out(last, 1)

    hf_ref[...] = st_scr[...]

    esteps = 128
    erow = esteps * SEQ_GROUP

    def epilogue(i, c):
        r = pl.multiple_of(i * erow, erow)
        z = jax.nn.gelu(y_rows(r, erow))
        g = jnp.dot(z.astype(BF16), glu_ref[...], preferred_element_type=F32)
        set_y_rows(r, erow, g[:, :SSM_W] * jax.nn.sigmoid(g[:, SSM_W:]))
        t0 = pl.multiple_of(i * esteps, esteps)
        for b in range(SEQ_GROUP):
            for h in range(2):
                o_ref[pl.ds(b * seq + t0, esteps), h * lanes:(h + 1) * lanes] = (
                    y_ref[h, pl.ds(r + b, esteps, stride=SEQ_GROUP), :].astype(BF16))
        return c
    lax.fori_loop(0, rows // erow, epilogue, 0)


def _ssm(u, layer, bw, cw, lam, h0, ssm_d, glu_bf, seq):
    rows = u.shape[1]
    grows = SEQ_GROUP * seq
    nbg = rows // grows
    crow = SSM_CHUNK * SEQ_GROUP
    const = lambda shape: pl.BlockSpec(shape, lambda i: (0,) * len(shape))
    return pl.pallas_call(
        _ssm_kernel,
        out_shape=(jax.ShapeDtypeStruct((rows, SSM_W), BF16),
                   jax.ShapeDtypeStruct((nbg, SEQ_GROUP, 4 * SSM_GP), F32)),
        grid=(nbg,),
        in_specs=[
            pl.BlockSpec((2, grows, SSM_W // 2), lambda i: (0, i, 0)),
            pl.BlockSpec((None, SSM_W, 4 * SSM_GP), lambda i: (layer, 0, 0)),
            pl.BlockSpec((None, 4 * SSM_GP, SSM_W), lambda i: (layer, 0, 0)),
            pl.BlockSpec((None, 4, SEQ_GROUP, SSM_GP), lambda i: (layer, 0, 0, 0)),
            pl.BlockSpec((None, SEQ_GROUP, 4 * SSM_GP), lambda i: (i, 0, 0)),
            const((1, SSM_W)),
            const((SSM_W, 2 * SSM_W)),
        ],
        out_specs=(pl.BlockSpec((grows, SSM_W), lambda i: (i, 0)),
                   pl.BlockSpec((None, SEQ_GROUP, 4 * SSM_GP), lambda i: (i, 0, 0))),
        scratch_shapes=[pltpu.VMEM((2, grows, SSM_W // 2), F32),
                        pltpu.VMEM((grows, SSM_W), BF16),
                        pltpu.VMEM((2, crow, 2 * SSM_GP), F32),
                        pltpu.VMEM((2, crow, 2 * SSM_GP), F32),
                        pltpu.VMEM((2, crow, 2 * SSM_GP), F32),
                        pltpu.VMEM((2, crow, 2 * SSM_GP), F32),
                        pltpu.VMEM((SEQ_GROUP, 4 * SSM_GP), F32)],
        compiler_params=pltpu.CompilerParams(
            dimension_semantics=("arbitrary",), vmem_limit_bytes=VMEM_LIMIT),
        name="ssm_scan",
    )(u, bw, cw, lam, h0, ssm_d.reshape(1, SSM_W), glu_bf)


def _pack2(lo, hi):
    return pltpu.pack_elementwise([lo, hi], packed_dtype=BF16)


def _unpack2(w, index):
    return pltpu.unpack_elementwise(w, index=index, packed_dtype=BF16, unpacked_dtype=F32)


def _mixout_kernel(*refs, route):
    if route:
        (x_ref, at_ref, ss_ref, up_ref, mod_ref, invc_ref, pw_ref, ps_ref, wo_ref, g2_ref, wr_ref, br_ref, tri_ref,
         x1_ref, h2_ref, route_ref, rt_ref, cnt_ref) = refs
    else:
        (x_ref, at_ref, ss_ref, up_ref, mod_ref, invc_ref, pw_ref, ps_ref, wo_ref, g2_ref,
         x1_ref, h2_ref) = refs
    nseq, seq, _ = up_ref.shape
    rows = nseq * seq
    zpad = jnp.zeros((POOL_PAD, POOL_W), F32)
    n_ext = seq + 2 * POOL_PAD
    back = lambda a, k: pltpu.roll(a, k, axis=0)
    ahead = lambda a, k: pltpu.roll(a, n_ext - k, axis=0)
    grp = lax.broadcasted_iota(jnp.int32, (n_ext, POOL_W), 1) // POOL_CH
    pooled = []
    for s in range(nseq):
        u = up_ref[s]
        ue = jnp.concatenate([zpad, u, zpad], axis=0)
        w2 = ue + back(ue, 1)
        w4 = back(w2, 1) + ahead(w2, 1)
        w8 = back(w4, 2) + ahead(w4, 2)
        w16 = back(w8, 4) + ahead(w8, 4)
        win = jnp.where(grp == 0, w2, jnp.where(grp == 1, w4, jnp.where(grp == 2, w8, w16)))
        pooled.append(win[POOL_PAD:POOL_PAD + seq] * invc_ref[...] - u)
    pooled = jnp.concatenate(pooled, axis=0)
    pool = jnp.dot(pooled.astype(BF16), pw_ref[...], preferred_element_type=F32) * ps_ref[...]
    mix = jnp.concatenate([at_ref[...].reshape(rows, ATTN_W), ss_ref[...].reshape(rows, SSM_W),
                           pool.astype(BF16)], axis=1)
    o = jnp.dot(mix, wo_ref[...], preferred_element_type=F32)
    mod = mod_ref[...]
    x1 = x_ref[...].reshape(rows, D_MODEL) + mod[2:3] * o
    x1_ref[...] = x1.reshape(nseq, seq, D_MODEL)
    h2 = _rms(x1, g2_ref[...]) * (1.0 + mod[4:5]) + mod[3:4]
    h2_ref[...] = h2.astype(BF16).reshape(nseq, seq, D_MODEL)
    if not route:
        return
    h_hi = h2.astype(BF16)
    h_lo = (h2 - h_hi.astype(F32)).astype(BF16)
    both = jnp.dot(h_hi, wr_ref[...], preferred_element_type=F32)
    logits = (both[:, :ROUTE_W] + both[:, ROUTE_W:]
              + jnp.dot(h_lo, wr_ref[:, :ROUTE_W], preferred_element_type=F32) + br_ref[...])
    lane = lax.broadcasted_iota(jnp.int32, logits.shape, 1).astype(F32)
    neg = float(np.finfo(np.float32).min)
    far = float(ROUTE_W)
    logits = jnp.where(lane < N_EXPERTS, logits, neg)
    m1 = logits.max(axis=-1, keepdims=True)
    i1 = jnp.where(logits == m1, lane, far).min(axis=-1, keepdims=True)
    rest = jnp.where(lane == i1, neg, logits)
    m2 = rest.max(axis=-1, keepdims=True)
    i2 = jnp.where(rest == m2, lane, far).min(axis=-1, keepdims=True)
    e2 = jnp.exp(m2 - m1)
    den = 1.0 + e2
    sel = jnp.where(lane == i1, 1.0, jnp.where(lane == i2, 1.0, 0.0))
    sel_bf = sel.astype(BF16)
    rank = jnp.concatenate([jnp.dot(tri_ref[...], sel_bf[s * seq:(s + 1) * seq], preferred_element_type=F32)
                            for s in range(nseq)], axis=0)
    r1 = jnp.where(lane == i1, rank, 0.0).sum(axis=-1, keepdims=True)
    r2 = jnp.where(lane == i2, rank, 0.0).sum(axis=-1, keepdims=True)
    out = jnp.zeros_like(logits)
    for j, col in enumerate((1.0 / den, e2 / den, i1, i2, r1, r2)):
        out = jnp.where(lane == j, col, out)
    route_ref[...] = out.reshape(nseq, seq, ROUTE_W)
    for s in range(nseq):
        part = slice(s * seq, (s + 1) * seq)
        rt_ref[s] = out[part].T[:8, :]
        cnt_ref[s] = jnp.broadcast_to(sel[part].sum(axis=0, keepdims=True), cnt_ref.shape[1:])


def _pool_inv_count(seq):
    t = np.arange(seq)
    cols = []
    for win in POOL_WINDOWS:
        lo = np.clip(t - win // 2, 0, seq)
        hi = np.clip(t + win // 2, 0, seq)
        cols.append(np.repeat((hi - lo).astype(np.float32)[:, None], POOL_CH, axis=1))
    return np.concatenate(cols, axis=1)


def _block_diag_pool(pool_w):
    eye = jnp.eye(len(POOL_WINDOWS), dtype=F32)
    m = eye[:, None, :, None] * pool_w[:, :, None, :]
    return m.reshape(POOL_W, POOL_W)


def _mixout(x, attn, ssm_tm, u_pool, mods_l, pool_w, pool_scale, w_out_bf, g2, latent, router):
    b, seq, _ = x.shape
    route = router is not None
    nseq = 1 if latent else max(1, MIX_ROWS // seq)
    mod_row = (lambda i: i) if latent else (lambda i: CTX_ROW)
    const = lambda shape: pl.BlockSpec(shape, lambda i: (0,) * len(shape))
    cnt = _pool_inv_count(seq)
    row_spec = lambda width: pl.BlockSpec((nseq, seq, width), lambda i: (i, 0, 0))
    in_specs = [
        row_spec(D_MODEL), row_spec(ATTN_W), row_spec(SSM_W), row_spec(POOL_W),
        pl.BlockSpec((None, 6, D_MODEL), lambda i: (mod_row(i), 0, 0)),
        const((seq, POOL_W)),
        const((POOL_W, POOL_W)),
        const((1, POOL_W)),
        const((D_MODEL, D_MODEL)),
        const((1, D_MODEL)),
    ]
    args = [x, attn, ssm_tm, u_pool, mods_l, jnp.asarray(1.0 / cnt, dtype=F32),
            _block_diag_pool(pool_w).astype(BF16), pool_scale.reshape(1, POOL_W), w_out_bf,
            g2.reshape(1, D_MODEL)]
    if route:
        wr, br = router
        tri = np.tril(np.ones((seq, seq), np.float32), -1)
        wr_pad = jnp.pad(wr, ((0, 0), (0, ROUTE_W - N_EXPERTS)))
        wr_hi = wr_pad.astype(BF16)
        wr_lo = (wr_pad - wr_hi.astype(F32)).astype(BF16)
        in_specs += [const((D_MODEL, 2 * ROUTE_W)), const((1, ROUTE_W)), const((seq, seq))]
        args += [jnp.concatenate([wr_hi, wr_lo], axis=1),
                 jnp.pad(br, (0, ROUTE_W - N_EXPERTS)).reshape(1, ROUTE_W),
                 jnp.asarray(tri, dtype=BF16)]
        out_shape = [jax.ShapeDtypeStruct((b, seq, D_MODEL), F32),
                     jax.ShapeDtypeStruct((b, seq, D_MODEL), BF16),
                     jax.ShapeDtypeStruct((b, seq, ROUTE_W), F32),
                     jax.ShapeDtypeStruct((b, 8, seq), F32),
                     jax.ShapeDtypeStruct((b, 8, ROUTE_W), F32)]
        out_specs = [row_spec(D_MODEL), row_spec(D_MODEL), row_spec(ROUTE_W),
                     pl.BlockSpec((nseq, 8, seq), lambda i: (i, 0, 0)),
                     pl.BlockSpec((nseq, 8, ROUTE_W), lambda i: (i, 0, 0))]
    else:
        out_shape = [jax.ShapeDtypeStruct((b, seq, D_MODEL), F32), jax.ShapeDtypeStruct((b, seq, D_MODEL), BF16)]
        out_specs = [row_spec(D_MODEL), row_spec(D_MODEL)]
    return pl.pallas_call(
        functools.partial(_mixout_kernel, route=route),
        out_shape=out_shape, grid=(b // nseq,), in_specs=in_specs, out_specs=out_specs,
        compiler_params=pltpu.CompilerParams(
            dimension_semantics=("arbitrary",), vmem_limit_bytes=VMEM_LIMIT),
        name="mixout_%s%s" % ("latent" if latent else "context", "_route" if route else ""),
    )(*args)


def _ffn_kernel(h_ref, x_ref, mod_ref, wg_ref, wu_ref, wd_ref, o_ref):
    h = h_ref[...]
    a = _silu(jnp.dot(h, wg_ref[...], preferred_element_type=F32)) * \
        jnp.dot(h, wu_ref[...], preferred_element_type=F32)
    f = jnp.dot(a.astype(BF16), wd_ref[...], preferred_element_type=F32)
    o_ref[...] = x_ref[...] + mod_ref[5:6] * f


def _ffn(h2, x1, mods_l, wg, wu, wd, latent, seq):
    rows = h2.shape[0]
    tm = 512
    ff = wg.shape[1]
    mod_row = (lambda i: (i * tm) // seq) if latent else (lambda i: CTX_ROW)
    resident = lambda shape: pl.BlockSpec(shape, lambda i: (0, 0), pipeline_mode=pl.Buffered(1))
    return pl.pallas_call(
        _ffn_kernel,
        out_shape=jax.ShapeDtypeStruct((rows, D_MODEL), F32),
        grid=(rows // tm,),
        in_specs=[pl.BlockSpec((tm, D_MODEL), lambda i: (i, 0)),
                  pl.BlockSpec((tm, D_MODEL), lambda i: (i, 0)),
                  pl.BlockSpec((None, 6, D_MODEL), lambda i: (mod_row(i), 0, 0)),
                  resident((D_MODEL, ff)), resident((D_MODEL, ff)), resident((ff, D_MODEL))],
        out_specs=pl.BlockSpec((tm, D_MODEL), lambda i: (i, 0)),
        compiler_params=pltpu.CompilerParams(
            dimension_semantics=("arbitrary",), vmem_limit_bytes=VMEM_LIMIT),
        name="ffn_dense",
    )(h2, x1, mods_l, wg, wu, wd)


def _route_tables(route_t, cnt, seq):
    b = route_t.shape[0]
    per_block = MOE_BLOCK // seq
    nb = b // per_block
    c = cnt[:, 0, :N_EXPERTS].astype(jnp.int32).reshape(nb, per_block, N_EXPERTS)
    before = jnp.cumsum(c, axis=1) - c
    total = c.sum(axis=1)
    aligned = (total + 7) // 8 * 8
    starts = jnp.cumsum(aligned, axis=1) - aligned
    base = (starts[:, None, :] + before).reshape(b, N_EXPERTS)
    expert = route_t[:, 2:4, :].astype(jnp.int32)
    rank = route_t[:, 4:6, :].astype(jnp.int32)
    slot = rank
    for e in range(N_EXPERTS):
        slot = slot + jnp.where(expert == e, base[:, e][:, None, None], 0)
    slots = slot.reshape(nb, per_block, 2, seq).transpose(0, 2, 1, 3).reshape(nb, 2, MOE_BLOCK)
    return starts.reshape(-1), total.reshape(-1), slots


MOE_SLOTS = 2 * MOE_BLOCK
MOE_ROWS = MOE_SLOTS + 8 * N_EXPERTS + MOE_TILE


MOE_GROUP = 16


MOE_PLANE = MOE_BLOCK + 8
MOE_SPARE_ROW = MOE_BLOCK
MOE_SPARE_SLOT = MOE_ROWS


def _moe_kernel(starts_ref, counts_ref, slots_ref, h_ref, route_ref, wg_ref, wu_ref, wt_ref, wd_ref, f_ref,
                xs_ref, stage_ref, z_ref, yb_ref, dst_ref, state_ref):
    b = pl.program_id(0)
    e = pl.program_id(1)
    half = D_MODEL // 2

    def send_rows(pending, buf):
        for r in range(MOE_TILE):
            z_ref[pl.ds(dst_ref[pending + r], 1), :] = yb_ref[buf, pl.ds(r, 1), :]

    @pl.when(e == 0)
    def _():
        xs_ref[MOE_SLOTS:, :] = jnp.zeros((MOE_ROWS - MOE_SLOTS, half), jnp.uint32)
        yb_ref[...] = jnp.zeros(yb_ref.shape, jnp.uint32)
        zero_row = jnp.zeros((1, half), jnp.uint32)
        for g in range(N_EXPERTS):
            end = starts_ref[b * N_EXPERTS + g] + counts_ref[b * N_EXPERTS + g]
            for r in range(7):
                xs_ref[pl.ds(end + r, 1), :] = zero_row
                dst_ref[end + r] = MOE_SPARE_ROW

        def spare(i, c):
            dst_ref[MOE_SLOTS + i] = MOE_SPARE_ROW
            return c
        lax.fori_loop(0, MOE_ROWS + MOE_TILE - MOE_SLOTS, spare, 0)

        def put(g, c):
            t0 = pl.multiple_of(g * MOE_GROUP, MOE_GROUP)
            hb = h_ref[pl.ds(t0, MOE_GROUP), :].astype(F32)
            stage_ref[...] = _pack2(hb[:, :half], hb[:, half:])
            for r in range(MOE_GROUP):
                row = stage_ref[r:r + 1, :]
                s0 = slots_ref[0, t0 + r]
                s1 = slots_ref[1, t0 + r]
                xs_ref[pl.ds(s0, 1), :] = row
                xs_ref[pl.ds(s1, 1), :] = row
                dst_ref[s0] = t0 + r
                dst_ref[s1] = t0 + r + MOE_PLANE
            return c
        lax.fori_loop(0, MOE_BLOCK // MOE_GROUP, put, 0)
        state_ref[0] = 0
        state_ref[1] = MOE_SPARE_SLOT

    start = starts_ref[b * N_EXPERTS + e]
    count = counts_ref[b * N_EXPERTS + e]

    def tile(j, carry):
        buf, pending = carry
        send_rows(pending, 1 - buf)
        s = pl.multiple_of(start + j * MOE_TILE, 8)
        xg = xs_ref[pl.ds(s, MOE_TILE), :]
        x_lo = _unpack2(xg, 0).astype(BF16)
        x_hi = _unpack2(xg, 1).astype(BF16)
        main = FF_EXPERT - MOE_TAIL

        def proj(w_ref, cols):
            return (jnp.dot(x_lo, w_ref[:half, cols], preferred_element_type=F32)
                    + jnp.dot(x_hi, w_ref[half:, cols], preferred_element_type=F32))
        tail = proj(wt_ref, slice(None))
        rest = tail.shape[1] // 2
        a = jnp.concatenate([_silu(proj(wg_ref, slice(0, main))) * proj(wu_ref, slice(0, main)),
                             _silu(tail[:, :rest]) * tail[:, rest:]], axis=1).astype(BF16)
        y = jnp.dot(a, wd_ref[...], preferred_element_type=F32)
        yb_ref[buf] = _pack2(y[:, :half], y[:, half:])
        return 1 - buf, s
    buf, pending = lax.fori_loop(0, (count + MOE_TILE - 1) // MOE_TILE, tile, (state_ref[0], state_ref[1]))
    state_ref[0] = buf
    state_ref[1] = pending

    @pl.when(e == pl.num_programs(1) - 1)
    def _():
        send_rows(pending, 1 - buf)
        rows = 256

        def blend(i, c):
            t0 = pl.multiple_of(i * rows, rows)
            z0 = z_ref[pl.ds(t0, rows), :]
            z1 = z_ref[pl.ds(pl.multiple_of(MOE_PLANE + t0, 8), rows), :]
            route = route_ref[pl.ds(t0, rows), :]
            w1 = route[:, 0:1]
            w2 = route[:, 1:2]
            f_ref[pl.ds(t0, rows), :half] = (w1 * _unpack2(z0, 0) + w2 * _unpack2(z1, 0)).astype(BF16)
            f_ref[pl.ds(t0, rows), half:] = (w1 * _unpack2(z0, 1) + w2 * _unpack2(z1, 1)).astype(BF16)
            return c
        lax.fori_loop(0, MOE_BLOCK // rows, blend, 0)


def _moe_experts(h2, route, starts, counts, slots, wg, wu, wd):
    rows = h2.shape[0]
    nb = rows // MOE_BLOCK
    ff = wd.shape[1]
    wt = jnp.concatenate([wg[:, :, ff - MOE_TAIL:], wu[:, :, ff - MOE_TAIL:]], axis=2)
    grid_spec = pltpu.PrefetchScalarGridSpec(
        num_scalar_prefetch=2,
        grid=(nb, N_EXPERTS),
        in_specs=[
            pl.BlockSpec((None, 2, MOE_BLOCK), lambda b, e, st, ct: (b, 0, 0), memory_space=pltpu.SMEM),
            pl.BlockSpec((MOE_BLOCK, D_MODEL), lambda b, e, st, ct: (b, 0), pipeline_mode=pl.Buffered(1)),
            pl.BlockSpec((MOE_BLOCK, ROUTE_W), lambda b, e, st, ct: (b, 0), pipeline_mode=pl.Buffered(1)),
            pl.BlockSpec((None, D_MODEL, ff), lambda b, e, st, ct: (e, 0, 0)),
            pl.BlockSpec((None, D_MODEL, ff), lambda b, e, st, ct: (e, 0, 0)),
            pl.BlockSpec((None, D_MODEL, 2 * MOE_TAIL), lambda b, e, st, ct: (e, 0, 0)),
            pl.BlockSpec((None, ff, D_MODEL), lambda b, e, st, ct: (e, 0, 0)),
        ],
        out_specs=pl.BlockSpec((MOE_BLOCK, D_MODEL), lambda b, e, st, ct: (b, 0)),
        scratch_shapes=[pltpu.VMEM((MOE_ROWS, D_MODEL // 2), jnp.uint32),
                        pltpu.VMEM((MOE_GROUP, D_MODEL // 2), jnp.uint32),
                        pltpu.VMEM((2 * MOE_PLANE, D_MODEL // 2), jnp.uint32),
                        pltpu.VMEM((2, MOE_TILE, D_MODEL // 2), jnp.uint32),
                        pltpu.SMEM((MOE_ROWS + MOE_TILE,), jnp.int32),
                        pltpu.SMEM((2,), jnp.int32)],
    )
    return pl.pallas_call(
        _moe_kernel,
        out_shape=jax.ShapeDtypeStruct((rows, D_MODEL), BF16),
        grid_spec=grid_spec,
        compiler_params=pltpu.CompilerParams(
            dimension_semantics=("arbitrary", "arbitrary"), vmem_limit_bytes=VMEM_LIMIT),
        name="moe_experts",
    )(starts, counts, slots, h2, route, wg, wu, wt, wd)


def _final_kernel(x_ref, f_ref, mod_ref, fg_ref, o_ref):
    y = x_ref[...] + mod_ref[5:6] * f_ref[...].astype(F32)
    o_ref[...] = _rms(y, fg_ref[...])


def _final(x1, f, mods_l, final_g, latent, seq):
    rows = x1.shape[0]
    tr = 1024
    mod_row = (lambda i: (i * tr) // seq) if latent else (lambda i: CTX_ROW)
    return pl.pallas_call(
        _final_kernel,
        out_shape=jax.ShapeDtypeStruct((rows, D_MODEL), F32),
        grid=(rows // tr,),
        in_specs=[pl.BlockSpec((tr, D_MODEL), lambda i: (i, 0)),
                  pl.BlockSpec((tr, D_MODEL), lambda i: (i, 0)),
                  pl.BlockSpec((None, 6, D_MODEL), lambda i: (mod_row(i), 0, 0)),
                  pl.BlockSpec((1, D_MODEL), lambda i: (0, 0))],
        out_specs=pl.BlockSpec((tr, D_MODEL), lambda i: (i, 0)),
        compiler_params=pltpu.CompilerParams(
            dimension_semantics=("arbitrary",), vmem_limit_bytes=VMEM_LIMIT),
        name="final_norm",
    )(x1, f, mods_l, final_g.reshape(1, D_MODEL))


def _dup_heads(a):
    b, n = a.shape[:2]
    return jnp.repeat(a, 2, axis=2).reshape(b, n, 2 * KV_W).astype(BF16)


def kernel(x_prompt, x_sample, c, cache_k, cache_v, state_ssm_re, state_ssm_im, c_ctx, mod_w, mod_b, norm1_g, norm2_g, w_in, w_out, q_norm_g, k_norm_g, ssm_a_re, ssm_a_im, ssm_log_dt, ssm_b_re, ssm_b_im, ssm_c_re, ssm_c_im, ssm_d, ssm_glu_w, pool_w, pool_scale, ffn_w_gate, ffn_w_up, ffn_w_down, moe_router_w, moe_router_b, moe_w_gate, moe_w_up, moe_w_down, final_g):
    bp, lp, _ = x_prompt.shape
    bs, ls, _ = x_sample.shape
    assert bs == SEQ_GROUP and bp % SEQ_GROUP == 0

    cond = jnp.zeros((MOD_ROWS, D_MODEL), F32).at[:bs].set(c).at[CTX_ROW].set(c_ctx)
    mods = _modulation(cond, mod_w, mod_b).reshape(DEPTH, MOD_ROWS, 6, D_MODEL)

    bw, cw, lam = _ssm_prep(ssm_a_re, ssm_a_im, ssm_log_dt, ssm_b_re, ssm_b_im, ssm_c_re, ssm_c_im)

    xp, xs = x_prompt, x_sample
    new_k, new_v, new_state = [], [], []
    for l in range(DEPTH):
        mods_l = mods[l]
        w_in_bf = w_in[l].astype(BF16)
        w_out_bf = w_out[l].astype(BF16)
        glu_bf = ssm_glu_w[l].astype(BF16)
        moe = l % 2 == 1
        i = l // 2
        router = (moe_router_w[i], moe_router_b[i]) if moe else None
        if moe:
            wg, wu, wd = (moe_w_gate[i].astype(BF16), moe_w_up[i].astype(BF16), moe_w_down[i].astype(BF16))
        else:
            wg, wu, wd = (ffn_w_gate[i].astype(BF16), ffn_w_up[i].astype(BF16), ffn_w_down[i].astype(BF16))
        for latent in (False, True):
            x = xs if latent else xp
            b, seq, _ = x.shape
            nbg = b // SEQ_GROUP
            rows = b * seq
            outs = _inproj(x.reshape(rows, D_MODEL), mods_l, norm1_g[l], w_in_bf, q_norm_g[l], k_norm_g[l],
                           latent, seq)
            q, kd, vd = (a.reshape(b, seq, a.shape[-1]) for a in outs[:3])
            u_ssm = outs[3]
            u_pool = outs[4].reshape(b, seq, POOL_W)
            if latent:
                kds = [kd, _dup_heads(cache_k[:, l])]
                vds = [vd, _dup_heads(cache_v[:, l])]
                h0 = jnp.concatenate([
                    state_ssm_re[:, l, 0].reshape(b, SSM_GP), state_ssm_im[:, l, 0].reshape(b, SSM_GP),
                    state_ssm_re[:, l, 1].reshape(b, SSM_GP), state_ssm_im[:, l, 1].reshape(b, SSM_GP)],
                    axis=1).reshape(nbg, SEQ_GROUP, 4 * SSM_GP)
            else:
                kds, vds = [kd], [vd]
                new_k.append(outs[5])
                new_v.append(outs[6])
                h0 = jnp.zeros((nbg, SEQ_GROUP, 4 * SSM_GP), F32)
            attn = _attention(q, kds, vds)
            y_ssm, hf = _ssm(u_ssm, l, bw, cw, lam, h0, ssm_d[l], glu_bf, seq)
            if not latent:
                new_state.append(hf.reshape(b, 2, 2, SSM_GROUPS, SSM_STATE))
            res = _mixout(x, attn, y_ssm.reshape(b, seq, SSM_W), u_pool, mods_l,
                          pool_w[l], pool_scale[l], w_out_bf, norm2_g[l], latent, router)
            x1 = res[0].reshape(rows, D_MODEL)
            if moe:
                route = res[2].reshape(rows, ROUTE_W)
                starts, counts, slots = _route_tables(res[3], res[4], seq)
                f = _moe_experts(res[1].reshape(rows, D_MODEL), route, starts, counts, slots, wg, wu, wd)
                y = _final(x1, f, mods_l, final_g, latent, seq)
            else:
                y = _ffn(res[1].reshape(rows, D_MODEL), x1, mods_l, wg, wu, wd, latent, seq)
            y = y.reshape(b, seq, D_MODEL)
            if latent:
                xs = y
            else:
                xp = y

    def cache_out(parts):
        a = jnp.stack(parts, axis=1).reshape(bp, DEPTH, N_KV_HEADS, HEAD_DIM, lp)
        return a.transpose(0, 1, 4, 2, 3)
    new_cache_k = cache_out(new_k)
    new_cache_v = cache_out(new_v)
    st = jnp.stack(new_state, axis=1)
    return (xp, xs, new_cache_k, new_cache_v, st[:, :, :, 0], st[:, :, :, 1])
```

```python
import functools

import numpy as np
import jax
import jax.numpy as jnp
from jax import lax
from jax.experimental import pallas as pl
from jax.experimental.pallas import tpu as pltpu

D_MODEL = 1024
DEPTH = 2
GRID_W = 64
ATTN_W = 512
HEAD_DIM = 64
N_HEADS = 8
N_KV_HEADS = 2
KV_W = 128
ROPE_THETA = 10000.0
SSM_W = 256
SSM_CH = 16
SSM_GROUPS = 16
SSM_STATE = 64
SSM_GP = SSM_GROUPS * SSM_STATE
POOL_W = 256
POOL_WINDOWS = (2, 4, 8, 16)
POOL_CH = 64
POOL_PAD = max(POOL_WINDOWS) // 2
IN_COLS = ATTN_W + 2 * KV_W + SSM_W + POOL_W
FF_DENSE = 2816
N_EXPERTS = 8
FF_EXPERT = 1408
EPS = 1e-6
LOG2_E = 1.4426950408889634

SEQ_GROUP = 8
SSM_CHUNK = 16
SSM_COLBLK = 512
ROUTE_W = 128
ATTN_ROWS = 512
MIX_ROWS = 1024
MOE_BLOCK = 2048
MOE_TILE = 128
MXU_TILE = 256
MOE_TAIL = FF_EXPERT % MXU_TILE
MOD_ROWS = 16
CTX_ROW = 8
VMEM_LIMIT = 56 * 1024 * 1024

F32 = jnp.float32
BF16 = jnp.bfloat16


def _silu(x):
    return x * jax.nn.sigmoid(x)


def _rms(x32, g):
    return x32 * lax.rsqrt(jnp.mean(x32 * x32, axis=-1, keepdims=True) + EPS) * g


def _mod_kernel(cond_ref, w_ref, b_ref, o_ref):
    s = _silu(cond_ref[...])
    w = w_ref[...]
    s_hi = s.astype(BF16)
    s_lo = (s - s_hi.astype(F32)).astype(BF16)
    w_hi = w.astype(BF16)
    w_lo = (w - w_hi.astype(F32)).astype(BF16)
    o_ref[...] = (jnp.dot(s_hi, w_hi, preferred_element_type=F32)
                  + jnp.dot(s_hi, w_lo, preferred_element_type=F32)
                  + jnp.dot(s_lo, w_hi, preferred_element_type=F32) + b_ref[...])


def _modulation(cond, mod_w, mod_b):
    tn = 3072
    n = 6 * D_MODEL
    return pl.pallas_call(
        _mod_kernel,
        out_shape=jax.ShapeDtypeStruct((DEPTH, MOD_ROWS, n), F32),
        grid=(DEPTH, n // tn),
        in_specs=[
            pl.BlockSpec((MOD_ROWS, D_MODEL), lambda l, j: (0, 0)),
            pl.BlockSpec((None, D_MODEL, tn), lambda l, j: (l, 0, j)),
            pl.BlockSpec((None, 1, tn), lambda l, j: (l, 0, j)),
        ],
        out_specs=pl.BlockSpec((None, MOD_ROWS, tn), lambda l, j: (l, 0, j)),
        compiler_params=pltpu.CompilerParams(
            dimension_semantics=("arbitrary", "arbitrary"), vmem_limit_bytes=VMEM_LIMIT),
        name="modulation",
    )(cond, mod_w, mod_b.reshape(DEPTH, 1, n))


def _ssm_prep_kernel(are_ref, aim_ref, dt_ref, bre_ref, bim_ref, cre_ref, cim_ref, bw_ref, cw_ref, lam_ref):
    row_group = lax.broadcasted_iota(jnp.int32, (SSM_W, SSM_GP), 0) // SSM_CH
    lane_group = lax.broadcasted_iota(jnp.int32, (SSM_W, SSM_GP), 1) // SSM_STATE
    own = row_group == lane_group
    for ld in range(2 * DEPTH):
        l, d = divmod(ld, 2)
        a_re = are_ref[ld:ld + 1, :]
        a_im = aim_ref[ld:ld + 1, :]
        dt = jnp.exp(dt_ref[ld:ld + 1, :])
        mag = jnp.exp(a_re * dt)
        l_re = mag * jnp.cos(a_im * dt)
        l_im = mag * jnp.sin(a_im * dt)
        lam_ref[l, 2 * d] = jnp.broadcast_to(l_re, (SEQ_GROUP, SSM_GP))
        lam_ref[l, 2 * d + 1] = jnp.broadcast_to(l_im, (SEQ_GROUP, SSM_GP))
        x = l_re - 1.0
        den = a_re * a_re + a_im * a_im
        f_re = (x * a_re + l_im * a_im) / den
        f_im = (l_im * a_re - x * a_im) / den
        b_re = bre_ref[ld]
        b_im = bim_ref[ld]
        bbar = (f_re * b_re - f_im * b_im, f_re * b_im + f_im * b_re)
        cmat = (cre_ref[ld], -cim_ref[ld])
        for ri in range(2):
            c0 = (2 * d + ri) * SSM_GP
            tiled = jnp.concatenate([bbar[ri]] * SSM_GROUPS, axis=0)
            bw_ref[l, :, c0:c0 + SSM_GP] = jnp.where(own, tiled, 0.0).astype(BF16)
            cw_ref[l, c0:c0 + SSM_GP, :] = jnp.where(own, cmat[ri], 0.0).T.astype(BF16)


def _ssm_prep(a_re, a_im, log_dt, b_re, b_im, c_re, c_im):
    ld = DEPTH * 2
    are = a_re.reshape(ld, SSM_GP)
    aim = a_im.reshape(ld, SSM_GP)
    dt = jnp.repeat(log_dt.reshape(ld, SSM_GROUPS), SSM_STATE, axis=1)
    to_rows = lambda b: b.reshape(ld, SSM_GROUPS, SSM_STATE, SSM_CH).transpose(0, 3, 1, 2).reshape(ld, SSM_CH, SSM_GP)
    repeat_lanes = lambda c: jnp.tile(c.reshape(ld, SSM_W, SSM_STATE), (1, 1, SSM_GROUPS))
    return pl.pallas_call(
        _ssm_prep_kernel,
        out_shape=(jax.ShapeDtypeStruct((DEPTH, SSM_W, 4 * SSM_GP), BF16),
                   jax.ShapeDtypeStruct((DEPTH, 4 * SSM_GP, SSM_W), BF16),
                   jax.ShapeDtypeStruct((DEPTH, 4, SEQ_GROUP, SSM_GP), F32)),
        compiler_params=pltpu.CompilerParams(vmem_limit_bytes=VMEM_LIMIT),
        name="ssm_prep",
    )(are, aim, dt, to_rows(b_re), to_rows(b_im), repeat_lanes(c_re), repeat_lanes(c_im))


def _inproj_kernel(*refs, latent):
    if latent:
        (x_ref, mod_ref, g_ref, w_ref, qg_ref, kg_ref, sq_ref, sk_ref, cos_ref, sa_ref, sb_ref,
         q_ref, kd_ref, vd_ref, us_ref, up_ref, wbf_ref) = refs
    else:
        (x_ref, mod_ref, g_ref, w_ref, qg_ref, kg_ref, sq_ref, sk_ref,
         q_ref, kd_ref, vd_ref, us_ref, up_ref, ko_ref, vo_ref, wbf_ref) = refs

    @pl.when(pl.program_id(0) == 0)
    def _():
        wbf_ref[...] = w_ref[...].astype(BF16)

    x = x_ref[...]
    mod = mod_ref[...]
    h = _rms(x, g_ref[...]) * (1.0 + mod[1:2]) + mod[0:1]
    p = jnp.dot(h.astype(BF16), wbf_ref[...], preferred_element_type=F32)
    q = p[:, :ATTN_W]
    k = p[:, ATTN_W:ATTN_W + KV_W]
    v = p[:, ATTN_W + KV_W:ATTN_W + 2 * KV_W]
    q_ms = jnp.dot((q * q).astype(BF16), sq_ref[...], preferred_element_type=F32)
    k_ms = jnp.dot((k * k).astype(BF16), sk_ref[...], preferred_element_type=F32)
    q = q * lax.rsqrt(q_ms + EPS) * qg_ref[...]
    k = k * lax.rsqrt(k_ms + EPS) * kg_ref[...]
    if latent:
        cos = cos_ref[...]
        sa = sa_ref[...]
        sb = sb_ref[...]
        cos4 = jnp.concatenate([cos] * 4, axis=1)
        sa4 = jnp.concatenate([sa] * 4, axis=1)
        sb4 = jnp.concatenate([sb] * 4, axis=1)
        q = (q * cos4 + pltpu.roll(q, ATTN_W - 16, axis=1) * sa4 + pltpu.roll(q, 16, axis=1) * sb4)
        k = (k * cos + pltpu.roll(k, KV_W - 16, axis=1) * sa + pltpu.roll(k, 16, axis=1) * sb)
    else:
        seq = ko_ref.shape[-1]
        for s in range(ko_ref.shape[0]):
            ko_ref[s] = k[s * seq:(s + 1) * seq, :].T
            vo_ref[s] = v[s * seq:(s + 1) * seq, :].T
    q_ref[...] = (q * (HEAD_DIM ** -0.5 * LOG2_E)).astype(BF16)
    lane = lax.broadcasted_iota(jnp.int32, k.shape, 1)
    first = lane < HEAD_DIM
    k_sw = pltpu.roll(k, HEAD_DIM, axis=1)
    v_sw = pltpu.roll(v, HEAD_DIM, axis=1)
    kd_ref[:, :KV_W] = jnp.where(first, k, k_sw).astype(BF16)
    kd_ref[:, KV_W:] = jnp.where(first, k_sw, k).astype(BF16)
    vd_ref[:, :KV_W] = jnp.where(first, v, v_sw).astype(BF16)
    vd_ref[:, KV_W:] = jnp.where(first, v_sw, v).astype(BF16)
    o3 = ATTN_W + 2 * KV_W
    us_ref[0] = p[:, o3:o3 + SSM_W // 2]
    us_ref[1] = p[:, o3 + SSM_W // 2:o3 + SSM_W]
    up_ref[...] = p[:, o3 + SSM_W:]


def _head_mean_matrix(width):
    i = np.arange(width) // HEAD_DIM
    return jnp.asarray((i[:, None] == i[None, :]).astype(np.float32) / HEAD_DIM, dtype=BF16)


def _rope_tables(seq):
    t = np.arange(seq)
    row = (t // GRID_W).astype(np.float64)
    col = (t % GRID_W).astype(np.float64)
    nf = HEAD_DIM // 4
    inv = ROPE_THETA ** (-np.arange(nf, dtype=np.float64) / nf)
    ang_r = row[:, None] * inv[None, :]
    ang_c = col[:, None] * inv[None, :]
    cos = np.concatenate([np.cos(ang_r), np.cos(ang_r), np.cos(ang_c), np.cos(ang_c)], axis=1)
    sin = np.concatenate([np.sin(ang_r), np.sin(ang_r), np.sin(ang_c), np.sin(ang_c)], axis=1)
    lower = np.tile(np.concatenate([np.ones(nf), np.zeros(nf)]), 2)[None, :]
    sa = -sin * lower
    sb = sin * (1.0 - lower)
    tile = lambda a: jnp.asarray(np.tile(a, (1, 2)), dtype=F32)
    return tile(cos), tile(sa), tile(sb)


def _inproj(x, mods_l, g1, w_in, layer, qg, kg, latent, seq):
    rows = x.shape[0]
    tr = 1024
    tiles_per_seq = max(seq // tr, 1)
    mod_row = (lambda i: (i * tr) // seq) if latent else (lambda i: CTX_ROW)
    const = lambda shape: pl.BlockSpec(shape, lambda i: (0,) * len(shape))
    row_spec = lambda width: pl.BlockSpec((tr, width), lambda i: (i, 0))
    in_specs = [
        row_spec(D_MODEL),
        pl.BlockSpec((None, 6, D_MODEL), lambda i: (mod_row(i), 0, 0)),
        const((1, D_MODEL)),
        pl.BlockSpec((None, D_MODEL, IN_COLS), lambda i: (layer, 0, 0), pipeline_mode=pl.Buffered(1)),
        const((1, ATTN_W)),
        const((1, KV_W)),
        const((ATTN_W, ATTN_W)),
        const((KV_W, KV_W)),
    ]
    args = [x, mods_l, g1.reshape(1, D_MODEL), w_in,
            jnp.tile(qg, N_HEADS).reshape(1, ATTN_W), jnp.tile(kg, N_KV_HEADS).reshape(1, KV_W),
            _head_mean_matrix(ATTN_W), _head_mean_matrix(KV_W)]
    widths = [(ATTN_W, BF16), (2 * KV_W, BF16), (2 * KV_W, BF16), (None, F32), (POOL_W, F32)]
    if latent:
        assert seq % tr == 0
        cos, sa, sb = _rope_tables(seq)
        in_specs += [pl.BlockSpec((tr, KV_W), lambda i: (i % tiles_per_seq, 0))] * 3
        args += [cos, sa, sb]
    out_shape = [jax.ShapeDtypeStruct((rows, w) if w else (2, rows, SSM_W // 2), dt) for w, dt in widths]
    out_specs = [row_spec(w) if w else pl.BlockSpec((2, tr, SSM_W // 2), lambda i: (0, i, 0)) for w, _ in widths]
    if not latent:
        assert tr % seq == 0
        out_shape += [jax.ShapeDtypeStruct((rows // seq, KV_W, seq), F32)] * 2
        out_specs += [pl.BlockSpec((tr // seq, KV_W, seq), lambda i: (i, 0, 0))] * 2
    return pl.pallas_call(
        functools.partial(_inproj_kernel, latent=latent),
        out_shape=out_shape, grid=(rows // tr,), in_specs=in_specs, out_specs=out_specs,
        scratch_shapes=[pltpu.VMEM((D_MODEL, IN_COLS), BF16)],
        compiler_params=pltpu.CompilerParams(
            dimension_semantics=("arbitrary",), vmem_limit_bytes=VMEM_LIMIT),
        name="inproj_latent" if latent else "inproj_context",
    )(*args)


def _attn_kernel(*refs, n_kv_src):
    q_ref = refs[0]
    k_refs = refs[1:1 + n_kv_src]
    v_refs = refs[1 + n_kv_src:1 + 2 * n_kv_src]
    o_ref = refs[1 + 2 * n_kv_src]
    nseq, tq, _ = q_ref.shape
    lane = lax.broadcasted_iota(jnp.int32, (tq, KV_W), 1)
    first = lane < HEAD_DIM
    for sq in range(nseq):
        for pair in range(N_HEADS // 2):
            kv = pair // 2
            cols = slice(pair * KV_W, (pair + 1) * KV_W)
            kcols = slice(kv * KV_W, (kv + 1) * KV_W)
            qp = q_ref[sq, :, cols]
            halves = []
            for half in range(2):
                qm = jnp.where(first if half == 0 else jnp.logical_not(first), qp, jnp.zeros_like(qp))
                scores = [lax.dot_general(qm, k_ref[sq, :, kcols], (((1,), (1,)), ((), ())),
                                          preferred_element_type=F32) for k_ref in k_refs]
                m = scores[0].max(axis=-1, keepdims=True)
                for s in scores[1:]:
                    m = jnp.maximum(m, s.max(axis=-1, keepdims=True))
                den = jnp.zeros((tq, 1), F32)
                acc = jnp.zeros((tq, KV_W), F32)
                for s, v_ref in zip(scores, v_refs):
                    p = jnp.exp2(s - m)
                    den = den + p.sum(axis=-1, keepdims=True)
                    acc = acc + jnp.dot(p.astype(BF16), v_ref[sq, :, kcols], preferred_element_type=F32)
                halves.append(acc / den)
            o_ref[sq, :, cols] = jnp.where(first, halves[0], halves[1]).astype(o_ref.dtype)


def _attention(q, kds, vds):
    b, seq, _ = q.shape
    tq = min(seq, ATTN_ROWS)
    nseq = max(1, 2 * ATTN_ROWS // seq)
    n_src = len(kds)
    kv_spec = lambda a: pl.BlockSpec((nseq, a.shape[1], 2 * KV_W), lambda i, t: (i, 0, 0))
    return pl.pallas_call(
        functools.partial(_attn_kernel, n_kv_src=n_src),
        out_shape=jax.ShapeDtypeStruct((b, seq, ATTN_W), BF16),
        grid=(b // nseq, seq // tq),
        in_specs=[pl.BlockSpec((nseq, tq, ATTN_W), lambda i, t: (i, t, 0))]
                 + [kv_spec(a) for a in kds] + [kv_spec(a) for a in vds],
        out_specs=pl.BlockSpec((nseq, tq, ATTN_W), lambda i, t: (i, t, 0)),
        compiler_params=pltpu.CompilerParams(
            dimension_semantics=("arbitrary", "arbitrary"), vmem_limit_bytes=VMEM_LIMIT),
        name="attention_%dsrc" % n_src,
    )(q, *kds, *vds)


def _ssm_kernel(u_ref, bw_ref, cw_ref, lam_ref, h0_ref, d_ref, glu_ref,
                o_ref, hf_ref, y_ref, utm_scr, bu0_scr, bu1_scr, hs0_scr, hs1_scr, st_scr):
    rows = u_ref.shape[1]
    seq = rows // SEQ_GROUP
    crow = SSM_CHUNK * SEQ_GROUP
    nchunks = rows // crow
    w2 = 2 * SSM_GP
    lanes = SSM_W // 2

    def time_major(c):
        t0 = c * SSM_CHUNK
        return jnp.concatenate(
            [jnp.concatenate([u_ref[h, pl.ds(t0 + s, SEQ_GROUP, stride=seq), :] for h in range(2)], axis=1)
             for s in range(SSM_CHUNK)], axis=0)

    def y_rows(r, n):
        return jnp.concatenate([y_ref[0, pl.ds(r, n), :], y_ref[1, pl.ds(r, n), :]], axis=1)

    def set_y_rows(r, n, val):
        y_ref[0, pl.ds(r, n), :] = val[:, :lanes]
        y_ref[1, pl.ds(r, n), :] = val[:, lanes:]

    st_scr[...] = h0_ref[...]

    def init_rows(i, c):
        r = pl.multiple_of(i * crow, crow)
        tm = time_major(i)
        set_y_rows(r, crow, tm * d_ref[...])
        utm_scr[pl.ds(r, crow), :] = tm.astype(BF16)
        return c
    lax.fori_loop(0, nchunks, init_rows, 0)

    last = nchunks - 1
    bu_scr = (bu0_scr, bu1_scr)
    hs_scr = (hs0_scr, hs1_scr)

    def project_in(i, buf):
        i = jnp.clip(i, 0, last)
        for d, c in ((0, i), (1, last - i)):
            r = pl.multiple_of(c * crow, crow)
            bu_scr[buf][d] = jnp.dot(utm_scr[pl.ds(r, crow), :], bw_ref[:, d * w2:(d + 1) * w2],
                                     preferred_element_type=F32)

    def recur(buf):
        for d in range(2):
            for cb in range(SSM_GP // SSM_COLBLK):
                re_c = slice(cb * SSM_COLBLK, (cb + 1) * SSM_COLBLK)
                im_c = slice(SSM_GP + cb * SSM_COLBLK, SSM_GP + (cb + 1) * SSM_COLBLK)
                l_re = lam_ref[2 * d, :, re_c]
                l_im = lam_ref[2 * d + 1, :, re_c]
                h_re = st_scr[:, d * w2 + cb * SSM_COLBLK:d * w2 + (cb + 1) * SSM_COLBLK]
                h_im = st_scr[:, d * w2 + SSM_GP + cb * SSM_COLBLK:d * w2 + SSM_GP + (cb + 1) * SSM_COLBLK]
                for s in range(SSM_CHUNK):
                    t = s if d == 0 else SSM_CHUNK - 1 - s
                    r = slice(t * SEQ_GROUP, (t + 1) * SEQ_GROUP)
                    n_re = l_re * h_re - l_im * h_im + bu_scr[buf][d, r, re_c]
                    n_im = l_re * h_im + l_im * h_re + bu_scr[buf][d, r, im_c]
                    hs_scr[buf][d, r, re_c] = n_re
                    hs_scr[buf][d, r, im_c] = n_im
                    h_re, h_im = n_re, n_im
                st_scr[:, d * w2 + cb * SSM_COLBLK:d * w2 + (cb + 1) * SSM_COLBLK] = h_re
                st_scr[:, d * w2 + SSM_GP + cb * SSM_COLBLK:d * w2 + SSM_GP + (cb + 1) * SSM_COLBLK] = h_im

    def project_out(i, buf):
        i = jnp.clip(i, 0, last)
        for d, c in ((0, i), (1, last - i)):
            r = pl.multiple_of(c * crow, crow)
            contrib = jnp.dot(hs_scr[buf][d].astype(BF16), cw_ref[d * w2:(d + 1) * w2, :],
                              preferred_element_type=F32)
            set_y_rows(r, crow, y_rows(r, crow) + contrib)

    hs1_scr[...] = jnp.zeros(hs1_scr.shape, F32)
    project_in(0, 0)

    def two_steps(j, c):
        i = 2 * j
        recur(0)
        project_in(i + 1, 1)
        project_out(i - 1, 1)
        recur(1)
        project_in(i + 2, 0)
        project_out(i, 0)
        return c
    lax.fori_loop(0, nchunks // 2, two_steps, 0)
    project_out(last, 1)

    hf_ref[...] = st_scr[...]

    esteps = 128
    erow = esteps * SEQ_GROUP

    def epilogue(i, c):
        r = pl.multiple_of(i * erow, erow)
        z = jax.nn.gelu(y_rows(r, erow))
        g = jnp.dot(z.astype(BF16), glu_ref[...], preferred_element_type=F32)
        set_y_rows(r, erow, g[:, :SSM_W] * jax.nn.sigmoid(g[:, SSM_W:]))
        t0 = pl.multiple_of(i * esteps, esteps)
        for b in range(SEQ_GROUP):
            for h in range(2):
                o_ref[pl.ds(b * seq + t0, esteps), h * lanes:(h + 1) * lanes] = (
                    y_ref[h, pl.ds(r + b, esteps, stride=SEQ_GROUP), :].astype(BF16))
        return c
    lax.fori_loop(0, rows // erow, epilogue, 0)


def _ssm(u, layer, bw, cw, lam, h0, ssm_d, glu_bf, seq):
    rows = u.shape[1]
    grows = SEQ_GROUP * seq
    nbg = rows // grows
    crow = SSM_CHUNK * SEQ_GROUP
    const = lambda shape: pl.BlockSpec(shape, lambda i: (0,) * len(shape))
    return pl.pallas_call(
        _ssm_kernel,
        out_shape=(jax.ShapeDtypeStruct((rows, SSM_W), BF16),
                   jax.ShapeDtypeStruct((nbg, SEQ_GROUP, 4 * SSM_GP), F32)),
        grid=(nbg,),
        in_specs=[
            pl.BlockSpec((2, grows, SSM_W // 2), lambda i: (0, i, 0)),
            pl.BlockSpec((None, SSM_W, 4 * SSM_GP), lambda i: (layer, 0, 0)),
            pl.BlockSpec((None, 4 * SSM_GP, SSM_W), lambda i: (layer, 0, 0)),
            pl.BlockSpec((None, 4, SEQ_GROUP, SSM_GP), lambda i: (layer, 0, 0, 0)),
            pl.BlockSpec((None, SEQ_GROUP, 4 * SSM_GP), lambda i: (i, 0, 0)),
            const((1, SSM_W)),
            const((SSM_W, 2 * SSM_W)),
        ],
        out_specs=(pl.BlockSpec((grows, SSM_W), lambda i: (i, 0)),
                   pl.BlockSpec((None, SEQ_GROUP, 4 * SSM_GP), lambda i: (i, 0, 0))),
        scratch_shapes=[pltpu.VMEM((2, grows, SSM_W // 2), F32),
                        pltpu.VMEM((grows, SSM_W), BF16),
                        pltpu.VMEM((2, crow, 2 * SSM_GP), F32),
                        pltpu.VMEM((2, crow, 2 * SSM_GP), F32),
                        pltpu.VMEM((2, crow, 2 * SSM_GP), F32),
                        pltpu.VMEM((2, crow, 2 * SSM_GP), F32),
                        pltpu.VMEM((SEQ_GROUP, 4 * SSM_GP), F32)],
        compiler_params=pltpu.CompilerParams(
            dimension_semantics=("arbitrary",), vmem_limit_bytes=VMEM_LIMIT),
        name="ssm_scan",
    )(u, bw, cw, lam, h0, ssm_d.reshape(1, SSM_W), glu_bf)


def _pack2(lo, hi):
    return pltpu.pack_elementwise([lo, hi], packed_dtype=BF16)


def _unpack2(w, index):
    return pltpu.unpack_elementwise(w, index=index, packed_dtype=BF16, unpacked_dtype=F32)


def _mixout_kernel(*refs, route):
    if route:
        (x_ref, at_ref, ss_ref, up_ref, mod_ref, invc_ref, pw_ref, ps_ref, wo_ref, g2_ref, wr_ref, br_ref, tri_ref,
         x1_ref, h2_ref, route_ref, rt_ref, cnt_ref, wobf_ref) = refs
    else:
        (x_ref, at_ref, ss_ref, up_ref, mod_ref, invc_ref, pw_ref, ps_ref, wo_ref, g2_ref,
         x1_ref, h2_ref, wobf_ref) = refs

    @pl.when(pl.program_id(0) == 0)
    def _():
        wobf_ref[...] = wo_ref[...].astype(BF16)

    nseq, seq, _ = up_ref.shape
    rows = nseq * seq
    zpad = jnp.zeros((POOL_PAD, POOL_W), F32)
    n_ext = seq + 2 * POOL_PAD
    back = lambda a, k: pltpu.roll(a, k, axis=0)
    ahead = lambda a, k: pltpu.roll(a, n_ext - k, axis=0)
    grp = lax.broadcasted_iota(jnp.int32, (n_ext, POOL_W), 1) // POOL_CH
    pooled = []
    for s in range(nseq):
        u = up_ref[s]
        ue = jnp.concatenate([zpad, u, zpad], axis=0)
        w2 = ue + back(ue, 1)
        w4 = back(w2, 1) + ahead(w2, 1)
        w8 = back(w4, 2) + ahead(w4, 2)
        w16 = back(w8, 4) + ahead(w8, 4)
        win = jnp.where(grp == 0, w2, jnp.where(grp == 1, w4, jnp.where(grp == 2, w8, w16)))
        pooled.append(win[POOL_PAD:POOL_PAD + seq] * invc_ref[...] - u)
    pooled = jnp.concatenate(pooled, axis=0)
    pool = jnp.dot(pooled.astype(BF16), pw_ref[...], preferred_element_type=F32) * ps_ref[...]
    mix = jnp.concatenate([at_ref[...].reshape(rows, ATTN_W), ss_ref[...].reshape(rows, SSM_W),
                           pool.astype(BF16)], axis=1)
    o = jnp.dot(mix, wobf_ref[...], preferred_element_type=F32)
    mod = mod_ref[...]
    x1 = x_ref[...].reshape(rows, D_MODEL) + mod[2:3] * o
    x1_ref[...] = x1.reshape(nseq, seq, D_MODEL)
    h2 = _rms(x1, g2_ref[...]) * (1.0 + mod[4:5]) + mod[3:4]
    h2_ref[...] = h2.astype(BF16).reshape(nseq, seq, D_MODEL)
    if not route:
        return
    h_hi = h2.astype(BF16)
    h_lo = (h2 - h_hi.astype(F32)).astype(BF16)
    both = jnp.dot(h_hi, wr_ref[...], preferred_element_type=F32)
    logits = (both[:, :ROUTE_W] + both[:, ROUTE_W:]
              + jnp.dot(h_lo, wr_ref[:, :ROUTE_W], preferred_element_type=F32) + br_ref[...])
    lane = lax.broadcasted_iota(jnp.int32, logits.shape, 1).astype(F32)
    neg = float(np.finfo(np.float32).min)
    far = float(ROUTE_W)
    logits = jnp.where(lane < N_EXPERTS, logits, neg)
    m1 = logits.max(axis=-1, keepdims=True)
    i1 = jnp.where(logits == m1, lane, far).min(axis=-1, keepdims=True)
    rest = jnp.where(lane == i1, neg, logits)
    m2 = rest.max(axis=-1, keepdims=True)
    i2 = jnp.where(rest == m2, lane, far).min(axis=-1, keepdims=True)
    e2 = jnp.exp(m2 - m1)
    den = 1.0 + e2
    sel = jnp.where(lane == i1, 1.0, jnp.where(lane == i2, 1.0, 0.0))
    sel_bf = sel.astype(BF16)
    rank = jnp.concatenate([jnp.dot(tri_ref[...], sel_bf[s * seq:(s + 1) * seq], preferred_element_type=F32)
                            for s in range(nseq)], axis=0)
    r1 = jnp.where(lane == i1, rank, 0.0).sum(axis=-1, keepdims=True)
    r2 = jnp.where(lane == i2, rank, 0.0).sum(axis=-1, keepdims=True)
    out = jnp.zeros_like(logits)
    for j, col in enumerate((1.0 / den, e2 / den, i1, i2, r1, r2)):
        out = jnp.where(lane == j, col, out)
    route_ref[...] = out.reshape(nseq, seq, ROUTE_W)
    for s in range(nseq):
        part = slice(s * seq, (s + 1) * seq)
        rt_ref[s] = out[part].T[:8, :]
        cnt_ref[s] = jnp.broadcast_to(sel[part].sum(axis=0, keepdims=True), cnt_ref.shape[1:])


def _pool_inv_count(seq):
    t = np.arange(seq)
    cols = []
    for win in POOL_WINDOWS:
        lo = np.clip(t - win // 2, 0, seq)
        hi = np.clip(t + win // 2, 0, seq)
        cols.append(np.repeat((hi - lo).astype(np.float32)[:, None], POOL_CH, axis=1))
    return np.concatenate(cols, axis=1)


def _block_diag_pool(pool_w):
    eye = jnp.eye(len(POOL_WINDOWS), dtype=F32)
    m = eye[:, None, :, None] * pool_w[:, :, None, :]
    return m.reshape(POOL_W, POOL_W)


def _mixout(x, attn, ssm_tm, u_pool, mods_l, pool_w, pool_scale, w_out, layer, g2, latent, router):
    b, seq, _ = x.shape
    route = router is not None
    nseq = 1 if latent else max(1, MIX_ROWS // seq)
    mod_row = (lambda i: i) if latent else (lambda i: CTX_ROW)
    const = lambda shape: pl.BlockSpec(shape, lambda i: (0,) * len(shape))
    cnt = _pool_inv_count(seq)
    row_spec = lambda width: pl.BlockSpec((nseq, seq, width), lambda i: (i, 0, 0))
    in_specs = [
        row_spec(D_MODEL), row_spec(ATTN_W), row_spec(SSM_W), row_spec(POOL_W),
        pl.BlockSpec((None, 6, D_MODEL), lambda i: (mod_row(i), 0, 0)),
        const((seq, POOL_W)),
        const((POOL_W, POOL_W)),
        const((1, POOL_W)),
        pl.BlockSpec((None, D_MODEL, D_MODEL), lambda i: (layer, 0, 0), pipeline_mode=pl.Buffered(1)),
        const((1, D_MODEL)),
    ]
    args = [x, attn, ssm_tm, u_pool, mods_l, jnp.asarray(1.0 / cnt, dtype=F32),
            _block_diag_pool(pool_w).astype(BF16), pool_scale.reshape(1, POOL_W), w_out,
            g2.reshape(1, D_MODEL)]
    if route:
        wr, br = router
        tri = np.tril(np.ones((seq, seq), np.float32), -1)
        wr_pad = jnp.pad(wr, ((0, 0), (0, ROUTE_W - N_EXPERTS)))
        wr_hi = wr_pad.astype(BF16)
        wr_lo = (wr_pad - wr_hi.astype(F32)).astype(BF16)
        in_specs += [const((D_MODEL, 2 * ROUTE_W)), const((1, ROUTE_W)), const((seq, seq))]
        args += [jnp.concatenate([wr_hi, wr_lo], axis=1),
                 jnp.pad(br, (0, ROUTE_W - N_EXPERTS)).reshape(1, ROUTE_W),
                 jnp.asarray(tri, dtype=BF16)]
        out_shape = [jax.ShapeDtypeStruct((b, seq, D_MODEL), F32),
                     jax.ShapeDtypeStruct((b, seq, D_MODEL), BF16),
                     jax.ShapeDtypeStruct((b, seq, ROUTE_W), F32),
                     jax.ShapeDtypeStruct((b, 8, seq), F32),
                     jax.ShapeDtypeStruct((b, 8, ROUTE_W), F32)]
        out_specs = [row_spec(D_MODEL), row_spec(D_MODEL), row_spec(ROUTE_W),
                     pl.BlockSpec((nseq, 8, seq), lambda i: (i, 0, 0)),
                     pl.BlockSpec((nseq, 8, ROUTE_W), lambda i: (i, 0, 0))]
    else:
        out_shape = [jax.ShapeDtypeStruct((b, seq, D_MODEL), F32), jax.ShapeDtypeStruct((b, seq, D_MODEL), BF16)]
        out_specs = [row_spec(D_MODEL), row_spec(D_MODEL)]
    return pl.pallas_call(
        functools.partial(_mixout_kernel, route=route),
        out_shape=out_shape, grid=(b // nseq,), in_specs=in_specs, out_specs=out_specs,
        scratch_shapes=[pltpu.VMEM((D_MODEL, D_MODEL), BF16)],
        compiler_params=pltpu.CompilerParams(
            dimension_semantics=("arbitrary",), vmem_limit_bytes=VMEM_LIMIT),
        name="mixout_%s%s" % ("latent" if latent else "context", "_route" if route else ""),
    )(*args)


def _ffn_kernel(h_ref, x_ref, mod_ref, wg_ref, wu_ref, wd_ref, o_ref):
    h = h_ref[...]
    a = _silu(jnp.dot(h, wg_ref[...], preferred_element_type=F32)) * \
        jnp.dot(h, wu_ref[...], preferred_element_type=F32)
    f = jnp.dot(a.astype(BF16), wd_ref[...], preferred_element_type=F32)
    o_ref[...] = x_ref[...] + mod_ref[5:6] * f


def _ffn(h2, x1, mods_l, wg, wu, wd, latent, seq):
    rows = h2.shape[0]
    tm = 512
    ff = wg.shape[1]
    mod_row = (lambda i: (i * tm) // seq) if latent else (lambda i: CTX_ROW)
    resident = lambda shape: pl.BlockSpec(shape, lambda i: (0, 0), pipeline_mode=pl.Buffered(1))
    return pl.pallas_call(
        _ffn_kernel,
        out_shape=jax.ShapeDtypeStruct((rows, D_MODEL), F32),
        grid=(rows // tm,),
        in_specs=[pl.BlockSpec((tm, D_MODEL), lambda i: (i, 0)),
                  pl.BlockSpec((tm, D_MODEL), lambda i: (i, 0)),
                  pl.BlockSpec((None, 6, D_MODEL), lambda i: (mod_row(i), 0, 0)),
                  resident((D_MODEL, ff)), resident((D_MODEL, ff)), resident((ff, D_MODEL))],
        out_specs=pl.BlockSpec((tm, D_MODEL), lambda i: (i, 0)),
        compiler_params=pltpu.CompilerParams(
            dimension_semantics=("arbitrary",), vmem_limit_bytes=VMEM_LIMIT),
        name="ffn_dense",
    )(h2, x1, mods_l, wg, wu, wd)


def _route_tables(route_t, cnt, seq):
    b = route_t.shape[0]
    per_block = MOE_BLOCK // seq
    nb = b // per_block
    c = cnt[:, 0, :N_EXPERTS].astype(jnp.int32).reshape(nb, per_block, N_EXPERTS)
    before = jnp.cumsum(c, axis=1) - c
    total = c.sum(axis=1)
    aligned = (total + 7) // 8 * 8
    starts = jnp.cumsum(aligned, axis=1) - aligned
    base = (starts[:, None, :] + before).reshape(b, N_EXPERTS)
    expert = route_t[:, 2:4, :].astype(jnp.int32)
    rank = route_t[:, 4:6, :].astype(jnp.int32)
    slot = rank
    for e in range(N_EXPERTS):
        slot = slot + jnp.where(expert == e, base[:, e][:, None, None], 0)
    slots = slot.reshape(nb, per_block, 2, seq).transpose(0, 2, 1, 3).reshape(nb, 2, MOE_BLOCK)
    return starts.reshape(-1), total.reshape(-1), slots


MOE_SLOTS = 2 * MOE_BLOCK
MOE_ROWS = MOE_SLOTS + 8 * N_EXPERTS + MOE_TILE


MOE_GROUP = 16


MOE_PLANE = MOE_BLOCK + 8
MOE_SPARE_ROW = MOE_BLOCK
MOE_SPARE_SLOT = MOE_ROWS


def _moe_kernel(starts_ref, counts_ref, slots_ref, h_ref, route_ref, wg_ref, wu_ref, wt_ref, wd_ref, f_ref,
                xs_ref, stage_ref, z_ref, yb_ref, dst_ref, state_ref):
    b = pl.program_id(0)
    e = pl.program_id(1)
    half = D_MODEL // 2

    def send_rows(pending, buf):
        for r in range(MOE_TILE):
            z_ref[pl.ds(dst_ref[pending + r], 1), :] = yb_ref[buf, pl.ds(r, 1), :]

    @pl.when(e == 0)
    def _():
        xs_ref[MOE_SLOTS:, :] = jnp.zeros((MOE_ROWS - MOE_SLOTS, half), jnp.uint32)
        yb_ref[...] = jnp.zeros(yb_ref.shape, jnp.uint32)
        zero_row = jnp.zeros((1, half), jnp.uint32)
        for g in range(N_EXPERTS):
            end = starts_ref[b * N_EXPERTS + g] + counts_ref[b * N_EXPERTS + g]
            for r in range(7):
                xs_ref[pl.ds(end + r, 1), :] = zero_row
                dst_ref[end + r] = MOE_SPARE_ROW

        def spare(i, c):
            dst_ref[MOE_SLOTS + i] = MOE_SPARE_ROW
            return c
        lax.fori_loop(0, MOE_ROWS + MOE_TILE - MOE_SLOTS, spare, 0)

        def put(g, c):
            t0 = pl.multiple_of(g * MOE_GROUP, MOE_GROUP)
            hb = h_ref[pl.ds(t0, MOE_GROUP), :].astype(F32)
            stage_ref[...] = _pack2(hb[:, :half], hb[:, half:])
            for r in range(MOE_GROUP):
                row = stage_ref[r:r + 1, :]
                s0 = slots_ref[0, t0 + r]
                s1 = slots_ref[1, t0 + r]
                xs_ref[pl.ds(s0, 1), :] = row
                xs_ref[pl.ds(s1, 1), :] = row
                dst_ref[s0] = t0 + r
                dst_ref[s1] = t0 + r + MOE_PLANE
            return c
        lax.fori_loop(0, MOE_BLOCK // MOE_GROUP, put, 0)
        state_ref[0] = 0
        state_ref[1] = MOE_SPARE_SLOT

    start = starts_ref[b * N_EXPERTS + e]
    count = counts_ref[b * N_EXPERTS + e]

    def tile(j, carry):
        buf, pending = carry
        send_rows(pending, 1 - buf)
        s = pl.multiple_of(start + j * MOE_TILE, 8)
        xg = xs_ref[pl.ds(s, MOE_TILE), :]
        x_lo = _unpack2(xg, 0).astype(BF16)
        x_hi = _unpack2(xg, 1).astype(BF16)
        main = FF_EXPERT - MOE_TAIL

        def proj(w_ref, cols):
            return (jnp.dot(x_lo, w_ref[:half, cols], preferred_element_type=F32)
                    + jnp.dot(x_hi, w_ref[half:, cols], preferred_element_type=F32))
        tail = proj(wt_ref, slice(None))
        rest = tail.shape[1] // 2
        a = jnp.concatenate([_silu(proj(wg_ref, slice(0, main))) * proj(wu_ref, slice(0, main)),
                             _silu(tail[:, :rest]) * tail[:, rest:]], axis=1).astype(BF16)
        y = jnp.dot(a, wd_ref[...], preferred_element_type=F32)
        yb_ref[buf] = _pack2(y[:, :half], y[:, half:])
        return 1 - buf, s
    buf, pending = lax.fori_loop(0, (count + MOE_TILE - 1) // MOE_TILE, tile, (state_ref[0], state_ref[1]))
    state_ref[0] = buf
    state_ref[1] = pending

    @pl.when(e == pl.num_programs(1) - 1)
    def _():
        send_rows(pending, 1 - buf)
        rows = 256

        def blend(i, c):
            t0 = pl.multiple_of(i * rows, rows)
            z0 = z_ref[pl.ds(t0, rows), :]
            z1 = z_ref[pl.ds(pl.multiple_of(MOE_PLANE + t0, 8), rows), :]
            route = route_ref[pl.ds(t0, rows), :]
            w1 = route[:, 0:1]
            w2 = route[:, 1:2]
            f_ref[pl.ds(t0, rows), :half] = (w1 * _unpack2(z0, 0) + w2 * _unpack2(z1, 0)).astype(BF16)
            f_ref[pl.ds(t0, rows), half:] = (w1 * _unpack2(z0, 1) + w2 * _unpack2(z1, 1)).astype(BF16)
            return c
        lax.fori_loop(0, MOE_BLOCK // rows, blend, 0)


def _moe_experts(h2, route, starts, counts, slots, wg, wu, wd):
    rows = h2.shape[0]
    nb = rows // MOE_BLOCK
    ff = wd.shape[1]
    wt = jnp.concatenate([wg[:, :, ff - MOE_TAIL:], wu[:, :, ff - MOE_TAIL:]], axis=2)
    grid_spec = pltpu.PrefetchScalarGridSpec(
        num_scalar_prefetch=2,
        grid=(nb, N_EXPERTS),
        in_specs=[
            pl.BlockSpec((None, 2, MOE_BLOCK), lambda b, e, st, ct: (b, 0, 0), memory_space=pltpu.SMEM),
            pl.BlockSpec((MOE_BLOCK, D_MODEL), lambda b, e, st, ct: (b, 0), pipeline_mode=pl.Buffered(1)),
            pl.BlockSpec((MOE_BLOCK, ROUTE_W), lambda b, e, st, ct: (b, 0), pipeline_mode=pl.Buffered(1)),
            pl.BlockSpec((None, D_MODEL, ff), lambda b, e, st, ct: (e, 0, 0)),
            pl.BlockSpec((None, D_MODEL, ff), lambda b, e, st, ct: (e, 0, 0)),
            pl.BlockSpec((None, D_MODEL, 2 * MOE_TAIL), lambda b, e, st, ct: (e, 0, 0)),
            pl.BlockSpec((None, ff, D_MODEL), lambda b, e, st, ct: (e, 0, 0)),
        ],
        out_specs=pl.BlockSpec((MOE_BLOCK, D_MODEL), lambda b, e, st, ct: (b, 0)),
        scratch_shapes=[pltpu.VMEM((MOE_ROWS, D_MODEL // 2), jnp.uint32),
                        pltpu.VMEM((MOE_GROUP, D_MODEL // 2), jnp.uint32),
                        pltpu.VMEM((2 * MOE_PLANE, D_MODEL // 2), jnp.uint32),
                        pltpu.VMEM((2, MOE_TILE, D_MODEL // 2), jnp.uint32),
                        pltpu.SMEM((MOE_ROWS + MOE_TILE,), jnp.int32),
                        pltpu.SMEM((2,), jnp.int32)],
    )
    return pl.pallas_call(
        _moe_kernel,
        out_shape=jax.ShapeDtypeStruct((rows, D_MODEL), BF16),
        grid_spec=grid_spec,
        compiler_params=pltpu.CompilerParams(
            dimension_semantics=("arbitrary", "arbitrary"), vmem_limit_bytes=VMEM_LIMIT),
        name="moe_experts",
    )(starts, counts, slots, h2, route, wg, wu, wt, wd)


def _final_kernel(x_ref, f_ref, mod_ref, fg_ref, o_ref):
    y = x_ref[...] + mod_ref[5:6] * f_ref[...].astype(F32)
    o_ref[...] = _rms(y, fg_ref[...])


def _final(x1, f, mods_l, final_g, latent, seq):
    rows = x1.shape[0]
    tr = 1024
    mod_row = (lambda i: (i * tr) // seq) if latent else (lambda i: CTX_ROW)
    return pl.pallas_call(
        _final_kernel,
        out_shape=jax.ShapeDtypeStruct((rows, D_MODEL), F32),
        grid=(rows // tr,),
        in_specs=[pl.BlockSpec((tr, D_MODEL), lambda i: (i, 0)),
                  pl.BlockSpec((tr, D_MODEL), lambda i: (i, 0)),
                  pl.BlockSpec((None, 6, D_MODEL), lambda i: (mod_row(i), 0, 0)),
                  pl.BlockSpec((1, D_MODEL), lambda i: (0, 0))],
        out_specs=pl.BlockSpec((tr, D_MODEL), lambda i: (i, 0)),
        compiler_params=pltpu.CompilerParams(
            dimension_semantics=("arbitrary",), vmem_limit_bytes=VMEM_LIMIT),
        name="final_norm",
    )(x1, f, mods_l, final_g.reshape(1, D_MODEL))


def _dup_heads(a):
    b, n = a.shape[:2]
    return jnp.repeat(a, 2, axis=2).reshape(b, n, 2 * KV_W).astype(BF16)


def kernel(x_prompt, x_sample, c, cache_k, cache_v, state_ssm_re, state_ssm_im, c_ctx, mod_w, mod_b, norm1_g, norm2_g, w_in, w_out, q_norm_g, k_norm_g, ssm_a_re, ssm_a_im, ssm_log_dt, ssm_b_re, ssm_b_im, ssm_c_re, ssm_c_im, ssm_d, ssm_glu_w, pool_w, pool_scale, ffn_w_gate, ffn_w_up, ffn_w_down, moe_router_w, moe_router_b, moe_w_gate, moe_w_up, moe_w_down, final_g):
    bp, lp, _ = x_prompt.shape
    bs, ls, _ = x_sample.shape
    assert bs == SEQ_GROUP and bp % SEQ_GROUP == 0

    cond = jnp.zeros((MOD_ROWS, D_MODEL), F32).at[:bs].set(c).at[CTX_ROW].set(c_ctx)
    mods = _modulation(cond, mod_w, mod_b).reshape(DEPTH, MOD_ROWS, 6, D_MODEL)

    bw, cw, lam = _ssm_prep(ssm_a_re, ssm_a_im, ssm_log_dt, ssm_b_re, ssm_b_im, ssm_c_re, ssm_c_im)

    xp, xs = x_prompt, x_sample
    new_k, new_v, new_state = [], [], []
    for l in range(DEPTH):
        mods_l = mods[l]
        glu_bf = ssm_glu_w[l].astype(BF16)
        moe = l % 2 == 1
        i = l // 2
        router = (moe_router_w[i], moe_router_b[i]) if moe else None
        if moe:
            wg, wu, wd = (moe_w_gate[i].astype(BF16), moe_w_up[i].astype(BF16), moe_w_down[i].astype(BF16))
        else:
            wg, wu, wd = (ffn_w_gate[i].astype(BF16), ffn_w_up[i].astype(BF16), ffn_w_down[i].astype(BF16))
        for latent in (False, True):
            x = xs if latent else xp
            b, seq, _ = x.shape
            nbg = b // SEQ_GROUP
            rows = b * seq
            outs = _inproj(x.reshape(rows, D_MODEL), mods_l, norm1_g[l], w_in, l, q_norm_g[l], k_norm_g[l],
                           latent, seq)
            q, kd, vd = (a.reshape(b, seq, a.shape[-1]) for a in outs[:3])
            u_ssm = outs[3]
            u_pool = outs[4].reshape(b, seq, POOL_W)
            if latent:
                kds = [kd, _dup_heads(cache_k[:, l])]
                vds = [vd, _dup_heads(cache_v[:, l])]
                h0 = jnp.concatenate([
                    state_ssm_re[:, l, 0].reshape(b, SSM_GP), state_ssm_im[:, l, 0].reshape(b, SSM_GP),
                    state_ssm_re[:, l, 1].reshape(b, SSM_GP), state_ssm_im[:, l, 1].reshape(b, SSM_GP)],
                    axis=1).reshape(nbg, SEQ_GROUP, 4 * SSM_GP)
            else:
                kds, vds = [kd], [vd]
                new_k.append(outs[5])
                new_v.append(outs[6])
                h0 = jnp.zeros((nbg, SEQ_GROUP, 4 * SSM_GP), F32)
            attn = _attention(q, kds, vds)
            y_ssm, hf = _ssm(u_ssm, l, bw, cw, lam, h0, ssm_d[l], glu_bf, seq)
            if not latent:
                new_state.append(hf.reshape(b, 2, 2, SSM_GROUPS, SSM_STATE))
            res = _mixout(x, attn, y_ssm.reshape(b, seq, SSM_W), u_pool, mods_l,
                          pool_w[l], pool_scale[l], w_out, l, norm2_g[l], latent, router)
            x1 = res[0].reshape(rows, D_MODEL)
            if moe:
                route = res[2].reshape(rows, ROUTE_W)
                starts, counts, slots = _route_tables(res[3], res[4], seq)
                f = _moe_experts(res[1].reshape(rows, D_MODEL), route, starts, counts, slots, wg, wu, wd)
                y = _final(x1, f, mods_l, final_g, latent, seq)
            else:
                y = _ffn(res[1].reshape(rows, D_MODEL), x1, mods_l, wg, wu, wd, latent, seq)
            y = y.reshape(b, seq, D_MODEL)
            if latent:
                xs = y
            else:
                xp = y

    def cache_out(parts):
        a = jnp.stack(parts, axis=1).reshape(bp, DEPTH, N_KV_HEADS, HEAD_DIM, lp)
        return a.transpose(0, 1, 4, 2, 3)
    new_cache_k = cache_out(new_k)
    new_cache_v = cache_out(new_v)
    st = jnp.stack(new_state, axis=1)
    return (xp, xs, new_cache_k, new_cache_v, st[:, :, :, 0], st[:, :, :, 1])
```

```python
import functools

import numpy as np
import jax
import jax.numpy as jnp
from jax import lax
from jax.experimental import pallas as pl
from jax.experimental.pallas import tpu as pltpu

D_MODEL = 1024
DEPTH = 2
GRID_W = 64
ATTN_W = 512
HEAD_DIM = 64
N_HEADS = 8
N_KV_HEADS = 2
KV_W = 128
ROPE_THETA = 10000.0
SSM_W = 256
SSM_CH = 16
SSM_GROUPS = 16
SSM_STATE = 64
SSM_GP = SSM_GROUPS * SSM_STATE
POOL_W = 256
POOL_WINDOWS = (2, 4, 8, 16)
POOL_CH = 64
POOL_PAD = max(POOL_WINDOWS) // 2
IN_COLS = ATTN_W + 2 * KV_W + SSM_W + POOL_W
FF_DENSE = 2816
N_EXPERTS = 8
FF_EXPERT = 1408
EPS = 1e-6
LOG2_E = 1.4426950408889634

SEQ_GROUP = 8
SSM_CHUNK = 16
SSM_COLBLK = 512
ROUTE_W = 128
ATTN_ROWS = 512
MIX_ROWS = 1024
MOE_BLOCK = 2048
MOE_TILE = 128
MXU_TILE = 256
MOE_TAIL = FF_EXPERT % MXU_TILE
MOD_ROWS = 16
CTX_ROW = 8
VMEM_LIMIT = 56 * 1024 * 1024

F32 = jnp.float32
BF16 = jnp.bfloat16


def _silu(x):
    return x * jax.nn.sigmoid(x)


def _rms(x32, g):
    return x32 * lax.rsqrt(jnp.mean(x32 * x32, axis=-1, keepdims=True) + EPS) * g


def _mod_kernel(cond_ref, w_ref, b_ref, o_ref):
    s = _silu(cond_ref[...])
    w = w_ref[...]
    s_hi = s.astype(BF16)
    s_lo = (s - s_hi.astype(F32)).astype(BF16)
    w_hi = w.astype(BF16)
    w_lo = (w - w_hi.astype(F32)).astype(BF16)
    o_ref[...] = (jnp.dot(s_hi, w_hi, preferred_element_type=F32)
                  + jnp.dot(s_hi, w_lo, preferred_element_type=F32)
                  + jnp.dot(s_lo, w_hi, preferred_element_type=F32) + b_ref[...])


def _modulation(cond, mod_w, mod_b):
    tn = 1536
    n = 6 * D_MODEL
    return pl.pallas_call(
        _mod_kernel,
        out_shape=jax.ShapeDtypeStruct((DEPTH, MOD_ROWS, n), F32),
        grid=(DEPTH, n // tn),
        in_specs=[
            pl.BlockSpec((MOD_ROWS, D_MODEL), lambda l, j: (0, 0)),
            pl.BlockSpec((None, D_MODEL, tn), lambda l, j: (l, 0, j)),
            pl.BlockSpec((None, 1, tn), lambda l, j: (l, 0, j)),
        ],
        out_specs=pl.BlockSpec((None, MOD_ROWS, tn), lambda l, j: (l, 0, j)),
        compiler_params=pltpu.CompilerParams(
            dimension_semantics=("arbitrary", "arbitrary"), vmem_limit_bytes=VMEM_LIMIT),
        name="modulation",
    )(cond, mod_w, mod_b.reshape(DEPTH, 1, n))


def _ssm_prep_kernel(are_ref, aim_ref, dt_ref, bre_ref, bim_ref, cre_ref, cim_ref, bw_ref, cw_ref, lam_ref):
    row_group = lax.broadcasted_iota(jnp.int32, (SSM_W, SSM_GP), 0) // SSM_CH
    lane_group = lax.broadcasted_iota(jnp.int32, (SSM_W, SSM_GP), 1) // SSM_STATE
    own = row_group == lane_group
    for ld in range(2 * DEPTH):
        l, d = divmod(ld, 2)
        a_re = are_ref[ld:ld + 1, :]
        a_im = aim_ref[ld:ld + 1, :]
        dt = jnp.exp(dt_ref[ld:ld + 1, :])
        mag = jnp.exp(a_re * dt)
        l_re = mag * jnp.cos(a_im * dt)
        l_im = mag * jnp.sin(a_im * dt)
        lam_ref[l, 2 * d] = jnp.broadcast_to(l_re, (SEQ_GROUP, SSM_GP))
        lam_ref[l, 2 * d + 1] = jnp.broadcast_to(l_im, (SEQ_GROUP, SSM_GP))
        x = l_re - 1.0
        den = a_re * a_re + a_im * a_im
        f_re = (x * a_re + l_im * a_im) / den
        f_im = (l_im * a_re - x * a_im) / den
        b_re = bre_ref[ld]
        b_im = bim_ref[ld]
        bbar = (f_re * b_re - f_im * b_im, f_re * b_im + f_im * b_re)
        cmat = (cre_ref[ld], -cim_ref[ld])
        for ri in range(2):
            c0 = (2 * d + ri) * SSM_GP
            tiled = jnp.concatenate([bbar[ri]] * SSM_GROUPS, axis=0)
            bw_ref[l, :, c0:c0 + SSM_GP] = jnp.where(own, tiled, 0.0).astype(BF16)
            cw_ref[l, c0:c0 + SSM_GP, :] = jnp.where(own, cmat[ri], 0.0).T.astype(BF16)


def _ssm_prep(a_re, a_im, log_dt, b_re, b_im, c_re, c_im):
    ld = DEPTH * 2
    are = a_re.reshape(ld, SSM_GP)
    aim = a_im.reshape(ld, SSM_GP)
    dt = jnp.repeat(log_dt.reshape(ld, SSM_GROUPS), SSM_STATE, axis=1)
    to_rows = lambda b: b.reshape(ld, SSM_GROUPS, SSM_STATE, SSM_CH).transpose(0, 3, 1, 2).reshape(ld, SSM_CH, SSM_GP)
    repeat_lanes = lambda c: jnp.tile(c.reshape(ld, SSM_W, SSM_STATE), (1, 1, SSM_GROUPS))
    return pl.pallas_call(
        _ssm_prep_kernel,
        out_shape=(jax.ShapeDtypeStruct((DEPTH, SSM_W, 4 * SSM_GP), BF16),
                   jax.ShapeDtypeStruct((DEPTH, 4 * SSM_GP, SSM_W), BF16),
                   jax.ShapeDtypeStruct((DEPTH, 4, SEQ_GROUP, SSM_GP), F32)),
        compiler_params=pltpu.CompilerParams(vmem_limit_bytes=VMEM_LIMIT),
        name="ssm_prep",
    )(are, aim, dt, to_rows(b_re), to_rows(b_im), repeat_lanes(c_re), repeat_lanes(c_im))


def _inproj_kernel(*refs, latent):
    if latent:
        (x_ref, mod_ref, g_ref, w_ref, qg_ref, kg_ref, sq_ref, sk_ref, cos_ref, sa_ref, sb_ref,
         q_ref, kd_ref, vd_ref, us_ref, up_ref, wbf_ref) = refs
    else:
        (x_ref, mod_ref, g_ref, w_ref, qg_ref, kg_ref, sq_ref, sk_ref,
         q_ref, kd_ref, vd_ref, us_ref, up_ref, ko_ref, vo_ref, wbf_ref) = refs

    @pl.when(pl.program_id(0) == 0)
    def _():
        wbf_ref[...] = w_ref[...].astype(BF16)

    x = x_ref[...]
    mod = mod_ref[...]
    h = _rms(x, g_ref[...]) * (1.0 + mod[1:2]) + mod[0:1]
    p = jnp.dot(h.astype(BF16), wbf_ref[...], preferred_element_type=F32)
    q = p[:, :ATTN_W]
    k = p[:, ATTN_W:ATTN_W + KV_W]
    v = p[:, ATTN_W + KV_W:ATTN_W + 2 * KV_W]
    q_ms = jnp.dot((q * q).astype(BF16), sq_ref[...], preferred_element_type=F32)
    k_ms = jnp.dot((k * k).astype(BF16), sk_ref[...], preferred_element_type=F32)
    q = q * lax.rsqrt(q_ms + EPS) * qg_ref[...]
    k = k * lax.rsqrt(k_ms + EPS) * kg_ref[...]
    if latent:
        cos = cos_ref[...]
        sa = sa_ref[...]
        sb = sb_ref[...]
        cos4 = jnp.concatenate([cos] * 4, axis=1)
        sa4 = jnp.concatenate([sa] * 4, axis=1)
        sb4 = jnp.concatenate([sb] * 4, axis=1)
        q = (q * cos4 + pltpu.roll(q, ATTN_W - 16, axis=1) * sa4 + pltpu.roll(q, 16, axis=1) * sb4)
        k = (k * cos + pltpu.roll(k, KV_W - 16, axis=1) * sa + pltpu.roll(k, 16, axis=1) * sb)
    else:
        seq = ko_ref.shape[-1]
        for s in range(ko_ref.shape[0]):
            ko_ref[s] = k[s * seq:(s + 1) * seq, :].T
            vo_ref[s] = v[s * seq:(s + 1) * seq, :].T
    q_ref[...] = (q * (HEAD_DIM ** -0.5 * LOG2_E)).astype(BF16)
    lane = lax.broadcasted_iota(jnp.int32, k.shape, 1)
    first = lane < HEAD_DIM
    k_sw = pltpu.roll(k, HEAD_DIM, axis=1)
    v_sw = pltpu.roll(v, HEAD_DIM, axis=1)
    kd_ref[:, :KV_W] = jnp.where(first, k, k_sw).astype(BF16)
    kd_ref[:, KV_W:] = jnp.where(first, k_sw, k).astype(BF16)
    vd_ref[:, :KV_W] = jnp.where(first, v, v_sw).astype(BF16)
    vd_ref[:, KV_W:] = jnp.where(first, v_sw, v).astype(BF16)
    o3 = ATTN_W + 2 * KV_W
    us_ref[0] = p[:, o3:o3 + SSM_W // 2]
    us_ref[1] = p[:, o3 + SSM_W // 2:o3 + SSM_W]
    up_ref[...] = p[:, o3 + SSM_W:]


def _head_mean_matrix(width):
    i = np.arange(width) // HEAD_DIM
    return jnp.asarray((i[:, None] == i[None, :]).astype(np.float32) / HEAD_DIM, dtype=BF16)


def _rope_tables(seq):
    t = np.arange(seq)
    row = (t // GRID_W).astype(np.float64)
    col = (t % GRID_W).astype(np.float64)
    nf = HEAD_DIM // 4
    inv = ROPE_THETA ** (-np.arange(nf, dtype=np.float64) / nf)
    ang_r = row[:, None] * inv[None, :]
    ang_c = col[:, None] * inv[None, :]
    cos = np.concatenate([np.cos(ang_r), np.cos(ang_r), np.cos(ang_c), np.cos(ang_c)], axis=1)
    sin = np.concatenate([np.sin(ang_r), np.sin(ang_r), np.sin(ang_c), np.sin(ang_c)], axis=1)
    lower = np.tile(np.concatenate([np.ones(nf), np.zeros(nf)]), 2)[None, :]
    sa = -sin * lower
    sb = sin * (1.0 - lower)
    tile = lambda a: jnp.asarray(np.tile(a, (1, 2)), dtype=F32)
    return tile(cos), tile(sa), tile(sb)


def _inproj(x, mods_l, g1, w_in, layer, qg, kg, latent, seq):
    rows = x.shape[0]
    tr = 1024
    tiles_per_seq = max(seq // tr, 1)
    mod_row = (lambda i: (i * tr) // seq) if latent else (lambda i: CTX_ROW)
    const = lambda shape: pl.BlockSpec(shape, lambda i: (0,) * len(shape))
    row_spec = lambda width: pl.BlockSpec((tr, width), lambda i: (i, 0))
    in_specs = [
        row_spec(D_MODEL),
        pl.BlockSpec((None, 6, D_MODEL), lambda i: (mod_row(i), 0, 0)),
        const((1, D_MODEL)),
        pl.BlockSpec((None, D_MODEL, IN_COLS), lambda i: (layer, 0, 0), pipeline_mode=pl.Buffered(1)),
        const((1, ATTN_W)),
        const((1, KV_W)),
        const((ATTN_W, ATTN_W)),
        const((KV_W, KV_W)),
    ]
    args = [x, mods_l, g1.reshape(1, D_MODEL), w_in,
            jnp.tile(qg, N_HEADS).reshape(1, ATTN_W), jnp.tile(kg, N_KV_HEADS).reshape(1, KV_W),
            _head_mean_matrix(ATTN_W), _head_mean_matrix(KV_W)]
    widths = [(ATTN_W, BF16), (2 * KV_W, BF16), (2 * KV_W, BF16), (None, F32), (POOL_W, F32)]
    if latent:
        assert seq % tr == 0
        cos, sa, sb = _rope_tables(seq)
        in_specs += [pl.BlockSpec((tr, KV_W), lambda i: (i % tiles_per_seq, 0))] * 3
        args += [cos, sa, sb]
    out_shape = [jax.ShapeDtypeStruct((rows, w) if w else (2, rows, SSM_W // 2), dt) for w, dt in widths]
    out_specs = [row_spec(w) if w else pl.BlockSpec((2, tr, SSM_W // 2), lambda i: (0, i, 0)) for w, _ in widths]
    if not latent:
        assert tr % seq == 0
        out_shape += [jax.ShapeDtypeStruct((rows // seq, KV_W, seq), F32)] * 2
        out_specs += [pl.BlockSpec((tr // seq, KV_W, seq), lambda i: (i, 0, 0))] * 2
    return pl.pallas_call(
        functools.partial(_inproj_kernel, latent=latent),
        out_shape=out_shape, grid=(rows // tr,), in_specs=in_specs, out_specs=out_specs,
        scratch_shapes=[pltpu.VMEM((D_MODEL, IN_COLS), BF16)],
        compiler_params=pltpu.CompilerParams(
            dimension_semantics=("arbitrary",), vmem_limit_bytes=VMEM_LIMIT),
        name="inproj_latent" if latent else "inproj_context",
    )(*args)


def _attn_kernel(*refs, n_kv_src):
    q_ref = refs[0]
    k_refs = refs[1:1 + n_kv_src]
    v_refs = refs[1 + n_kv_src:1 + 2 * n_kv_src]
    o_ref = refs[1 + 2 * n_kv_src]
    nseq, tq, _ = q_ref.shape
    lane = lax.broadcasted_iota(jnp.int32, (tq, KV_W), 1)
    first = lane < HEAD_DIM
    for sq in range(nseq):
        for pair in range(N_HEADS // 2):
            kv = pair // 2
            cols = slice(pair * KV_W, (pair + 1) * KV_W)
            kcols = slice(kv * KV_W, (kv + 1) * KV_W)
            qp = q_ref[sq, :, cols]
            halves = []
            for half in range(2):
                qm = jnp.where(first if half == 0 else jnp.logical_not(first), qp, jnp.zeros_like(qp))
                scores = [lax.dot_general(qm, k_ref[sq, :, kcols], (((1,), (1,)), ((), ())),
                                          preferred_element_type=F32) for k_ref in k_refs]
                m = scores[0].max(axis=-1, keepdims=True)
                for s in scores[1:]:
                    m = jnp.maximum(m, s.max(axis=-1, keepdims=True))
                den = jnp.zeros((tq, 1), F32)
                acc = jnp.zeros((tq, KV_W), F32)
                for s, v_ref in zip(scores, v_refs):
                    p = jnp.exp2(s - m)
                    den = den + p.sum(axis=-1, keepdims=True)
                    acc = acc + jnp.dot(p.astype(BF16), v_ref[sq, :, kcols], preferred_element_type=F32)
                halves.append(acc / den)
            o_ref[sq, :, cols] = jnp.where(first, halves[0], halves[1]).astype(o_ref.dtype)


def _attention(q, kds, vds):
    b, seq, _ = q.shape
    tq = min(seq, ATTN_ROWS)
    nseq = max(1, 2 * ATTN_ROWS // seq)
    n_src = len(kds)
    kv_spec = lambda a: pl.BlockSpec((nseq, a.shape[1], 2 * KV_W), lambda i, t: (i, 0, 0))
    return pl.pallas_call(
        functools.partial(_attn_kernel, n_kv_src=n_src),
        out_shape=jax.ShapeDtypeStruct((b, seq, ATTN_W), BF16),
        grid=(b // nseq, seq // tq),
        in_specs=[pl.BlockSpec((nseq, tq, ATTN_W), lambda i, t: (i, t, 0))]
                 + [kv_spec(a) for a in kds] + [kv_spec(a) for a in vds],
        out_specs=pl.BlockSpec((nseq, tq, ATTN_W), lambda i, t: (i, t, 0)),
        compiler_params=pltpu.CompilerParams(
            dimension_semantics=("arbitrary", "arbitrary"), vmem_limit_bytes=VMEM_LIMIT),
        name="attention_%dsrc" % n_src,
    )(q, *kds, *vds)


def _ssm_kernel(u_ref, bw_ref, cw_ref, lam_ref, h0_ref, d_ref, glu_ref,
                o_ref, hf_ref, y_ref, utm_scr, bu0_scr, bu1_scr, hs0_scr, hs1_scr, st_scr):
    rows = u_ref.shape[1]
    seq = rows // SEQ_GROUP
    crow = SSM_CHUNK * SEQ_GROUP
    nchunks = rows // crow
    w2 = 2 * SSM_GP
    lanes = SSM_W // 2

    def time_major(c):
        t0 = c * SSM_CHUNK
        return jnp.concatenate(
            [jnp.concatenate([u_ref[h, pl.ds(t0 + s, SEQ_GROUP, stride=seq), :] for h in range(2)], axis=1)
             for s in range(SSM_CHUNK)], axis=0)

    def y_rows(r, n):
        return jnp.concatenate([y_ref[0, pl.ds(r, n), :], y_ref[1, pl.ds(r, n), :]], axis=1)

    def set_y_rows(r, n, val):
        y_ref[0, pl.ds(r, n), :] = val[:, :lanes]
        y_ref[1, pl.ds(r, n), :] = val[:, lanes:]

    st_scr[...] = h0_ref[...]

    def init_rows(i, c):
        r = pl.multiple_of(i * crow, crow)
        tm = time_major(i)
        set_y_rows(r, crow, tm * d_ref[...])
        utm_scr[pl.ds(r, crow), :] = tm.astype(BF16)
        return c
    lax.fori_loop(0, nchunks, init_rows, 0)

    last = nchunks - 1
    bu_scr = (bu0_scr, bu1_scr)
    hs_scr = (hs0_scr, hs1_scr)

    def project_in(i, buf):
        i = jnp.clip(i, 0, last)
        for d, c in ((0, i), (1, last - i)):
            r = pl.multiple_of(c * crow, crow)
            bu_scr[buf][d] = jnp.dot(utm_scr[pl.ds(r, crow), :], bw_ref[:, d * w2:(d + 1) * w2],
                                     preferred_element_type=F32)

    def recur(buf):
        for d in range(2):
            for cb in range(SSM_GP // SSM_COLBLK):
                re_c = slice(cb * SSM_COLBLK, (cb + 1) * SSM_COLBLK)
                im_c = slice(SSM_GP + cb * SSM_COLBLK, SSM_GP + (cb + 1) * SSM_COLBLK)
                l_re = lam_ref[2 * d, :, re_c]
                l_im = lam_ref[2 * d + 1, :, re_c]
                h_re = st_scr[:, d * w2 + cb * SSM_COLBLK:d * w2 + (cb + 1) * SSM_COLBLK]
                h_im = st_scr[:, d * w2 + SSM_GP + cb * SSM_COLBLK:d * w2 + SSM_GP + (cb + 1) * SSM_COLBLK]
                for s in range(SSM_CHUNK):
                    t = s if d == 0 else SSM_CHUNK - 1 - s
                    r = slice(t * SEQ_GROUP, (t + 1) * SEQ_GROUP)
                    n_re = l_re * h_re - l_im * h_im + bu_scr[buf][d, r, re_c]
                    n_im = l_re * h_im + l_im * h_re + bu_scr[buf][d, r, im_c]
                    hs_scr[buf][d, r, re_c] = n_re
                    hs_scr[buf][d, r, im_c] = n_im
                    h_re, h_im = n_re, n_im
                st_scr[:, d * w2 + cb * SSM_COLBLK:d * w2 + (cb + 1) * SSM_COLBLK] = h_re
                st_scr[:, d * w2 + SSM_GP + cb * SSM_COLBLK:d * w2 + SSM_GP + (cb + 1) * SSM_COLBLK] = h_im

    def project_out(i, buf):
        i = jnp.clip(i, 0, last)
        for d, c in ((0, i), (1, last - i)):
            r = pl.multiple_of(c * crow, crow)
            contrib = jnp.dot(hs_scr[buf][d].astype(BF16), cw_ref[d * w2:(d + 1) * w2, :],
                              preferred_element_type=F32)
            set_y_rows(r, crow, y_rows(r, crow) + contrib)

    hs1_scr[...] = jnp.zeros(hs1_scr.shape, F32)
    project_in(0, 0)

    def two_steps(j, c):
        i = 2 * j
        recur(0)
        project_in(i + 1, 1)
        project_out(i - 1, 1)
        recur(1)
        project_in(i + 2, 0)
        project_out(i, 0)
        return c
    lax.fori_loop(0, nchunks // 2, two_steps, 0)
    project_out(last, 1)

    hf_ref[...] = st_scr[...]

    esteps = 128
    erow = esteps * SEQ_GROUP

    def epilogue(i, c):
        r = pl.multiple_of(i * erow, erow)
        z = jax.nn.gelu(y_rows(r, erow))
        g = jnp.dot(z.astype(BF16), glu_ref[...], preferred_element_type=F32)
        set_y_rows(r, erow, g[:, :SSM_W] * jax.nn.sigmoid(g[:, SSM_W:]))
        t0 = pl.multiple_of(i * esteps, esteps)
        for b in range(SEQ_GROUP):
            for h in range(2):
                o_ref[pl.ds(b * seq + t0, esteps), h * lanes:(h + 1) * lanes] = (
                    y_ref[h, pl.ds(r + b, esteps, stride=SEQ_GROUP), :].astype(BF16))
        return c
    lax.fori_loop(0, rows // erow, epilogue, 0)


def _ssm(u, layer, bw, cw, lam, h0, ssm_d, glu_bf, seq):
    rows = u.shape[1]
    grows = SEQ_GROUP * seq
    nbg = rows // grows
    crow = SSM_CHUNK * SEQ_GROUP
    const = lambda shape: pl.BlockSpec(shape, lambda i: (0,) * len(shape))
    return pl.pallas_call(
        _ssm_kernel,
        out_shape=(jax.ShapeDtypeStruct((rows, SSM_W), BF16),
                   jax.ShapeDtypeStruct((nbg, SEQ_GROUP, 4 * SSM_GP), F32)),
        grid=(nbg,),
        in_specs=[
            pl.BlockSpec((2, grows, SSM_W // 2), lambda i: (0, i, 0)),
            pl.BlockSpec((None, SSM_W, 4 * SSM_GP), lambda i: (layer, 0, 0)),
            pl.BlockSpec((None, 4 * SSM_GP, SSM_W), lambda i: (layer, 0, 0)),
            pl.BlockSpec((None, 4, SEQ_GROUP, SSM_GP), lambda i: (layer, 0, 0, 0)),
            pl.BlockSpec((None, SEQ_GROUP, 4 * SSM_GP), lambda i: (i, 0, 0)),
            const((1, SSM_W)),
            const((SSM_W, 2 * SSM_W)),
        ],
        out_specs=(pl.BlockSpec((grows, SSM_W), lambda i: (i, 0)),
                   pl.BlockSpec((None, SEQ_GROUP, 4 * SSM_GP), lambda i: (i, 0, 0))),
        scratch_shapes=[pltpu.VMEM((2, grows, SSM_W // 2), F32),
                        pltpu.VMEM((grows, SSM_W), BF16),
                        pltpu.VMEM((2, crow, 2 * SSM_GP), F32),
                        pltpu.VMEM((2, crow, 2 * SSM_GP), F32),
                        pltpu.VMEM((2, crow, 2 * SSM_GP), F32),
                        pltpu.VMEM((2, crow, 2 * SSM_GP), F32),
                        pltpu.VMEM((SEQ_GROUP, 4 * SSM_GP), F32)],
        compiler_params=pltpu.CompilerParams(
            dimension_semantics=("arbitrary",), vmem_limit_bytes=VMEM_LIMIT),
        name="ssm_scan",
    )(u, bw, cw, lam, h0, ssm_d.reshape(1, SSM_W), glu_bf)


def _pack2(lo, hi):
    return pltpu.pack_elementwise([lo, hi], packed_dtype=BF16)


def _unpack2(w, index):
    return pltpu.unpack_elementwise(w, index=index, packed_dtype=BF16, unpacked_dtype=F32)


def _mixout_kernel(*refs, route):
    if route:
        (x_ref, at_ref, ss_ref, up_ref, mod_ref, invc_ref, pw_ref, ps_ref, wo_ref, g2_ref, wr_ref, br_ref, tri_ref,
         x1_ref, h2_ref, route_ref, rt_ref, cnt_ref, wobf_ref) = refs
    else:
        (x_ref, at_ref, ss_ref, up_ref, mod_ref, invc_ref, pw_ref, ps_ref, wo_ref, g2_ref,
         x1_ref, h2_ref, wobf_ref) = refs

    @pl.when(pl.program_id(0) == 0)
    def _():
        wobf_ref[...] = wo_ref[...].astype(BF16)

    nseq, seq, _ = up_ref.shape
    rows = nseq * seq
    zpad = jnp.zeros((POOL_PAD, POOL_W), F32)
    n_ext = seq + 2 * POOL_PAD
    back = lambda a, k: pltpu.roll(a, k, axis=0)
    ahead = lambda a, k: pltpu.roll(a, n_ext - k, axis=0)
    grp = lax.broadcasted_iota(jnp.int32, (n_ext, POOL_W), 1) // POOL_CH
    pooled = []
    for s in range(nseq):
        u = up_ref[s]
        ue = jnp.concatenate([zpad, u, zpad], axis=0)
        w2 = ue + back(ue, 1)
        w4 = back(w2, 1) + ahead(w2, 1)
        w8 = back(w4, 2) + ahead(w4, 2)
        w16 = back(w8, 4) + ahead(w8, 4)
        win = jnp.where(grp == 0, w2, jnp.where(grp == 1, w4, jnp.where(grp == 2, w8, w16)))
        pooled.append(win[POOL_PAD:POOL_PAD + seq] * invc_ref[...] - u)
    pooled = jnp.concatenate(pooled, axis=0)
    pool = jnp.dot(pooled.astype(BF16), pw_ref[...], preferred_element_type=F32) * ps_ref[...]
    mix = jnp.concatenate([at_ref[...].reshape(rows, ATTN_W), ss_ref[...].reshape(rows, SSM_W),
                           pool.astype(BF16)], axis=1)
    o = jnp.dot(mix, wobf_ref[...], preferred_element_type=F32)
    mod = mod_ref[...]
    x1 = x_ref[...].reshape(rows, D_MODEL) + mod[2:3] * o
    x1_ref[...] = x1.reshape(nseq, seq, D_MODEL)
    h2 = _rms(x1, g2_ref[...]) * (1.0 + mod[4:5]) + mod[3:4]
    h2_ref[...] = h2.astype(BF16).reshape(nseq, seq, D_MODEL)
    if not route:
        return
    h_hi = h2.astype(BF16)
    h_lo = (h2 - h_hi.astype(F32)).astype(BF16)
    both = jnp.dot(h_hi, wr_ref[...], preferred_element_type=F32)
    logits = (both[:, :ROUTE_W] + both[:, ROUTE_W:]
              + jnp.dot(h_lo, wr_ref[:, :ROUTE_W], preferred_element_type=F32) + br_ref[...])
    lane = lax.broadcasted_iota(jnp.int32, logits.shape, 1).astype(F32)
    neg = float(np.finfo(np.float32).min)
    far = float(ROUTE_W)
    logits = jnp.where(lane < N_EXPERTS, logits, neg)
    m1 = logits.max(axis=-1, keepdims=True)
    i1 = jnp.where(logits == m1, lane, far).min(axis=-1, keepdims=True)
    rest = jnp.where(lane == i1, neg, logits)
    m2 = rest.max(axis=-1, keepdims=True)
    i2 = jnp.where(rest == m2, lane, far).min(axis=-1, keepdims=True)
    e2 = jnp.exp(m2 - m1)
    den = 1.0 + e2
    sel = jnp.where(lane == i1, 1.0, jnp.where(lane == i2, 1.0, 0.0))
    sel_bf = sel.astype(BF16)
    rank = jnp.concatenate([jnp.dot(tri_ref[...], sel_bf[s * seq:(s + 1) * seq], preferred_element_type=F32)
                            for s in range(nseq)], axis=0)
    r1 = jnp.where(lane == i1, rank, 0.0).sum(axis=-1, keepdims=True)
    r2 = jnp.where(lane == i2, rank, 0.0).sum(axis=-1, keepdims=True)
    out = jnp.zeros_like(logits)
    for j, col in enumerate((1.0 / den, e2 / den, i1, i2, r1, r2)):
        out = jnp.where(lane == j, col, out)
    route_ref[...] = out.reshape(nseq, seq, ROUTE_W)
    for s in range(nseq):
        part = slice(s * seq, (s + 1) * seq)
        rt_ref[s] = out[part].T[:8, :]
        cnt_ref[s] = jnp.broadcast_to(sel[part].sum(axis=0, keepdims=True), cnt_ref.shape[1:])


def _pool_inv_count(seq):
    t = np.arange(seq)
    cols = []
    for win in POOL_WINDOWS:
        lo = np.clip(t - win // 2, 0, seq)
        hi = np.clip(t + win // 2, 0, seq)
        cols.append(np.repeat((hi - lo).astype(np.float32)[:, None], POOL_CH, axis=1))
    return np.concatenate(cols, axis=1)


def _block_diag_pool(pool_w):
    eye = jnp.eye(len(POOL_WINDOWS), dtype=F32)
    m = eye[:, None, :, None] * pool_w[:, :, None, :]
    return m.reshape(POOL_W, POOL_W)


def _mixout(x, attn, ssm_tm, u_pool, mods_l, pool_w, pool_scale, w_out, layer, g2, latent, router):
    b, seq, _ = x.shape
    route = router is not None
    nseq = 1 if latent else max(1, MIX_ROWS // seq)
    mod_row = (lambda i: i) if latent else (lambda i: CTX_ROW)
    const = lambda shape: pl.BlockSpec(shape, lambda i: (0,) * len(shape))
    cnt = _pool_inv_count(seq)
    row_spec = lambda width: pl.BlockSpec((nseq, seq, width), lambda i: (i, 0, 0))
    in_specs = [
        row_spec(D_MODEL), row_spec(ATTN_W), row_spec(SSM_W), row_spec(POOL_W),
        pl.BlockSpec((None, 6, D_MODEL), lambda i: (mod_row(i), 0, 0)),
        const((seq, POOL_W)),
        const((POOL_W, POOL_W)),
        const((1, POOL_W)),
        pl.BlockSpec((None, D_MODEL, D_MODEL), lambda i: (layer, 0, 0), pipeline_mode=pl.Buffered(1)),
        const((1, D_MODEL)),
    ]
    args = [x, attn, ssm_tm, u_pool, mods_l, jnp.asarray(1.0 / cnt, dtype=F32),
            _block_diag_pool(pool_w).astype(BF16), pool_scale.reshape(1, POOL_W), w_out,
            g2.reshape(1, D_MODEL)]
    if route:
        wr, br = router
        tri = np.tril(np.ones((seq, seq), np.float32), -1)
        wr_pad = jnp.pad(wr, ((0, 0), (0, ROUTE_W - N_EXPERTS)))
        wr_hi = wr_pad.astype(BF16)
        wr_lo = (wr_pad - wr_hi.astype(F32)).astype(BF16)
        in_specs += [const((D_MODEL, 2 * ROUTE_W)), const((1, ROUTE_W)), const((seq, seq))]
        args += [jnp.concatenate([wr_hi, wr_lo], axis=1),
                 jnp.pad(br, (0, ROUTE_W - N_EXPERTS)).reshape(1, ROUTE_W),
                 jnp.asarray(tri, dtype=BF16)]
        out_shape = [jax.ShapeDtypeStruct((b, seq, D_MODEL), F32),
                     jax.ShapeDtypeStruct((b, seq, D_MODEL), BF16),
                     jax.ShapeDtypeStruct((b, seq, ROUTE_W), F32),
                     jax.ShapeDtypeStruct((b, 8, seq), F32),
                     jax.ShapeDtypeStruct((b, 8, ROUTE_W), F32)]
        out_specs = [row_spec(D_MODEL), row_spec(D_MODEL), row_spec(ROUTE_W),
                     pl.BlockSpec((nseq, 8, seq), lambda i: (i, 0, 0)),
                     pl.BlockSpec((nseq, 8, ROUTE_W), lambda i: (i, 0, 0))]
    else:
        out_shape = [jax.ShapeDtypeStruct((b, seq, D_MODEL), F32), jax.ShapeDtypeStruct((b, seq, D_MODEL), BF16)]
        out_specs = [row_spec(D_MODEL), row_spec(D_MODEL)]
    return pl.pallas_call(
        functools.partial(_mixout_kernel, route=route),
        out_shape=out_shape, grid=(b // nseq,), in_specs=in_specs, out_specs=out_specs,
        scratch_shapes=[pltpu.VMEM((D_MODEL, D_MODEL), BF16)],
        compiler_params=pltpu.CompilerParams(
            dimension_semantics=("arbitrary",), vmem_limit_bytes=VMEM_LIMIT),
        name="mixout_%s%s" % ("latent" if latent else "context", "_route" if route else ""),
    )(*args)


def _ffn_kernel(h_ref, x_ref, mod_ref, wg_ref, wu_ref, wd_ref, o_ref):
    h = h_ref[...]
    a = _silu(jnp.dot(h, wg_ref[...], preferred_element_type=F32)) * \
        jnp.dot(h, wu_ref[...], preferred_element_type=F32)
    f = jnp.dot(a.astype(BF16), wd_ref[...], preferred_element_type=F32)
    o_ref[...] = x_ref[...] + mod_ref[5:6] * f


def _ffn(h2, x1, mods_l, wg, wu, wd, latent, seq):
    rows = h2.shape[0]
    tm = 512
    ff = wg.shape[1]
    mod_row = (lambda i: (i * tm) // seq) if latent else (lambda i: CTX_ROW)
    resident = lambda shape: pl.BlockSpec(shape, lambda i: (0, 0), pipeline_mode=pl.Buffered(1))
    return pl.pallas_call(
        _ffn_kernel,
        out_shape=jax.ShapeDtypeStruct((rows, D_MODEL), F32),
        grid=(rows // tm,),
        in_specs=[pl.BlockSpec((tm, D_MODEL), lambda i: (i, 0)),
                  pl.BlockSpec((tm, D_MODEL), lambda i: (i, 0)),
                  pl.BlockSpec((None, 6, D_MODEL), lambda i: (mod_row(i), 0, 0)),
                  resident((D_MODEL, ff)), resident((D_MODEL, ff)), resident((ff, D_MODEL))],
        out_specs=pl.BlockSpec((tm, D_MODEL), lambda i: (i, 0)),
        compiler_params=pltpu.CompilerParams(
            dimension_semantics=("arbitrary",), vmem_limit_bytes=VMEM_LIMIT),
        name="ffn_dense",
    )(h2, x1, mods_l, wg, wu, wd)


def _route_tables(route_t, cnt, seq):
    b = route_t.shape[0]
    per_block = MOE_BLOCK // seq
    nb = b // per_block
    c = cnt[:, 0, :N_EXPERTS].astype(jnp.int32).reshape(nb, per_block, N_EXPERTS)
    before = jnp.cumsum(c, axis=1) - c
    total = c.sum(axis=1)
    aligned = (total + 7) // 8 * 8
    starts = jnp.cumsum(aligned, axis=1) - aligned
    base = (starts[:, None, :] + before).reshape(b, N_EXPERTS)
    expert = route_t[:, 2:4, :].astype(jnp.int32)
    rank = route_t[:, 4:6, :].astype(jnp.int32)
    slot = rank
    for e in range(N_EXPERTS):
        slot = slot + jnp.where(expert == e, base[:, e][:, None, None], 0)
    slots = slot.reshape(nb, per_block, 2, seq).transpose(0, 2, 1, 3).reshape(nb, 2, MOE_BLOCK)
    return starts.reshape(-1), total.reshape(-1), slots


MOE_SLOTS = 2 * MOE_BLOCK
MOE_ROWS = MOE_SLOTS + 8 * N_EXPERTS + MOE_TILE


MOE_GROUP = 16


MOE_PLANE = MOE_BLOCK + 8
MOE_SPARE_ROW = MOE_BLOCK
MOE_SPARE_SLOT = MOE_ROWS


def _moe_kernel(starts_ref, counts_ref, slots_ref, h_ref, route_ref, wg_ref, wu_ref, wt_ref, wd_ref, f_ref,
                xs_ref, stage_ref, z_ref, yb_ref, dst_ref, state_ref):
    b = pl.program_id(0)
    e = pl.program_id(1)
    half = D_MODEL // 2

    def send_rows(pending, buf):
        for r in range(MOE_TILE):
            z_ref[pl.ds(dst_ref[pending + r], 1), :] = yb_ref[buf, pl.ds(r, 1), :]

    @pl.when(e == 0)
    def _():
        xs_ref[MOE_SLOTS:, :] = jnp.zeros((MOE_ROWS - MOE_SLOTS, half), jnp.uint32)
        yb_ref[...] = jnp.zeros(yb_ref.shape, jnp.uint32)
        zero_row = jnp.zeros((1, half), jnp.uint32)
        for g in range(N_EXPERTS):
            end = starts_ref[b * N_EXPERTS + g] + counts_ref[b * N_EXPERTS + g]
            for r in range(7):
                xs_ref[pl.ds(end + r, 1), :] = zero_row
                dst_ref[end + r] = MOE_SPARE_ROW

        def spare(i, c):
            dst_ref[MOE_SLOTS + i] = MOE_SPARE_ROW
            return c
        lax.fori_loop(0, MOE_ROWS + MOE_TILE - MOE_SLOTS, spare, 0)

        def put(g, c):
            t0 = pl.multiple_of(g * MOE_GROUP, MOE_GROUP)
            hb = h_ref[pl.ds(t0, MOE_GROUP), :].astype(F32)
            stage_ref[...] = _pack2(hb[:, :half], hb[:, half:])
            for r in range(MOE_GROUP):
                row = stage_ref[r:r + 1, :]
                s0 = slots_ref[0, t0 + r]
                s1 = slots_ref[1, t0 + r]
                xs_ref[pl.ds(s0, 1), :] = row
                xs_ref[pl.ds(s1, 1), :] = row
                dst_ref[s0] = t0 + r
                dst_ref[s1] = t0 + r + MOE_PLANE
            return c
        lax.fori_loop(0, MOE_BLOCK // MOE_GROUP, put, 0)
        state_ref[0] = 0
        state_ref[1] = MOE_SPARE_SLOT

    start = starts_ref[b * N_EXPERTS + e]
    count = counts_ref[b * N_EXPERTS + e]

    def tile(j, carry):
        buf, pending = carry
        send_rows(pending, 1 - buf)
        s = pl.multiple_of(start + j * MOE_TILE, 8)
        xg = xs_ref[pl.ds(s, MOE_TILE), :]
        x_lo = _unpack2(xg, 0).astype(BF16)
        x_hi = _unpack2(xg, 1).astype(BF16)
        main = FF_EXPERT - MOE_TAIL

        def proj(w_ref, cols):
            return (jnp.dot(x_lo, w_ref[:half, cols], preferred_element_type=F32)
                    + jnp.dot(x_hi, w_ref[half:, cols], preferred_element_type=F32))
        tail = proj(wt_ref, slice(None))
        rest = tail.shape[1] // 2
        a = jnp.concatenate([_silu(proj(wg_ref, slice(0, main))) * proj(wu_ref, slice(0, main)),
                             _silu(tail[:, :rest]) * tail[:, rest:]], axis=1).astype(BF16)
        y = jnp.dot(a, wd_ref[...], preferred_element_type=F32)
        yb_ref[buf] = _pack2(y[:, :half], y[:, half:])
        return 1 - buf, s
    buf, pending = lax.fori_loop(0, (count + MOE_TILE - 1) // MOE_TILE, tile, (state_ref[0], state_ref[1]))
    state_ref[0] = buf
    state_ref[1] = pending

    @pl.when(e == pl.num_programs(1) - 1)
    def _():
        send_rows(pending, 1 - buf)
        rows = 256

        def blend(i, c):
            t0 = pl.multiple_of(i * rows, rows)
            z0 = z_ref[pl.ds(t0, rows), :]
            z1 = z_ref[pl.ds(pl.multiple_of(MOE_PLANE + t0, 8), rows), :]
            route = route_ref[pl.ds(t0, rows), :]
            w1 = route[:, 0:1]
            w2 = route[:, 1:2]
            f_ref[pl.ds(t0, rows), :half] = (w1 * _unpack2(z0, 0) + w2 * _unpack2(z1, 0)).astype(BF16)
            f_ref[pl.ds(t0, rows), half:] = (w1 * _unpack2(z0, 1) + w2 * _unpack2(z1, 1)).astype(BF16)
            return c
        lax.fori_loop(0, MOE_BLOCK // rows, blend, 0)


def _moe_experts(h2, route, starts, counts, slots, wg, wu, wd):
    rows = h2.shape[0]
    nb = rows // MOE_BLOCK
    ff = wd.shape[1]
    wt = jnp.concatenate([wg[:, :, ff - MOE_TAIL:], wu[:, :, ff - MOE_TAIL:]], axis=2)
    grid_spec = pltpu.PrefetchScalarGridSpec(
        num_scalar_prefetch=2,
        grid=(nb, N_EXPERTS),
        in_specs=[
            pl.BlockSpec((None, 2, MOE_BLOCK), lambda b, e, st, ct: (b, 0, 0), memory_space=pltpu.SMEM),
            pl.BlockSpec((MOE_BLOCK, D_MODEL), lambda b, e, st, ct: (b, 0), pipeline_mode=pl.Buffered(1)),
            pl.BlockSpec((MOE_BLOCK, ROUTE_W), lambda b, e, st, ct: (b, 0), pipeline_mode=pl.Buffered(1)),
            pl.BlockSpec((None, D_MODEL, ff), lambda b, e, st, ct: (e, 0, 0)),
            pl.BlockSpec((None, D_MODEL, ff), lambda b, e, st, ct: (e, 0, 0)),
            pl.BlockSpec((None, D_MODEL, 2 * MOE_TAIL), lambda b, e, st, ct: (e, 0, 0)),
            pl.BlockSpec((None, ff, D_MODEL), lambda b, e, st, ct: (e, 0, 0)),
        ],
        out_specs=pl.BlockSpec((MOE_BLOCK, D_MODEL), lambda b, e, st, ct: (b, 0)),
        scratch_shapes=[pltpu.VMEM((MOE_ROWS, D_MODEL // 2), jnp.uint32),
                        pltpu.VMEM((MOE_GROUP, D_MODEL // 2), jnp.uint32),
                        pltpu.VMEM((2 * MOE_PLANE, D_MODEL // 2), jnp.uint32),
                        pltpu.VMEM((2, MOE_TILE, D_MODEL // 2), jnp.uint32),
                        pltpu.SMEM((MOE_ROWS + MOE_TILE,), jnp.int32),
                        pltpu.SMEM((2,), jnp.int32)],
    )
    return pl.pallas_call(
        _moe_kernel,
        out_shape=jax.ShapeDtypeStruct((rows, D_MODEL), BF16),
        grid_spec=grid_spec,
        compiler_params=pltpu.CompilerParams(
            dimension_semantics=("arbitrary", "arbitrary"), vmem_limit_bytes=VMEM_LIMIT),
        name="moe_experts",
    )(starts, counts, slots, h2, route, wg, wu, wt, wd)


def _final_kernel(x_ref, f_ref, mod_ref, fg_ref, o_ref):
    y = x_ref[...] + mod_ref[0, 5:6] * f_ref[...].astype(F32)
    o_ref[...] = _rms(y, fg_ref[...])


def _final_outer(x_hbm, f_hbm, mod_hbm, fg_hbm, o_hbm, *, tr, mod_row):
    rows = x_hbm.shape[0]
    deep = lambda: pl.BlockSpec((tr, D_MODEL), lambda i: (i, 0), pipeline_mode=pl.Buffered(3))
    pltpu.emit_pipeline(
        _final_kernel,
        grid=(rows // tr,),
        in_specs=[deep(), deep(),
                  pl.BlockSpec((1, 6, D_MODEL), lambda i: (mod_row(i), 0, 0)),
                  pl.BlockSpec((1, D_MODEL), lambda i: (0, 0))],
        out_specs=[pl.BlockSpec((tr, D_MODEL), lambda i: (i, 0))],
    )(x_hbm, f_hbm, mod_hbm, fg_hbm, o_hbm)


def _final(x1, f, mods_l, final_g, latent, seq):
    rows = x1.shape[0]
    tr = 1024
    mod_row = (lambda i: (i * tr) // seq) if latent else (lambda i: CTX_ROW)
    anywhere = pl.BlockSpec(memory_space=pl.ANY)
    return pl.pallas_call(
        functools.partial(_final_outer, tr=tr, mod_row=mod_row),
        out_shape=jax.ShapeDtypeStruct((rows, D_MODEL), F32),
        in_specs=[anywhere] * 4,
        out_specs=anywhere,
        compiler_params=pltpu.CompilerParams(vmem_limit_bytes=VMEM_LIMIT),
        name="final_norm",
    )(x1, f, mods_l, final_g.reshape(1, D_MODEL))


def _dup_heads(a):
    b, n = a.shape[:2]
    return jnp.repeat(a, 2, axis=2).reshape(b, n, 2 * KV_W).astype(BF16)


def kernel(x_prompt, x_sample, c, cache_k, cache_v, state_ssm_re, state_ssm_im, c_ctx, mod_w, mod_b, norm1_g, norm2_g, w_in, w_out, q_norm_g, k_norm_g, ssm_a_re, ssm_a_im, ssm_log_dt, ssm_b_re, ssm_b_im, ssm_c_re, ssm_c_im, ssm_d, ssm_glu_w, pool_w, pool_scale, ffn_w_gate, ffn_w_up, ffn_w_down, moe_router_w, moe_router_b, moe_w_gate, moe_w_up, moe_w_down, final_g):
    bp, lp, _ = x_prompt.shape
    bs, ls, _ = x_sample.shape
    assert bs == SEQ_GROUP and bp % SEQ_GROUP == 0

    cond = jnp.zeros((MOD_ROWS, D_MODEL), F32).at[:bs].set(c).at[CTX_ROW].set(c_ctx)
    mods = _modulation(cond, mod_w, mod_b).reshape(DEPTH, MOD_ROWS, 6, D_MODEL)

    bw, cw, lam = _ssm_prep(ssm_a_re, ssm_a_im, ssm_log_dt, ssm_b_re, ssm_b_im, ssm_c_re, ssm_c_im)

    xp, xs = x_prompt, x_sample
    new_k, new_v, new_state = [], [], []
    for l in range(DEPTH):
        mods_l = mods[l]
        glu_bf = ssm_glu_w[l].astype(BF16)
        moe = l % 2 == 1
        i = l // 2
        router = (moe_router_w[i], moe_router_b[i]) if moe else None
        if moe:
            wg, wu, wd = (moe_w_gate[i].astype(BF16), moe_w_up[i].astype(BF16), moe_w_down[i].astype(BF16))
        else:
            wg, wu, wd = (ffn_w_gate[i].astype(BF16), ffn_w_up[i].astype(BF16), ffn_w_down[i].astype(BF16))
        for latent in (False, True):
            x = xs if latent else xp
            b, seq, _ = x.shape
            nbg = b // SEQ_GROUP
            rows = b * seq
            outs = _inproj(x.reshape(rows, D_MODEL), mods_l, norm1_g[l], w_in, l, q_norm_g[l], k_norm_g[l],
                           latent, seq)
            q, kd, vd = (a.reshape(b, seq, a.shape[-1]) for a in outs[:3])
            u_ssm = outs[3]
            u_pool = outs[4].reshape(b, seq, POOL_W)
            if latent:
                kds = [kd, _dup_heads(cache_k[:, l])]
                vds = [vd, _dup_heads(cache_v[:, l])]
                h0 = jnp.concatenate([
                    state_ssm_re[:, l, 0].reshape(b, SSM_GP), state_ssm_im[:, l, 0].reshape(b, SSM_GP),
                    state_ssm_re[:, l, 1].reshape(b, SSM_GP), state_ssm_im[:, l, 1].reshape(b, SSM_GP)],
                    axis=1).reshape(nbg, SEQ_GROUP, 4 * SSM_GP)
            else:
                kds, vds = [kd], [vd]
                new_k.append(outs[5])
                new_v.append(outs[6])
                h0 = jnp.zeros((nbg, SEQ_GROUP, 4 * SSM_GP), F32)
            attn = _attention(q, kds, vds)
            y_ssm, hf = _ssm(u_ssm, l, bw, cw, lam, h0, ssm_d[l], glu_bf, seq)
            if not latent:
                new_state.append(hf.reshape(b, 2, 2, SSM_GROUPS, SSM_STATE))
            res = _mixout(x, attn, y_ssm.reshape(b, seq, SSM_W), u_pool, mods_l,
                          pool_w[l], pool_scale[l], w_out, l, norm2_g[l], latent, router)
            x1 = res[0].reshape(rows, D_MODEL)
            if moe:
                route = res[2].reshape(rows, ROUTE_W)
                starts, counts, slots = _route_tables(res[3], res[4], seq)
                f = _moe_experts(res[1].reshape(rows, D_MODEL), route, starts, counts, slots, wg, wu, wd)
                y = _final(x1, f, mods_l, final_g, latent, seq)
            else:
                y = _ffn(res[1].reshape(rows, D_MODEL), x1, mods_l, wg, wu, wd, latent, seq)
            y = y.reshape(b, seq, D_MODEL)
            if latent:
                xs = y
            else:
                xp = y

    def cache_out(parts):
        a = jnp.stack(parts, axis=1).reshape(bp, DEPTH, N_KV_HEADS, HEAD_DIM, lp)
        return a.transpose(0, 1, 4, 2, 3)
    new_cache_k = cache_out(new_k)
    new_cache_v = cache_out(new_v)
    st = jnp.stack(new_state, axis=1)
    return (xp, xs, new_cache_k, new_cache_v, st[:, :, :, 0], st[:, :, :, 1])
```
